```python
import jax, jax.numpy as jnp
from jax import lax
import numpy as np

D_MODEL = 1024
BATCH = 1
SEQ = 16384
DEPTH = 2

N_EVEN = (DEPTH + 1) // 2
N_ODD = DEPTH // 2
EPS = 1e-6

CHUNK = 128
A_WIDTH = D_MODEL // 2
A_GROUPS = 4
A_HD = A_WIDTH // A_GROUPS
B_WIDTH = D_MODEL // 2
POOL_WINDOWS = (2, 4, 8, 16)
B_GROUPS = len(POOL_WINDOWS)
B_HD = B_WIDTH // B_GROUPS
E_IN = 2 * A_WIDTH + B_WIDTH
E_OUT = A_WIDTH + B_WIDTH
C_WIDTH = D_MODEL // 2
CONV_W = 3
MLA_HEADS = 4
Q_LORA = 256
KV_LORA = 256
QK_NOPE = 128
QK_ROPE = 64
QK_HD = QK_NOPE + QK_ROPE
V_HD = 128
ROPE_THETA = 10000.0
Q_BLOCK = 128
O_IN = 3 * C_WIDTH + Q_LORA + KV_LORA + QK_ROPE
O_OUT = C_WIDTH + MLA_HEADS * V_HD
D_FF = 2816
N_EXPERTS = 8
TOP_K = 2
D_FF_EXPERT = 3584

kernel_name = "hybrid_gmlp_pool_conv_mla_moe_adaln"


def rmsnorm(x, w):
    xf = x.astype(jnp.float32)
    y = xf * lax.rsqrt(jnp.mean(xf * xf, axis=-1, keepdims=True) + EPS)
    return (y * w.astype(jnp.float32)).astype(x.dtype)


def layernorm(x, w, b):
    xf = x.astype(jnp.float32)
    mu = jnp.mean(xf, axis=-1, keepdims=True)
    var = jnp.mean(jnp.square(xf - mu), axis=-1, keepdims=True)
    y = (xf - mu) * lax.rsqrt(var + EPS)
    return (y * w.astype(jnp.float32) + b.astype(jnp.float32)).astype(x.dtype)


def swiglu(h, w_gu, w_down):
    g, u = jnp.split(h @ w_gu, 2, axis=-1)
    return (jax.nn.silu(g) * u) @ w_down


def mixer_a(uv, ln_w, ln_b, w_s, b_s):
    bsz, s, _ = uv.shape
    u, v = jnp.split(jax.nn.gelu(uv), 2, axis=-1)
    v = layernorm(v, ln_w, ln_b)
    v = v.reshape(bsz, s // CHUNK, CHUNK, A_GROUPS, A_HD)
    w = w_s * jnp.tril(jnp.ones((CHUNK, CHUNK), dtype=w_s.dtype))[None]
    sv = jnp.einsum('gts,bnsgd->bntgd', w, v) + b_s.T[None, None, :, :, None]
    return u * sv.reshape(bsz, s, A_WIDTH)


def mixer_b(p, w_grp, scale):
    bsz, s, _ = p.shape
    cs = jnp.pad(jnp.cumsum(p.astype(jnp.float32), axis=1), ((0, 0), (1, 0), (0, 0)))
    t = jnp.arange(s)
    pooled = []
    for gi, win in enumerate(POOL_WINDOWS):
        csg = cs[..., gi * B_HD:(gi + 1) * B_HD]
        hi = csg[:, 1:]
        lo = jnp.pad(csg, ((0, 0), (win - 1, 0), (0, 0)))[:, :s]
        count = jnp.minimum(t + 1, win).astype(jnp.float32)[None, :, None]
        pooled.append((hi - lo) / count)
    pooled = jnp.concatenate(pooled, axis=-1).astype(p.dtype) - p
    y = jnp.einsum('bsgd,gde->bsge', pooled.reshape(bsz, s, B_GROUPS, B_HD), w_grp)
    return y.reshape(bsz, s, B_WIDTH) * scale


def mixer_c(bch, conv_w):
    s = bch.shape[1]
    bg, cg, h = jnp.split(bch, 3, axis=-1)
    z = jnp.pad(cg * h, ((0, 0), (CONV_W - 1, 0), (0, 0)))
    conv = sum(conv_w[k] * z[:, k:k + s] for k in range(CONV_W))
    return bg * conv


def rope(x, positions):
    inv_freq = ROPE_THETA ** (-jnp.arange(0, QK_ROPE, 2, dtype=jnp.float32) / QK_ROPE)
    ang = positions.astype(jnp.float32)[..., None] * inv_freq
    cos = jnp.cos(ang)[:, :, None, :].astype(x.dtype)
    sin = jnp.sin(ang)[:, :, None, :].astype(x.dtype)
    x1, x2 = jnp.split(x, 2, axis=-1)
    return jnp.concatenate([x1 * cos - x2 * sin, x2 * cos + x1 * sin], axis=-1)


def mixer_d(cq, ckv, kpe, positions, q_a_norm, w_uq, kv_norm, w_ukv, q_norm_w, k_norm_w):
    bsz, s, _ = cq.shape
    q = (rmsnorm(cq, q_a_norm) @ w_uq).reshape(bsz, s, MLA_HEADS, QK_HD)
    kv = (rmsnorm(ckv, kv_norm) @ w_ukv).reshape(bsz, s, MLA_HEADS, QK_NOPE + V_HD)
    k_nope, v = kv[..., :QK_NOPE], kv[..., QK_NOPE:]
    k_pe = jnp.broadcast_to(kpe[:, :, None, :], (bsz, s, MLA_HEADS, QK_ROPE))
    k = jnp.concatenate([k_nope, k_pe], axis=-1)
    q = rmsnorm(q, q_norm_w)
    k = rmsnorm(k, k_norm_w)
    q = jnp.concatenate([q[..., :QK_NOPE], rope(q[..., QK_NOPE:], positions)], axis=-1)
    k = jnp.concatenate([k[..., :QK_NOPE], rope(k[..., QK_NOPE:], positions)], axis=-1)
    nb = s // Q_BLOCK
    qb = q.reshape(bsz, nb, Q_BLOCK, MLA_HEADS, QK_HD).transpose(1, 0, 2, 3, 4)
    key_pos = jnp.arange(s)
    sm_scale = QK_HD ** -0.5

    def block(args):
        qi, bi = args
        sc = jnp.einsum('bqhd,bkhd->bhqk', qi, k, preferred_element_type=jnp.float32) * sm_scale
        qpos = bi * Q_BLOCK + jnp.arange(Q_BLOCK)
        sc = jnp.where(key_pos[None, :] <= qpos[:, None], sc, -1e30)
        pr = jax.nn.softmax(sc, axis=-1).astype(v.dtype)
        return jnp.einsum('bhqk,bkhd->bqhd', pr, v)

    o = lax.map(block, (qb, jnp.arange(nb)))
    return o.transpose(1, 0, 2, 3, 4).reshape(bsz, s, MLA_HEADS * V_HD)


def moe_swiglu(h, router_w, w_gu, w_down):
    logits = (h @ router_w).astype(jnp.float32)
    top_v, top_i = lax.top_k(logits, TOP_K)
    wts = jax.nn.softmax(top_v, axis=-1)
    gates = jnp.einsum('bsk,bske->bse', wts, jax.nn.one_hot(top_i, N_EXPERTS, dtype=jnp.float32)).astype(h.dtype)
    y = jnp.zeros_like(h)
    for e in range(N_EXPERTS):
        y = y + gates[..., e:e + 1] * swiglu(h, w_gu[e], w_down[e])
    return y


def setup_inputs(seed: int = 0) -> dict:
    key = jax.random.key(seed)
    keys = jax.random.split(key, 40)
    ctr = [0]

    def nrm(shape, scale):
        k = keys[ctr[0]]
        ctr[0] += 1
        return jax.random.normal(k, shape, jnp.float32) * scale

    def gain(shape):
        return 1.0 + nrm(shape, 0.1)

    D = D_MODEL
    return {
        "x": nrm((BATCH, SEQ, D), 1.0),
        "c": nrm((BATCH, D), 1.0),
        "positions": jnp.broadcast_to(jnp.arange(SEQ, dtype=jnp.int32), (BATCH, SEQ)),
        "norm_mix_w": gain((DEPTH, D)),
        "norm_ffn_w": gain((DEPTH, D)),
        "ada_w": nrm((DEPTH, D, 6 * D), D ** -0.5),
        "ada_b": nrm((DEPTH, 6 * D), 0.02),
        "e_w_in": nrm((N_EVEN, D, E_IN), D ** -0.5),
        "a_ln_w": gain((N_EVEN, A_WIDTH)),
        "a_ln_b": nrm((N_EVEN, A_WIDTH), 0.02),
        "a_w_s": nrm((N_EVEN, A_GROUPS, CHUNK, CHUNK), CHUNK ** -0.5),
        "a_b_s": gain((N_EVEN, A_GROUPS, CHUNK)),
        "b_w_grp": nrm((N_EVEN, B_GROUPS, B_HD, B_HD), B_HD ** -0.5),
        "b_scale": gain((N_EVEN, B_WIDTH)),
        "e_w_out": nrm((N_EVEN, E_OUT, D), E_OUT ** -0.5),
        "ffn_w_gu": nrm((N_EVEN, D, 2 * D_FF), D ** -0.5),
        "ffn_w_down": nrm((N_EVEN, D_FF, D), D_FF ** -0.5),
        "o_w_in": nrm((N_ODD, D, O_IN), D ** -0.5),
        "c_conv_w": nrm((N_ODD, CONV_W, C_WIDTH), CONV_W ** -0.5),
        "q_a_norm": gain((N_ODD, Q_LORA)),
        "w_uq": nrm((N_ODD, Q_LORA, MLA_HEADS * QK_HD), Q_LORA ** -0.5),
        "kv_norm": gain((N_ODD, KV_LORA)),
        "w_ukv": nrm((N_ODD, KV_LORA, MLA_HEADS * (QK_NOPE + V_HD)), KV_LORA ** -0.5),
        "q_norm_w": gain((N_ODD, QK_HD)),
        "k_norm_w": gain((N_ODD, QK_HD)),
        "o_w_out": nrm((N_ODD, O_OUT, D), O_OUT ** -0.5),
        "router_w": nrm((N_ODD, D, N_EXPERTS), D ** -0.5),
        "moe_w_gu": nrm((N_ODD, N_EXPERTS, D, 2 * D_FF_EXPERT), D ** -0.5),
        "moe_w_down": nrm((N_ODD, N_EXPERTS, D_FF_EXPERT, D), D_FF_EXPERT ** -0.5),
    }


def reference(x, c, positions, norm_mix_w, norm_ffn_w, ada_w, ada_b,
              e_w_in, a_ln_w, a_ln_b, a_w_s, a_b_s, b_w_grp, b_scale, e_w_out,
              ffn_w_gu, ffn_w_down,
              o_w_in, c_conv_w, q_a_norm, w_uq, kv_norm, w_ukv, q_norm_w, k_norm_w,
              o_w_out, router_w, moe_w_gu, moe_w_down):
    c_act = jax.nn.silu(c)
    for layer in range(DEPTH):
        mod = c_act @ ada_w[layer] + ada_b[layer]
        sh_m, sc_m, g_m, sh_f, sc_f, g_f = [m[:, None, :] for m in jnp.split(mod, 6, axis=-1)]
        h = rmsnorm(x, norm_mix_w[layer]) * (1 + sc_m) + sh_m
        if layer % 2 == 0:
            i = layer // 2
            p = h @ e_w_in[i]
            ya = mixer_a(p[..., :2 * A_WIDTH], a_ln_w[i], a_ln_b[i], a_w_s[i], a_b_s[i])
            yb = mixer_b(p[..., 2 * A_WIDTH:], b_w_grp[i], b_scale[i])
            mix = jnp.concatenate([ya, yb], axis=-1) @ e_w_out[i]
        else:
            i = layer // 2
            p = h @ o_w_in[i]
            o0 = 3 * C_WIDTH
            o1 = o0 + Q_LORA
            o2 = o1 + KV_LORA
            yc = mixer_c(p[..., :o0], c_conv_w[i])
            yd = mixer_d(p[..., o0:o1], p[..., o1:o2], p[..., o2:], positions,
                         q_a_norm[i], w_uq[i], kv_norm[i], w_ukv[i], q_norm_w[i], k_norm_w[i])
            mix = jnp.concatenate([yc, yd], axis=-1) @ o_w_out[i]
        x = x + g_m * mix
        h = rmsnorm(x, norm_ffn_w[layer]) * (1 + sc_f) + sh_f
        if layer % 2 == 0:
            ffn = swiglu(h, ffn_w_gu[layer // 2], ffn_w_down[layer // 2])
        else:
            i = layer // 2
            ffn = moe_swiglu(h, router_w[i], moe_w_gu[i], moe_w_down[i])
        x = x + g_f * ffn
    return x
```

```python
import functools

import jax
import jax.numpy as jnp
from jax import lax
from jax.experimental import pallas as pl
from jax.experimental.pallas import tpu as pltpu

D_MODEL = 1024
SEQ = 16384
EPS = 1e-6
CHUNK = 128
A_WIDTH = 512
A_GROUPS = 4
B_WIDTH = 512
POOL_WINDOWS = (2, 4, 8, 16)
B_HD = 128
C_WIDTH = 512
MLA_HEADS = 4
Q_LORA = 256
KV_LORA = 256
QK_NOPE = 128
QK_ROPE = 64
QK_HD = QK_NOPE + QK_ROPE
V_HD = 128
ROPE_THETA = 10000.0
D_FF = 2816
N_EXPERTS = 8
D_FF_EXPERT = 3584

LANES = 128
POOL_HALO = 16
CONV_HALO = 8
QK_PAD = 256
VMEM_LIMIT = 56 * 1024 * 1024

F32 = jnp.float32
BF16 = jnp.bfloat16


def _params(sem, vmem=VMEM_LIMIT):
    return pltpu.CompilerParams(dimension_semantics=sem, vmem_limit_bytes=vmem)


def _const_spec(shape):
    nd = len(shape)
    return pl.BlockSpec(shape, lambda *_: (0,) * nd)


def _rms_mod(x, nw, sc, sh):
    ms = jnp.mean(x * x, axis=-1, keepdims=True)
    return (x * lax.rsqrt(ms + EPS)) * nw * (1.0 + sc) + sh


def _ada_kernel(c_ref, w_ref, b_ref, o_ref):
    c = c_ref[...]
    ca = c * jax.nn.sigmoid(c)
    o_ref[0] = jnp.sum(w_ref[0] * ca, axis=0, keepdims=True) + b_ref[0]


def _ada_mod(c, ada_w, ada_b):
    depth, d, n = ada_w.shape
    tn = 768
    return pl.pallas_call(
        _ada_kernel,
        grid=(depth, n // tn),
        in_specs=[
            pl.BlockSpec((d, 1), lambda l, j: (0, 0)),
            pl.BlockSpec((1, d, tn), lambda l, j: (l, 0, j)),
            pl.BlockSpec((1, 1, tn), lambda l, j: (l, 0, j)),
        ],
        out_specs=pl.BlockSpec((1, 1, tn), lambda l, j: (l, 0, j)),
        out_shape=jax.ShapeDtypeStruct((depth, 1, n), F32),
        compiler_params=_params(("arbitrary", "arbitrary")),
        name="ada_mod",
    )(c.reshape(d, 1), ada_w, ada_b.reshape(depth, 1, n))


def _even_mix_kernel(x_ref, mod_ref, nw_ref, win_ref, lnw_ref, lnb_ref, ws_ref, bs_ref, wg_ref, bsc_ref,
                     wout_ref, o_ref, halo_ref, sv_ref, yb_ref):
    tm = x_ref.shape[0]
    i = pl.program_id(0)

    @pl.when(i == 0)
    def _():
        halo_ref[...] = jnp.zeros_like(halo_ref)

    d = D_MODEL
    x = x_ref[...]
    sh, sc, gate = mod_ref[:, 0:d], mod_ref[:, d:2 * d], mod_ref[:, 2 * d:3 * d]
    h = _rms_mod(x, nw_ref[...], sc, sh).astype(BF16)
    p = jnp.dot(h, win_ref[...], preferred_element_type=F32)

    gl = jax.nn.gelu(p[:, :2 * A_WIDTH])
    u = gl[:, :A_WIDTH]
    v = gl[:, A_WIDTH:]
    mu = jnp.mean(v, axis=-1, keepdims=True)
    vc = v - mu
    var = jnp.mean(vc * vc, axis=-1, keepdims=True)
    vn = (vc * lax.rsqrt(var + EPS) * lnw_ref[...] + lnb_ref[...]).astype(BF16)
    row = lax.broadcasted_iota(jnp.int32, (CHUNK, CHUNK), 0)
    col = lax.broadcasted_iota(jnp.int32, (CHUNK, CHUNK), 1)
    for g in range(A_GROUPS):
        w = jnp.where(col <= row, ws_ref[g], 0.0).astype(BF16)
        b = bs_ref[g]
        for c in range(tm // CHUNK):
            blk = vn[c * CHUNK:(c + 1) * CHUNK, g * LANES:(g + 1) * LANES]
            sv_ref[c * CHUNK:(c + 1) * CHUNK, g * LANES:(g + 1) * LANES] = (
                jnp.dot(w, blk, preferred_element_type=F32) + b)
    ya = (u * sv_ref[...]).astype(BF16)

    pb = p[:, 2 * A_WIDTH:]
    ext = jnp.concatenate([halo_ref[...], pb], axis=0)
    halo_ref[...] = pb[tm - POOL_HALO:, :]
    t_glob = i * tm + lax.broadcasted_iota(jnp.int32, (tm, 1), 0)
    s = ext
    width = 1
    for g, win in enumerate(POOL_WINDOWS):
        while width < win:
            s = s + pltpu.roll(s, width, 0)
            width *= 2
        cnt = jnp.minimum(t_glob + 1, win).astype(F32)
        sl = slice(g * B_HD, (g + 1) * B_HD)
        pooled = s[POOL_HALO:, sl] / cnt
        dg = (pooled - pb[:, sl]).astype(BF16)
        yb_ref[:, sl] = jnp.dot(dg, wg_ref[g], preferred_element_type=F32)
    yb = (yb_ref[...] * bsc_ref[...]).astype(BF16)

    mix = (jnp.dot(ya, wout_ref[:A_WIDTH, :], preferred_element_type=F32)
           + jnp.dot(yb, wout_ref[A_WIDTH:, :], preferred_element_type=F32))
    o_ref[...] = x + gate * mix


def _even_mix(x, mod, nw, w_in, ln_w, ln_b, w_s, b_s, w_grp, b_scale, w_out, tm=512):
    s, d = x.shape
    return pl.pallas_call(
        _even_mix_kernel,
        grid=(s // tm,),
        in_specs=[
            pl.BlockSpec((tm, d), lambda i: (i, 0)),
            _const_spec(mod.shape), _const_spec(nw.shape), _const_spec(w_in.shape),
            _const_spec(ln_w.shape), _const_spec(ln_b.shape), _const_spec(w_s.shape), _const_spec(b_s.shape),
            _const_spec(w_grp.shape), _const_spec(b_scale.shape), _const_spec(w_out.shape),
        ],
        out_specs=pl.BlockSpec((tm, d), lambda i: (i, 0)),
        out_shape=jax.ShapeDtypeStruct((s, d), F32),
        scratch_shapes=[pltpu.VMEM((POOL_HALO, B_WIDTH), F32), pltpu.VMEM((tm, A_WIDTH), F32),
                        pltpu.VMEM((tm, B_WIDTH), F32)],
        compiler_params=_params(("arbitrary",)),
        name="even_mix",
    )(x, mod, nw, w_in, ln_w, ln_b, w_s, b_s, w_grp, b_scale, w_out)


def _ffn_kernel(x_ref, mod_ref, nw_ref, wgu_ref, wd_ref, o_ref, *, n_chunks):
    d = D_MODEL
    x = x_ref[...]
    sh, sc, gate = mod_ref[:, 3 * d:4 * d], mod_ref[:, 4 * d:5 * d], mod_ref[:, 5 * d:6 * d]
    h = _rms_mod(x, nw_ref[...], sc, sh).astype(BF16)
    ff = wd_ref.shape[0]
    tf = ff // n_chunks
    acc = jnp.zeros(x.shape, F32)
    for f in range(n_chunks):
        g = jnp.dot(h, wgu_ref[:, f * tf:(f + 1) * tf], preferred_element_type=F32)
        u = jnp.dot(h, wgu_ref[:, ff + f * tf:ff + (f + 1) * tf], preferred_element_type=F32)
        act = (g * jax.nn.sigmoid(g) * u).astype(BF16)
        acc = acc + jnp.dot(act, wd_ref[f * tf:(f + 1) * tf, :], preferred_element_type=F32)
    o_ref[...] = x + gate * acc


def _ffn(x, mod, nw, w_gu, w_down, tm=512, n_chunks=2):
    s, d = x.shape
    return pl.pallas_call(
        functools.partial(_ffn_kernel, n_chunks=n_chunks),
        grid=(s // tm,),
        in_specs=[
            pl.BlockSpec((tm, d), lambda i: (i, 0)),
            _const_spec(mod.shape), _const_spec(nw.shape), _const_spec(w_gu.shape), _const_spec(w_down.shape),
        ],
        out_specs=pl.BlockSpec((tm, d), lambda i: (i, 0)),
        out_shape=jax.ShapeDtypeStruct((s, d), F32),
        compiler_params=_params(("arbitrary",)),
        name="ffn",
    )(x, mod, nw, w_gu, w_down)


def _rope_table_kernel(pos_ref, invf_ref, cos_ref, sin_ref):
    ang = pos_ref[...].astype(F32) * invf_ref[...]
    cos_ref[...] = jnp.cos(ang)
    sin_ref[...] = jnp.sin(ang)


def _rope_tables(positions):
    s = positions.shape[-1]
    half = QK_ROPE // 2
    inv_freq = ROPE_THETA ** (-jnp.arange(0, QK_ROPE, 2, dtype=F32) / QK_ROPE)
    cos_t, sin_t = pl.pallas_call(
        _rope_table_kernel,
        out_shape=(jax.ShapeDtypeStruct((half, s), F32), jax.ShapeDtypeStruct((half, s), F32)),
        name="rope_tables",
    )(positions.reshape(1, s), inv_freq.reshape(half, 1))
    return jnp.tile(cos_t.T, (1, 2)), jnp.tile(sin_t.T, (1, 2))


def _rope_rot(x):
    n = x.shape[-1]
    lane = lax.broadcasted_iota(jnp.int32, x.shape, 1)
    half = QK_ROPE // 2
    fwd = pltpu.roll(x, half, 1)
    bwd = pltpu.roll(x, n - half, 1)
    return jnp.where((lane % QK_ROPE) < half, -bwd, fwd)


def _odd_prep_kernel(x_ref, mod_ref, nw_ref, win_ref, wkpe_ref, cw_ref, qan_ref, wuq_ref, kvn_ref, wukv_ref,
                     qnw_ref, knw_ref, cos_ref, sin_ref, yc_ref, q_ref, k_ref, v_ref, halo_ref):
    tm = x_ref.shape[0]
    i = pl.program_id(0)

    @pl.when(i == 0)
    def _():
        halo_ref[...] = jnp.zeros_like(halo_ref)

    d = D_MODEL
    x = x_ref[...]
    sh, sc = mod_ref[:, 0:d], mod_ref[:, d:2 * d]
    h = _rms_mod(x, nw_ref[...], sc, sh).astype(BF16)
    p = jnp.dot(h, win_ref[...], preferred_element_type=F32)
    kpe = jnp.dot(h, wkpe_ref[...], preferred_element_type=F32)[:, :QK_ROPE]

    cw = C_WIDTH
    bg, cg, hh = p[:, 0:cw], p[:, cw:2 * cw], p[:, 2 * cw:3 * cw]
    z = cg * hh
    ext = jnp.concatenate([halo_ref[...], z], axis=0)
    halo_ref[...] = z[tm - CONV_HALO:, :]
    z1 = pltpu.roll(ext, 1, 0)[CONV_HALO:, :]
    z2 = pltpu.roll(ext, 2, 0)[CONV_HALO:, :]
    conv = cw_ref[0:1, :] * z2 + cw_ref[1:2, :] * z1 + cw_ref[2:3, :] * z
    yc_ref[...] = (bg * conv).astype(BF16)

    o0 = 3 * cw
    cq = p[:, o0:o0 + Q_LORA]
    ckv = p[:, o0 + Q_LORA:o0 + Q_LORA + KV_LORA]
    cqn = (cq * lax.rsqrt(jnp.mean(cq * cq, axis=-1, keepdims=True) + EPS) * qan_ref[...]).astype(BF16)
    ckvn = (ckv * lax.rsqrt(jnp.mean(ckv * ckv, axis=-1, keepdims=True) + EPS) * kvn_ref[...]).astype(BF16)
    q = jnp.dot(cqn, wuq_ref[...], preferred_element_type=F32)
    kv = jnp.dot(ckvn, wukv_ref[...], preferred_element_type=F32)

    nh = MLA_HEADS
    cos = jnp.concatenate([cos_ref[...]] * nh, axis=1)
    sin = jnp.concatenate([sin_ref[...]] * nh, axis=1)
    qn_w, qr_w = qnw_ref[:, :QK_NOPE], qnw_ref[:, QK_NOPE:]
    kn_w, kr_w = knw_ref[:, :QK_NOPE], knw_ref[:, QK_NOPE:]
    sm_scale = QK_HD ** -0.5

    qr_all = q[:, nh * QK_NOPE:]
    qr_w4 = jnp.concatenate([qr_w] * nh, axis=1)
    qr_all = qr_all * qr_w4
    qr_all = qr_all * cos + _rope_rot(qr_all) * sin
    kr = kpe * kr_w
    kr = kr * cos[:, :QK_ROPE] + _rope_rot(kr) * sin[:, :QK_ROPE]
    kpe_ss = jnp.sum(kpe * kpe, axis=-1, keepdims=True)
    zpad = jnp.zeros((tm, QK_PAD - QK_HD), BF16)
    for hd in range(nh):
        qn = q[:, hd * QK_NOPE:(hd + 1) * QK_NOPE]
        qr_raw = q[:, nh * QK_NOPE + hd * QK_ROPE:nh * QK_NOPE + (hd + 1) * QK_ROPE]
        q_ss = jnp.sum(qn * qn, axis=-1, keepdims=True) + jnp.sum(qr_raw * qr_raw, axis=-1, keepdims=True)
        q_inv = lax.rsqrt(q_ss / QK_HD + EPS) * sm_scale
        q_ref[hd] = jnp.concatenate(
            [(qn * q_inv * qn_w).astype(BF16),
             (qr_all[:, hd * QK_ROPE:(hd + 1) * QK_ROPE] * q_inv).astype(BF16), zpad], axis=1)
        kn = kv[:, hd * (QK_NOPE + V_HD):hd * (QK_NOPE + V_HD) + QK_NOPE]
        k_ss = jnp.sum(kn * kn, axis=-1, keepdims=True) + kpe_ss
        k_inv = lax.rsqrt(k_ss / QK_HD + EPS)
        k_ref[hd] = jnp.concatenate(
            [(kn * k_inv * kn_w).astype(BF16), (kr * k_inv).astype(BF16), zpad], axis=1)
        v_ref[hd] = kv[:, hd * (QK_NOPE + V_HD) + QK_NOPE:(hd + 1) * (QK_NOPE + V_HD)].astype(BF16)


def _odd_prep(x, mod, nw, w_in, w_kpe, conv_w, q_a_norm, w_uq, kv_norm, w_ukv, q_norm_w, k_norm_w, cos, sin,
              tm=512):
    s, d = x.shape
    nh = MLA_HEADS
    consts = [mod, nw, w_in, w_kpe, conv_w, q_a_norm, w_uq, kv_norm, w_ukv, q_norm_w, k_norm_w]
    return pl.pallas_call(
        _odd_prep_kernel,
        grid=(s // tm,),
        in_specs=[pl.BlockSpec((tm, d), lambda i: (i, 0))] + [_const_spec(a.shape) for a in consts] + [
            pl.BlockSpec((tm, QK_ROPE), lambda i: (i, 0)), pl.BlockSpec((tm, QK_ROPE), lambda i: (i, 0))],
        out_specs=[
            pl.BlockSpec((tm, C_WIDTH), lambda i: (i, 0)),
            pl.BlockSpec((nh, tm, QK_PAD), lambda i: (0, i, 0)),
            pl.BlockSpec((nh, tm, QK_PAD), lambda i: (0, i, 0)),
            pl.BlockSpec((nh, tm, V_HD), lambda i: (0, i, 0)),
        ],
        out_shape=[
            jax.ShapeDtypeStruct((s, C_WIDTH), BF16),
            jax.ShapeDtypeStruct((nh, s, QK_PAD), BF16),
            jax.ShapeDtypeStruct((nh, s, QK_PAD), BF16),
            jax.ShapeDtypeStruct((nh, s, V_HD), BF16),
        ],
        scratch_shapes=[pltpu.VMEM((CONV_HALO, C_WIDTH), F32)],
        compiler_params=_params(("arbitrary",)),
        name="odd_prep",
    )(x, *consts, cos, sin)


def _attn_kernel(q_ref, k_ref, v_ref, o_ref, m_ref, l_ref, acc_ref, *, tq, tk):
    qi = pl.program_id(1)
    q = q_ref[0]
    m_ref[...] = jnp.full_like(m_ref, -jnp.inf)
    l_ref[...] = jnp.zeros_like(l_ref)
    acc_ref[...] = jnp.zeros_like(acc_ref)

    def step(j, masked):
        start = pl.multiple_of(j * tk, tk)
        k = k_ref[0, pl.ds(start, tk), :]
        v = v_ref[0, pl.ds(start, tk), :]
        s = lax.dot_general(q, k, (((1,), (1,)), ((), ())), preferred_element_type=F32)
        if masked:
            rows = lax.broadcasted_iota(jnp.int32, (tq, tk), 0)
            cols = lax.broadcasted_iota(jnp.int32, (tq, tk), 1)
            s = jnp.where(cols <= rows, s, -1e30)
        m_prev = m_ref[...]
        m_new = jnp.maximum(m_prev, jnp.max(s, axis=-1, keepdims=True))
        alpha = jnp.exp(m_prev - m_new)
        p = jnp.exp(s - m_new)
        l_ref[...] = alpha * l_ref[...] + jnp.sum(p, axis=-1, keepdims=True)
        acc_ref[...] = alpha * acc_ref[...] + jnp.dot(p.astype(BF16), v, preferred_element_type=F32)
        m_ref[...] = m_new

    def body(j, carry):
        step(j, False)
        return carry

    lax.fori_loop(0, qi, body, 0)
    step(qi, True)
    o_ref[...] = (acc_ref[...] / l_ref[...]).astype(o_ref.dtype)


def _attention(q, k, v, tq=512):
    nh, s, _ = q.shape
    tk = tq
    return pl.pallas_call(
        functools.partial(_attn_kernel, tq=tq, tk=tk),
        grid=(nh, s // tq),
        in_specs=[
            pl.BlockSpec((1, tq, QK_PAD), lambda h, i: (h, i, 0)),
            pl.BlockSpec((1, s, QK_PAD), lambda h, i: (h, 0, 0)),
            pl.BlockSpec((1, s, V_HD), lambda h, i: (h, 0, 0)),
        ],
        out_specs=pl.BlockSpec((tq, V_HD), lambda h, i: (i, h)),
        out_shape=jax.ShapeDtypeStruct((s, nh * V_HD), BF16),
        scratch_shapes=[pltpu.VMEM((tq, 1), F32), pltpu.VMEM((tq, 1), F32), pltpu.VMEM((tq, V_HD), F32)],
        compiler_params=_params(("arbitrary", "arbitrary")),
        name="mla_attention",
    )(q, k, v)


def _odd_out_kernel(x_ref, yc_ref, yd_ref, mod_ref, nw_ref, wout_ref, wr_ref, x_out_ref, h_ref, gates_ref):
    d = D_MODEL
    x = x_ref[...]
    gate_m = mod_ref[:, 2 * d:3 * d]
    sh, sc = mod_ref[:, 3 * d:4 * d], mod_ref[:, 4 * d:5 * d]
    mix = (jnp.dot(yc_ref[...], wout_ref[:C_WIDTH, :], preferred_element_type=F32)
           + jnp.dot(yd_ref[...], wout_ref[C_WIDTH:, :], preferred_element_type=F32))
    x1 = x + gate_m * mix
    x_out_ref[...] = x1
    h = _rms_mod(x1, nw_ref[...], sc, sh)
    h_ref[...] = h.astype(BF16)

    logits = jnp.dot(h, wr_ref[...], preferred_element_type=F32, precision=lax.Precision.HIGHEST)
    lane = lax.broadcasted_iota(jnp.int32, logits.shape, 1)
    logits = jnp.where(lane < N_EXPERTS, logits, -jnp.inf)
    m1 = jnp.max(logits, axis=-1, keepdims=True)
    i1 = jnp.min(jnp.where(logits == m1, lane, LANES), axis=-1, keepdims=True)
    rest = jnp.where(lane == i1, -jnp.inf, logits)
    m2 = jnp.max(rest, axis=-1, keepdims=True)
    i2 = jnp.min(jnp.where(rest == m2, lane, LANES), axis=-1, keepdims=True)
    e2 = jnp.exp(m2 - m1)
    w1 = 1.0 / (1.0 + e2)
    w2 = e2 / (1.0 + e2)
    gates_ref[...] = jnp.where(lane == i1, w1, 0.0) + jnp.where(lane == i2, w2, 0.0)


def _odd_out(x, yc, yd, mod, nw, w_out, w_router, tm=512):
    s, d = x.shape
    return pl.pallas_call(
        _odd_out_kernel,
        grid=(s // tm,),
        in_specs=[
            pl.BlockSpec((tm, d), lambda i: (i, 0)),
            pl.BlockSpec((tm, C_WIDTH), lambda i: (i, 0)),
            pl.BlockSpec((tm, MLA_HEADS * V_HD), lambda i: (i, 0)),
            _const_spec(mod.shape), _const_spec(nw.shape), _const_spec(w_out.shape), _const_spec(w_router.shape),
        ],
        out_specs=[
            pl.BlockSpec((tm, d), lambda i: (i, 0)),
            pl.BlockSpec((tm, d), lambda i: (i, 0)),
            pl.BlockSpec((tm, LANES), lambda i: (i, 0)),
        ],
        out_shape=[
            jax.ShapeDtypeStruct((s, d), F32),
            jax.ShapeDtypeStruct((s, d), BF16),
            jax.ShapeDtypeStruct((s, LANES), F32),
        ],
        compiler_params=_params(("arbitrary",)),
        name="odd_out_router",
    )(x, yc, yd, mod, nw, w_out, w_router)


def _moe_kernel(x_ref, h_ref, gates_ref, mod_ref, wg_ref, wu_ref, wd_ref, o_ref, acc_ref):
    e = pl.program_id(1)
    f = pl.program_id(2)

    @pl.when((e == 0) & (f == 0))
    def _():
        acc_ref[...] = jnp.zeros_like(acc_ref)

    h = h_ref[...]
    g = jnp.dot(h, wg_ref[0], preferred_element_type=F32)
    u = jnp.dot(h, wu_ref[0], preferred_element_type=F32)
    act = (g * jax.nn.sigmoid(g) * u).astype(BF16)
    y = jnp.dot(act, wd_ref[0], preferred_element_type=F32)
    lane = lax.broadcasted_iota(jnp.int32, gates_ref.shape, 1)
    gate = jnp.sum(jnp.where(lane == e, gates_ref[...], 0.0), axis=-1, keepdims=True)
    acc_ref[...] += gate * y

    @pl.when((e == pl.num_programs(1) - 1) & (f == pl.num_programs(2) - 1))
    def _():
        d = D_MODEL
        o_ref[...] = x_ref[...] + mod_ref[:, 5 * d:6 * d] * acc_ref[...]


def _moe_dense(x, h, gates, mod, w_gu, w_down, tm=1024, tf=896):
    s, d = x.shape
    ne, _, ff2 = w_gu.shape
    ff = ff2 // 2
    nf = ff // tf
    return pl.pallas_call(
        _moe_kernel,
        grid=(s // tm, ne, nf),
        in_specs=[
            pl.BlockSpec((tm, d), lambda i, e, f: (i, 0)),
            pl.BlockSpec((tm, d), lambda i, e, f: (i, 0)),
            pl.BlockSpec((tm, LANES), lambda i, e, f: (i, 0)),
            pl.BlockSpec(mod.shape, lambda i, e, f: (0, 0)),
            pl.BlockSpec((1, d, tf), lambda i, e, f: (e, 0, f)),
            pl.BlockSpec((1, d, tf), lambda i, e, f: (e, 0, nf + f)),
            pl.BlockSpec((1, tf, d), lambda i, e, f: (e, f, 0)),
        ],
        out_specs=pl.BlockSpec((tm, d), lambda i, e, f: (i, 0)),
        out_shape=jax.ShapeDtypeStruct((s, d), F32),
        scratch_shapes=[pltpu.VMEM((tm, d), F32)],
        compiler_params=_params(("arbitrary", "arbitrary", "arbitrary")),
        name="moe_experts",
    )(x, h, gates, mod, w_gu, w_gu, w_down)


def kernel(x, c, positions, norm_mix_w, norm_ffn_w, ada_w, ada_b, e_w_in, a_ln_w, a_ln_b, a_w_s, a_b_s, b_w_grp,
           b_scale, e_w_out, ffn_w_gu, ffn_w_down, o_w_in, c_conv_w, q_a_norm, w_uq, kv_norm, w_ukv, q_norm_w,
           k_norm_w, o_w_out, router_w, moe_w_gu, moe_w_down):
    bsz, s, d = x.shape
    assert bsz == 1 and d == D_MODEL
    depth = ada_w.shape[0]
    nh = MLA_HEADS
    xs = x.reshape(s, d)
    mod = _ada_mod(c, ada_w, ada_b)
    cos, sin = _rope_tables(positions)
    row = lambda a: a.reshape(1, -1)

    for layer in range(depth):
        i = layer // 2
        m = mod[layer]
        if layer % 2 == 0:
            xs = _even_mix(
                xs, m, row(norm_mix_w[layer]), e_w_in[i].astype(BF16), row(a_ln_w[i]), row(a_ln_b[i]), a_w_s[i],
                a_b_s[i].reshape(A_GROUPS, CHUNK, 1), b_w_grp[i].astype(BF16), row(b_scale[i]),
                e_w_out[i].astype(BF16))
            xs = _ffn(xs, m, row(norm_ffn_w[layer]), ffn_w_gu[i].astype(BF16), ffn_w_down[i].astype(BF16))
        else:
            o2 = 3 * C_WIDTH + Q_LORA + KV_LORA
            w_in = o_w_in[i][:, :o2].astype(BF16)
            w_kpe = jnp.pad(o_w_in[i][:, o2:], ((0, 0), (0, LANES - QK_ROPE))).astype(BF16)
            wq = w_uq[i].reshape(Q_LORA, nh, QK_HD)
            wq = jnp.concatenate([wq[:, :, :QK_NOPE].reshape(Q_LORA, nh * QK_NOPE),
                                  wq[:, :, QK_NOPE:].reshape(Q_LORA, nh * QK_ROPE)], axis=1).astype(BF16)
            yc, q, k, v = _odd_prep(
                xs, m, row(norm_mix_w[layer]), w_in, w_kpe, c_conv_w[i], row(q_a_norm[i]), wq, row(kv_norm[i]),
                w_ukv[i].astype(BF16), row(q_norm_w[i]), row(k_norm_w[i]), cos, sin)
            yd = _attention(q, k, v)
            w_router = jnp.pad(router_w[i], ((0, 0), (0, LANES - N_EXPERTS)))
            xs, hb, gates = _odd_out(xs, yc, yd, m, row(norm_ffn_w[layer]), o_w_out[i].astype(BF16), w_router)
            xs = _moe_dense(xs, hb, gates, m, moe_w_gu[i].astype(BF16), moe_w_down[i].astype(BF16))
    return xs.reshape(bsz, s, d)
```

```python
import functools

import jax
import jax.numpy as jnp
from jax import lax
from jax.experimental import pallas as pl
from jax.experimental.pallas import tpu as pltpu

D_MODEL = 1024
SEQ = 16384
EPS = 1e-6
CHUNK = 128
A_WIDTH = 512
A_GROUPS = 4
B_WIDTH = 512
POOL_WINDOWS = (2, 4, 8, 16)
B_HD = 128
C_WIDTH = 512
MLA_HEADS = 4
Q_LORA = 256
KV_LORA = 256
QK_NOPE = 128
QK_ROPE = 64
QK_HD = QK_NOPE + QK_ROPE
V_HD = 128
ROPE_THETA = 10000.0
D_FF = 2816
N_EXPERTS = 8
D_FF_EXPERT = 3584

LANES = 128
POOL_HALO = 16
CONV_HALO = 8
QK_PAD = 256
V_PAD = 144
LOG2E = 1.4426950408889634
MASK_VALUE = -1e30
VMEM_LIMIT = 56 * 1024 * 1024

F32 = jnp.float32
BF16 = jnp.bfloat16


def _params(sem, vmem=VMEM_LIMIT):
    return pltpu.CompilerParams(dimension_semantics=sem, vmem_limit_bytes=vmem)


def _const_spec(shape):
    nd = len(shape)
    return pl.BlockSpec(shape, lambda *_: (0,) * nd)


def _rms_mod(x, nw, sc, sh):
    ms = jnp.mean(x * x, axis=-1, keepdims=True)
    return (x * lax.rsqrt(ms + EPS)) * nw * (1.0 + sc) + sh


def _ada_kernel(c_ref, w_ref, b_ref, o_ref):
    c = c_ref[...]
    ca = c * jax.nn.sigmoid(c)
    o_ref[0] = jnp.sum(w_ref[0] * ca, axis=0, keepdims=True) + b_ref[0]


def _ada_mod(c, ada_w, ada_b):
    depth, d, n = ada_w.shape
    tn = 768
    return pl.pallas_call(
        _ada_kernel,
        grid=(depth, n // tn),
        in_specs=[
            pl.BlockSpec((d, 1), lambda l, j: (0, 0)),
            pl.BlockSpec((1, d, tn), lambda l, j: (l, 0, j)),
            pl.BlockSpec((1, 1, tn), lambda l, j: (l, 0, j)),
        ],
        out_specs=pl.BlockSpec((1, 1, tn), lambda l, j: (l, 0, j)),
        out_shape=jax.ShapeDtypeStruct((depth, 1, n), F32),
        compiler_params=_params(("arbitrary", "arbitrary")),
        name="ada_mod",
    )(c.reshape(d, 1), ada_w, ada_b.reshape(depth, 1, n))


def _even_mix_kernel(x_ref, mod_ref, nw_ref, win_ref, lnw_ref, lnb_ref, ws_ref, bs_ref, wg_ref, bsc_ref,
                     wout_ref, o_ref, halo_ref, sv_ref, yb_ref):
    tm = x_ref.shape[0]
    i = pl.program_id(0)

    @pl.when(i == 0)
    def _():
        halo_ref[...] = jnp.zeros_like(halo_ref)

    d = D_MODEL
    x = x_ref[...]
    sh, sc, gate = mod_ref[:, 0:d], mod_ref[:, d:2 * d], mod_ref[:, 2 * d:3 * d]
    h = _rms_mod(x, nw_ref[...], sc, sh).astype(BF16)
    p = jnp.dot(h, win_ref[...], preferred_element_type=F32)

    gl = jax.nn.gelu(p[:, :2 * A_WIDTH])
    u = gl[:, :A_WIDTH]
    v = gl[:, A_WIDTH:]
    mu = jnp.mean(v, axis=-1, keepdims=True)
    vc = v - mu
    var = jnp.mean(vc * vc, axis=-1, keepdims=True)
    vn = (vc * lax.rsqrt(var + EPS) * lnw_ref[...] + lnb_ref[...]).astype(BF16)
    row = lax.broadcasted_iota(jnp.int32, (CHUNK, CHUNK), 0)
    col = lax.broadcasted_iota(jnp.int32, (CHUNK, CHUNK), 1)
    for g in range(A_GROUPS):
        w = jnp.where(col <= row, ws_ref[g], 0.0).astype(BF16)
        b = bs_ref[g]
        for c in range(tm // CHUNK):
            blk = vn[c * CHUNK:(c + 1) * CHUNK, g * LANES:(g + 1) * LANES]
            sv_ref[c * CHUNK:(c + 1) * CHUNK, g * LANES:(g + 1) * LANES] = (
                jnp.dot(w, blk, preferred_element_type=F32) + b)
    ya = (u * sv_ref[...]).astype(BF16)

    pb = p[:, 2 * A_WIDTH:]
    ext = jnp.concatenate([halo_ref[...], pb], axis=0)
    halo_ref[...] = pb[tm - POOL_HALO:, :]
    t_glob = i * tm + lax.broadcasted_iota(jnp.int32, (tm, 1), 0)
    s = ext
    width = 1
    for g, win in enumerate(POOL_WINDOWS):
        while width < win:
            s = s + pltpu.roll(s, width, 0)
            width *= 2
        cnt = jnp.minimum(t_glob + 1, win).astype(F32)
        sl = slice(g * B_HD, (g + 1) * B_HD)
        pooled = s[POOL_HALO:, sl] / cnt
        dg = (pooled - pb[:, sl]).astype(BF16)
        yb_ref[:, sl] = jnp.dot(dg, wg_ref[g], preferred_element_type=F32)
    yb = (yb_ref[...] * bsc_ref[...]).astype(BF16)

    mix = (jnp.dot(ya, wout_ref[:A_WIDTH, :], preferred_element_type=F32)
           + jnp.dot(yb, wout_ref[A_WIDTH:, :], preferred_element_type=F32))
    o_ref[...] = x + gate * mix


def _even_mix(x, mod, nw, w_in, ln_w, ln_b, w_s, b_s, w_grp, b_scale, w_out, tm=512):
    s, d = x.shape
    return pl.pallas_call(
        _even_mix_kernel,
        grid=(s // tm,),
        in_specs=[
            pl.BlockSpec((tm, d), lambda i: (i, 0)),
            _const_spec(mod.shape), _const_spec(nw.shape), _const_spec(w_in.shape),
            _const_spec(ln_w.shape), _const_spec(ln_b.shape), _const_spec(w_s.shape), _const_spec(b_s.shape),
            _const_spec(w_grp.shape), _const_spec(b_scale.shape), _const_spec(w_out.shape),
        ],
        out_specs=pl.BlockSpec((tm, d), lambda i: (i, 0)),
        out_shape=jax.ShapeDtypeStruct((s, d), F32),
        scratch_shapes=[pltpu.VMEM((POOL_HALO, B_WIDTH), F32), pltpu.VMEM((tm, A_WIDTH), F32),
                        pltpu.VMEM((tm, B_WIDTH), F32)],
        compiler_params=_params(("arbitrary",)),
        name="even_mix",
    )(x, mod, nw, w_in, ln_w, ln_b, w_s, b_s, w_grp, b_scale, w_out)


def _ffn_kernel(x_ref, mod_ref, nw_ref, wgu_ref, wd_ref, o_ref, *, n_chunks):
    d = D_MODEL
    x = x_ref[...]
    sh, sc, gate = mod_ref[:, 3 * d:4 * d], mod_ref[:, 4 * d:5 * d], mod_ref[:, 5 * d:6 * d]
    h = _rms_mod(x, nw_ref[...], sc, sh).astype(BF16)
    ff = wd_ref.shape[0]
    tf = ff // n_chunks
    acc = jnp.zeros(x.shape, F32)
    for f in range(n_chunks):
        g = jnp.dot(h, wgu_ref[:, f * tf:(f + 1) * tf], preferred_element_type=F32)
        u = jnp.dot(h, wgu_ref[:, ff + f * tf:ff + (f + 1) * tf], preferred_element_type=F32)
        act = (g * jax.nn.sigmoid(g) * u).astype(BF16)
        acc = acc + jnp.dot(act, wd_ref[f * tf:(f + 1) * tf, :], preferred_element_type=F32)
    o_ref[...] = x + gate * acc


def _ffn(x, mod, nw, w_gu, w_down, tm=512, n_chunks=2):
    s, d = x.shape
    return pl.pallas_call(
        functools.partial(_ffn_kernel, n_chunks=n_chunks),
        grid=(s // tm,),
        in_specs=[
            pl.BlockSpec((tm, d), lambda i: (i, 0)),
            _const_spec(mod.shape), _const_spec(nw.shape), _const_spec(w_gu.shape), _const_spec(w_down.shape),
        ],
        out_specs=pl.BlockSpec((tm, d), lambda i: (i, 0)),
        out_shape=jax.ShapeDtypeStruct((s, d), F32),
        compiler_params=_params(("arbitrary",)),
        name="ffn",
    )(x, mod, nw, w_gu, w_down)


def _rope_table_kernel(pos_ref, invf_ref, cos_ref, sin_ref):
    ang = pos_ref[...].astype(F32) * invf_ref[...]
    cos_ref[...] = jnp.cos(ang)
    sin_ref[...] = jnp.sin(ang)


def _rope_tables(positions):
    s = positions.shape[-1]
    half = QK_ROPE // 2
    inv_freq = ROPE_THETA ** (-jnp.arange(0, QK_ROPE, 2, dtype=F32) / QK_ROPE)
    cos_t, sin_t = pl.pallas_call(
        _rope_table_kernel,
        out_shape=(jax.ShapeDtypeStruct((half, s), F32), jax.ShapeDtypeStruct((half, s), F32)),
        name="rope_tables",
    )(positions.reshape(1, s), inv_freq.reshape(half, 1))
    return jnp.tile(cos_t.T, (1, 2)), jnp.tile(sin_t.T, (1, 2))


def _rope_rot(x):
    n = x.shape[-1]
    lane = lax.broadcasted_iota(jnp.int32, x.shape, 1)
    half = QK_ROPE // 2
    fwd = pltpu.roll(x, half, 1)
    bwd = pltpu.roll(x, n - half, 1)
    return jnp.where((lane % QK_ROPE) < half, -bwd, fwd)


def _odd_prep_kernel(x_ref, mod_ref, nw_ref, win_ref, wkpe_ref, cw_ref, qan_ref, wuq_ref, kvn_ref, wukv_ref,
                     qnw_ref, knw_ref, cos_ref, sin_ref, yc_ref, qt_ref, k_ref, vt_ref, halo_ref):
    tm = x_ref.shape[0]
    i = pl.program_id(0)

    @pl.when(i == 0)
    def _():
        halo_ref[...] = jnp.zeros_like(halo_ref)

    d = D_MODEL
    x = x_ref[...]
    sh, sc = mod_ref[:, 0:d], mod_ref[:, d:2 * d]
    h = _rms_mod(x, nw_ref[...], sc, sh).astype(BF16)
    p = jnp.dot(h, win_ref[...], preferred_element_type=F32)
    kpe = jnp.dot(h, wkpe_ref[...], preferred_element_type=F32)[:, :QK_ROPE]

    cw = C_WIDTH
    bg, cg, hh = p[:, 0:cw], p[:, cw:2 * cw], p[:, 2 * cw:3 * cw]
    z = cg * hh
    ext = jnp.concatenate([halo_ref[...], z], axis=0)
    halo_ref[...] = z[tm - CONV_HALO:, :]
    z1 = pltpu.roll(ext, 1, 0)[CONV_HALO:, :]
    z2 = pltpu.roll(ext, 2, 0)[CONV_HALO:, :]
    conv = cw_ref[0:1, :] * z2 + cw_ref[1:2, :] * z1 + cw_ref[2:3, :] * z
    yc_ref[...] = (bg * conv).astype(BF16)

    o0 = 3 * cw
    cq = p[:, o0:o0 + Q_LORA]
    ckv = p[:, o0 + Q_LORA:o0 + Q_LORA + KV_LORA]
    cqn = (cq * lax.rsqrt(jnp.mean(cq * cq, axis=-1, keepdims=True) + EPS) * qan_ref[...]).astype(BF16)
    ckvn = (ckv * lax.rsqrt(jnp.mean(ckv * ckv, axis=-1, keepdims=True) + EPS) * kvn_ref[...]).astype(BF16)
    q = jnp.dot(cqn, wuq_ref[...], preferred_element_type=F32)
    kv = jnp.dot(ckvn, wukv_ref[...], preferred_element_type=F32)

    nh = MLA_HEADS
    cos = jnp.concatenate([cos_ref[...]] * nh, axis=1)
    sin = jnp.concatenate([sin_ref[...]] * nh, axis=1)
    qn_w, qr_w = qnw_ref[:, :QK_NOPE], qnw_ref[:, QK_NOPE:]
    kn_w, kr_w = knw_ref[:, :QK_NOPE], knw_ref[:, QK_NOPE:]
    sm_scale = QK_HD ** -0.5

    qr_all = q[:, nh * QK_NOPE:]
    qr_w4 = jnp.concatenate([qr_w] * nh, axis=1)
    qr_all = qr_all * qr_w4
    qr_all = qr_all * cos + _rope_rot(qr_all) * sin
    kr = kpe * kr_w
    kr = kr * cos[:, :QK_ROPE] + _rope_rot(kr) * sin[:, :QK_ROPE]
    kpe_ss = jnp.sum(kpe * kpe, axis=-1, keepdims=True)
    zpad = jnp.zeros((tm, QK_PAD - QK_HD), F32)
    ones_rows = (lax.broadcasted_iota(jnp.int32, (V_PAD - V_HD, tm), 0) == 0).astype(BF16)
    for hd in range(nh):
        qn = q[:, hd * QK_NOPE:(hd + 1) * QK_NOPE]
        qr_raw = q[:, nh * QK_NOPE + hd * QK_ROPE:nh * QK_NOPE + (hd + 1) * QK_ROPE]
        q_ss = jnp.sum(qn * qn, axis=-1, keepdims=True) + jnp.sum(qr_raw * qr_raw, axis=-1, keepdims=True)
        q_inv = lax.rsqrt(q_ss / QK_HD + EPS) * (sm_scale * LOG2E)
        qh = jnp.concatenate([qn * q_inv * qn_w, qr_all[:, hd * QK_ROPE:(hd + 1) * QK_ROPE] * q_inv, zpad], axis=1)
        qt_ref[hd] = qh.T.astype(BF16)
        kn = kv[:, hd * (QK_NOPE + V_HD):hd * (QK_NOPE + V_HD) + QK_NOPE]
        k_ss = jnp.sum(kn * kn, axis=-1, keepdims=True) + kpe_ss
        k_inv = lax.rsqrt(k_ss / QK_HD + EPS)
        k_ref[hd] = jnp.concatenate([kn * k_inv * kn_w, kr * k_inv, zpad], axis=1).astype(BF16)
        vh = kv[:, hd * (QK_NOPE + V_HD) + QK_NOPE:(hd + 1) * (QK_NOPE + V_HD)]
        vt_ref[hd, :V_HD, :] = vh.T.astype(BF16)
        vt_ref[hd, V_HD:, :] = ones_rows


def _odd_prep(x, mod, nw, w_in, w_kpe, conv_w, q_a_norm, w_uq, kv_norm, w_ukv, q_norm_w, k_norm_w, cos, sin,
              tm=512):
    s, d = x.shape
    nh = MLA_HEADS
    consts = [mod, nw, w_in, w_kpe, conv_w, q_a_norm, w_uq, kv_norm, w_ukv, q_norm_w, k_norm_w]
    return pl.pallas_call(
        _odd_prep_kernel,
        grid=(s // tm,),
        in_specs=[pl.BlockSpec((tm, d), lambda i: (i, 0))] + [_const_spec(a.shape) for a in consts] + [
            pl.BlockSpec((tm, QK_ROPE), lambda i: (i, 0)), pl.BlockSpec((tm, QK_ROPE), lambda i: (i, 0))],
        out_specs=[
            pl.BlockSpec((tm, C_WIDTH), lambda i: (i, 0)),
            pl.BlockSpec((nh, QK_PAD, tm), lambda i: (0, 0, i)),
            pl.BlockSpec((nh, tm, QK_PAD), lambda i: (0, i, 0)),
            pl.BlockSpec((nh, V_PAD, tm), lambda i: (0, 0, i)),
        ],
        out_shape=[
            jax.ShapeDtypeStruct((s, C_WIDTH), BF16),
            jax.ShapeDtypeStruct((nh, QK_PAD, s), BF16),
            jax.ShapeDtypeStruct((nh, s, QK_PAD), BF16),
            jax.ShapeDtypeStruct((nh, V_PAD, s), BF16),
        ],
        scratch_shapes=[pltpu.VMEM((CONV_HALO, C_WIDTH), F32)],
        compiler_params=_params(("arbitrary",)),
        name="odd_prep",
    )(x, *consts, cos, sin)


def _attn_kernel(qt_ref, k_ref, vt_ref, o_ref, s0_ref, s1_ref, m_ref, acc_ref, *, tq, tk):
    qi = pl.program_id(1)
    qt = qt_ref[0]
    m_ref[...] = jnp.full_like(m_ref, MASK_VALUE)
    acc_ref[...] = jnp.zeros_like(acc_ref)

    def scores(j, s_ref):
        start = pl.multiple_of(j * tk, tk)
        s_ref[...] = jnp.dot(k_ref[0, pl.ds(start, tk), :], qt, preferred_element_type=F32)

    def consume(j, s_ref, diag_offset=None):
        s = s_ref[...]
        if diag_offset is not None:
            key = diag_offset + lax.broadcasted_iota(jnp.int32, (tk, tq), 0)
            qry = lax.broadcasted_iota(jnp.int32, (tk, tq), 1)
            s = jnp.where(key <= qry, s, MASK_VALUE)
        m_prev = m_ref[...]
        m_new = jnp.maximum(m_prev, jnp.max(s, axis=0, keepdims=True))
        alpha = jnp.exp2(m_prev - m_new)
        p = jnp.exp2((s - m_new).astype(BF16))
        start = pl.multiple_of(j * tk, tk)
        vt = vt_ref[0, :, pl.ds(start, tk)]
        acc_ref[...] = alpha * acc_ref[...] + jnp.dot(vt, p, preferred_element_type=F32)
        m_ref[...] = m_new

    scores(0, s0_ref)

    def body(t, carry):
        j = 2 * t
        scores(j + 1, s1_ref)
        consume(j, s0_ref)
        scores(j + 2, s0_ref)
        consume(j + 1, s1_ref)
        return carry

    lax.fori_loop(0, qi, body, 0)
    j = 2 * qi
    scores(j + 1, s1_ref)
    consume(j, s0_ref, diag_offset=0)
    consume(j + 1, s1_ref, diag_offset=tk)
    acc = acc_ref[...]
    out_t = acc[:V_HD, :] / acc[V_HD:V_HD + 1, :]
    o_ref[...] = out_t.T.astype(o_ref.dtype)


def _attention(qt, k, vt, tq=1024, tk=512):
    nh, _, s = qt.shape
    assert tq == 2 * tk
    return pl.pallas_call(
        functools.partial(_attn_kernel, tq=tq, tk=tk),
        grid=(nh, s // tq),
        in_specs=[
            pl.BlockSpec((1, QK_PAD, tq), lambda h, i: (h, 0, i)),
            pl.BlockSpec((1, s, QK_PAD), lambda h, i: (h, 0, 0)),
            pl.BlockSpec((1, V_PAD, s), lambda h, i: (h, 0, 0)),
        ],
        out_specs=pl.BlockSpec((tq, V_HD), lambda h, i: (i, h)),
        out_shape=jax.ShapeDtypeStruct((s, nh * V_HD), BF16),
        scratch_shapes=[pltpu.VMEM((tk, tq), F32), pltpu.VMEM((tk, tq), F32), pltpu.VMEM((1, tq), F32),
                        pltpu.VMEM((V_PAD, tq), F32)],
        compiler_params=_params(("arbitrary", "arbitrary")),
        name="mla_attention",
    )(qt, k, vt)


def _odd_out_kernel(x_ref, yc_ref, yd_ref, mod_ref, nw_ref, wout_ref, wr_ref, x_out_ref, h_ref, gates_ref):
    d = D_MODEL
    x = x_ref[...]
    gate_m = mod_ref[:, 2 * d:3 * d]
    sh, sc = mod_ref[:, 3 * d:4 * d], mod_ref[:, 4 * d:5 * d]
    mix = (jnp.dot(yc_ref[...], wout_ref[:C_WIDTH, :], preferred_element_type=F32)
           + jnp.dot(yd_ref[...], wout_ref[C_WIDTH:, :], preferred_element_type=F32))
    x1 = x + gate_m * mix
    x_out_ref[...] = x1
    h = _rms_mod(x1, nw_ref[...], sc, sh)
    h_ref[...] = h.astype(BF16)

    logits = jnp.dot(h, wr_ref[...], preferred_element_type=F32, precision=lax.Precision.HIGHEST)
    lane = lax.broadcasted_iota(jnp.int32, logits.shape, 1)
    logits = jnp.where(lane < N_EXPERTS, logits, -jnp.inf)
    m1 = jnp.max(logits, axis=-1, keepdims=True)
    i1 = jnp.min(jnp.where(logits == m1, lane, LANES), axis=-1, keepdims=True)
    rest = jnp.where(lane == i1, -jnp.inf, logits)
    m2 = jnp.max(rest, axis=-1, keepdims=True)
    i2 = jnp.min(jnp.where(rest == m2, lane, LANES), axis=-1, keepdims=True)
    e2 = jnp.exp(m2 - m1)
    w1 = 1.0 / (1.0 + e2)
    w2 = e2 / (1.0 + e2)
    gates_ref[...] = jnp.where(lane == i1, w1, 0.0) + jnp.where(lane == i2, w2, 0.0)


def _odd_out(x, yc, yd, mod, nw, w_out, w_router, tm=512):
    s, d = x.shape
    return pl.pallas_call(
        _odd_out_kernel,
        grid=(s // tm,),
        in_specs=[
            pl.BlockSpec((tm, d), lambda i: (i, 0)),
            pl.BlockSpec((tm, C_WIDTH), lambda i: (i, 0)),
            pl.BlockSpec((tm, MLA_HEADS * V_HD), lambda i: (i, 0)),
            _const_spec(mod.shape), _const_spec(nw.shape), _const_spec(w_out.shape), _const_spec(w_router.shape),
        ],
        out_specs=[
            pl.BlockSpec((tm, d), lambda i: (i, 0)),
            pl.BlockSpec((tm, d), lambda i: (i, 0)),
            pl.BlockSpec((tm, LANES), lambda i: (i, 0)),
        ],
        out_shape=[
            jax.ShapeDtypeStruct((s, d), F32),
            jax.ShapeDtypeStruct((s, d), BF16),
            jax.ShapeDtypeStruct((s, LANES), F32),
        ],
        compiler_params=_params(("arbitrary",)),
        name="odd_out_router",
    )(x, yc, yd, mod, nw, w_out, w_router)


def _moe_kernel(x_ref, h_ref, gates_ref, mod_ref, wg_ref, wu_ref, wd_ref, o_ref, acc_ref):
    e = pl.program_id(1)
    f = pl.program_id(2)

    @pl.when((e == 0) & (f == 0))
    def _():
        acc_ref[...] = jnp.zeros_like(acc_ref)

    h = h_ref[...]
    g = jnp.dot(h, wg_ref[0], preferred_element_type=F32)
    u = jnp.dot(h, wu_ref[0], preferred_element_type=F32)
    act = (g * jax.nn.sigmoid(g) * u).astype(BF16)
    y = jnp.dot(act, wd_ref[0], preferred_element_type=F32)
    lane = lax.broadcasted_iota(jnp.int32, gates_ref.shape, 1)
    gate = jnp.sum(jnp.where(lane == e, gates_ref[...], 0.0), axis=-1, keepdims=True)
    acc_ref[...] += gate * y

    @pl.when((e == pl.num_programs(1) - 1) & (f == pl.num_programs(2) - 1))
    def _():
        d = D_MODEL
        o_ref[...] = x_ref[...] + mod_ref[:, 5 * d:6 * d] * acc_ref[...]


def _moe_dense(x, h, gates, mod, w_gu, w_down, tm=1024, tf=896):
    s, d = x.shape
    ne, _, ff2 = w_gu.shape
    ff = ff2 // 2
    nf = ff // tf
    return pl.pallas_call(
        _moe_kernel,
        grid=(s // tm, ne, nf),
        in_specs=[
            pl.BlockSpec((tm, d), lambda i, e, f: (i, 0)),
            pl.BlockSpec((tm, d), lambda i, e, f: (i, 0)),
            pl.BlockSpec((tm, LANES), lambda i, e, f: (i, 0)),
            pl.BlockSpec(mod.shape, lambda i, e, f: (0, 0)),
            pl.BlockSpec((1, d, tf), lambda i, e, f: (e, 0, f)),
            pl.BlockSpec((1, d, tf), lambda i, e, f: (e, 0, nf + f)),
            pl.BlockSpec((1, tf, d), lambda i, e, f: (e, f, 0)),
        ],
        out_specs=pl.BlockSpec((tm, d), lambda i, e, f: (i, 0)),
        out_shape=jax.ShapeDtypeStruct((s, d), F32),
        scratch_shapes=[pltpu.VMEM((tm, d), F32)],
        compiler_params=_params(("arbitrary", "arbitrary", "arbitrary")),
        name="moe_experts",
    )(x, h, gates, mod, w_gu, w_gu, w_down)


def kernel(x, c, positions, norm_mix_w, norm_ffn_w, ada_w, ada_b, e_w_in, a_ln_w, a_ln_b, a_w_s, a_b_s, b_w_grp,
           b_scale, e_w_out, ffn_w_gu, ffn_w_down, o_w_in, c_conv_w, q_a_norm, w_uq, kv_norm, w_ukv, q_norm_w,
           k_norm_w, o_w_out, router_w, moe_w_gu, moe_w_down):
    bsz, s, d = x.shape
    assert bsz == 1 and d == D_MODEL
    depth = ada_w.shape[0]
    nh = MLA_HEADS
    xs = x.reshape(s, d)
    mod = _ada_mod(c, ada_w, ada_b)
    cos, sin = _rope_tables(positions)
    row = lambda a: a.reshape(1, -1)

    for layer in range(depth):
        i = layer // 2
        m = mod[layer]
        if layer % 2 == 0:
            xs = _even_mix(
                xs, m, row(norm_mix_w[layer]), e_w_in[i].astype(BF16), row(a_ln_w[i]), row(a_ln_b[i]), a_w_s[i],
                a_b_s[i].reshape(A_GROUPS, CHUNK, 1), b_w_grp[i].astype(BF16), row(b_scale[i]),
                e_w_out[i].astype(BF16))
            xs = _ffn(xs, m, row(norm_ffn_w[layer]), ffn_w_gu[i].astype(BF16), ffn_w_down[i].astype(BF16))
        else:
            o2 = 3 * C_WIDTH + Q_LORA + KV_LORA
            w_in = o_w_in[i][:, :o2].astype(BF16)
            w_kpe = jnp.pad(o_w_in[i][:, o2:], ((0, 0), (0, LANES - QK_ROPE))).astype(BF16)
            wq = w_uq[i].reshape(Q_LORA, nh, QK_HD)
            wq = jnp.concatenate([wq[:, :, :QK_NOPE].reshape(Q_LORA, nh * QK_NOPE),
                                  wq[:, :, QK_NOPE:].reshape(Q_LORA, nh * QK_ROPE)], axis=1).astype(BF16)
            yc, qt, k, vt = _odd_prep(
                xs, m, row(norm_mix_w[layer]), w_in, w_kpe, c_conv_w[i], row(q_a_norm[i]), wq, row(kv_norm[i]),
                w_ukv[i].astype(BF16), row(q_norm_w[i]), row(k_norm_w[i]), cos, sin)
            yd = _attention(qt, k, vt)
            w_router = jnp.pad(router_w[i], ((0, 0), (0, LANES - N_EXPERTS)))
            xs, hb, gates = _odd_out(xs, yc, yd, m, row(norm_ffn_w[layer]), o_w_out[i].astype(BF16), w_router)
            xs = _moe_dense(xs, hb, gates, m, moe_w_gu[i].astype(BF16), moe_w_down[i].astype(BF16))
    return xs.reshape(bsz, s, d)
```

```python
import functools

import jax
import jax.numpy as jnp
from jax import lax
from jax.experimental import pallas as pl
from jax.experimental.pallas import tpu as pltpu
from jax.experimental.pallas import tpu_sc as plsc

D_MODEL = 1024
SEQ = 16384
EPS = 1e-6
CHUNK = 128
A_WIDTH = 512
A_GROUPS = 4
B_WIDTH = 512
POOL_WINDOWS = (2, 4, 8, 16)
B_HD = 128
C_WIDTH = 512
MLA_HEADS = 4
Q_LORA = 256
KV_LORA = 256
QK_NOPE = 128
QK_ROPE = 64
QK_HD = QK_NOPE + QK_ROPE
V_HD = 128
ROPE_THETA = 10000.0
D_FF = 2816
N_EXPERTS = 8
D_FF_EXPERT = 3584

LANES = 128
POOL_HALO = 16
CONV_HALO = 8
QK_PAD = 256
V_PAD = 144
LOG2E = 1.4426950408889634
MASK_VALUE = -1e30
MOE_TILE = 512
SC_ROWS = 64
VMEM_LIMIT = 56 * 1024 * 1024

F32 = jnp.float32
BF16 = jnp.bfloat16


def _params(sem, vmem=VMEM_LIMIT):
    return pltpu.CompilerParams(dimension_semantics=sem, vmem_limit_bytes=vmem)


def _const_spec(shape):
    nd = len(shape)
    return pl.BlockSpec(shape, lambda *_: (0,) * nd)


def _rms_mod(x, nw, sc, sh):
    ms = jnp.mean(x * x, axis=-1, keepdims=True)
    return (x * lax.rsqrt(ms + EPS)) * nw * (1.0 + sc) + sh


def _ada_kernel(c_ref, w_ref, b_ref, o_ref):
    c = c_ref[...]
    ca = c * jax.nn.sigmoid(c)
    o_ref[0] = jnp.sum(w_ref[0] * ca, axis=0, keepdims=True) + b_ref[0]


def _ada_mod(c, ada_w, ada_b):
    depth, d, n = ada_w.shape
    tn = 768
    return pl.pallas_call(
        _ada_kernel,
        grid=(depth, n // tn),
        in_specs=[
            pl.BlockSpec((d, 1), lambda l, j: (0, 0)),
            pl.BlockSpec((1, d, tn), lambda l, j: (l, 0, j)),
            pl.BlockSpec((1, 1, tn), lambda l, j: (l, 0, j)),
        ],
        out_specs=pl.BlockSpec((1, 1, tn), lambda l, j: (l, 0, j)),
        out_shape=jax.ShapeDtypeStruct((depth, 1, n), F32),
        compiler_params=_params(("arbitrary", "arbitrary")),
        name="ada_mod",
    )(c.reshape(d, 1), ada_w, ada_b.reshape(depth, 1, n))


def _even_mix_kernel(x_ref, mod_ref, nw_ref, win_ref, lnw_ref, lnb_ref, ws_ref, bs_ref, wg_ref, bsc_ref,
                     wout_ref, o_ref, halo_ref, sv_ref, yb_ref):
    tm = x_ref.shape[0]
    i = pl.program_id(0)

    @pl.when(i == 0)
    def _():
        halo_ref[...] = jnp.zeros_like(halo_ref)

    d = D_MODEL
    x = x_ref[...]
    sh, sc, gate = mod_ref[:, 0:d], mod_ref[:, d:2 * d], mod_ref[:, 2 * d:3 * d]
    h = _rms_mod(x, nw_ref[...], sc, sh).astype(BF16)
    p = jnp.dot(h, win_ref[...], preferred_element_type=F32)

    gl = jax.nn.gelu(p[:, :2 * A_WIDTH])
    u = gl[:, :A_WIDTH]
    v = gl[:, A_WIDTH:]
    mu = jnp.mean(v, axis=-1, keepdims=True)
    vc = v - mu
    var = jnp.mean(vc * vc, axis=-1, keepdims=True)
    vn = (vc * lax.rsqrt(var + EPS) * lnw_ref[...] + lnb_ref[...]).astype(BF16)
    row = lax.broadcasted_iota(jnp.int32, (CHUNK, CHUNK), 0)
    col = lax.broadcasted_iota(jnp.int32, (CHUNK, CHUNK), 1)
    for g in range(A_GROUPS):
        w = jnp.where(col <= row, ws_ref[g], 0.0).astype(BF16)
        b = bs_ref[g]
        for c in range(tm // CHUNK):
            blk = vn[c * CHUNK:(c + 1) * CHUNK, g * LANES:(g + 1) * LANES]
            sv_ref[c * CHUNK:(c + 1) * CHUNK, g * LANES:(g + 1) * LANES] = (
                jnp.dot(w, blk, preferred_element_type=F32) + b)
    ya = (u * sv_ref[...]).astype(BF16)

    pb = p[:, 2 * A_WIDTH:]
    ext = jnp.concatenate([halo_ref[...], pb], axis=0)
    halo_ref[...] = pb[tm - POOL_HALO:, :]
    t_glob = i * tm + lax.broadcasted_iota(jnp.int32, (tm, 1), 0)
    s = ext
    width = 1
    for g, win in enumerate(POOL_WINDOWS):
        while width < win:
            s = s + pltpu.roll(s, width, 0)
            width *= 2
        cnt = jnp.minimum(t_glob + 1, win).astype(F32)
        sl = slice(g * B_HD, (g + 1) * B_HD)
        pooled = s[POOL_HALO:, sl] / cnt
        dg = (pooled - pb[:, sl]).astype(BF16)
        yb_ref[:, sl] = jnp.dot(dg, wg_ref[g], preferred_element_type=F32)
    yb = (yb_ref[...] * bsc_ref[...]).astype(BF16)

    mix = (jnp.dot(ya, wout_ref[:A_WIDTH, :], preferred_element_type=F32)
           + jnp.dot(yb, wout_ref[A_WIDTH:, :], preferred_element_type=F32))
    o_ref[...] = x + gate * mix


def _even_mix(x, mod, nw, w_in, ln_w, ln_b, w_s, b_s, w_grp, b_scale, w_out, tm=512):
    s, d = x.shape
    return pl.pallas_call(
        _even_mix_kernel,
        grid=(s // tm,),
        in_specs=[
            pl.BlockSpec((tm, d), lambda i: (i, 0)),
            _const_spec(mod.shape), _const_spec(nw.shape), _const_spec(w_in.shape),
            _const_spec(ln_w.shape), _const_spec(ln_b.shape), _const_spec(w_s.shape), _const_spec(b_s.shape),
            _const_spec(w_grp.shape), _const_spec(b_scale.shape), _const_spec(w_out.shape),
        ],
        out_specs=pl.BlockSpec((tm, d), lambda i: (i, 0)),
        out_shape=jax.ShapeDtypeStruct((s, d), F32),
        scratch_shapes=[pltpu.VMEM((POOL_HALO, B_WIDTH), F32), pltpu.VMEM((tm, A_WIDTH), F32),
                        pltpu.VMEM((tm, B_WIDTH), F32)],
        compiler_params=_params(("arbitrary",)),
        name="even_mix",
    )(x, mod, nw, w_in, ln_w, ln_b, w_s, b_s, w_grp, b_scale, w_out)


def _ffn_kernel(x_ref, mod_ref, nw_ref, wgu_ref, wd_ref, o_ref, *, n_chunks):
    d = D_MODEL
    x = x_ref[...]
    sh, sc, gate = mod_ref[:, 3 * d:4 * d], mod_ref[:, 4 * d:5 * d], mod_ref[:, 5 * d:6 * d]
    h = _rms_mod(x, nw_ref[...], sc, sh).astype(BF16)
    ff = wd_ref.shape[0]
    tf = ff // n_chunks
    acc = jnp.zeros(x.shape, F32)
    for f in range(n_chunks):
        g = jnp.dot(h, wgu_ref[:, f * tf:(f + 1) * tf], preferred_element_type=F32)
        u = jnp.dot(h, wgu_ref[:, ff + f * tf:ff + (f + 1) * tf], preferred_element_type=F32)
        act = (g * jax.nn.sigmoid(g) * u).astype(BF16)
        acc = acc + jnp.dot(act, wd_ref[f * tf:(f + 1) * tf, :], preferred_element_type=F32)
    o_ref[...] = x + gate * acc


def _ffn(x, mod, nw, w_gu, w_down, tm=512, n_chunks=2):
    s, d = x.shape
    return pl.pallas_call(
        functools.partial(_ffn_kernel, n_chunks=n_chunks),
        grid=(s // tm,),
        in_specs=[
            pl.BlockSpec((tm, d), lambda i: (i, 0)),
            _const_spec(mod.shape), _const_spec(nw.shape), _const_spec(w_gu.shape), _const_spec(w_down.shape),
        ],
        out_specs=pl.BlockSpec((tm, d), lambda i: (i, 0)),
        out_shape=jax.ShapeDtypeStruct((s, d), F32),
        compiler_params=_params(("arbitrary",)),
        name="ffn",
    )(x, mod, nw, w_gu, w_down)


def _rope_table_kernel(pos_ref, invf_ref, cos_ref, sin_ref):
    ang = pos_ref[...].astype(F32) * invf_ref[...]
    cos_ref[...] = jnp.cos(ang)
    sin_ref[...] = jnp.sin(ang)


def _rope_tables(positions):
    s = positions.shape[-1]
    half = QK_ROPE // 2
    inv_freq = ROPE_THETA ** (-jnp.arange(0, QK_ROPE, 2, dtype=F32) / QK_ROPE)
    cos_t, sin_t = pl.pallas_call(
        _rope_table_kernel,
        out_shape=(jax.ShapeDtypeStruct((half, s), F32), jax.ShapeDtypeStruct((half, s), F32)),
        name="rope_tables",
    )(positions.reshape(1, s), inv_freq.reshape(half, 1))
    return jnp.tile(cos_t.T, (1, 2)), jnp.tile(sin_t.T, (1, 2))


def _rope_rot(x):
    n = x.shape[-1]
    lane = lax.broadcasted_iota(jnp.int32, x.shape, 1)
    half = QK_ROPE // 2
    fwd = pltpu.roll(x, half, 1)
    bwd = pltpu.roll(x, n - half, 1)
    return jnp.where((lane % QK_ROPE) < half, -bwd, fwd)


def _odd_prep_kernel(x_ref, mod_ref, nw_ref, win_ref, wkpe_ref, cw_ref, qan_ref, wuq_ref, kvn_ref, wukv_ref,
                     qnw_ref, knw_ref, cos_ref, sin_ref, yc_ref, qt_ref, k_ref, vt_ref, halo_ref):
    tm = x_ref.shape[0]
    i = pl.program_id(0)

    @pl.when(i == 0)
    def _():
        halo_ref[...] = jnp.zeros_like(halo_ref)

    d = D_MODEL
    x = x_ref[...]
    sh, sc = mod_ref[:, 0:d], mod_ref[:, d:2 * d]
    h = _rms_mod(x, nw_ref[...], sc, sh).astype(BF16)
    p = jnp.dot(h, win_ref[...], preferred_element_type=F32)
    kpe = jnp.dot(h, wkpe_ref[...], preferred_element_type=F32)[:, :QK_ROPE]

    cw = C_WIDTH
    bg, cg, hh = p[:, 0:cw], p[:, cw:2 * cw], p[:, 2 * cw:3 * cw]
    z = cg * hh
    ext = jnp.concatenate([halo_ref[...], z], axis=0)
    halo_ref[...] = z[tm - CONV_HALO:, :]
    z1 = pltpu.roll(ext, 1, 0)[CONV_HALO:, :]
    z2 = pltpu.roll(ext, 2, 0)[CONV_HALO:, :]
    conv = cw_ref[0:1, :] * z2 + cw_ref[1:2, :] * z1 + cw_ref[2:3, :] * z
    yc_ref[...] = (bg * conv).astype(BF16)

    o0 = 3 * cw
    cq = p[:, o0:o0 + Q_LORA]
    ckv = p[:, o0 + Q_LORA:o0 + Q_LORA + KV_LORA]
    cqn = (cq * lax.rsqrt(jnp.mean(cq * cq, axis=-1, keepdims=True) + EPS) * qan_ref[...]).astype(BF16)
    ckvn = (ckv * lax.rsqrt(jnp.mean(ckv * ckv, axis=-1, keepdims=True) + EPS) * kvn_ref[...]).astype(BF16)
    q = jnp.dot(cqn, wuq_ref[...], preferred_element_type=F32)
    kv = jnp.dot(ckvn, wukv_ref[...], preferred_element_type=F32)

    nh = MLA_HEADS
    cos = jnp.concatenate([cos_ref[...]] * nh, axis=1)
    sin = jnp.concatenate([sin_ref[...]] * nh, axis=1)
    qn_w, qr_w = qnw_ref[:, :QK_NOPE], qnw_ref[:, QK_NOPE:]
    kn_w, kr_w = knw_ref[:, :QK_NOPE], knw_ref[:, QK_NOPE:]
    sm_scale = QK_HD ** -0.5

    qr_all = q[:, nh * QK_NOPE:]
    qr_w4 = jnp.concatenate([qr_w] * nh, axis=1)
    qr_all = qr_all * qr_w4
    qr_all = qr_all * cos + _rope_rot(qr_all) * sin
    kr = kpe * kr_w
    kr = kr * cos[:, :QK_ROPE] + _rope_rot(kr) * sin[:, :QK_ROPE]
    kpe_ss = jnp.sum(kpe * kpe, axis=-1, keepdims=True)
    zpad = jnp.zeros((tm, QK_PAD - QK_HD), F32)
    ones_rows = (lax.broadcasted_iota(jnp.int32, (V_PAD - V_HD, tm), 0) == 0).astype(BF16)
    for hd in range(nh):
        qn = q[:, hd * QK_NOPE:(hd + 1) * QK_NOPE]
        qr_raw = q[:, nh * QK_NOPE + hd * QK_ROPE:nh * QK_NOPE + (hd + 1) * QK_ROPE]
        q_ss = jnp.sum(qn * qn, axis=-1, keepdims=True) + jnp.sum(qr_raw * qr_raw, axis=-1, keepdims=True)
        q_inv = lax.rsqrt(q_ss / QK_HD + EPS) * (sm_scale * LOG2E)
        qh = jnp.concatenate([qn * q_inv * qn_w, qr_all[:, hd * QK_ROPE:(hd + 1) * QK_ROPE] * q_inv, zpad], axis=1)
        qt_ref[hd] = qh.T.astype(BF16)
        kn = kv[:, hd * (QK_NOPE + V_HD):hd * (QK_NOPE + V_HD) + QK_NOPE]
        k_ss = jnp.sum(kn * kn, axis=-1, keepdims=True) + kpe_ss
        k_inv = lax.rsqrt(k_ss / QK_HD + EPS)
        k_ref[hd] = jnp.concatenate([kn * k_inv * kn_w, kr * k_inv, zpad], axis=1).astype(BF16)
        vh = kv[:, hd * (QK_NOPE + V_HD) + QK_NOPE:(hd + 1) * (QK_NOPE + V_HD)]
        vt_ref[hd, :V_HD, :] = vh.T.astype(BF16)
        vt_ref[hd, V_HD:, :] = ones_rows


def _odd_prep(x, mod, nw, w_in, w_kpe, conv_w, q_a_norm, w_uq, kv_norm, w_ukv, q_norm_w, k_norm_w, cos, sin,
              tm=512):
    s, d = x.shape
    nh = MLA_HEADS
    consts = [mod, nw, w_in, w_kpe, conv_w, q_a_norm, w_uq, kv_norm, w_ukv, q_norm_w, k_norm_w]
    return pl.pallas_call(
        _odd_prep_kernel,
        grid=(s // tm,),
        in_specs=[pl.BlockSpec((tm, d), lambda i: (i, 0))] + [_const_spec(a.shape) for a in consts] + [
            pl.BlockSpec((tm, QK_ROPE), lambda i: (i, 0)), pl.BlockSpec((tm, QK_ROPE), lambda i: (i, 0))],
        out_specs=[
            pl.BlockSpec((tm, C_WIDTH), lambda i: (i, 0)),
            pl.BlockSpec((nh, QK_PAD, tm), lambda i: (0, 0, i)),
            pl.BlockSpec((nh, tm, QK_PAD), lambda i: (0, i, 0)),
            pl.BlockSpec((nh, V_PAD, tm), lambda i: (0, 0, i)),
        ],
        out_shape=[
            jax.ShapeDtypeStruct((s, C_WIDTH), BF16),
            jax.ShapeDtypeStruct((nh, QK_PAD, s), BF16),
            jax.ShapeDtypeStruct((nh, s, QK_PAD), BF16),
            jax.ShapeDtypeStruct((nh, V_PAD, s), BF16),
        ],
        scratch_shapes=[pltpu.VMEM((CONV_HALO, C_WIDTH), F32)],
        compiler_params=_params(("arbitrary",)),
        name="odd_prep",
    )(x, *consts, cos, sin)


def _attn_kernel(qt_ref, k_ref, vt_ref, o_ref, s0_ref, s1_ref, m_ref, acc_ref, *, tq, tk):
    qi = pl.program_id(1)
    qt = qt_ref[0]
    m_ref[...] = jnp.full_like(m_ref, MASK_VALUE)
    acc_ref[...] = jnp.zeros_like(acc_ref)

    def scores(j, s_ref):
        start = pl.multiple_of(j * tk, tk)
        s_ref[...] = jnp.dot(k_ref[0, pl.ds(start, tk), :], qt, preferred_element_type=F32)

    def consume(j, s_ref, diag_offset=None):
        s = s_ref[...]
        if diag_offset is not None:
            key = diag_offset + lax.broadcasted_iota(jnp.int32, (tk, tq), 0)
            qry = lax.broadcasted_iota(jnp.int32, (tk, tq), 1)
            s = jnp.where(key <= qry, s, MASK_VALUE)
        m_prev = m_ref[...]
        m_new = jnp.maximum(m_prev, jnp.max(s, axis=0, keepdims=True))
        alpha = jnp.exp2(m_prev - m_new)
        p = jnp.exp2((s - m_new).astype(BF16))
        start = pl.multiple_of(j * tk, tk)
        vt = vt_ref[0, :, pl.ds(start, tk)]
        acc_ref[...] = alpha * acc_ref[...] + jnp.dot(vt, p, preferred_element_type=F32)
        m_ref[...] = m_new

    scores(0, s0_ref)

    def body(t, carry):
        j = 2 * t
        scores(j + 1, s1_ref)
        consume(j, s0_ref)
        scores(j + 2, s0_ref)
        consume(j + 1, s1_ref)
        return carry

    lax.fori_loop(0, qi, body, 0)
    j = 2 * qi
    scores(j + 1, s1_ref)
    consume(j, s0_ref, diag_offset=0)
    consume(j + 1, s1_ref, diag_offset=tk)
    acc = acc_ref[...]
    out_t = acc[:V_HD, :] / acc[V_HD:V_HD + 1, :]
    o_ref[...] = out_t.T.astype(o_ref.dtype)


def _attention(qt, k, vt, tq=1024, tk=512):
    nh, _, s = qt.shape
    assert tq == 2 * tk
    return pl.pallas_call(
        functools.partial(_attn_kernel, tq=tq, tk=tk),
        grid=(nh, s // tq),
        in_specs=[
            pl.BlockSpec((1, QK_PAD, tq), lambda h, i: (h, 0, i)),
            pl.BlockSpec((1, s, QK_PAD), lambda h, i: (h, 0, 0)),
            pl.BlockSpec((1, V_PAD, s), lambda h, i: (h, 0, 0)),
        ],
        out_specs=pl.BlockSpec((tq, V_HD), lambda h, i: (i, h)),
        out_shape=jax.ShapeDtypeStruct((s, nh * V_HD), BF16),
        scratch_shapes=[pltpu.VMEM((tk, tq), F32), pltpu.VMEM((tk, tq), F32), pltpu.VMEM((1, tq), F32),
                        pltpu.VMEM((V_PAD, tq), F32)],
        compiler_params=_params(("arbitrary", "arbitrary")),
        name="mla_attention",
    )(qt, k, vt)


def _pack_bf16_pairs(x):
    n = x.shape[1] // 2
    lo = pltpu.bitcast(x[:, :n].astype(BF16).astype(F32), jnp.uint32)
    hi = pltpu.bitcast(x[:, n:].astype(BF16).astype(F32), jnp.uint32)
    return (lo >> 16) | hi


def _unpack_bf16_pairs(p):
    lo = pltpu.bitcast(p << 16, F32)
    hi = pltpu.bitcast(p & jnp.uint32(0xFFFF0000), F32)
    return jnp.concatenate([lo, hi], axis=1)


def _odd_out_kernel(x_ref, yc_ref, yd_ref, mod_ref, nw_ref, wout_ref, wr_ref, x_out_ref, h_ref, rw_ref, ridx_ref,
                    cnt_ref):
    d = D_MODEL
    x = x_ref[...]
    gate_m = mod_ref[:, 2 * d:3 * d]
    sh, sc = mod_ref[:, 3 * d:4 * d], mod_ref[:, 4 * d:5 * d]
    mix = (jnp.dot(yc_ref[...], wout_ref[:C_WIDTH, :], preferred_element_type=F32)
           + jnp.dot(yd_ref[...], wout_ref[C_WIDTH:, :], preferred_element_type=F32))
    x1 = x + gate_m * mix
    x_out_ref[...] = x1
    h = _rms_mod(x1, nw_ref[...], sc, sh)
    h_ref[...] = _pack_bf16_pairs(h)

    logits = jnp.dot(h, wr_ref[...], preferred_element_type=F32, precision=lax.Precision.HIGHEST)
    lane = lax.broadcasted_iota(jnp.int32, logits.shape, 1)
    logits = jnp.where(lane < N_EXPERTS, logits, -jnp.inf)
    m1 = jnp.max(logits, axis=-1, keepdims=True)
    i1 = jnp.min(jnp.where(logits == m1, lane, LANES), axis=-1, keepdims=True)
    rest = jnp.where(lane == i1, -jnp.inf, logits)
    m2 = jnp.max(rest, axis=-1, keepdims=True)
    i2 = jnp.min(jnp.where(rest == m2, lane, LANES), axis=-1, keepdims=True)
    e2 = jnp.exp(m2 - m1)
    w1 = 1.0 / (1.0 + e2)
    w2 = e2 / (1.0 + e2)
    rw_ref[...] = jnp.where(lane == 0, w1, jnp.where(lane == 1, w2, 0.0))

    i = pl.program_id(0)

    @pl.when(i == 0)
    def _():
        cnt_ref[...] = jnp.zeros_like(cnt_ref)

    tm = x.shape[0]
    onehot = jnp.where((lane == i1) | (lane == i2), 1.0, 0.0)
    r = lax.broadcasted_iota(jnp.int32, (tm, tm), 0)
    cidx = lax.broadcasted_iota(jnp.int32, (tm, tm), 1)
    before = jnp.where(cidx < r, 1.0, 0.0).astype(BF16)
    prefix = jnp.dot(before, onehot.astype(BF16), preferred_element_type=F32) + cnt_ref[...]
    r1 = jnp.sum(jnp.where(lane == i1, prefix, 0.0), axis=-1, keepdims=True).astype(jnp.int32)
    r2 = jnp.sum(jnp.where(lane == i2, prefix, 0.0), axis=-1, keepdims=True).astype(jnp.int32)
    cnt_ref[...] += jnp.sum(onehot, axis=0, keepdims=True)
    ridx_ref[...] = jnp.where(lane == 0, i1, jnp.where(lane == 1, i2, jnp.where(lane == 2, r1,
                                                                                 jnp.where(lane == 3, r2, 0))))


def _odd_out(x, yc, yd, mod, nw, w_out, w_router, tm=512):
    s, d = x.shape
    return pl.pallas_call(
        _odd_out_kernel,
        grid=(s // tm,),
        in_specs=[
            pl.BlockSpec((tm, d), lambda i: (i, 0)),
            pl.BlockSpec((tm, C_WIDTH), lambda i: (i, 0)),
            pl.BlockSpec((tm, MLA_HEADS * V_HD), lambda i: (i, 0)),
            _const_spec(mod.shape), _const_spec(nw.shape), _const_spec(w_out.shape), _const_spec(w_router.shape),
        ],
        out_specs=[
            pl.BlockSpec((tm, d), lambda i: (i, 0)),
            pl.BlockSpec((tm, d // 2), lambda i: (i, 0)),
            pl.BlockSpec((tm, LANES), lambda i: (i, 0)),
            pl.BlockSpec((tm, LANES), lambda i: (i, 0)),
            pl.BlockSpec((1, LANES), lambda i: (0, 0)),
        ],
        out_shape=[
            jax.ShapeDtypeStruct((s, d), F32),
            jax.ShapeDtypeStruct((s, d // 2), jnp.uint32),
            jax.ShapeDtypeStruct((s, LANES), F32),
            jax.ShapeDtypeStruct((s, LANES), jnp.int32),
            jax.ShapeDtypeStruct((1, LANES), F32),
        ],
        compiler_params=_params(("arbitrary",)),
        name="odd_out_router",
    )(x, yc, yd, mod, nw, w_out, w_router)


def _sc_workers():
    info = plsc.get_sparse_core_info()
    return info.num_cores, info.num_cores * info.num_subcores


def _sc_scatter_rows(x, idx0, idx1, out_rows):
    n, w = x.shape
    nc, nw = _sc_workers()
    per_w = n // nw
    nch = per_w // SC_ROWS
    mesh = plsc.VectorSubcoreMesh(core_axis_name="c", subcore_axis_name="s")

    @functools.partial(
        pl.kernel, mesh=mesh, out_type=jax.ShapeDtypeStruct((out_rows, w), x.dtype),
        scratch_types=[pltpu.VMEM((nch, SC_ROWS), jnp.int32), pltpu.VMEM((nch, SC_ROWS), jnp.int32),
                       pltpu.VMEM((SC_ROWS, w), x.dtype), pltpu.SemaphoreType.DMA],
        name="moe_dispatch")
    def scatter(x_hbm, i0_hbm, i1_hbm, out_hbm, i0_v, i1_v, rows_v, sem):
        wid = lax.axis_index("s") * nc + lax.axis_index("c")
        pltpu.sync_copy(i0_hbm.at[wid], i0_v)
        pltpu.sync_copy(i1_hbm.at[wid], i1_v)

        def body(c, carry):
            pltpu.sync_copy(x_hbm.at[pl.ds(wid * per_w + c * SC_ROWS, SC_ROWS)], rows_v)
            pltpu.async_copy(rows_v, out_hbm.at[i0_v.at[c]], sem).wait()
            pltpu.async_copy(rows_v, out_hbm.at[i1_v.at[c]], sem).wait()
            return carry

        lax.fori_loop(0, nch, body, 0)

    return scatter(x, idx0.reshape(nw, nch, SC_ROWS), idx1.reshape(nw, nch, SC_ROWS))


def _sc_gather_rows(table, idx):
    _, w = table.shape
    b = idx.shape[0]
    nc, nw = _sc_workers()
    per_w = b // nw
    nch = per_w // SC_ROWS
    mesh = plsc.VectorSubcoreMesh(core_axis_name="c", subcore_axis_name="s")

    @functools.partial(
        pl.kernel, mesh=mesh, out_type=jax.ShapeDtypeStruct((b, w), table.dtype),
        scratch_types=[pltpu.VMEM((nch, SC_ROWS), jnp.int32), pltpu.VMEM((SC_ROWS, w), table.dtype),
                       pltpu.SemaphoreType.DMA],
        name="moe_combine_gather")
    def gather(table_hbm, idx_hbm, out_hbm, idx_v, rows_v, sem):
        wid = lax.axis_index("s") * nc + lax.axis_index("c")
        pltpu.sync_copy(idx_hbm.at[wid], idx_v)

        def body(c, carry):
            pltpu.async_copy(table_hbm.at[idx_v.at[c]], rows_v, sem).wait()
            pltpu.sync_copy(rows_v, out_hbm.at[pl.ds(wid * per_w + c * SC_ROWS, SC_ROWS)])
            return carry

        lax.fori_loop(0, nch, body, 0)

    return gather(table, idx.reshape(nw, nch, SC_ROWS))


def _moe_kernel(te_ref, nv_ref, xs_ref, wg_ref, wu_ref, wd_ref, ys_ref, x_scr, acc_ref):
    j = pl.program_id(0)
    f = pl.program_id(1)
    nf = pl.num_programs(1)

    @pl.when(j < nv_ref[0])
    def _():
        @pl.when(f == 0)
        def _():
            x_scr[...] = _unpack_bf16_pairs(xs_ref[...]).astype(BF16)

        h = x_scr[...]
        g = jnp.dot(h, wg_ref[0], preferred_element_type=F32)
        u = jnp.dot(h, wu_ref[0], preferred_element_type=F32)
        act = (g * jax.nn.sigmoid(g) * u).astype(BF16)
        y = jnp.dot(act, wd_ref[0], preferred_element_type=F32)

        @pl.when(f == 0)
        def _():
            acc_ref[...] = y

        @pl.when(f > 0)
        def _():
            acc_ref[...] += y

        @pl.when(f == nf - 1)
        def _():
            ys_ref[...] = _pack_bf16_pairs(acc_ref[...])


def _moe_grouped(xs, tile_expert, n_valid, w_gu, w_down, tm, tf=1792):
    p_rows, dh = xs.shape
    d = 2 * dh
    ne, _, ff2 = w_gu.shape
    ff = ff2 // 2
    nf = ff // tf

    def tile(j, nv):
        return jnp.minimum(j, nv[0] - 1)

    def chunk(j, f, nv):
        return jnp.where(j < nv[0], f, nf - 1)

    grid_spec = pltpu.PrefetchScalarGridSpec(
        num_scalar_prefetch=2,
        grid=(p_rows // tm, nf),
        in_specs=[
            pl.BlockSpec((tm, dh), lambda j, f, te, nv: (tile(j, nv), 0)),
            pl.BlockSpec((1, d, tf), lambda j, f, te, nv: (te[tile(j, nv)], 0, chunk(j, f, nv))),
            pl.BlockSpec((1, d, tf), lambda j, f, te, nv: (te[tile(j, nv)], 0, nf + chunk(j, f, nv))),
            pl.BlockSpec((1, tf, d), lambda j, f, te, nv: (te[tile(j, nv)], chunk(j, f, nv), 0)),
        ],
        out_specs=pl.BlockSpec((tm, dh), lambda j, f, te, nv: (tile(j, nv), 0)),
        scratch_shapes=[pltpu.VMEM((tm, d), BF16), pltpu.VMEM((tm, d), F32)],
    )
    return pl.pallas_call(
        _moe_kernel,
        grid_spec=grid_spec,
        out_shape=jax.ShapeDtypeStruct((p_rows, dh), jnp.uint32),
        compiler_params=_params(("arbitrary", "arbitrary")),
        name="moe_experts",
    )(tile_expert, n_valid, xs, w_gu, w_gu, w_down)


def _moe_combine_kernel(x_ref, y0_ref, y1_ref, rw_ref, mod_ref, o_ref):
    d = D_MODEL
    w1 = rw_ref[:, 0:1]
    w2 = rw_ref[:, 1:2]
    y = w1 * _unpack_bf16_pairs(y0_ref[...]) + w2 * _unpack_bf16_pairs(y1_ref[...])
    o_ref[...] = x_ref[...] + mod_ref[:, 5 * d:6 * d] * y


def _moe_combine(x, yg, rw, mod, tm=512):
    s, d = x.shape
    nb = s // tm
    return pl.pallas_call(
        _moe_combine_kernel,
        grid=(nb,),
        in_specs=[
            pl.BlockSpec((tm, d), lambda i: (i, 0)),
            pl.BlockSpec((tm, d // 2), lambda i: (i, 0)),
            pl.BlockSpec((tm, d // 2), lambda i: (nb + i, 0)),
            pl.BlockSpec((tm, LANES), lambda i: (i, 0)),
            _const_spec(mod.shape),
        ],
        out_specs=pl.BlockSpec((tm, d), lambda i: (i, 0)),
        out_shape=jax.ShapeDtypeStruct((s, d), F32),
        compiler_params=_params(("arbitrary",)),
        name="moe_combine",
    )(x, yg, yg, rw, mod)


def _moe_sparse(x, h_packed, rw, ridx, counts, mod, w_gu, w_down, tm=MOE_TILE):
    s = x.shape[0]
    ne = w_gu.shape[0]
    n_tiles = (2 * s) // tm + ne
    cnt = counts[0, :ne].astype(jnp.int32)
    padded = ((cnt + tm - 1) // tm) * tm
    ends = jnp.cumsum(padded)
    offs = ends - padded
    experts = jnp.arange(ne, dtype=jnp.int32)
    off_of = lambda e: jnp.sum(jnp.where(e[:, None] == experts[None, :], offs[None, :], 0), axis=1)
    pos0 = off_of(ridx[:, 0]) + ridx[:, 2]
    pos1 = off_of(ridx[:, 1]) + ridx[:, 3]
    tile_start = jnp.arange(n_tiles, dtype=jnp.int32) * tm
    tile_expert = jnp.minimum(jnp.sum(tile_start[:, None] >= ends[None, :], axis=1), ne - 1).astype(jnp.int32)
    n_valid = (ends[-1] // tm).reshape(1).astype(jnp.int32)
    xs = _sc_scatter_rows(h_packed, pos0, pos1, n_tiles * tm)
    ys = _moe_grouped(xs, tile_expert, n_valid, w_gu, w_down, tm)
    yg = _sc_gather_rows(ys, jnp.concatenate([pos0, pos1]))
    return _moe_combine(x, yg, rw, mod)


def kernel(x, c, positions, norm_mix_w, norm_ffn_w, ada_w, ada_b, e_w_in, a_ln_w, a_ln_b, a_w_s, a_b_s, b_w_grp,
           b_scale, e_w_out, ffn_w_gu, ffn_w_down, o_w_in, c_conv_w, q_a_norm, w_uq, kv_norm, w_ukv, q_norm_w,
           k_norm_w, o_w_out, router_w, moe_w_gu, moe_w_down):
    bsz, s, d = x.shape
    assert bsz == 1 and d == D_MODEL
    depth = ada_w.shape[0]
    nh = MLA_HEADS
    xs = x.reshape(s, d)
    mod = _ada_mod(c, ada_w, ada_b)
    cos, sin = _rope_tables(positions)
    row = lambda a: a.reshape(1, -1)

    for layer in range(depth):
        i = layer // 2
        m = mod[layer]
        if layer % 2 == 0:
            xs = _even_mix(
                xs, m, row(norm_mix_w[layer]), e_w_in[i].astype(BF16), row(a_ln_w[i]), row(a_ln_b[i]), a_w_s[i],
                a_b_s[i].reshape(A_GROUPS, CHUNK, 1), b_w_grp[i].astype(BF16), row(b_scale[i]),
                e_w_out[i].astype(BF16))
            xs = _ffn(xs, m, row(norm_ffn_w[layer]), ffn_w_gu[i].astype(BF16), ffn_w_down[i].astype(BF16))
        else:
            o2 = 3 * C_WIDTH + Q_LORA + KV_LORA
            w_in = o_w_in[i][:, :o2].astype(BF16)
            w_kpe = jnp.pad(o_w_in[i][:, o2:], ((0, 0), (0, LANES - QK_ROPE))).astype(BF16)
            wq = w_uq[i].reshape(Q_LORA, nh, QK_HD)
            wq = jnp.concatenate([wq[:, :, :QK_NOPE].reshape(Q_LORA, nh * QK_NOPE),
                                  wq[:, :, QK_NOPE:].reshape(Q_LORA, nh * QK_ROPE)], axis=1).astype(BF16)
            yc, qt, k, vt = _odd_prep(
                xs, m, row(norm_mix_w[layer]), w_in, w_kpe, c_conv_w[i], row(q_a_norm[i]), wq, row(kv_norm[i]),
                w_ukv[i].astype(BF16), row(q_norm_w[i]), row(k_norm_w[i]), cos, sin)
            yd = _attention(qt, k, vt)
            w_router = jnp.pad(router_w[i], ((0, 0), (0, LANES - N_EXPERTS)))
            xs, hp, rw, ridx, counts = _odd_out(xs, yc, yd, m, row(norm_ffn_w[layer]), o_w_out[i].astype(BF16),
                                                w_router)
            xs = _moe_sparse(xs, hp, rw, ridx, counts, m, moe_w_gu[i].astype(BF16), moe_w_down[i].astype(BF16))
    return xs.reshape(bsz, s, d)
```

```python
import functools

import jax
import jax.numpy as jnp
from jax import lax
from jax.experimental import pallas as pl
from jax.experimental.pallas import tpu as pltpu
from jax.experimental.pallas import tpu_sc as plsc

D_MODEL = 1024
SEQ = 16384
EPS = 1e-6
CHUNK = 128
A_WIDTH = 512
A_GROUPS = 4
B_WIDTH = 512
POOL_WINDOWS = (2, 4, 8, 16)
B_HD = 128
C_WIDTH = 512
MLA_HEADS = 4
Q_LORA = 256
KV_LORA = 256
QK_NOPE = 128
QK_ROPE = 64
QK_HD = QK_NOPE + QK_ROPE
V_HD = 128
ROPE_THETA = 10000.0
D_FF = 2816
N_EXPERTS = 8
D_FF_EXPERT = 3584

LANES = 128
POOL_HALO = 16
CONV_HALO = 8
QK_PAD = 256
V_PAD = 144
LOG2E = 1.4426950408889634
MASK_VALUE = -1e30
MOE_TILE = 512
SC_ROWS = 64
VMEM_LIMIT = 56 * 1024 * 1024

F32 = jnp.float32
BF16 = jnp.bfloat16


def _params(sem, vmem=VMEM_LIMIT, flags=None):
    return pltpu.CompilerParams(dimension_semantics=sem, vmem_limit_bytes=vmem, flags=flags)


def _const_spec(shape):
    nd = len(shape)
    return pl.BlockSpec(shape, lambda *_: (0,) * nd)


def _rms_mod(x, nw, sc, sh):
    ms = jnp.mean(x * x, axis=-1, keepdims=True)
    return (x * lax.rsqrt(ms + EPS)) * nw * (1.0 + sc) + sh


def _ada_kernel(c_ref, w_ref, b_ref, o_ref):
    c = c_ref[...]
    ca = c * jax.nn.sigmoid(c)
    o_ref[0] = jnp.sum(w_ref[0] * ca, axis=0, keepdims=True) + b_ref[0]


def _ada_mod(c, ada_w, ada_b):
    depth, d, n = ada_w.shape
    tn = 768
    return pl.pallas_call(
        _ada_kernel,
        grid=(depth, n // tn),
        in_specs=[
            pl.BlockSpec((d, 1), lambda l, j: (0, 0)),
            pl.BlockSpec((1, d, tn), lambda l, j: (l, 0, j)),
            pl.BlockSpec((1, 1, tn), lambda l, j: (l, 0, j)),
        ],
        out_specs=pl.BlockSpec((1, 1, tn), lambda l, j: (l, 0, j)),
        out_shape=jax.ShapeDtypeStruct((depth, 1, n), F32),
        compiler_params=_params(("arbitrary", "arbitrary")),
        name="ada_mod",
    )(c.reshape(d, 1), ada_w, ada_b.reshape(depth, 1, n))


def _even_mix_kernel(x_ref, mod_ref, nw_ref, win_ref, lnw_ref, lnb_ref, ws_ref, bs_ref, wg_ref, bsc_ref,
                     wout_ref, o_ref, halo_ref, sv_ref, yb_ref):
    tm = x_ref.shape[0]
    i = pl.program_id(0)

    @pl.when(i == 0)
    def _():
        halo_ref[...] = jnp.zeros_like(halo_ref)

    d = D_MODEL
    x = x_ref[...]
    sh, sc, gate = mod_ref[:, 0:d], mod_ref[:, d:2 * d], mod_ref[:, 2 * d:3 * d]
    h = _rms_mod(x, nw_ref[...], sc, sh).astype(BF16)
    p = jnp.dot(h, win_ref[...], preferred_element_type=F32)

    gl = jax.nn.gelu(p[:, :2 * A_WIDTH])
    u = gl[:, :A_WIDTH]
    v = gl[:, A_WIDTH:]
    mu = jnp.mean(v, axis=-1, keepdims=True)
    vc = v - mu
    var = jnp.mean(vc * vc, axis=-1, keepdims=True)
    vn = (vc * lax.rsqrt(var + EPS) * lnw_ref[...] + lnb_ref[...]).astype(BF16)
    row = lax.broadcasted_iota(jnp.int32, (CHUNK, CHUNK), 0)
    col = lax.broadcasted_iota(jnp.int32, (CHUNK, CHUNK), 1)
    for g in range(A_GROUPS):
        w = jnp.where(col <= row, ws_ref[g], 0.0).astype(BF16)
        b = bs_ref[g]
        for c in range(tm // CHUNK):
            blk = vn[c * CHUNK:(c + 1) * CHUNK, g * LANES:(g + 1) * LANES]
            sv_ref[c * CHUNK:(c + 1) * CHUNK, g * LANES:(g + 1) * LANES] = (
                jnp.dot(w, blk, preferred_element_type=F32) + b)
    ya = (u * sv_ref[...]).astype(BF16)

    pb = p[:, 2 * A_WIDTH:]
    ext = jnp.concatenate([halo_ref[...], pb], axis=0)
    halo_ref[...] = pb[tm - POOL_HALO:, :]
    t_glob = i * tm + lax.broadcasted_iota(jnp.int32, (tm, 1), 0)
    s = ext
    width = 1
    for g, win in enumerate(POOL_WINDOWS):
        while width < win:
            s = s + pltpu.roll(s, width, 0)
            width *= 2
        cnt = jnp.minimum(t_glob + 1, win).astype(F32)
        sl = slice(g * B_HD, (g + 1) * B_HD)
        pooled = s[POOL_HALO:, sl] / cnt
        dg = (pooled - pb[:, sl]).astype(BF16)
        yb_ref[:, sl] = jnp.dot(dg, wg_ref[g], preferred_element_type=F32)
    yb = (yb_ref[...] * bsc_ref[...]).astype(BF16)

    mix = (jnp.dot(ya, wout_ref[:A_WIDTH, :], preferred_element_type=F32)
           + jnp.dot(yb, wout_ref[A_WIDTH:, :], preferred_element_type=F32))
    o_ref[...] = x + gate * mix


def _even_mix(x, mod, nw, w_in, ln_w, ln_b, w_s, b_s, w_grp, b_scale, w_out, tm=512):
    s, d = x.shape
    return pl.pallas_call(
        _even_mix_kernel,
        grid=(s // tm,),
        in_specs=[
            pl.BlockSpec((tm, d), lambda i: (i, 0)),
            _const_spec(mod.shape), _const_spec(nw.shape), _const_spec(w_in.shape),
            _const_spec(ln_w.shape), _const_spec(ln_b.shape), _const_spec(w_s.shape), _const_spec(b_s.shape),
            _const_spec(w_grp.shape), _const_spec(b_scale.shape), _const_spec(w_out.shape),
        ],
        out_specs=pl.BlockSpec((tm, d), lambda i: (i, 0)),
        out_shape=jax.ShapeDtypeStruct((s, d), F32),
        scratch_shapes=[pltpu.VMEM((POOL_HALO, B_WIDTH), F32), pltpu.VMEM((tm, A_WIDTH), F32),
                        pltpu.VMEM((tm, B_WIDTH), F32)],
        compiler_params=_params(("arbitrary",)),
        name="even_mix",
    )(x, mod, nw, w_in, ln_w, ln_b, w_s, b_s, w_grp, b_scale, w_out)


def _ffn_kernel(x_ref, mod_ref, nw_ref, wgu_ref, wd_ref, o_ref, *, n_chunks):
    d = D_MODEL
    x = x_ref[...]
    sh, sc, gate = mod_ref[:, 3 * d:4 * d], mod_ref[:, 4 * d:5 * d], mod_ref[:, 5 * d:6 * d]
    h = _rms_mod(x, nw_ref[...], sc, sh).astype(BF16)
    ff = wd_ref.shape[0]
    tf = ff // n_chunks
    acc = jnp.zeros(x.shape, F32)
    for f in range(n_chunks):
        g = jnp.dot(h, wgu_ref[:, f * tf:(f + 1) * tf], preferred_element_type=F32)
        u = jnp.dot(h, wgu_ref[:, ff + f * tf:ff + (f + 1) * tf], preferred_element_type=F32)
        act = (g * jax.nn.sigmoid(g) * u).astype(BF16)
        acc = acc + jnp.dot(act, wd_ref[f * tf:(f + 1) * tf, :], preferred_element_type=F32)
    o_ref[...] = x + gate * acc


def _ffn(x, mod, nw, w_gu, w_down, tm=512, n_chunks=2):
    s, d = x.shape
    return pl.pallas_call(
        functools.partial(_ffn_kernel, n_chunks=n_chunks),
        grid=(s // tm,),
        in_specs=[
            pl.BlockSpec((tm, d), lambda i: (i, 0)),
            _const_spec(mod.shape), _const_spec(nw.shape), _const_spec(w_gu.shape), _const_spec(w_down.shape),
        ],
        out_specs=pl.BlockSpec((tm, d), lambda i: (i, 0)),
        out_shape=jax.ShapeDtypeStruct((s, d), F32),
        compiler_params=_params(("arbitrary",)),
        name="ffn",
    )(x, mod, nw, w_gu, w_down)


def _rope_table_kernel(pos_ref, invf_ref, cos_ref, sin_ref):
    ang = pos_ref[...].astype(F32) * invf_ref[...]
    cos_ref[...] = jnp.cos(ang)
    sin_ref[...] = jnp.sin(ang)


def _rope_tables(positions):
    s = positions.shape[-1]
    half = QK_ROPE // 2
    inv_freq = ROPE_THETA ** (-jnp.arange(0, QK_ROPE, 2, dtype=F32) / QK_ROPE)
    cos_t, sin_t = pl.pallas_call(
        _rope_table_kernel,
        out_shape=(jax.ShapeDtypeStruct((half, s), F32), jax.ShapeDtypeStruct((half, s), F32)),
        name="rope_tables",
    )(positions.reshape(1, s), inv_freq.reshape(half, 1))
    return jnp.tile(cos_t.T, (1, 2)), jnp.tile(sin_t.T, (1, 2))


def _rope_rot(x):
    n = x.shape[-1]
    lane = lax.broadcasted_iota(jnp.int32, x.shape, 1)
    half = QK_ROPE // 2
    fwd = pltpu.roll(x, half, 1)
    bwd = pltpu.roll(x, n - half, 1)
    return jnp.where((lane % QK_ROPE) < half, -bwd, fwd)


def _odd_prep_kernel(x_ref, mod_ref, nw_ref, win_ref, wkpe_ref, cw_ref, qan_ref, wuq_ref, kvn_ref, wukv_ref,
                     qnw_ref, knw_ref, cos_ref, sin_ref, yc_ref, qt_ref, k_ref, vt_ref, halo_ref):
    tm = x_ref.shape[0]
    i = pl.program_id(0)

    @pl.when(i == 0)
    def _():
        halo_ref[...] = jnp.zeros_like(halo_ref)

    d = D_MODEL
    x = x_ref[...]
    sh, sc = mod_ref[:, 0:d], mod_ref[:, d:2 * d]
    h = _rms_mod(x, nw_ref[...], sc, sh).astype(BF16)
    p = jnp.dot(h, win_ref[...], preferred_element_type=F32)
    kpe = jnp.dot(h, wkpe_ref[...], preferred_element_type=F32)[:, :QK_ROPE]

    cw = C_WIDTH
    bg, cg, hh = p[:, 0:cw], p[:, cw:2 * cw], p[:, 2 * cw:3 * cw]
    z = cg * hh
    ext = jnp.concatenate([halo_ref[...], z], axis=0)
    halo_ref[...] = z[tm - CONV_HALO:, :]
    z1 = pltpu.roll(ext, 1, 0)[CONV_HALO:, :]
    z2 = pltpu.roll(ext, 2, 0)[CONV_HALO:, :]
    conv = cw_ref[0:1, :] * z2 + cw_ref[1:2, :] * z1 + cw_ref[2:3, :] * z
    yc_ref[...] = (bg * conv).astype(BF16)

    o0 = 3 * cw
    cq = p[:, o0:o0 + Q_LORA]
    ckv = p[:, o0 + Q_LORA:o0 + Q_LORA + KV_LORA]
    cqn = (cq * lax.rsqrt(jnp.mean(cq * cq, axis=-1, keepdims=True) + EPS) * qan_ref[...]).astype(BF16)
    ckvn = (ckv * lax.rsqrt(jnp.mean(ckv * ckv, axis=-1, keepdims=True) + EPS) * kvn_ref[...]).astype(BF16)
    q = jnp.dot(cqn, wuq_ref[...], preferred_element_type=F32)
    kv = jnp.dot(ckvn, wukv_ref[...], preferred_element_type=F32)

    nh = MLA_HEADS
    cos = jnp.concatenate([cos_ref[...]] * nh, axis=1)
    sin = jnp.concatenate([sin_ref[...]] * nh, axis=1)
    qn_w, qr_w = qnw_ref[:, :QK_NOPE], qnw_ref[:, QK_NOPE:]
    kn_w, kr_w = knw_ref[:, :QK_NOPE], knw_ref[:, QK_NOPE:]
    sm_scale = QK_HD ** -0.5

    qr_all = q[:, nh * QK_NOPE:]
    qr_w4 = jnp.concatenate([qr_w] * nh, axis=1)
    qr_all = qr_all * qr_w4
    qr_all = qr_all * cos + _rope_rot(qr_all) * sin
    kr = kpe * kr_w
    kr = kr * cos[:, :QK_ROPE] + _rope_rot(kr) * sin[:, :QK_ROPE]
    kpe_ss = jnp.sum(kpe * kpe, axis=-1, keepdims=True)
    zpad = jnp.zeros((tm, QK_PAD - QK_HD), F32)
    ones_rows = (lax.broadcasted_iota(jnp.int32, (V_PAD - V_HD, tm), 0) == 0).astype(BF16)
    for hd in range(nh):
        qn = q[:, hd * QK_NOPE:(hd + 1) * QK_NOPE]
        qr_raw = q[:, nh * QK_NOPE + hd * QK_ROPE:nh * QK_NOPE + (hd + 1) * QK_ROPE]
        q_ss = jnp.sum(qn * qn, axis=-1, keepdims=True) + jnp.sum(qr_raw * qr_raw, axis=-1, keepdims=True)
        q_inv = lax.rsqrt(q_ss / QK_HD + EPS) * (sm_scale * LOG2E)
        qh = jnp.concatenate([qn * q_inv * qn_w, qr_all[:, hd * QK_ROPE:(hd + 1) * QK_ROPE] * q_inv, zpad], axis=1)
        qt_ref[hd] = qh.T.astype(BF16)
        kn = kv[:, hd * (QK_NOPE + V_HD):hd * (QK_NOPE + V_HD) + QK_NOPE]
        k_ss = jnp.sum(kn * kn, axis=-1, keepdims=True) + kpe_ss
        k_inv = lax.rsqrt(k_ss / QK_HD + EPS)
        k_ref[hd] = jnp.concatenate([kn * k_inv * kn_w, kr * k_inv, zpad], axis=1).astype(BF16)
        vh = kv[:, hd * (QK_NOPE + V_HD) + QK_NOPE:(hd + 1) * (QK_NOPE + V_HD)]
        vt_ref[hd, :V_HD, :] = vh.T.astype(BF16)
        vt_ref[hd, V_HD:, :] = ones_rows


def _odd_prep(x, mod, nw, w_in, w_kpe, conv_w, q_a_norm, w_uq, kv_norm, w_ukv, q_norm_w, k_norm_w, cos, sin,
              tm=1024):
    s, d = x.shape
    nh = MLA_HEADS
    consts = [mod, nw, w_in, w_kpe, conv_w, q_a_norm, w_uq, kv_norm, w_ukv, q_norm_w, k_norm_w]
    return pl.pallas_call(
        _odd_prep_kernel,
        grid=(s // tm,),
        in_specs=[pl.BlockSpec((tm, d), lambda i: (i, 0))] + [_const_spec(a.shape) for a in consts] + [
            pl.BlockSpec((tm, QK_ROPE), lambda i: (i, 0)), pl.BlockSpec((tm, QK_ROPE), lambda i: (i, 0))],
        out_specs=[
            pl.BlockSpec((tm, C_WIDTH), lambda i: (i, 0)),
            pl.BlockSpec((nh, QK_PAD, tm), lambda i: (0, 0, i)),
            pl.BlockSpec((nh, tm, QK_PAD), lambda i: (0, i, 0)),
            pl.BlockSpec((nh, V_PAD, tm), lambda i: (0, 0, i)),
        ],
        out_shape=[
            jax.ShapeDtypeStruct((s, C_WIDTH), BF16),
            jax.ShapeDtypeStruct((nh, QK_PAD, s), BF16),
            jax.ShapeDtypeStruct((nh, s, QK_PAD), BF16),
            jax.ShapeDtypeStruct((nh, V_PAD, s), BF16),
        ],
        scratch_shapes=[pltpu.VMEM((CONV_HALO, C_WIDTH), F32)],
        compiler_params=_params(("arbitrary",)),
        name="odd_prep",
    )(x, *consts, cos, sin)


def _attn_kernel(qt_ref, k_ref, vt_ref, o_ref, s0_ref, s1_ref, bm0_ref, bm1_ref, m_ref, acc_ref, *, tq, tk):
    qi = pl.program_id(1)
    m_ref[...] = jnp.full_like(m_ref, MASK_VALUE)
    acc_ref[...] = jnp.zeros_like(acc_ref)

    def scores(j, s_ref, bm_ref, diag_offset=None):
        start = pl.multiple_of(j * tk, tk)
        s = jnp.dot(k_ref[0, pl.ds(start, tk), :], qt_ref[0], preferred_element_type=F32)
        if diag_offset is not None:
            key = diag_offset + lax.broadcasted_iota(jnp.int32, (tk, tq), 0)
            qry = lax.broadcasted_iota(jnp.int32, (tk, tq), 1)
            s = jnp.where(key <= qry, s, MASK_VALUE)
        s_ref[...] = s
        bm_ref[...] = jnp.max(s, axis=0, keepdims=True)

    def consume(j, s_ref, bm_ref):
        m_prev = m_ref[...]
        m_new = jnp.maximum(m_prev, bm_ref[...])
        alpha = jnp.exp2(m_prev - m_new)
        p = jnp.exp2((s_ref[...] - m_new).astype(BF16))
        start = pl.multiple_of(j * tk, tk)
        vt = vt_ref[0, :, pl.ds(start, tk)]
        acc_ref[...] = alpha * acc_ref[...] + jnp.dot(vt, p, preferred_element_type=F32)
        m_ref[...] = m_new

    def diagonal(jd):
        scores(jd + 1, s1_ref, bm1_ref, diag_offset=tk)
        consume(jd, s0_ref, bm0_ref)
        consume(jd + 1, s1_ref, bm1_ref)

    @pl.when(qi > 0)
    def _():
        scores(0, s0_ref, bm0_ref)

    def body(t, carry):
        j = 2 * t
        scores(j + 1, s1_ref, bm1_ref)
        consume(j, s0_ref, bm0_ref)
        scores(j + 2, s0_ref, bm0_ref)
        consume(j + 1, s1_ref, bm1_ref)
        return carry

    lax.fori_loop(0, qi - 1, body, 0)

    @pl.when(qi > 0)
    def _():
        j = 2 * (qi - 1)
        scores(j + 1, s1_ref, bm1_ref)
        consume(j, s0_ref, bm0_ref)
        scores(j + 2, s0_ref, bm0_ref, diag_offset=0)
        consume(j + 1, s1_ref, bm1_ref)
        diagonal(j + 2)

    @pl.when(qi == 0)
    def _():
        scores(0, s0_ref, bm0_ref, diag_offset=0)
        diagonal(0)

    acc = acc_ref[...]
    out_t = acc[:V_HD, :] / acc[V_HD:V_HD + 1, :]
    o_ref[...] = out_t.T.astype(o_ref.dtype)


def _attention(qt, k, vt, tq=1024, tk=512):
    nh, _, s = qt.shape
    assert tq == 2 * tk
    return pl.pallas_call(
        functools.partial(_attn_kernel, tq=tq, tk=tk),
        grid=(nh, s // tq),
        in_specs=[
            pl.BlockSpec((1, QK_PAD, tq), lambda h, i: (h, 0, i)),
            pl.BlockSpec((1, s, QK_PAD), lambda h, i: (h, 0, 0)),
            pl.BlockSpec((1, V_PAD, s), lambda h, i: (h, 0, 0)),
        ],
        out_specs=pl.BlockSpec((tq, V_HD), lambda h, i: (i, h)),
        out_shape=jax.ShapeDtypeStruct((s, nh * V_HD), BF16),
        scratch_shapes=[pltpu.VMEM((tk, tq), F32), pltpu.VMEM((tk, tq), F32), pltpu.VMEM((1, tq), F32),
                        pltpu.VMEM((1, tq), F32), pltpu.VMEM((1, tq), F32), pltpu.VMEM((V_PAD, tq), F32)],
        compiler_params=_params(("arbitrary", "arbitrary")),
        name="mla_attention",
    )(qt, k, vt)


def _pack_bf16_pairs(x):
    n = x.shape[1] // 2
    lo = pltpu.bitcast(x[:, :n].astype(BF16).astype(F32), jnp.uint32)
    hi = pltpu.bitcast(x[:, n:].astype(BF16).astype(F32), jnp.uint32)
    return (lo >> 16) | hi


def _unpack_bf16_pairs(p):
    lo = pltpu.bitcast(p << 16, F32)
    hi = pltpu.bitcast(p & jnp.uint32(0xFFFF0000), F32)
    return jnp.concatenate([lo, hi], axis=1)


def _odd_out_kernel(x_ref, yc_ref, yd_ref, mod_ref, nw_ref, wout_ref, wr_ref, before_ref, x_out_ref, h_ref, rw_ref,
                    ridx_ref, cnt_ref):
    d = D_MODEL
    x = x_ref[...]
    gate_m = mod_ref[:, 2 * d:3 * d]
    sh, sc = mod_ref[:, 3 * d:4 * d], mod_ref[:, 4 * d:5 * d]
    mix = (jnp.dot(yc_ref[...], wout_ref[:C_WIDTH, :], preferred_element_type=F32)
           + jnp.dot(yd_ref[...], wout_ref[C_WIDTH:, :], preferred_element_type=F32))
    x1 = x + gate_m * mix
    x_out_ref[...] = x1
    h = _rms_mod(x1, nw_ref[...], sc, sh)
    h_ref[...] = _pack_bf16_pairs(h)

    h_hi = h.astype(BF16)
    h_lo = (h - h_hi.astype(F32)).astype(BF16)
    hw = jnp.dot(h_hi, wr_ref[...], preferred_element_type=F32)
    logits = hw[:, :LANES] + (hw[:, LANES:] + jnp.dot(h_lo, wr_ref[:, :LANES], preferred_element_type=F32))
    lane = lax.broadcasted_iota(jnp.int32, logits.shape, 1)
    logits = jnp.where(lane < N_EXPERTS, logits, -jnp.inf)
    m1 = jnp.max(logits, axis=-1, keepdims=True)
    i1 = jnp.min(jnp.where(logits == m1, lane, LANES), axis=-1, keepdims=True)
    rest = jnp.where(lane == i1, -jnp.inf, logits)
    m2 = jnp.max(rest, axis=-1, keepdims=True)
    i2 = jnp.min(jnp.where(rest == m2, lane, LANES), axis=-1, keepdims=True)
    e2 = jnp.exp(m2 - m1)
    w1 = 1.0 / (1.0 + e2)
    w2 = e2 / (1.0 + e2)
    rw_ref[...] = jnp.where(lane == 0, w1, jnp.where(lane == 1, w2, 0.0))

    i = pl.program_id(0)

    @pl.when(i == 0)
    def _():
        cnt_ref[...] = jnp.zeros_like(cnt_ref)

    onehot = jnp.where((lane == i1) | (lane == i2), 1.0, 0.0)
    prefix = jnp.dot(before_ref[...], onehot.astype(BF16), preferred_element_type=F32) + cnt_ref[...]
    r1 = jnp.sum(jnp.where(lane == i1, prefix, 0.0), axis=-1, keepdims=True).astype(jnp.int32)
    r2 = jnp.sum(jnp.where(lane == i2, prefix, 0.0), axis=-1, keepdims=True).astype(jnp.int32)
    cnt_ref[...] += jnp.sum(onehot, axis=0, keepdims=True)
    ridx_ref[...] = jnp.where(lane == 0, i1, jnp.where(lane == 1, i2, jnp.where(lane == 2, r1,
                                                                                 jnp.where(lane == 3, r2, 0))))


def _odd_out(x, yc, yd, mod, nw, w_out, w_router, tm=1024):
    s, d = x.shape
    return pl.pallas_call(
        _odd_out_kernel,
        grid=(s // tm,),
        in_specs=[
            pl.BlockSpec((tm, d), lambda i: (i, 0)),
            pl.BlockSpec((tm, C_WIDTH), lambda i: (i, 0)),
            pl.BlockSpec((tm, MLA_HEADS * V_HD), lambda i: (i, 0)),
            _const_spec(mod.shape), _const_spec(nw.shape), _const_spec(w_out.shape), _const_spec(w_router.shape),
            _const_spec((tm, tm)),
        ],
        out_specs=[
            pl.BlockSpec((tm, d), lambda i: (i, 0)),
            pl.BlockSpec((tm, d // 2), lambda i: (i, 0)),
            pl.BlockSpec((tm, LANES), lambda i: (i, 0)),
            pl.BlockSpec((tm, LANES), lambda i: (i, 0)),
            pl.BlockSpec((1, LANES), lambda i: (0, 0)),
        ],
        out_shape=[
            jax.ShapeDtypeStruct((s, d), F32),
            jax.ShapeDtypeStruct((s, d // 2), jnp.uint32),
            jax.ShapeDtypeStruct((s, LANES), F32),
            jax.ShapeDtypeStruct((s, LANES), jnp.int32),
            jax.ShapeDtypeStruct((1, LANES), F32),
        ],
        compiler_params=_params(("arbitrary",)),
        name="odd_out_router",
    )(x, yc, yd, mod, nw, w_out, w_router, jnp.tril(jnp.ones((tm, tm), BF16), -1))


def _sc_workers():
    info = plsc.get_sparse_core_info()
    return info.num_cores, info.num_cores * info.num_subcores


def _sc_scatter_rows(x, idx0, idx1, out_rows):
    n, w = x.shape
    nc, nw = _sc_workers()
    per_w = n // nw
    nch = per_w // SC_ROWS
    mesh = plsc.VectorSubcoreMesh(core_axis_name="c", subcore_axis_name="s")

    @functools.partial(
        pl.kernel, mesh=mesh, out_type=jax.ShapeDtypeStruct((out_rows, w), x.dtype),
        scratch_types=[pltpu.VMEM((nch, SC_ROWS), jnp.int32), pltpu.VMEM((nch, SC_ROWS), jnp.int32),
                       pltpu.VMEM((SC_ROWS, w), x.dtype), pltpu.SemaphoreType.DMA],
        name="moe_dispatch")
    def scatter(x_hbm, i0_hbm, i1_hbm, out_hbm, i0_v, i1_v, rows_v, sem):
        wid = lax.axis_index("s") * nc + lax.axis_index("c")
        pltpu.sync_copy(i0_hbm.at[wid], i0_v)
        pltpu.sync_copy(i1_hbm.at[wid], i1_v)

        def body(c, carry):
            pltpu.sync_copy(x_hbm.at[pl.ds(wid * per_w + c * SC_ROWS, SC_ROWS)], rows_v)
            pltpu.async_copy(rows_v, out_hbm.at[i0_v.at[c]], sem).wait()
            pltpu.async_copy(rows_v, out_hbm.at[i1_v.at[c]], sem).wait()
            return carry

        lax.fori_loop(0, nch, body, 0)

    return scatter(x, idx0.reshape(nw, nch, SC_ROWS), idx1.reshape(nw, nch, SC_ROWS))


def _sc_gather_rows(table, idx):
    _, w = table.shape
    b = idx.shape[0]
    nc, nw = _sc_workers()
    per_w = b // nw
    nch = per_w // SC_ROWS
    mesh = plsc.VectorSubcoreMesh(core_axis_name="c", subcore_axis_name="s")

    @functools.partial(
        pl.kernel, mesh=mesh, out_type=jax.ShapeDtypeStruct((b, w), table.dtype),
        scratch_types=[pltpu.VMEM((nch, SC_ROWS), jnp.int32), pltpu.VMEM((SC_ROWS, w), table.dtype),
                       pltpu.SemaphoreType.DMA],
        name="moe_combine_gather")
    def gather(table_hbm, idx_hbm, out_hbm, idx_v, rows_v, sem):
        wid = lax.axis_index("s") * nc + lax.axis_index("c")
        pltpu.sync_copy(idx_hbm.at[wid], idx_v)

        def body(c, carry):
            pltpu.async_copy(table_hbm.at[idx_v.at[c]], rows_v, sem).wait()
            pltpu.sync_copy(rows_v, out_hbm.at[pl.ds(wid * per_w + c * SC_ROWS, SC_ROWS)])
            return carry

        lax.fori_loop(0, nch, body, 0)

    return gather(table, idx.reshape(nw, nch, SC_ROWS))


def _moe_kernel(te_ref, nv_ref, xs_ref, wg_ref, wu_ref, wd_ref, ys_ref, x_scr, acc_ref):
    j = pl.program_id(0)
    f = pl.program_id(1)
    nf = pl.num_programs(1)

    @pl.when(j < nv_ref[0])
    def _():
        @pl.when(f == 0)
        def _():
            x_scr[...] = _unpack_bf16_pairs(xs_ref[...]).astype(BF16)

        h = x_scr[...]
        g = jnp.dot(h, wg_ref[0], preferred_element_type=F32)
        u = jnp.dot(h, wu_ref[0], preferred_element_type=F32)
        act = (g * jax.nn.sigmoid(g) * u).astype(BF16)
        y = jnp.dot(act, wd_ref[0], preferred_element_type=F32)

        @pl.when(f == 0)
        def _():
            acc_ref[...] = y

        @pl.when(f > 0)
        def _():
            acc_ref[...] += y

        @pl.when(f == nf - 1)
        def _():
            ys_ref[...] = _pack_bf16_pairs(acc_ref[...])


def _moe_grouped(xs, tile_expert, n_valid, w_gu, w_down, tm, tf=1792):
    p_rows, dh = xs.shape
    d = 2 * dh
    ne, _, ff2 = w_gu.shape
    ff = ff2 // 2
    nf = ff // tf

    def tile(j, nv):
        return jnp.minimum(j, nv[0] - 1)

    def chunk(j, f, nv):
        return jnp.where(j < nv[0], f, nf - 1)

    grid_spec = pltpu.PrefetchScalarGridSpec(
        num_scalar_prefetch=2,
        grid=(p_rows // tm, nf),
        in_specs=[
            pl.BlockSpec((tm, dh), lambda j, f, te, nv: (tile(j, nv), 0)),
            pl.BlockSpec((1, d, tf), lambda j, f, te, nv: (te[tile(j, nv)], 0, chunk(j, f, nv))),
            pl.BlockSpec((1, d, tf), lambda j, f, te, nv: (te[tile(j, nv)], 0, nf + chunk(j, f, nv))),
            pl.BlockSpec((1, tf, d), lambda j, f, te, nv: (te[tile(j, nv)], chunk(j, f, nv), 0)),
        ],
        out_specs=pl.BlockSpec((tm, dh), lambda j, f, te, nv: (tile(j, nv), 0)),
        scratch_shapes=[pltpu.VMEM((tm, d), BF16), pltpu.VMEM((tm, d), F32)],
    )
    return pl.pallas_call(
        _moe_kernel,
        grid_spec=grid_spec,
        out_shape=jax.ShapeDtypeStruct((p_rows, dh), jnp.uint32),
        compiler_params=_params(("arbitrary", "arbitrary")),
        name="moe_experts",
    )(tile_expert, n_valid, xs, w_gu, w_gu, w_down)


def _moe_combine_kernel(x_ref, y0_ref, y1_ref, rw_ref, mod_ref, o_ref):
    d = D_MODEL
    w1 = rw_ref[:, 0:1]
    w2 = rw_ref[:, 1:2]
    y = w1 * _unpack_bf16_pairs(y0_ref[...]) + w2 * _unpack_bf16_pairs(y1_ref[...])
    o_ref[...] = x_ref[...] + mod_ref[:, 5 * d:6 * d] * y


def _moe_combine(x, yg, rw, mod, tm=512):
    s, d = x.shape
    nb = s // tm
    return pl.pallas_call(
        _moe_combine_kernel,
        grid=(nb,),
        in_specs=[
            pl.BlockSpec((tm, d), lambda i: (i, 0)),
            pl.BlockSpec((tm, d // 2), lambda i: (i, 0)),
            pl.BlockSpec((tm, d // 2), lambda i: (nb + i, 0)),
            pl.BlockSpec((tm, LANES), lambda i: (i, 0)),
            _const_spec(mod.shape),
        ],
        out_specs=pl.BlockSpec((tm, d), lambda i: (i, 0)),
        out_shape=jax.ShapeDtypeStruct((s, d), F32),
        compiler_params=_params(("arbitrary",)),
        name="moe_combine",
    )(x, yg, yg, rw, mod)


def _moe_sparse(x, h_packed, rw, ridx, counts, mod, w_gu, w_down, tm=MOE_TILE):
    s = x.shape[0]
    ne = w_gu.shape[0]
    n_tiles = (2 * s) // tm + ne
    cnt = counts[0, :ne].astype(jnp.int32)
    padded = ((cnt + tm - 1) // tm) * tm
    ends = jnp.cumsum(padded)
    offs = ends - padded
    experts = jnp.arange(ne, dtype=jnp.int32)
    off_of = lambda e: jnp.sum(jnp.where(e[:, None] == experts[None, :], offs[None, :], 0), axis=1)
    pos0 = off_of(ridx[:, 0]) + ridx[:, 2]
    pos1 = off_of(ridx[:, 1]) + ridx[:, 3]
    tile_start = jnp.arange(n_tiles, dtype=jnp.int32) * tm
    tile_expert = jnp.minimum(jnp.sum(tile_start[:, None] >= ends[None, :], axis=1), ne - 1).astype(jnp.int32)
    n_valid = (ends[-1] // tm).reshape(1).astype(jnp.int32)
    xs = _sc_scatter_rows(h_packed, pos0, pos1, n_tiles * tm)
    ys = _moe_grouped(xs, tile_expert, n_valid, w_gu, w_down, tm)
    yg = _sc_gather_rows(ys, jnp.concatenate([pos0, pos1]))
    return _moe_combine(x, yg, rw, mod)


def kernel(x, c, positions, norm_mix_w, norm_ffn_w, ada_w, ada_b, e_w_in, a_ln_w, a_ln_b, a_w_s, a_b_s, b_w_grp,
           b_scale, e_w_out, ffn_w_gu, ffn_w_down, o_w_in, c_conv_w, q_a_norm, w_uq, kv_norm, w_ukv, q_norm_w,
           k_norm_w, o_w_out, router_w, moe_w_gu, moe_w_down):
    bsz, s, d = x.shape
    assert bsz == 1 and d == D_MODEL
    depth = ada_w.shape[0]
    nh = MLA_HEADS
    xs = x.reshape(s, d)
    mod = _ada_mod(c, ada_w, ada_b)
    cos, sin = _rope_tables(positions)
    row = lambda a: a.reshape(1, -1)

    for layer in range(depth):
        i = layer // 2
        m = mod[layer]
        if layer % 2 == 0:
            xs = _even_mix(
                xs, m, row(norm_mix_w[layer]), e_w_in[i].astype(BF16), row(a_ln_w[i]), row(a_ln_b[i]), a_w_s[i],
                a_b_s[i].reshape(A_GROUPS, CHUNK, 1), b_w_grp[i].astype(BF16), row(b_scale[i]),
                e_w_out[i].astype(BF16))
            xs = _ffn(xs, m, row(norm_ffn_w[layer]), ffn_w_gu[i].astype(BF16), ffn_w_down[i].astype(BF16))
        else:
            o2 = 3 * C_WIDTH + Q_LORA + KV_LORA
            w_in = o_w_in[i][:, :o2].astype(BF16)
            w_kpe = jnp.pad(o_w_in[i][:, o2:], ((0, 0), (0, LANES - QK_ROPE))).astype(BF16)
            wq = w_uq[i].reshape(Q_LORA, nh, QK_HD)
            wq = jnp.concatenate([wq[:, :, :QK_NOPE].reshape(Q_LORA, nh * QK_NOPE),
                                  wq[:, :, QK_NOPE:].reshape(Q_LORA, nh * QK_ROPE)], axis=1).astype(BF16)
            yc, qt, k, vt = _odd_prep(
                xs, m, row(norm_mix_w[layer]), w_in, w_kpe, c_conv_w[i], row(q_a_norm[i]), wq, row(kv_norm[i]),
                w_ukv[i].astype(BF16), row(q_norm_w[i]), row(k_norm_w[i]), cos, sin)
            yd = _attention(qt, k, vt)
            wr = jnp.pad(router_w[i], ((0, 0), (0, LANES - N_EXPERTS)))
            wr_hi = lax.reduce_precision(wr, exponent_bits=8, mantissa_bits=7)
            w_router = jnp.concatenate([wr_hi, wr - wr_hi], axis=1).astype(BF16)
            xs, hp, rw, ridx, counts = _odd_out(xs, yc, yd, m, row(norm_ffn_w[layer]), o_w_out[i].astype(BF16),
                                                w_router)
            xs = _moe_sparse(xs, hp, rw, ridx, counts, m, moe_w_gu[i].astype(BF16), moe_w_down[i].astype(BF16))
    return xs.reshape(bsz, s, d)
```

```python
import functools

import jax
import jax.numpy as jnp
from jax import lax
from jax.experimental import pallas as pl
from jax.experimental.pallas import tpu as pltpu
from jax.experimental.pallas import tpu_sc as plsc

D_MODEL = 1024
SEQ = 16384
EPS = 1e-6
CHUNK = 128
A_WIDTH = 512
A_GROUPS = 4
B_WIDTH = 512
POOL_WINDOWS = (2, 4, 8, 16)
B_HD = 128
C_WIDTH = 512
MLA_HEADS = 4
Q_LORA = 256
KV_LORA = 256
QK_NOPE = 128
QK_ROPE = 64
QK_HD = QK_NOPE + QK_ROPE
V_HD = 128
ROPE_THETA = 10000.0
D_FF = 2816
N_EXPERTS = 8
D_FF_EXPERT = 3584

LANES = 128
POOL_HALO = 16
CONV_HALO = 8
QK_PAD = 256
V_PAD = 144
LOG2E = 1.4426950408889634
MASK_VALUE = -1e30
MOE_TILE = 512
SC_ROWS = 64
VMEM_LIMIT = 56 * 1024 * 1024

F32 = jnp.float32
BF16 = jnp.bfloat16


def _params(sem, vmem=VMEM_LIMIT, flags=None):
    return pltpu.CompilerParams(dimension_semantics=sem, vmem_limit_bytes=vmem, flags=flags)


def _const_spec(shape, single=False):
    nd = len(shape)
    return pl.BlockSpec(shape, lambda *_: (0,) * nd, pipeline_mode=pl.Buffered(1) if single else None)


def _rms_mod(x, nw, sc, sh):
    ms = jnp.mean(x * x, axis=-1, keepdims=True)
    return (x * lax.rsqrt(ms + EPS)) * nw * (1.0 + sc) + sh


def _ada_kernel(c_ref, w_ref, b_ref, o_ref):
    c = c_ref[...]
    ca = c * jax.nn.sigmoid(c)
    o_ref[0] = jnp.sum(w_ref[0] * ca, axis=0, keepdims=True) + b_ref[0]


def _ada_mod(c, ada_w, ada_b):
    depth, d, n = ada_w.shape
    tn = 768
    return pl.pallas_call(
        _ada_kernel,
        grid=(depth, n // tn),
        in_specs=[
            pl.BlockSpec((d, 1), lambda l, j: (0, 0)),
            pl.BlockSpec((1, d, tn), lambda l, j: (l, 0, j)),
            pl.BlockSpec((1, 1, tn), lambda l, j: (l, 0, j)),
        ],
        out_specs=pl.BlockSpec((1, 1, tn), lambda l, j: (l, 0, j)),
        out_shape=jax.ShapeDtypeStruct((depth, 1, n), F32),
        compiler_params=_params(("arbitrary", "arbitrary")),
        name="ada_mod",
    )(c.reshape(d, 1), ada_w, ada_b.reshape(depth, 1, n))


def _even_mix_kernel(x_ref, mod_ref, nw_ref, win_ref, lnw_ref, lnb_ref, ws_ref, bs_ref, wg_ref, bsc_ref,
                     wout_ref, o_ref, halo_ref, sv_ref, yb_ref):
    tm = x_ref.shape[0]
    i = pl.program_id(0)

    @pl.when(i == 0)
    def _():
        halo_ref[...] = jnp.zeros_like(halo_ref)

    d = D_MODEL
    x = x_ref[...]
    sh, sc, gate = mod_ref[:, 0:d], mod_ref[:, d:2 * d], mod_ref[:, 2 * d:3 * d]
    h = _rms_mod(x, nw_ref[...], sc, sh).astype(BF16)
    p = jnp.dot(h, win_ref[...], preferred_element_type=F32)

    gl = jax.nn.gelu(p[:, :2 * A_WIDTH])
    u = gl[:, :A_WIDTH]
    v = gl[:, A_WIDTH:]
    mu = jnp.mean(v, axis=-1, keepdims=True)
    vc = v - mu
    var = jnp.mean(vc * vc, axis=-1, keepdims=True)
    vn = (vc * lax.rsqrt(var + EPS) * lnw_ref[...] + lnb_ref[...]).astype(BF16)
    row = lax.broadcasted_iota(jnp.int32, (CHUNK, CHUNK), 0)
    col = lax.broadcasted_iota(jnp.int32, (CHUNK, CHUNK), 1)
    for g in range(A_GROUPS):
        w = jnp.where(col <= row, ws_ref[g], 0.0).astype(BF16)
        b = bs_ref[g]
        for c in range(tm // CHUNK):
            blk = vn[c * CHUNK:(c + 1) * CHUNK, g * LANES:(g + 1) * LANES]
            sv_ref[c * CHUNK:(c + 1) * CHUNK, g * LANES:(g + 1) * LANES] = (
                jnp.dot(w, blk, preferred_element_type=F32) + b)
    ya = (u * sv_ref[...]).astype(BF16)

    pb = p[:, 2 * A_WIDTH:]
    ext = jnp.concatenate([halo_ref[...], pb], axis=0)
    halo_ref[...] = pb[tm - POOL_HALO:, :]
    t_glob = i * tm + lax.broadcasted_iota(jnp.int32, (tm, 1), 0)
    s = ext
    width = 1
    for g, win in enumerate(POOL_WINDOWS):
        while width < win:
            s = s + pltpu.roll(s, width, 0)
            width *= 2
        cnt = jnp.minimum(t_glob + 1, win).astype(F32)
        sl = slice(g * B_HD, (g + 1) * B_HD)
        pooled = s[POOL_HALO:, sl] / cnt
        dg = (pooled - pb[:, sl]).astype(BF16)
        yb_ref[:, sl] = jnp.dot(dg, wg_ref[g], preferred_element_type=F32)
    yb = (yb_ref[...] * bsc_ref[...]).astype(BF16)

    mix = (jnp.dot(ya, wout_ref[:A_WIDTH, :], preferred_element_type=F32)
           + jnp.dot(yb, wout_ref[A_WIDTH:, :], preferred_element_type=F32))
    o_ref[...] = x + gate * mix


def _even_mix(x, mod, nw, w_in, ln_w, ln_b, w_s, b_s, w_grp, b_scale, w_out, tm=512):
    s, d = x.shape
    return pl.pallas_call(
        _even_mix_kernel,
        grid=(s // tm,),
        in_specs=[
            pl.BlockSpec((tm, d), lambda i: (i, 0)),
            _const_spec(mod.shape), _const_spec(nw.shape), _const_spec(w_in.shape),
            _const_spec(ln_w.shape), _const_spec(ln_b.shape), _const_spec(w_s.shape), _const_spec(b_s.shape),
            _const_spec(w_grp.shape), _const_spec(b_scale.shape), _const_spec(w_out.shape),
        ],
        out_specs=pl.BlockSpec((tm, d), lambda i: (i, 0)),
        out_shape=jax.ShapeDtypeStruct((s, d), F32),
        scratch_shapes=[pltpu.VMEM((POOL_HALO, B_WIDTH), F32), pltpu.VMEM((tm, A_WIDTH), F32),
                        pltpu.VMEM((tm, B_WIDTH), F32)],
        compiler_params=_params(("arbitrary",)),
        name="even_mix",
    )(x, mod, nw, w_in, ln_w, ln_b, w_s, b_s, w_grp, b_scale, w_out)


def _ffn_kernel(x_ref, mod_ref, nw_ref, wgu_ref, wd_ref, side_ref, o_ref, side_out_ref, *, n_chunks):
    side_out_ref[...] = side_ref[...].astype(BF16)
    d = D_MODEL
    x = x_ref[...]
    sh, sc, gate = mod_ref[:, 3 * d:4 * d], mod_ref[:, 4 * d:5 * d], mod_ref[:, 5 * d:6 * d]
    h = _rms_mod(x, nw_ref[...], sc, sh).astype(BF16)
    ff = wd_ref.shape[0]
    tf = ff // n_chunks
    acc = jnp.zeros(x.shape, F32)
    for f in range(n_chunks):
        g = jnp.dot(h, wgu_ref[:, f * tf:(f + 1) * tf], preferred_element_type=F32)
        u = jnp.dot(h, wgu_ref[:, ff + f * tf:ff + (f + 1) * tf], preferred_element_type=F32)
        act = (g * jax.nn.sigmoid(g) * u).astype(BF16)
        acc = acc + jnp.dot(act, wd_ref[f * tf:(f + 1) * tf, :], preferred_element_type=F32)
    o_ref[...] = x + gate * acc


def _ffn(x, mod, nw, w_gu, w_down, side, tm=512, n_chunks=2):
    s, d = x.shape
    steps = s // tm
    ne, sk, sn = side.shape
    parts = steps // ne
    assert parts * ne == steps and sn % (parts * LANES) == 0
    side_spec = pl.BlockSpec((1, sk, sn // parts), lambda i: (i // parts, 0, i % parts))
    return pl.pallas_call(
        functools.partial(_ffn_kernel, n_chunks=n_chunks),
        grid=(steps,),
        in_specs=[
            pl.BlockSpec((tm, d), lambda i: (i, 0)),
            _const_spec(mod.shape), _const_spec(nw.shape), _const_spec(w_gu.shape, single=True),
            _const_spec(w_down.shape, single=True), side_spec,
        ],
        out_specs=[pl.BlockSpec((tm, d), lambda i: (i, 0)), side_spec],
        out_shape=[jax.ShapeDtypeStruct((s, d), F32), jax.ShapeDtypeStruct(side.shape, BF16)],
        compiler_params=_params(("arbitrary",)),
        name="ffn",
    )(x, mod, nw, w_gu, w_down, side)


def _rope_table_kernel(pos_ref, invf_ref, cos_ref, sin_ref):
    ang = pos_ref[...].astype(F32) * invf_ref[...]
    cos_ref[...] = jnp.cos(ang)
    sin_ref[...] = jnp.sin(ang)


def _rope_tables(positions):
    s = positions.shape[-1]
    half = QK_ROPE // 2
    inv_freq = ROPE_THETA ** (-jnp.arange(0, QK_ROPE, 2, dtype=F32) / QK_ROPE)
    cos_t, sin_t = pl.pallas_call(
        _rope_table_kernel,
        out_shape=(jax.ShapeDtypeStruct((half, s), F32), jax.ShapeDtypeStruct((half, s), F32)),
        name="rope_tables",
    )(positions.reshape(1, s), inv_freq.reshape(half, 1))
    return jnp.tile(cos_t.T, (1, 2)), jnp.tile(sin_t.T, (1, 2)), cos_t, sin_t


def _rope_rot(x):
    n = x.shape[-1]
    lane = lax.broadcasted_iota(jnp.int32, x.shape, 1)
    half = QK_ROPE // 2
    fwd = pltpu.roll(x, half, 1)
    bwd = pltpu.roll(x, n - half, 1)
    return jnp.where((lane % QK_ROPE) < half, -bwd, fwd)


def _odd_prep_kernel(x_ref, mod_ref, nw_ref, win_ref, wkpe_ref, cw_ref, qan_ref, wuqt_ref, kvn_ref, wk_ref, wvt_ref,
                     qnw_ref, knw_ref, cos_ref, sin_ref, cost_ref, sint_ref, yc_ref, qt_ref, k_ref, vt_ref,
                     halo_ref):
    tm = x_ref.shape[0]
    i = pl.program_id(0)

    @pl.when(i == 0)
    def _():
        halo_ref[...] = jnp.zeros_like(halo_ref)

    d = D_MODEL
    x = x_ref[...]
    sh, sc = mod_ref[:, 0:d], mod_ref[:, d:2 * d]
    h = _rms_mod(x, nw_ref[...], sc, sh).astype(BF16)
    p = jnp.dot(h, win_ref[...], preferred_element_type=F32)
    kpe = jnp.dot(h, wkpe_ref[...], preferred_element_type=F32)[:, :QK_ROPE]

    cw = C_WIDTH
    bg, cg, hh = p[:, 0:cw], p[:, cw:2 * cw], p[:, 2 * cw:3 * cw]
    z = cg * hh
    ext = jnp.concatenate([halo_ref[...], z], axis=0)
    halo_ref[...] = z[tm - CONV_HALO:, :]
    z1 = pltpu.roll(ext, 1, 0)[CONV_HALO:, :]
    z2 = pltpu.roll(ext, 2, 0)[CONV_HALO:, :]
    conv = cw_ref[0:1, :] * z2 + cw_ref[1:2, :] * z1 + cw_ref[2:3, :] * z
    yc_ref[...] = (bg * conv).astype(BF16)

    o0 = 3 * cw
    cq = p[:, o0:o0 + Q_LORA]
    ckv = p[:, o0 + Q_LORA:o0 + Q_LORA + KV_LORA]
    cqn = cq * lax.rsqrt(jnp.mean(cq * cq, axis=-1, keepdims=True) + EPS) * qan_ref[...]
    ckvn = ckv * lax.rsqrt(jnp.mean(ckv * ckv, axis=-1, keepdims=True) + EPS) * kvn_ref[...]
    cqn_t = cqn.T.astype(BF16)
    ckvn_t = ckvn.T.astype(BF16)
    q_t = jnp.dot(wuqt_ref[...], cqn_t, preferred_element_type=F32)
    v_t = jnp.dot(wvt_ref[...], ckvn_t, preferred_element_type=F32)
    kn_all = jnp.dot(ckvn.astype(BF16), wk_ref[...], preferred_element_type=F32)

    nh = MLA_HEADS
    half = QK_ROPE // 2
    sm_scale = QK_HD ** -0.5
    cos_t, sin_t = cost_ref[...], sint_ref[...]
    kn_w, kr_w = knw_ref[:, :QK_NOPE], knw_ref[:, QK_NOPE:]
    kr = kpe * kr_w
    kr = kr * cos_ref[...] + _rope_rot(kr) * sin_ref[...]
    kpe_ss = jnp.sum(kpe * kpe, axis=-1, keepdims=True)
    zrows = jnp.zeros((QK_PAD - QK_HD, tm), F32)
    zpad = jnp.zeros((tm, QK_PAD - QK_HD), F32)
    ones_rows = (lax.broadcasted_iota(jnp.int32, (V_PAD - V_HD, tm), 0) == 0).astype(BF16)
    for hd in range(nh):
        qh = q_t[hd * QK_HD:(hd + 1) * QK_HD, :]
        q_inv = lax.rsqrt(jnp.sum(qh * qh, axis=0, keepdims=True) / QK_HD + EPS) * (sm_scale * LOG2E)
        qw = qh * qnw_ref[...]
        x1, x2 = qw[QK_NOPE:QK_NOPE + half, :], qw[QK_NOPE + half:, :]
        qt_ref[hd] = jnp.concatenate(
            [qw[:QK_NOPE, :] * q_inv, (x1 * cos_t - x2 * sin_t) * q_inv, (x2 * cos_t + x1 * sin_t) * q_inv, zrows],
            axis=0).astype(BF16)
        kn = kn_all[:, hd * QK_NOPE:(hd + 1) * QK_NOPE]
        k_ss = jnp.sum(kn * kn, axis=-1, keepdims=True) + kpe_ss
        k_inv = lax.rsqrt(k_ss / QK_HD + EPS)
        k_ref[hd] = jnp.concatenate([kn * k_inv * kn_w, kr * k_inv, zpad], axis=1).astype(BF16)
        vt_ref[hd, :V_HD, :] = v_t[hd * V_HD:(hd + 1) * V_HD, :].astype(BF16)
        vt_ref[hd, V_HD:, :] = ones_rows


def _odd_prep(x, mod, nw, w_in, w_kpe, conv_w, q_a_norm, w_uq_t, kv_norm, w_k, w_v_t, q_norm_col, k_norm_w, cos, sin,
              cos_t, sin_t, tm=1024):
    s, d = x.shape
    nh = MLA_HEADS
    half = QK_ROPE // 2
    consts = [mod, nw, w_in, w_kpe, conv_w, q_a_norm, w_uq_t, kv_norm, w_k, w_v_t, q_norm_col, k_norm_w]
    return pl.pallas_call(
        _odd_prep_kernel,
        grid=(s // tm,),
        in_specs=[pl.BlockSpec((tm, d), lambda i: (i, 0))] + [_const_spec(a.shape) for a in consts] + [
            pl.BlockSpec((tm, QK_ROPE), lambda i: (i, 0)), pl.BlockSpec((tm, QK_ROPE), lambda i: (i, 0)),
            pl.BlockSpec((half, tm), lambda i: (0, i)), pl.BlockSpec((half, tm), lambda i: (0, i))],
        out_specs=[
            pl.BlockSpec((tm, C_WIDTH), lambda i: (i, 0)),
            pl.BlockSpec((nh, QK_PAD, tm), lambda i: (0, 0, i)),
            pl.BlockSpec((nh, tm, QK_PAD), lambda i: (0, i, 0)),
            pl.BlockSpec((nh, V_PAD, tm), lambda i: (0, 0, i)),
        ],
        out_shape=[
            jax.ShapeDtypeStruct((s, C_WIDTH), BF16),
            jax.ShapeDtypeStruct((nh, QK_PAD, s), BF16),
            jax.ShapeDtypeStruct((nh, s, QK_PAD), BF16),
            jax.ShapeDtypeStruct((nh, V_PAD, s), BF16),
        ],
        scratch_shapes=[pltpu.VMEM((CONV_HALO, C_WIDTH), F32)],
        compiler_params=_params(("arbitrary",)),
        name="odd_prep",
    )(x, *consts, cos, sin, cos_t, sin_t)


def _attn_kernel(qt_ref, k_ref, vt_ref, o_ref, s0_ref, s1_ref, bm0_ref, bm1_ref, m_ref, acc_ref, *, tq, tk):
    qi = pl.program_id(1)
    m_ref[...] = jnp.full_like(m_ref, MASK_VALUE)
    acc_ref[...] = jnp.zeros_like(acc_ref)

    def scores(j, s_ref, bm_ref, diag_offset=None):
        start = pl.multiple_of(j * tk, tk)
        s = jnp.dot(k_ref[0, pl.ds(start, tk), :], qt_ref[0], preferred_element_type=F32)
        if diag_offset is not None:
            key = diag_offset + lax.broadcasted_iota(jnp.int32, (tk, tq), 0)
            qry = lax.broadcasted_iota(jnp.int32, (tk, tq), 1)
            s = jnp.where(key <= qry, s, MASK_VALUE)
        s_ref[...] = s
        bm_ref[...] = jnp.max(s, axis=0, keepdims=True)

    def consume(j, s_ref, bm_ref):
        m_prev = m_ref[...]
        m_new = jnp.maximum(m_prev, bm_ref[...])
        alpha = jnp.exp2(m_prev - m_new)
        p = jnp.exp2((s_ref[...] - m_new).astype(BF16))
        start = pl.multiple_of(j * tk, tk)
        vt = vt_ref[0, :, pl.ds(start, tk)]
        acc_ref[...] = alpha * acc_ref[...] + jnp.dot(vt, p, preferred_element_type=F32)
        m_ref[...] = m_new

    def diagonal(jd):
        scores(jd + 1, s1_ref, bm1_ref, diag_offset=tk)
        consume(jd, s0_ref, bm0_ref)
        consume(jd + 1, s1_ref, bm1_ref)

    @pl.when(qi > 0)
    def _():
        scores(0, s0_ref, bm0_ref)

    def body(t, carry):
        j = 2 * t
        scores(j + 1, s1_ref, bm1_ref)
        consume(j, s0_ref, bm0_ref)
        scores(j + 2, s0_ref, bm0_ref)
        consume(j + 1, s1_ref, bm1_ref)
        return carry

    lax.fori_loop(0, qi - 1, body, 0)

    @pl.when(qi > 0)
    def _():
        j = 2 * (qi - 1)
        scores(j + 1, s1_ref, bm1_ref)
        consume(j, s0_ref, bm0_ref)
        scores(j + 2, s0_ref, bm0_ref, diag_offset=0)
        consume(j + 1, s1_ref, bm1_ref)
        diagonal(j + 2)

    @pl.when(qi == 0)
    def _():
        scores(0, s0_ref, bm0_ref, diag_offset=0)
        diagonal(0)

    acc = acc_ref[...]
    out_t = acc[:V_HD, :] / acc[V_HD:V_HD + 1, :]
    o_ref[...] = out_t.T.astype(o_ref.dtype)


def _attention(qt, k, vt, tq=1024, tk=512):
    nh, _, s = qt.shape
    assert tq == 2 * tk
    return pl.pallas_call(
        functools.partial(_attn_kernel, tq=tq, tk=tk),
        grid=(nh, s // tq),
        in_specs=[
            pl.BlockSpec((1, QK_PAD, tq), lambda h, i: (h, 0, i)),
            pl.BlockSpec((1, s, QK_PAD), lambda h, i: (h, 0, 0)),
            pl.BlockSpec((1, V_PAD, s), lambda h, i: (h, 0, 0)),
        ],
        out_specs=pl.BlockSpec((tq, V_HD), lambda h, i: (i, h)),
        out_shape=jax.ShapeDtypeStruct((s, nh * V_HD), BF16),
        scratch_shapes=[pltpu.VMEM((tk, tq), F32), pltpu.VMEM((tk, tq), F32), pltpu.VMEM((1, tq), F32),
                        pltpu.VMEM((1, tq), F32), pltpu.VMEM((1, tq), F32), pltpu.VMEM((V_PAD, tq), F32)],
        compiler_params=_params(("arbitrary", "arbitrary")),
        name="mla_attention",
    )(qt, k, vt)


def _pack_bf16_pairs(x):
    n = x.shape[1] // 2
    lo = pltpu.bitcast(x[:, :n].astype(BF16).astype(F32), jnp.uint32)
    hi = pltpu.bitcast(x[:, n:].astype(BF16).astype(F32), jnp.uint32)
    return (lo >> 16) | hi


def _unpack_bf16_pairs(p):
    lo = pltpu.bitcast(p << 16, F32)
    hi = pltpu.bitcast(p & jnp.uint32(0xFFFF0000), F32)
    return jnp.concatenate([lo, hi], axis=1)


def _odd_out_kernel(x_ref, yc_ref, yd_ref, mod_ref, nw_ref, wout_ref, wr_ref, before_ref, x_out_ref, h_ref, rw_ref,
                    ridx_ref, cnt_ref):
    d = D_MODEL
    x = x_ref[...]
    gate_m = mod_ref[:, 2 * d:3 * d]
    sh, sc = mod_ref[:, 3 * d:4 * d], mod_ref[:, 4 * d:5 * d]
    mix = (jnp.dot(yc_ref[...], wout_ref[:C_WIDTH, :], preferred_element_type=F32)
           + jnp.dot(yd_ref[...], wout_ref[C_WIDTH:, :], preferred_element_type=F32))
    x1 = x + gate_m * mix
    x_out_ref[...] = x1
    h = _rms_mod(x1, nw_ref[...], sc, sh)
    h_ref[...] = _pack_bf16_pairs(h)

    h_hi = h.astype(BF16)
    h_lo = (h - h_hi.astype(F32)).astype(BF16)
    hw = jnp.dot(h_hi, wr_ref[...], preferred_element_type=F32)
    logits = hw[:, :LANES] + (hw[:, LANES:] + jnp.dot(h_lo, wr_ref[:, :LANES], preferred_element_type=F32))
    lane = lax.broadcasted_iota(jnp.int32, logits.shape, 1)
    logits = jnp.where(lane < N_EXPERTS, logits, -jnp.inf)
    m1 = jnp.max(logits, axis=-1, keepdims=True)
    i1 = jnp.min(jnp.where(logits == m1, lane, LANES), axis=-1, keepdims=True)
    rest = jnp.where(lane == i1, -jnp.inf, logits)
    m2 = jnp.max(rest, axis=-1, keepdims=True)
    i2 = jnp.min(jnp.where(rest == m2, lane, LANES), axis=-1, keepdims=True)
    e2 = jnp.exp(m2 - m1)
    w1 = 1.0 / (1.0 + e2)
    w2 = e2 / (1.0 + e2)
    rw_ref[...] = jnp.where(lane == 0, w1, jnp.where(lane == 1, w2, 0.0))

    i = pl.program_id(0)

    @pl.when(i == 0)
    def _():
        cnt_ref[...] = jnp.zeros_like(cnt_ref)

    onehot = jnp.where((lane == i1) | (lane == i2), 1.0, 0.0)
    prefix = jnp.dot(before_ref[...], onehot.astype(BF16), preferred_element_type=F32) + cnt_ref[...]
    r1 = jnp.sum(jnp.where(lane == i1, prefix, 0.0), axis=-1, keepdims=True).astype(jnp.int32)
    r2 = jnp.sum(jnp.where(lane == i2, prefix, 0.0), axis=-1, keepdims=True).astype(jnp.int32)
    cnt_ref[...] += jnp.sum(onehot, axis=0, keepdims=True)
    ridx_ref[...] = jnp.where(lane == 0, i1, jnp.where(lane == 1, i2, jnp.where(lane == 2, r1,
                                                                                 jnp.where(lane == 3, r2, 0))))


def _odd_out(x, yc, yd, mod, nw, w_out, w_router, tm=1024):
    s, d = x.shape
    return pl.pallas_call(
        _odd_out_kernel,
        grid=(s // tm,),
        in_specs=[
            pl.BlockSpec((tm, d), lambda i: (i, 0)),
            pl.BlockSpec((tm, C_WIDTH), lambda i: (i, 0)),
            pl.BlockSpec((tm, MLA_HEADS * V_HD), lambda i: (i, 0)),
            _const_spec(mod.shape), _const_spec(nw.shape), _const_spec(w_out.shape), _const_spec(w_router.shape),
            _const_spec((tm, tm)),
        ],
        out_specs=[
            pl.BlockSpec((tm, d), lambda i: (i, 0)),
            pl.BlockSpec((tm, d // 2), lambda i: (i, 0)),
            pl.BlockSpec((tm, LANES), lambda i: (i, 0)),
            pl.BlockSpec((tm, LANES), lambda i: (i, 0)),
            pl.BlockSpec((1, LANES), lambda i: (0, 0)),
        ],
        out_shape=[
            jax.ShapeDtypeStruct((s, d), F32),
            jax.ShapeDtypeStruct((s, d // 2), jnp.uint32),
            jax.ShapeDtypeStruct((s, LANES), F32),
            jax.ShapeDtypeStruct((s, LANES), jnp.int32),
            jax.ShapeDtypeStruct((1, LANES), F32),
        ],
        compiler_params=_params(("arbitrary",)),
        name="odd_out_router",
    )(x, yc, yd, mod, nw, w_out, w_router, jnp.tril(jnp.ones((tm, tm), BF16), -1))


def _sc_workers():
    info = plsc.get_sparse_core_info()
    return info.num_cores, info.num_cores * info.num_subcores


def _sc_scatter_rows(x, idx0, idx1, out_rows):
    n, w = x.shape
    nc, nw = _sc_workers()
    per_w = n // nw
    nch = per_w // SC_ROWS
    mesh = plsc.VectorSubcoreMesh(core_axis_name="c", subcore_axis_name="s")

    @functools.partial(
        pl.kernel, mesh=mesh, out_type=jax.ShapeDtypeStruct((out_rows, w), x.dtype),
        scratch_types=[pltpu.VMEM((nch, SC_ROWS), jnp.int32), pltpu.VMEM((nch, SC_ROWS), jnp.int32),
                       pltpu.VMEM((SC_ROWS, w), x.dtype), pltpu.SemaphoreType.DMA],
        name="moe_dispatch")
    def scatter(x_hbm, i0_hbm, i1_hbm, out_hbm, i0_v, i1_v, rows_v, sem):
        wid = lax.axis_index("s") * nc + lax.axis_index("c")
        pltpu.sync_copy(i0_hbm.at[wid], i0_v)
        pltpu.sync_copy(i1_hbm.at[wid], i1_v)

        def body(c, carry):
            pltpu.sync_copy(x_hbm.at[pl.ds(wid * per_w + c * SC_ROWS, SC_ROWS)], rows_v)
            pltpu.async_copy(rows_v, out_hbm.at[i0_v.at[c]], sem).wait()
            pltpu.async_copy(rows_v, out_hbm.at[i1_v.at[c]], sem).wait()
            return carry

        lax.fori_loop(0, nch, body, 0)

    return scatter(x, idx0.reshape(nw, nch, SC_ROWS), idx1.reshape(nw, nch, SC_ROWS))


def _sc_gather_rows(table, idx):
    _, w = table.shape
    b = idx.shape[0]
    nc, nw = _sc_workers()
    per_w = b // nw
    nch = per_w // SC_ROWS
    mesh = plsc.VectorSubcoreMesh(core_axis_name="c", subcore_axis_name="s")

    @functools.partial(
        pl.kernel, mesh=mesh, out_type=jax.ShapeDtypeStruct((b, w), table.dtype),
        scratch_types=[pltpu.VMEM((nch, SC_ROWS), jnp.int32), pltpu.VMEM((SC_ROWS, w), table.dtype),
                       pltpu.SemaphoreType.DMA],
        name="moe_combine_gather")
    def gather(table_hbm, idx_hbm, out_hbm, idx_v, rows_v, sem):
        wid = lax.axis_index("s") * nc + lax.axis_index("c")
        pltpu.sync_copy(idx_hbm.at[wid], idx_v)

        def body(c, carry):
            pltpu.async_copy(table_hbm.at[idx_v.at[c]], rows_v, sem).wait()
            pltpu.sync_copy(rows_v, out_hbm.at[pl.ds(wid * per_w + c * SC_ROWS, SC_ROWS)])
            return carry

        lax.fori_loop(0, nch, body, 0)

    return gather(table, idx.reshape(nw, nch, SC_ROWS))


def _moe_kernel(te_ref, nv_ref, xs_ref, wg_ref, wu_ref, wd_ref, ys_ref, x_scr, acc_ref):
    j = pl.program_id(0)
    f = pl.program_id(1)
    nf = pl.num_programs(1)

    @pl.when(j < nv_ref[0])
    def _():
        @pl.when(f == 0)
        def _():
            x_scr[...] = _unpack_bf16_pairs(xs_ref[...]).astype(BF16)

        h = x_scr[...]
        g = jnp.dot(h, wg_ref[0], preferred_element_type=F32)
        u = jnp.dot(h, wu_ref[0], preferred_element_type=F32)
        act = (g * jax.nn.sigmoid(g) * u).astype(BF16)
        y = jnp.dot(act, wd_ref[0], preferred_element_type=F32)

        @pl.when(f == 0)
        def _():
            acc_ref[...] = y

        @pl.when(f > 0)
        def _():
            acc_ref[...] += y

        @pl.when(f == nf - 1)
        def _():
            ys_ref[...] = _pack_bf16_pairs(acc_ref[...])


def _moe_grouped(xs, tile_expert, n_valid, w_gu, w_down, tm, tf=1792):
    p_rows, dh = xs.shape
    d = 2 * dh
    ne, _, ff2 = w_gu.shape
    ff = ff2 // 2
    nf = ff // tf

    def tile(j, nv):
        return jnp.minimum(j, nv[0] - 1)

    def chunk(j, f, nv):
        return jnp.where(j < nv[0], f, nf - 1)

    grid_spec = pltpu.PrefetchScalarGridSpec(
        num_scalar_prefetch=2,
        grid=(p_rows // tm, nf),
        in_specs=[
            pl.BlockSpec((tm, dh), lambda j, f, te, nv: (tile(j, nv), 0)),
            pl.BlockSpec((1, d, tf), lambda j, f, te, nv: (te[tile(j, nv)], 0, chunk(j, f, nv))),
            pl.BlockSpec((1, d, tf), lambda j, f, te, nv: (te[tile(j, nv)], 0, nf + chunk(j, f, nv))),
            pl.BlockSpec((1, tf, d), lambda j, f, te, nv: (te[tile(j, nv)], chunk(j, f, nv), 0)),
        ],
        out_specs=pl.BlockSpec((tm, dh), lambda j, f, te, nv: (tile(j, nv), 0)),
        scratch_shapes=[pltpu.VMEM((tm, d), BF16), pltpu.VMEM((tm, d), F32)],
    )
    return pl.pallas_call(
        _moe_kernel,
        grid_spec=grid_spec,
        out_shape=jax.ShapeDtypeStruct((p_rows, dh), jnp.uint32),
        compiler_params=_params(("arbitrary", "arbitrary")),
        name="moe_experts",
    )(tile_expert, n_valid, xs, w_gu, w_gu, w_down)


def _moe_combine_kernel(x_ref, y0_ref, y1_ref, rw_ref, mod_ref, o_ref):
    d = D_MODEL
    w1 = rw_ref[:, 0:1]
    w2 = rw_ref[:, 1:2]
    y = w1 * _unpack_bf16_pairs(y0_ref[...]) + w2 * _unpack_bf16_pairs(y1_ref[...])
    o_ref[...] = x_ref[...] + mod_ref[:, 5 * d:6 * d] * y


def _moe_combine(x, yg, rw, mod, tm=512):
    s, d = x.shape
    nb = s // tm
    return pl.pallas_call(
        _moe_combine_kernel,
        grid=(nb,),
        in_specs=[
            pl.BlockSpec((tm, d), lambda i: (i, 0)),
            pl.BlockSpec((tm, d // 2), lambda i: (i, 0)),
            pl.BlockSpec((tm, d // 2), lambda i: (nb + i, 0)),
            pl.BlockSpec((tm, LANES), lambda i: (i, 0)),
            _const_spec(mod.shape),
        ],
        out_specs=pl.BlockSpec((tm, d), lambda i: (i, 0)),
        out_shape=jax.ShapeDtypeStruct((s, d), F32),
        compiler_params=_params(("arbitrary",)),
        name="moe_combine",
    )(x, yg, yg, rw, mod)


def _moe_sparse(x, h_packed, rw, ridx, counts, mod, w_gu, w_down, tm=MOE_TILE):
    s = x.shape[0]
    ne = w_gu.shape[0]
    n_tiles = (2 * s) // tm + ne
    cnt = counts[0, :ne].astype(jnp.int32)
    padded = ((cnt + tm - 1) // tm) * tm
    ends = jnp.cumsum(padded)
    offs = ends - padded
    experts = jnp.arange(ne, dtype=jnp.int32)
    off_of = lambda e: jnp.sum(jnp.where(e[:, None] == experts[None, :], offs[None, :], 0), axis=1)
    pos0 = off_of(ridx[:, 0]) + ridx[:, 2]
    pos1 = off_of(ridx[:, 1]) + ridx[:, 3]
    tile_start = jnp.arange(n_tiles, dtype=jnp.int32) * tm
    tile_expert = jnp.minimum(jnp.sum(tile_start[:, None] >= ends[None, :], axis=1), ne - 1).astype(jnp.int32)
    n_valid = (ends[-1] // tm).reshape(1).astype(jnp.int32)
    xs = _sc_scatter_rows(h_packed, pos0, pos1, n_tiles * tm)
    ys = _moe_grouped(xs, tile_expert, n_valid, w_gu, w_down, tm)
    yg = _sc_gather_rows(ys, jnp.concatenate([pos0, pos1]))
    return _moe_combine(x, yg, rw, mod)


def kernel(x, c, positions, norm_mix_w, norm_ffn_w, ada_w, ada_b, e_w_in, a_ln_w, a_ln_b, a_w_s, a_b_s, b_w_grp,
           b_scale, e_w_out, ffn_w_gu, ffn_w_down, o_w_in, c_conv_w, q_a_norm, w_uq, kv_norm, w_ukv, q_norm_w,
           k_norm_w, o_w_out, router_w, moe_w_gu, moe_w_down):
    bsz, s, d = x.shape
    assert bsz == 1 and d == D_MODEL
    depth = ada_w.shape[0]
    nh = MLA_HEADS
    xs = x.reshape(s, d)
    mod = _ada_mod(c, ada_w, ada_b)
    cos, sin, cos_t, sin_t = _rope_tables(positions)
    row = lambda a: a.reshape(1, -1)

    for layer in range(depth):
        i = layer // 2
        m = mod[layer]
        if layer % 2 == 0:
            xs = _even_mix(
                xs, m, row(norm_mix_w[layer]), e_w_in[i].astype(BF16), row(a_ln_w[i]), row(a_ln_b[i]), a_w_s[i],
                a_b_s[i].reshape(A_GROUPS, CHUNK, 1), b_w_grp[i].astype(BF16), row(b_scale[i]),
                e_w_out[i].astype(BF16))
            has_next = layer + 1 < depth
            side = moe_w_gu[i] if has_next else jnp.zeros((s // 512, 8, LANES), F32)
            xs, side_bf16 = _ffn(xs, m, row(norm_ffn_w[layer]), ffn_w_gu[i].astype(BF16),
                                 ffn_w_down[i].astype(BF16), side)
            moe_w_gu_bf16 = side_bf16 if has_next else None
        else:
            o2 = 3 * C_WIDTH + Q_LORA + KV_LORA
            w_in = o_w_in[i][:, :o2].astype(BF16)
            w_kpe = jnp.pad(o_w_in[i][:, o2:], ((0, 0), (0, LANES - QK_ROPE))).astype(BF16)
            wkv = w_ukv[i].reshape(KV_LORA, nh, QK_NOPE + V_HD)
            w_k = wkv[:, :, :QK_NOPE].reshape(KV_LORA, nh * QK_NOPE).astype(BF16)
            w_v_t = wkv[:, :, QK_NOPE:].reshape(KV_LORA, nh * V_HD).T.astype(BF16)
            yc, qt, k, vt = _odd_prep(
                xs, m, row(norm_mix_w[layer]), w_in, w_kpe, c_conv_w[i], row(q_a_norm[i]), w_uq[i].T.astype(BF16),
                row(kv_norm[i]), w_k, w_v_t, q_norm_w[i].reshape(QK_HD, 1), row(k_norm_w[i]), cos, sin, cos_t, sin_t)
            yd = _attention(qt, k, vt)
            wr = jnp.pad(router_w[i], ((0, 0), (0, LANES - N_EXPERTS)))
            wr_hi = lax.reduce_precision(wr, exponent_bits=8, mantissa_bits=7)
            w_router = jnp.concatenate([wr_hi, wr - wr_hi], axis=1).astype(BF16)
            xs, hp, rw, ridx, counts = _odd_out(xs, yc, yd, m, row(norm_ffn_w[layer]), o_w_out[i].astype(BF16),
                                                w_router)
            xs = _moe_sparse(xs, hp, rw, ridx, counts, m, moe_w_gu_bf16, moe_w_down[i].astype(BF16))
    return xs.reshape(bsz, s, d)
```

```python
import functools

import jax
import jax.numpy as jnp
from jax import lax
from jax.experimental import pallas as pl
from jax.experimental.pallas import tpu as pltpu
from jax.experimental.pallas import tpu_sc as plsc

D_MODEL = 1024
SEQ = 16384
EPS = 1e-6
CHUNK = 128
A_WIDTH = 512
A_GROUPS = 4
B_WIDTH = 512
POOL_WINDOWS = (2, 4, 8, 16)
B_HD = 128
C_WIDTH = 512
MLA_HEADS = 4
Q_LORA = 256
KV_LORA = 256
QK_NOPE = 128
QK_ROPE = 64
QK_HD = QK_NOPE + QK_ROPE
V_HD = 128
ROPE_THETA = 10000.0
D_FF = 2816
N_EXPERTS = 8
D_FF_EXPERT = 3584

LANES = 128
SUBLANES = 8
POOL_HALO = 16
CONV_HALO = 8
QK_PAD = 256
V_PAD = 144
LOG2E = 1.4426950408889634
MASK_VALUE = -1e30
MOE_TILE = 512
COMBINE_PARTS = 2
SC_ROWS = 64
VMEM_LIMIT = 56 * 1024 * 1024

F32 = jnp.float32
BF16 = jnp.bfloat16


def _params(sem, vmem=VMEM_LIMIT, flags=None):
    return pltpu.CompilerParams(dimension_semantics=sem, vmem_limit_bytes=vmem, flags=flags)


def _const_spec(shape, single=False):
    nd = len(shape)
    return pl.BlockSpec(shape, lambda *_: (0,) * nd, pipeline_mode=pl.Buffered(1) if single else None)


def _rms_mod(x, nw, sc, sh):
    ms = jnp.mean(x * x, axis=-1, keepdims=True)
    return (x * lax.rsqrt(ms + EPS)) * nw * (1.0 + sc) + sh


def _ada_kernel(c_ref, w_ref, b_ref, o_ref):
    c = c_ref[...]
    ca = c * jax.nn.sigmoid(c)
    o_ref[0] = jnp.sum(w_ref[0] * ca, axis=0, keepdims=True) + b_ref[0]


def _ada_mod(c, ada_w, ada_b):
    depth, d, n = ada_w.shape
    tn = 768
    return pl.pallas_call(
        _ada_kernel,
        grid=(depth, n // tn),
        in_specs=[
            pl.BlockSpec((d, 1), lambda l, j: (0, 0)),
            pl.BlockSpec((1, d, tn), lambda l, j: (l, 0, j)),
            pl.BlockSpec((1, 1, tn), lambda l, j: (l, 0, j)),
        ],
        out_specs=pl.BlockSpec((1, 1, tn), lambda l, j: (l, 0, j)),
        out_shape=jax.ShapeDtypeStruct((depth, 1, n), F32),
        compiler_params=_params(("arbitrary", "arbitrary")),
        name="ada_mod",
    )(c.reshape(d, 1), ada_w, ada_b.reshape(depth, 1, n))


def _even_mix_kernel(x_ref, mod_ref, nw_ref, win_ref, lnw_ref, lnb_ref, ws_ref, bs_ref, wg_ref, bsc_ref,
                     wout_ref, o_ref, halo_ref, sv_ref, yb_ref):
    tm = x_ref.shape[0]
    i = pl.program_id(0)

    @pl.when(i == 0)
    def _():
        halo_ref[...] = jnp.zeros_like(halo_ref)

    d = D_MODEL
    x = x_ref[...]
    sh, sc, gate = mod_ref[:, 0:d], mod_ref[:, d:2 * d], mod_ref[:, 2 * d:3 * d]
    h = _rms_mod(x, nw_ref[...], sc, sh).astype(BF16)
    p = jnp.dot(h, win_ref[...], preferred_element_type=F32)

    gl = jax.nn.gelu(p[:, :2 * A_WIDTH])
    u = gl[:, :A_WIDTH]
    v = gl[:, A_WIDTH:]
    mu = jnp.mean(v, axis=-1, keepdims=True)
    vc = v - mu
    var = jnp.mean(vc * vc, axis=-1, keepdims=True)
    vn = (vc * lax.rsqrt(var + EPS) * lnw_ref[...] + lnb_ref[...]).astype(BF16)
    row = lax.broadcasted_iota(jnp.int32, (CHUNK, CHUNK), 0)
    col = lax.broadcasted_iota(jnp.int32, (CHUNK, CHUNK), 1)
    for g in range(A_GROUPS):
        w = jnp.where(col <= row, ws_ref[g], 0.0).astype(BF16)
        b = bs_ref[g]
        for c in range(tm // CHUNK):
            blk = vn[c * CHUNK:(c + 1) * CHUNK, g * LANES:(g + 1) * LANES]
            sv_ref[c * CHUNK:(c + 1) * CHUNK, g * LANES:(g + 1) * LANES] = (
                jnp.dot(w, blk, preferred_element_type=F32) + b)
    ya = (u * sv_ref[...]).astype(BF16)

    pb = p[:, 2 * A_WIDTH:]
    ext = jnp.concatenate([halo_ref[...], pb], axis=0)
    halo_ref[...] = pb[tm - POOL_HALO:, :]
    t_glob = i * tm + lax.broadcasted_iota(jnp.int32, (tm, 1), 0)
    s = ext
    width = 1
    for g, win in enumerate(POOL_WINDOWS):
        while width < win:
            s = s + pltpu.roll(s, width, 0)
            width *= 2
        cnt = jnp.minimum(t_glob + 1, win).astype(F32)
        sl = slice(g * B_HD, (g + 1) * B_HD)
        pooled = s[POOL_HALO:, sl] / cnt
        dg = (pooled - pb[:, sl]).astype(BF16)
        yb_ref[:, sl] = jnp.dot(dg, wg_ref[g], preferred_element_type=F32)
    yb = (yb_ref[...] * bsc_ref[...]).astype(BF16)

    mix = (jnp.dot(ya, wout_ref[:A_WIDTH, :], preferred_element_type=F32)
           + jnp.dot(yb, wout_ref[A_WIDTH:, :], preferred_element_type=F32))
    o_ref[...] = x + gate * mix


def _even_mix(x, mod, nw, w_in, ln_w, ln_b, w_s, b_s, w_grp, b_scale, w_out, tm=512):
    s, d = x.shape
    return pl.pallas_call(
        _even_mix_kernel,
        grid=(s // tm,),
        in_specs=[
            pl.BlockSpec((tm, d), lambda i: (i, 0)),
            _const_spec(mod.shape), _const_spec(nw.shape), _const_spec(w_in.shape),
            _const_spec(ln_w.shape), _const_spec(ln_b.shape), _const_spec(w_s.shape), _const_spec(b_s.shape),
            _const_spec(w_grp.shape), _const_spec(b_scale.shape), _const_spec(w_out.shape),
        ],
        out_specs=pl.BlockSpec((tm, d), lambda i: (i, 0)),
        out_shape=jax.ShapeDtypeStruct((s, d), F32),
        scratch_shapes=[pltpu.VMEM((POOL_HALO, B_WIDTH), F32), pltpu.VMEM((tm, A_WIDTH), F32),
                        pltpu.VMEM((tm, B_WIDTH), F32)],
        compiler_params=_params(("arbitrary",)),
        name="even_mix",
    )(x, mod, nw, w_in, ln_w, ln_b, w_s, b_s, w_grp, b_scale, w_out)


def _ffn_kernel(x_ref, mod_ref, nw_ref, wgu_ref, wd_ref, side_ref, o_ref, side_out_ref, *, n_chunks):
    side_out_ref[...] = side_ref[...].astype(BF16)
    d = D_MODEL
    x = x_ref[...]
    sh, sc, gate = mod_ref[:, 3 * d:4 * d], mod_ref[:, 4 * d:5 * d], mod_ref[:, 5 * d:6 * d]
    h = _rms_mod(x, nw_ref[...], sc, sh).astype(BF16)
    ff = wd_ref.shape[0]
    tf = ff // n_chunks
    acc = jnp.zeros(x.shape, F32)
    for f in range(n_chunks):
        g = jnp.dot(h, wgu_ref[:, f * tf:(f + 1) * tf], preferred_element_type=F32)
        u = jnp.dot(h, wgu_ref[:, ff + f * tf:ff + (f + 1) * tf], preferred_element_type=F32)
        act = (g * jax.nn.sigmoid(g) * u).astype(BF16)
        acc = acc + jnp.dot(act, wd_ref[f * tf:(f + 1) * tf, :], preferred_element_type=F32)
    o_ref[...] = x + gate * acc


def _ffn(x, mod, nw, w_gu, w_down, side, tm=512, n_chunks=11):
    s, d = x.shape
    steps = s // tm
    ne, sk, sn = side.shape
    parts = steps // ne
    assert parts * ne == steps and sn % (parts * LANES) == 0
    side_spec = pl.BlockSpec((1, sk, sn // parts), lambda i: (i // parts, 0, i % parts))
    return pl.pallas_call(
        functools.partial(_ffn_kernel, n_chunks=n_chunks),
        grid=(steps,),
        in_specs=[
            pl.BlockSpec((tm, d), lambda i: (i, 0)),
            _const_spec(mod.shape), _const_spec(nw.shape), _const_spec(w_gu.shape, single=True),
            _const_spec(w_down.shape, single=True), side_spec,
        ],
        out_specs=[pl.BlockSpec((tm, d), lambda i: (i, 0)), side_spec],
        out_shape=[jax.ShapeDtypeStruct((s, d), F32), jax.ShapeDtypeStruct(side.shape, BF16)],
        compiler_params=_params(("arbitrary",)),
        name="ffn",
    )(x, mod, nw, w_gu, w_down, side)


def _rope_table_kernel(pos_ref, invf_ref, cos_ref, sin_ref):
    ang = pos_ref[...].astype(F32) * invf_ref[...]
    cos_ref[...] = jnp.cos(ang)
    sin_ref[...] = jnp.sin(ang)


def _rope_tables(positions):
    s = positions.shape[-1]
    half = QK_ROPE // 2
    inv_freq = ROPE_THETA ** (-jnp.arange(0, QK_ROPE, 2, dtype=F32) / QK_ROPE)
    cos_t, sin_t = pl.pallas_call(
        _rope_table_kernel,
        out_shape=(jax.ShapeDtypeStruct((half, s), F32), jax.ShapeDtypeStruct((half, s), F32)),
        name="rope_tables",
    )(positions.reshape(1, s), inv_freq.reshape(half, 1))
    return jnp.tile(cos_t.T, (1, 2)), jnp.tile(sin_t.T, (1, 2)), cos_t, sin_t


def _rope_rot(x):
    n = x.shape[-1]
    lane = lax.broadcasted_iota(jnp.int32, x.shape, 1)
    half = QK_ROPE // 2
    fwd = pltpu.roll(x, half, 1)
    bwd = pltpu.roll(x, n - half, 1)
    return jnp.where((lane % QK_ROPE) < half, -bwd, fwd)


def _odd_prep_kernel(x_ref, mod_ref, nw_ref, win_ref, wkpe_ref, cw_ref, qan_ref, wuqt_ref, kvn_ref, wk_ref, wvt_ref,
                     qnw_ref, knw_ref, cos_ref, sin_ref, cost_ref, sint_ref, yc_ref, qt_ref, k_ref, vt_ref,
                     halo_ref):
    tm = x_ref.shape[0]
    i = pl.program_id(0)

    @pl.when(i == 0)
    def _():
        halo_ref[...] = jnp.zeros_like(halo_ref)

    d = D_MODEL
    x = x_ref[...]
    sh, sc = mod_ref[:, 0:d], mod_ref[:, d:2 * d]
    h = _rms_mod(x, nw_ref[...], sc, sh).astype(BF16)
    p = jnp.dot(h, win_ref[...], preferred_element_type=F32)
    kpe = jnp.dot(h, wkpe_ref[...], preferred_element_type=F32)[:, :QK_ROPE]

    cw = C_WIDTH
    bg, cg, hh = p[:, 0:cw], p[:, cw:2 * cw], p[:, 2 * cw:3 * cw]
    z = cg * hh
    ext = jnp.concatenate([halo_ref[...], z], axis=0)
    halo_ref[...] = z[tm - CONV_HALO:, :]
    z1 = pltpu.roll(ext, 1, 0)[CONV_HALO:, :]
    z2 = pltpu.roll(ext, 2, 0)[CONV_HALO:, :]
    conv = cw_ref[0:1, :] * z2 + cw_ref[1:2, :] * z1 + cw_ref[2:3, :] * z
    yc_ref[...] = (bg * conv).astype(BF16)

    o0 = 3 * cw
    cq = p[:, o0:o0 + Q_LORA]
    ckv = p[:, o0 + Q_LORA:o0 + Q_LORA + KV_LORA]
    cqn = cq * lax.rsqrt(jnp.mean(cq * cq, axis=-1, keepdims=True) + EPS) * qan_ref[...]
    ckvn = ckv * lax.rsqrt(jnp.mean(ckv * ckv, axis=-1, keepdims=True) + EPS) * kvn_ref[...]
    cqn_t = cqn.T.astype(BF16)
    ckvn_t = ckvn.T.astype(BF16)
    q_t = jnp.dot(wuqt_ref[...], cqn_t, preferred_element_type=F32)
    v_t = jnp.dot(wvt_ref[...], ckvn_t, preferred_element_type=F32)
    kn_all = jnp.dot(ckvn.astype(BF16), wk_ref[...], preferred_element_type=F32)

    nh = MLA_HEADS
    half = QK_ROPE // 2
    sm_scale = QK_HD ** -0.5
    cos_t, sin_t = cost_ref[...], sint_ref[...]
    kn_w, kr_w = knw_ref[:, :QK_NOPE], knw_ref[:, QK_NOPE:]
    kr = kpe * kr_w
    kr = kr * cos_ref[...] + _rope_rot(kr) * sin_ref[...]
    kpe_ss = jnp.sum(kpe * kpe, axis=-1, keepdims=True)
    zrows = jnp.zeros((QK_PAD - QK_HD, tm), F32)
    zpad = jnp.zeros((tm, QK_PAD - QK_HD), F32)
    ones_rows = (lax.broadcasted_iota(jnp.int32, (V_PAD - V_HD, tm), 0) == 0).astype(BF16)
    for hd in range(nh):
        qh = q_t[hd * QK_HD:(hd + 1) * QK_HD, :]
        q_inv = lax.rsqrt(jnp.sum(qh * qh, axis=0, keepdims=True) / QK_HD + EPS) * (sm_scale * LOG2E)
        qw = qh * qnw_ref[...]
        x1, x2 = qw[QK_NOPE:QK_NOPE + half, :], qw[QK_NOPE + half:, :]
        qt_ref[hd] = jnp.concatenate(
            [qw[:QK_NOPE, :] * q_inv, (x1 * cos_t - x2 * sin_t) * q_inv, (x2 * cos_t + x1 * sin_t) * q_inv, zrows],
            axis=0).astype(BF16)
        kn = kn_all[:, hd * QK_NOPE:(hd + 1) * QK_NOPE]
        k_ss = jnp.sum(kn * kn, axis=-1, keepdims=True) + kpe_ss
        k_inv = lax.rsqrt(k_ss / QK_HD + EPS)
        k_ref[hd] = jnp.concatenate([kn * k_inv * kn_w, kr * k_inv, zpad], axis=1).astype(BF16)
        vt_ref[hd, :V_HD, :] = v_t[hd * V_HD:(hd + 1) * V_HD, :].astype(BF16)
        vt_ref[hd, V_HD:, :] = ones_rows


def _odd_prep(x, mod, nw, w_in, w_kpe, conv_w, q_a_norm, w_uq_t, kv_norm, w_k, w_v_t, q_norm_col, k_norm_w, cos, sin,
              cos_t, sin_t, tm=1024):
    s, d = x.shape
    nh = MLA_HEADS
    half = QK_ROPE // 2
    consts = [mod, nw, w_in, w_kpe, conv_w, q_a_norm, w_uq_t, kv_norm, w_k, w_v_t, q_norm_col, k_norm_w]
    return pl.pallas_call(
        _odd_prep_kernel,
        grid=(s // tm,),
        in_specs=[pl.BlockSpec((tm, d), lambda i: (i, 0))] + [_const_spec(a.shape) for a in consts] + [
            pl.BlockSpec((tm, QK_ROPE), lambda i: (i, 0)), pl.BlockSpec((tm, QK_ROPE), lambda i: (i, 0)),
            pl.BlockSpec((half, tm), lambda i: (0, i)), pl.BlockSpec((half, tm), lambda i: (0, i))],
        out_specs=[
            pl.BlockSpec((tm, C_WIDTH), lambda i: (i, 0)),
            pl.BlockSpec((nh, QK_PAD, tm), lambda i: (0, 0, i)),
            pl.BlockSpec((nh, tm, QK_PAD), lambda i: (0, i, 0)),
            pl.BlockSpec((nh, V_PAD, tm), lambda i: (0, 0, i)),
        ],
        out_shape=[
            jax.ShapeDtypeStruct((s, C_WIDTH), BF16),
            jax.ShapeDtypeStruct((nh, QK_PAD, s), BF16),
            jax.ShapeDtypeStruct((nh, s, QK_PAD), BF16),
            jax.ShapeDtypeStruct((nh, V_PAD, s), BF16),
        ],
        scratch_shapes=[pltpu.VMEM((CONV_HALO, C_WIDTH), F32)],
        compiler_params=_params(("arbitrary",)),
        name="odd_prep",
    )(x, *consts, cos, sin, cos_t, sin_t)


def _attn_kernel(qt_ref, k_ref, vt_ref, o_ref, s0_ref, s1_ref, bm0_ref, bm1_ref, m_ref, acc_ref, *, tq, tk):
    qi = pl.program_id(1)
    m_ref[...] = jnp.full_like(m_ref, MASK_VALUE)
    acc_ref[...] = jnp.zeros_like(acc_ref)

    def scores(j, s_ref, bm_ref, diag_offset=None):
        start = pl.multiple_of(j * tk, tk)
        s = jnp.dot(k_ref[0, pl.ds(start, tk), :], qt_ref[0], preferred_element_type=F32)
        if diag_offset is not None:
            key = diag_offset + lax.broadcasted_iota(jnp.int32, (tk, tq), 0)
            qry = lax.broadcasted_iota(jnp.int32, (tk, tq), 1)
            s = jnp.where(key <= qry, s, MASK_VALUE)
        s_ref[...] = s
        bm_ref[...] = jnp.max(s, axis=0, keepdims=True)

    def consume(j, s_ref, bm_ref):
        m_prev = m_ref[...]
        m_new = jnp.maximum(m_prev, bm_ref[...])
        alpha = jnp.exp2(m_prev - m_new)
        p = jnp.exp2((s_ref[...] - m_new).astype(BF16))
        start = pl.multiple_of(j * tk, tk)
        vt = vt_ref[0, :, pl.ds(start, tk)]
        acc_ref[...] = alpha * acc_ref[...] + jnp.dot(vt, p, preferred_element_type=F32)
        m_ref[...] = m_new

    r = tq // tk
    slots = ((s0_ref, bm0_ref), (s1_ref, bm1_ref))

    def trip(j0, then_diagonal):
        for b in range(r):
            first_diag = then_diagonal and b == r - 1
            scores(j0 + b + 1, *slots[(b + 1) % 2], diag_offset=0 if first_diag else None)
            consume(j0 + b, *slots[b % 2])

    def diagonal(jd):
        for b in range(r):
            if b + 1 < r:
                scores(jd + b + 1, *slots[(b + 1) % 2], diag_offset=(b + 1) * tk)
            consume(jd + b, *slots[b % 2])

    @pl.when(qi > 0)
    def _():
        scores(0, *slots[0])

    def body(t, carry):
        trip(r * t, False)
        return carry

    lax.fori_loop(0, qi - 1, body, 0)

    @pl.when(qi > 0)
    def _():
        trip(r * (qi - 1), True)
        diagonal(r * qi)

    @pl.when(qi == 0)
    def _():
        scores(0, *slots[0], diag_offset=0)
        diagonal(0)

    acc = acc_ref[...]
    out_t = acc[:V_HD, :] / acc[V_HD:V_HD + 1, :]
    o_ref[...] = out_t.T.astype(o_ref.dtype)


def _attention(qt, k, vt, tq=1024, tk=512):
    nh, _, s = qt.shape
    assert tq % (2 * tk) == 0
    return pl.pallas_call(
        functools.partial(_attn_kernel, tq=tq, tk=tk),
        grid=(nh, s // tq),
        in_specs=[
            pl.BlockSpec((1, QK_PAD, tq), lambda h, i: (h, 0, i)),
            pl.BlockSpec((1, s, QK_PAD), lambda h, i: (h, 0, 0)),
            pl.BlockSpec((1, V_PAD, s), lambda h, i: (h, 0, 0)),
        ],
        out_specs=pl.BlockSpec((tq, V_HD), lambda h, i: (i, h)),
        out_shape=jax.ShapeDtypeStruct((s, nh * V_HD), BF16),
        scratch_shapes=[pltpu.VMEM((tk, tq), F32), pltpu.VMEM((tk, tq), F32), pltpu.VMEM((1, tq), F32),
                        pltpu.VMEM((1, tq), F32), pltpu.VMEM((1, tq), F32), pltpu.VMEM((V_PAD, tq), F32)],
        compiler_params=_params(("arbitrary", "arbitrary")),
        name="mla_attention",
    )(qt, k, vt)


def _pack_bf16_pairs(x):
    n = x.shape[1] // 2
    lo = pltpu.bitcast(x[:, :n].astype(BF16).astype(F32), jnp.uint32)
    hi = pltpu.bitcast(x[:, n:].astype(BF16).astype(F32), jnp.uint32)
    return (lo >> 16) | hi


def _unpack_bf16_pairs(p):
    lo = pltpu.bitcast(p << 16, F32)
    hi = pltpu.bitcast(p & jnp.uint32(0xFFFF0000), F32)
    return jnp.concatenate([lo, hi], axis=1)


def _odd_out_kernel(x_ref, yc_ref, yd_ref, mod_ref, nw_ref, wout_ref, wr_ref, before_ref, x_out_ref, h_ref, rw_ref,
                    ridx_ref, cnt_ref):
    d = D_MODEL
    x = x_ref[...]
    gate_m = mod_ref[:, 2 * d:3 * d]
    sh, sc = mod_ref[:, 3 * d:4 * d], mod_ref[:, 4 * d:5 * d]
    mix = (jnp.dot(yc_ref[...], wout_ref[:C_WIDTH, :], preferred_element_type=F32)
           + jnp.dot(yd_ref[...], wout_ref[C_WIDTH:, :], preferred_element_type=F32))
    x1 = x + gate_m * mix
    x_out_ref[...] = x1
    h = _rms_mod(x1, nw_ref[...], sc, sh)
    h_ref[...] = _pack_bf16_pairs(h)

    h_hi = h.astype(BF16)
    h_lo = (h - h_hi.astype(F32)).astype(BF16)
    hw = jnp.dot(h_hi, wr_ref[...], preferred_element_type=F32)
    logits = hw[:, :LANES] + (hw[:, LANES:] + jnp.dot(h_lo, wr_ref[:, :LANES], preferred_element_type=F32))
    lane = lax.broadcasted_iota(jnp.int32, logits.shape, 1)
    logits = jnp.where(lane < N_EXPERTS, logits, -jnp.inf)
    m1 = jnp.max(logits, axis=-1, keepdims=True)
    i1 = jnp.min(jnp.where(logits == m1, lane, LANES), axis=-1, keepdims=True)
    rest = jnp.where(lane == i1, -jnp.inf, logits)
    m2 = jnp.max(rest, axis=-1, keepdims=True)
    i2 = jnp.min(jnp.where(rest == m2, lane, LANES), axis=-1, keepdims=True)
    e2 = jnp.exp(m2 - m1)
    w1 = 1.0 / (1.0 + e2)
    w2 = e2 / (1.0 + e2)
    rw_ref[...] = jnp.where(lane == 0, w1, jnp.where(lane == 1, w2, 0.0))

    i = pl.program_id(0)

    @pl.when(i == 0)
    def _():
        cnt_ref[...] = jnp.zeros_like(cnt_ref)

    onehot = jnp.where((lane == i1) | (lane == i2), 1.0, 0.0)
    prefix = jnp.dot(before_ref[...], onehot.astype(BF16), preferred_element_type=F32) + cnt_ref[...]
    r1 = jnp.sum(jnp.where(lane == i1, prefix, 0.0), axis=-1, keepdims=True)
    r2 = jnp.sum(jnp.where(lane == i2, prefix, 0.0), axis=-1, keepdims=True)
    cnt_ref[...] += jnp.sum(onehot, axis=0, keepdims=True)
    cols = jnp.where(lane == 0, i1.astype(F32), jnp.where(lane == 1, i2.astype(F32),
                                                          jnp.where(lane == 2, r1, jnp.where(lane == 3, r2, 0.0))))
    ridx_ref[...] = cols.T[:ridx_ref.shape[0], :].astype(jnp.int32)


def _odd_out(x, yc, yd, mod, nw, w_out, w_router, tm=1024):
    s, d = x.shape
    return pl.pallas_call(
        _odd_out_kernel,
        grid=(s // tm,),
        in_specs=[
            pl.BlockSpec((tm, d), lambda i: (i, 0)),
            pl.BlockSpec((tm, C_WIDTH), lambda i: (i, 0)),
            pl.BlockSpec((tm, MLA_HEADS * V_HD), lambda i: (i, 0)),
            _const_spec(mod.shape), _const_spec(nw.shape), _const_spec(w_out.shape), _const_spec(w_router.shape),
            _const_spec((tm, tm)),
        ],
        out_specs=[
            pl.BlockSpec((tm, d), lambda i: (i, 0)),
            pl.BlockSpec((tm, d // 2), lambda i: (i, 0)),
            pl.BlockSpec((tm, LANES), lambda i: (i, 0)),
            pl.BlockSpec((SUBLANES, tm), lambda i: (0, i)),
            pl.BlockSpec((1, LANES), lambda i: (0, 0)),
        ],
        out_shape=[
            jax.ShapeDtypeStruct((s, d), F32),
            jax.ShapeDtypeStruct((s, d // 2), jnp.uint32),
            jax.ShapeDtypeStruct((s, LANES), F32),
            jax.ShapeDtypeStruct((SUBLANES, s), jnp.int32),
            jax.ShapeDtypeStruct((1, LANES), F32),
        ],
        compiler_params=_params(("arbitrary",)),
        name="odd_out_router",
    )(x, yc, yd, mod, nw, w_out, w_router, jnp.tril(jnp.ones((tm, tm), BF16), -1))


def _sc_workers():
    info = plsc.get_sparse_core_info()
    return info.num_cores, info.num_cores * info.num_subcores


def _sc_scatter_rows(x, idx0, idx1, out_rows):
    n, w = x.shape
    nc, nw = _sc_workers()
    per_w = n // nw
    nch = per_w // SC_ROWS
    mesh = plsc.VectorSubcoreMesh(core_axis_name="c", subcore_axis_name="s")

    @functools.partial(
        pl.kernel, mesh=mesh, out_type=jax.ShapeDtypeStruct((out_rows, w), x.dtype),
        scratch_types=[pltpu.VMEM((nch, SC_ROWS), jnp.int32), pltpu.VMEM((nch, SC_ROWS), jnp.int32),
                       pltpu.VMEM((SC_ROWS, w), x.dtype), pltpu.SemaphoreType.DMA],
        name="moe_dispatch")
    def scatter(x_hbm, i0_hbm, i1_hbm, out_hbm, i0_v, i1_v, rows_v, sem):
        wid = lax.axis_index("s") * nc + lax.axis_index("c")
        pltpu.sync_copy(i0_hbm.at[wid], i0_v)
        pltpu.sync_copy(i1_hbm.at[wid], i1_v)

        def body(c, carry):
            pltpu.sync_copy(x_hbm.at[pl.ds(wid * per_w + c * SC_ROWS, SC_ROWS)], rows_v)
            pltpu.async_copy(rows_v, out_hbm.at[i0_v.at[c]], sem).wait()
            pltpu.async_copy(rows_v, out_hbm.at[i1_v.at[c]], sem).wait()
            return carry

        lax.fori_loop(0, nch, body, 0)

    return scatter(x, idx0.reshape(nw, nch, SC_ROWS), idx1.reshape(nw, nch, SC_ROWS))


def _sc_gather_rows(table, idx):
    _, w = table.shape
    b = idx.shape[0]
    nc, nw = _sc_workers()
    per_w = b // nw
    nch = per_w // SC_ROWS
    mesh = plsc.VectorSubcoreMesh(core_axis_name="c", subcore_axis_name="s")

    assert nch % 2 == 0 and nch >= 2

    @functools.partial(
        pl.kernel, mesh=mesh, out_type=jax.ShapeDtypeStruct((b, w), table.dtype),
        scratch_types=[pltpu.VMEM((nch, SC_ROWS), jnp.int32), pltpu.VMEM((SC_ROWS, w), table.dtype),
                       pltpu.VMEM((SC_ROWS, w), table.dtype), pltpu.SemaphoreType.DMA, pltpu.SemaphoreType.DMA],
        name="moe_combine_gather")
    def gather(table_hbm, idx_hbm, out_hbm, idx_v, buf0, buf1, sem0, sem1):
        wid = lax.axis_index("s") * nc + lax.axis_index("c")
        pltpu.sync_copy(idx_hbm.at[wid], idx_v)
        base = wid * per_w

        def fetch(c, buf, sem):
            return pltpu.make_async_copy(table_hbm.at[idx_v.at[c]], buf, sem)

        def put(c, buf):
            pltpu.sync_copy(buf, out_hbm.at[pl.ds(base + c * SC_ROWS, SC_ROWS)])

        def pair(c, prefetch_next):
            fetch(c + 1, buf1, sem1).start()
            fetch(c, buf0, sem0).wait()
            put(c, buf0)
            if prefetch_next:
                fetch(c + 2, buf0, sem0).start()
            fetch(c + 1, buf1, sem1).wait()
            put(c + 1, buf1)

        fetch(0, buf0, sem0).start()

        def body(t, carry):
            pair(2 * t, True)
            return carry

        lax.fori_loop(0, nch // 2 - 1, body, 0)
        pair(nch - 2, False)

    return gather(table, idx.reshape(nw, nch, SC_ROWS))


def _moe_kernel(te_ref, nv_ref, xs_ref, wg_ref, wu_ref, wd_ref, ys_ref, x_scr, acc_ref):
    j = pl.program_id(0)
    f = pl.program_id(1)
    nf = pl.num_programs(1)

    @pl.when(j < nv_ref[0])
    def _():
        @pl.when(f == 0)
        def _():
            x_scr[...] = _unpack_bf16_pairs(xs_ref[...]).astype(BF16)

        h = x_scr[...]
        g = jnp.dot(h, wg_ref[0], preferred_element_type=F32)
        u = jnp.dot(h, wu_ref[0], preferred_element_type=F32)
        act = (g * jax.nn.sigmoid(g) * u).astype(BF16)
        y = jnp.dot(act, wd_ref[0], preferred_element_type=F32)

        @pl.when(f == 0)
        def _():
            acc_ref[...] = y

        @pl.when(f > 0)
        def _():
            acc_ref[...] += y

        @pl.when(f == nf - 1)
        def _():
            ys_ref[...] = _pack_bf16_pairs(acc_ref[...])


def _moe_grouped(xs, tile_expert, n_valid, w_gu, w_down, tm, tf=1792):
    p_rows, dh = xs.shape
    d = 2 * dh
    ne, _, ff2 = w_gu.shape
    ff = ff2 // 2
    nf = ff // tf

    def tile(j, nv):
        return jnp.minimum(j, nv[0] - 1)

    def chunk(j, f, nv):
        return jnp.where(j < nv[0], f, nf - 1)

    grid_spec = pltpu.PrefetchScalarGridSpec(
        num_scalar_prefetch=2,
        grid=(p_rows // tm, nf),
        in_specs=[
            pl.BlockSpec((tm, dh), lambda j, f, te, nv: (tile(j, nv), 0)),
            pl.BlockSpec((1, d, tf), lambda j, f, te, nv: (te[tile(j, nv)], 0, chunk(j, f, nv))),
            pl.BlockSpec((1, d, tf), lambda j, f, te, nv: (te[tile(j, nv)], 0, nf + chunk(j, f, nv))),
            pl.BlockSpec((1, tf, d), lambda j, f, te, nv: (te[tile(j, nv)], chunk(j, f, nv), 0)),
        ],
        out_specs=pl.BlockSpec((tm, dh), lambda j, f, te, nv: (tile(j, nv), 0)),
        scratch_shapes=[pltpu.VMEM((tm, d), BF16), pltpu.VMEM((tm, d), F32)],
    )
    return pl.pallas_call(
        _moe_kernel,
        grid_spec=grid_spec,
        out_shape=jax.ShapeDtypeStruct((p_rows, dh), jnp.uint32),
        compiler_params=_params(("arbitrary", "arbitrary")),
        name="moe_experts",
    )(tile_expert, n_valid, xs, w_gu, w_gu, w_down)


def _moe_combine_kernel(x_ref, y0_ref, y1_ref, rw_ref, mod_ref, *rest):
    o_ref = rest[-1]
    d = D_MODEL
    w1 = rw_ref[:, 0:1]
    w2 = rw_ref[:, 1:2]
    y = w1 * _unpack_bf16_pairs(y0_ref[...]) + w2 * _unpack_bf16_pairs(y1_ref[...])
    o_ref[...] = x_ref[...] + mod_ref[:, 5 * d:6 * d] * y


def _moe_combine(x, yg, rw, mod, prev, part, n_parts, tm=512):
    s, d = x.shape
    nb = s // tm // n_parts
    first = part * nb
    in_specs = [
        pl.BlockSpec((tm, d), lambda i: (first + i, 0)),
        pl.BlockSpec((tm, d // 2), lambda i: (i, 0)),
        pl.BlockSpec((tm, d // 2), lambda i: (nb + i, 0)),
        pl.BlockSpec((tm, LANES), lambda i: (first + i, 0)),
        _const_spec(mod.shape),
    ]
    args = [x, yg, yg, rw, mod]
    aliases = {}
    if prev is not None:
        in_specs.append(pl.BlockSpec(memory_space=pl.ANY))
        args.append(prev)
        aliases = {len(args) - 1: 0}
    return pl.pallas_call(
        _moe_combine_kernel,
        grid=(nb,),
        in_specs=in_specs,
        out_specs=pl.BlockSpec((tm, d), lambda i: (first + i, 0)),
        out_shape=jax.ShapeDtypeStruct((s, d), F32),
        input_output_aliases=aliases,
        compiler_params=_params(("arbitrary",)),
        name="moe_combine",
    )(*args)


def _moe_sparse(x, h_packed, rw, ridx, counts, mod, w_gu, w_down, tm=MOE_TILE):
    s = x.shape[0]
    ne = w_gu.shape[0]
    n_tiles = (2 * s) // tm + ne
    cnt = counts[0, :ne].astype(jnp.int32)
    padded = ((cnt + tm - 1) // tm) * tm
    ends = jnp.cumsum(padded)
    offs = ends - padded
    experts = jnp.arange(ne, dtype=jnp.int32)
    off_of = lambda e: jnp.sum(jnp.where(e[:, None] == experts[None, :], offs[None, :], 0), axis=1)
    pos0 = off_of(ridx[0]) + ridx[2]
    pos1 = off_of(ridx[1]) + ridx[3]
    tile_start = jnp.arange(n_tiles, dtype=jnp.int32) * tm
    tile_expert = jnp.minimum(jnp.sum(tile_start[:, None] >= ends[None, :], axis=1), ne - 1).astype(jnp.int32)
    n_valid = (ends[-1] // tm).reshape(1).astype(jnp.int32)
    xs = _sc_scatter_rows(h_packed, pos0, pos1, n_tiles * tm)
    ys = _moe_grouped(xs, tile_expert, n_valid, w_gu, w_down, tm)
    out = None
    sp = s // COMBINE_PARTS
    for part in range(COMBINE_PARTS):
        rows = slice(part * sp, (part + 1) * sp)
        yg = _sc_gather_rows(ys, jnp.concatenate([pos0[rows], pos1[rows]]))
        out = _moe_combine(x, yg, rw, mod, out, part, COMBINE_PARTS)
    return out


def kernel(x, c, positions, norm_mix_w, norm_ffn_w, ada_w, ada_b, e_w_in, a_ln_w, a_ln_b, a_w_s, a_b_s, b_w_grp,
           b_scale, e_w_out, ffn_w_gu, ffn_w_down, o_w_in, c_conv_w, q_a_norm, w_uq, kv_norm, w_ukv, q_norm_w,
           k_norm_w, o_w_out, router_w, moe_w_gu, moe_w_down):
    bsz, s, d = x.shape
    assert bsz == 1 and d == D_MODEL
    depth = ada_w.shape[0]
    nh = MLA_HEADS
    xs = x.reshape(s, d)
    mod = _ada_mod(c, ada_w, ada_b)
    cos, sin, cos_t, sin_t = _rope_tables(positions)
    row = lambda a: a.reshape(1, -1)

    for layer in range(depth):
        i = layer // 2
        m = mod[layer]
        if layer % 2 == 0:
            xs = _even_mix(
                xs, m, row(norm_mix_w[layer]), e_w_in[i].astype(BF16), row(a_ln_w[i]), row(a_ln_b[i]), a_w_s[i],
                a_b_s[i].reshape(A_GROUPS, CHUNK, 1), b_w_grp[i].astype(BF16), row(b_scale[i]),
                e_w_out[i].astype(BF16))
            has_next = layer + 1 < depth
            side = moe_w_gu[i] if has_next else jnp.zeros((s // 512, 8, LANES), F32)
            xs, side_bf16 = _ffn(xs, m, row(norm_ffn_w[layer]), ffn_w_gu[i].astype(BF16),
                                 ffn_w_down[i].astype(BF16), side)
            moe_w_gu_bf16 = side_bf16 if has_next else None
        else:
            o2 = 3 * C_WIDTH + Q_LORA + KV_LORA
            w_in = o_w_in[i][:, :o2].astype(BF16)
            w_kpe = jnp.pad(o_w_in[i][:, o2:], ((0, 0), (0, LANES - QK_ROPE))).astype(BF16)
            wkv = w_ukv[i].reshape(KV_LORA, nh, QK_NOPE + V_HD)
            w_k = wkv[:, :, :QK_NOPE].reshape(KV_LORA, nh * QK_NOPE).astype(BF16)
            w_v_t = wkv[:, :, QK_NOPE:].reshape(KV_LORA, nh * V_HD).T.astype(BF16)
            yc, qt, k, vt = _odd_prep(
                xs, m, row(norm_mix_w[layer]), w_in, w_kpe, c_conv_w[i], row(q_a_norm[i]), w_uq[i].T.astype(BF16),
                row(kv_norm[i]), w_k, w_v_t, q_norm_w[i].reshape(QK_HD, 1), row(k_norm_w[i]), cos, sin, cos_t, sin_t)
            yd = _attention(qt, k, vt)
            wr = jnp.pad(router_w[i], ((0, 0), (0, LANES - N_EXPERTS)))
            wr_hi = lax.reduce_precision(wr, exponent_bits=8, mantissa_bits=7)
            w_router = jnp.concatenate([wr_hi, wr - wr_hi], axis=1).astype(BF16)
            xs, hp, rw, ridx, counts = _odd_out(xs, yc, yd, m, row(norm_ffn_w[layer]), o_w_out[i].astype(BF16),
                                                w_router)
            xs = _moe_sparse(xs, hp, rw, ridx, counts, m, moe_w_gu_bf16, moe_w_down[i].astype(BF16))
    return xs.reshape(bsz, s, d)
```

```python
import functools

import jax
import jax.numpy as jnp
from jax import lax
from jax.experimental import pallas as pl
from jax.experimental.pallas import tpu as pltpu
from jax.experimental.pallas import tpu_sc as plsc

D_MODEL = 1024
SEQ = 16384
EPS = 1e-6
CHUNK = 128
A_WIDTH = 512
A_GROUPS = 4
B_WIDTH = 512
POOL_WINDOWS = (2, 4, 8, 16)
B_HD = 128
C_WIDTH = 512
MLA_HEADS = 4
Q_LORA = 256
KV_LORA = 256
QK_NOPE = 128
QK_ROPE = 64
QK_HD = QK_NOPE + QK_ROPE
V_HD = 128
ROPE_THETA = 10000.0
D_FF = 2816
N_EXPERTS = 8
D_FF_EXPERT = 3584

LANES = 128
SUBLANES = 8
POOL_HALO = 16
CONV_HALO = 8
QK_PAD = 256
V_PAD = 144
LOG2E = 1.4426950408889634
MASK_VALUE = -1e30
MOE_TILE = 512
COMBINE_PARTS = 2
SC_ROWS = 64
VMEM_LIMIT = 56 * 1024 * 1024

F32 = jnp.float32
BF16 = jnp.bfloat16


def _params(sem, vmem=VMEM_LIMIT, flags=None):
    return pltpu.CompilerParams(dimension_semantics=sem, vmem_limit_bytes=vmem, flags=flags)


def _const_spec(shape, single=False):
    nd = len(shape)
    return pl.BlockSpec(shape, lambda *_: (0,) * nd, pipeline_mode=pl.Buffered(1) if single else None)


def _rms_mod(x, nw, sc, sh):
    ms = jnp.mean(x * x, axis=-1, keepdims=True)
    return (x * lax.rsqrt(ms + EPS)) * nw * (1.0 + sc) + sh


def _ada_kernel(c_ref, w_ref, b_ref, o_ref):
    c = c_ref[...]
    ca = c * jax.nn.sigmoid(c)
    o_ref[0] = jnp.sum(w_ref[0] * ca, axis=0, keepdims=True) + b_ref[0]


def _ada_mod(c, ada_w, ada_b):
    depth, d, n = ada_w.shape
    tn = 1536
    return pl.pallas_call(
        _ada_kernel,
        grid=(depth, n // tn),
        in_specs=[
            pl.BlockSpec((d, 1), lambda l, j: (0, 0)),
            pl.BlockSpec((1, d, tn), lambda l, j: (l, 0, j)),
            pl.BlockSpec((1, 1, tn), lambda l, j: (l, 0, j)),
        ],
        out_specs=pl.BlockSpec((1, 1, tn), lambda l, j: (l, 0, j)),
        out_shape=jax.ShapeDtypeStruct((depth, 1, n), F32),
        compiler_params=_params(("arbitrary", "arbitrary")),
        name="ada_mod",
    )(c.reshape(d, 1), ada_w, ada_b.reshape(depth, 1, n))


def _even_mix_kernel(x_ref, mod_ref, nw_ref, win_ref, lnw_ref, lnb_ref, ws_ref, bs_ref, wg_ref, bsc_ref,
                     wout_ref, side_ref, o_ref, side_out_ref, halo_ref, sv_ref, yb_ref):
    side_out_ref[...] = side_ref[...].astype(BF16)
    tm = x_ref.shape[0]
    i = pl.program_id(0)

    @pl.when(i == 0)
    def _():
        halo_ref[...] = jnp.zeros_like(halo_ref)

    d = D_MODEL
    x = x_ref[...]
    sh, sc, gate = mod_ref[:, 0:d], mod_ref[:, d:2 * d], mod_ref[:, 2 * d:3 * d]
    h = _rms_mod(x, nw_ref[...], sc, sh).astype(BF16)
    p = jnp.dot(h, win_ref[...], preferred_element_type=F32)

    gl = jax.nn.gelu(p[:, :2 * A_WIDTH])
    u = gl[:, :A_WIDTH]
    v = gl[:, A_WIDTH:]
    mu = jnp.mean(v, axis=-1, keepdims=True)
    vc = v - mu
    var = jnp.mean(vc * vc, axis=-1, keepdims=True)
    vn = (vc * lax.rsqrt(var + EPS) * lnw_ref[...] + lnb_ref[...]).astype(BF16)
    row = lax.broadcasted_iota(jnp.int32, (CHUNK, CHUNK), 0)
    col = lax.broadcasted_iota(jnp.int32, (CHUNK, CHUNK), 1)
    for g in range(A_GROUPS):
        w = jnp.where(col <= row, ws_ref[g], 0.0).astype(BF16)
        b = bs_ref[g]
        for c in range(tm // CHUNK):
            blk = vn[c * CHUNK:(c + 1) * CHUNK, g * LANES:(g + 1) * LANES]
            sv_ref[c * CHUNK:(c + 1) * CHUNK, g * LANES:(g + 1) * LANES] = (
                jnp.dot(w, blk, preferred_element_type=F32) + b)
    ya = (u * sv_ref[...]).astype(BF16)

    pb = p[:, 2 * A_WIDTH:]
    ext = jnp.concatenate([halo_ref[...], pb], axis=0)
    halo_ref[...] = pb[tm - POOL_HALO:, :]
    t_glob = i * tm + lax.broadcasted_iota(jnp.int32, (tm, 1), 0)
    s = ext
    width = 1
    for g, win in enumerate(POOL_WINDOWS):
        while width < win:
            s = s + pltpu.roll(s, width, 0)
            width *= 2
        cnt = jnp.minimum(t_glob + 1, win).astype(F32)
        sl = slice(g * B_HD, (g + 1) * B_HD)
        pooled = s[POOL_HALO:, sl] / cnt
        dg = (pooled - pb[:, sl]).astype(BF16)
        yb_ref[:, sl] = jnp.dot(dg, wg_ref[g], preferred_element_type=F32)
    yb = (yb_ref[...] * bsc_ref[...]).astype(BF16)

    mix = (jnp.dot(ya, wout_ref[:A_WIDTH, :], preferred_element_type=F32)
           + jnp.dot(yb, wout_ref[A_WIDTH:, :], preferred_element_type=F32))
    o_ref[...] = x + gate * mix


def _side_cast_spec(side, steps, axis):
    ne = side.shape[0]
    parts = steps // ne
    assert parts * ne == steps and side.shape[axis] % parts == 0
    block = list(side.shape)
    block[0] = 1
    block[axis] //= parts
    assert block[1] % 16 == 0 and block[2] % LANES == 0
    if axis == 1:
        return pl.BlockSpec(tuple(block), lambda i: (i // parts, i % parts, 0))
    return pl.BlockSpec(tuple(block), lambda i: (i // parts, 0, i % parts))


def _even_mix(x, mod, nw, w_in, ln_w, ln_b, w_s, b_s, w_grp, b_scale, w_out, side, tm=512):
    s, d = x.shape
    steps = s // tm
    side_spec = _side_cast_spec(side, steps, 1)
    return pl.pallas_call(
        _even_mix_kernel,
        grid=(steps,),
        in_specs=[
            pl.BlockSpec((tm, d), lambda i: (i, 0)),
            _const_spec(mod.shape), _const_spec(nw.shape), _const_spec(w_in.shape),
            _const_spec(ln_w.shape), _const_spec(ln_b.shape), _const_spec(w_s.shape), _const_spec(b_s.shape),
            _const_spec(w_grp.shape), _const_spec(b_scale.shape), _const_spec(w_out.shape), side_spec,
        ],
        out_specs=[pl.BlockSpec((tm, d), lambda i: (i, 0)), side_spec],
        out_shape=[jax.ShapeDtypeStruct((s, d), F32), jax.ShapeDtypeStruct(side.shape, BF16)],
        scratch_shapes=[pltpu.VMEM((POOL_HALO, B_WIDTH), F32), pltpu.VMEM((tm, A_WIDTH), F32),
                        pltpu.VMEM((tm, B_WIDTH), F32)],
        compiler_params=_params(("arbitrary",)),
        name="even_mix",
    )(x, mod, nw, w_in, ln_w, ln_b, w_s, b_s, w_grp, b_scale, w_out, side)


def _ffn_kernel(x_ref, mod_ref, nw_ref, wgu_ref, wd_ref, side_ref, o_ref, side_out_ref, *, n_chunks):
    side_out_ref[...] = side_ref[...].astype(BF16)
    d = D_MODEL
    x = x_ref[...]
    sh, sc, gate = mod_ref[:, 3 * d:4 * d], mod_ref[:, 4 * d:5 * d], mod_ref[:, 5 * d:6 * d]
    h = _rms_mod(x, nw_ref[...], sc, sh).astype(BF16)
    ff = wd_ref.shape[0]
    tf = ff // n_chunks
    acc = jnp.zeros(x.shape, F32)
    for f in range(n_chunks):
        g = jnp.dot(h, wgu_ref[:, f * tf:(f + 1) * tf], preferred_element_type=F32)
        u = jnp.dot(h, wgu_ref[:, ff + f * tf:ff + (f + 1) * tf], preferred_element_type=F32)
        act = (g * jax.nn.sigmoid(g) * u).astype(BF16)
        acc = acc + jnp.dot(act, wd_ref[f * tf:(f + 1) * tf, :], preferred_element_type=F32)
    o_ref[...] = x + gate * acc


def _ffn(x, mod, nw, w_gu, w_down, side, tm=512, n_chunks=11):
    s, d = x.shape
    steps = s // tm
    side_spec = _side_cast_spec(side, steps, 2)
    return pl.pallas_call(
        functools.partial(_ffn_kernel, n_chunks=n_chunks),
        grid=(steps,),
        in_specs=[
            pl.BlockSpec((tm, d), lambda i: (i, 0)),
            _const_spec(mod.shape), _const_spec(nw.shape), _const_spec(w_gu.shape, single=True),
            _const_spec(w_down.shape, single=True), side_spec,
        ],
        out_specs=[pl.BlockSpec((tm, d), lambda i: (i, 0)), side_spec],
        out_shape=[jax.ShapeDtypeStruct((s, d), F32), jax.ShapeDtypeStruct(side.shape, BF16)],
        compiler_params=_params(("arbitrary",)),
        name="ffn",
    )(x, mod, nw, w_gu, w_down, side)


def _rope_table_kernel(pos_ref, invf_ref, cos_ref, sin_ref):
    ang = pos_ref[...].astype(F32) * invf_ref[...]
    cos_ref[...] = jnp.cos(ang)
    sin_ref[...] = jnp.sin(ang)


def _rope_tables(positions):
    s = positions.shape[-1]
    half = QK_ROPE // 2
    inv_freq = ROPE_THETA ** (-jnp.arange(0, QK_ROPE, 2, dtype=F32) / QK_ROPE)
    cos_t, sin_t = pl.pallas_call(
        _rope_table_kernel,
        out_shape=(jax.ShapeDtypeStruct((half, s), F32), jax.ShapeDtypeStruct((half, s), F32)),
        name="rope_tables",
    )(positions.reshape(1, s), inv_freq.reshape(half, 1))
    return jnp.tile(cos_t.T, (1, 2)), jnp.tile(sin_t.T, (1, 2)), cos_t, sin_t


def _rope_rot(x):
    n = x.shape[-1]
    lane = lax.broadcasted_iota(jnp.int32, x.shape, 1)
    half = QK_ROPE // 2
    fwd = pltpu.roll(x, half, 1)
    bwd = pltpu.roll(x, n - half, 1)
    return jnp.where((lane % QK_ROPE) < half, -bwd, fwd)


def _odd_prep_kernel(x_ref, mod_ref, nw_ref, win_ref, wkpe_ref, cw_ref, qan_ref, wuqt_ref, kvn_ref, wk_ref, wvt_ref,
                     qnw_ref, knw_ref, cos_ref, sin_ref, cost_ref, sint_ref, yc_ref, qt_ref, k_ref, vt_ref,
                     halo_ref):
    tm = x_ref.shape[0]
    i = pl.program_id(0)

    @pl.when(i == 0)
    def _():
        halo_ref[...] = jnp.zeros_like(halo_ref)

    d = D_MODEL
    x = x_ref[...]
    sh, sc = mod_ref[:, 0:d], mod_ref[:, d:2 * d]
    h = _rms_mod(x, nw_ref[...], sc, sh).astype(BF16)
    p = jnp.dot(h, win_ref[...], preferred_element_type=F32)
    kpe = jnp.dot(h, wkpe_ref[...], preferred_element_type=F32)[:, :QK_ROPE]

    cw = C_WIDTH
    bg, cg, hh = p[:, 0:cw], p[:, cw:2 * cw], p[:, 2 * cw:3 * cw]
    z = cg * hh
    ext = jnp.concatenate([halo_ref[...], z], axis=0)
    halo_ref[...] = z[tm - CONV_HALO:, :]
    z1 = pltpu.roll(ext, 1, 0)[CONV_HALO:, :]
    z2 = pltpu.roll(ext, 2, 0)[CONV_HALO:, :]
    conv = cw_ref[0:1, :] * z2 + cw_ref[1:2, :] * z1 + cw_ref[2:3, :] * z
    yc_ref[...] = (bg * conv).astype(BF16)

    o0 = 3 * cw
    cq = p[:, o0:o0 + Q_LORA]
    ckv = p[:, o0 + Q_LORA:o0 + Q_LORA + KV_LORA]
    cqn = cq * lax.rsqrt(jnp.mean(cq * cq, axis=-1, keepdims=True) + EPS) * qan_ref[...]
    ckvn = ckv * lax.rsqrt(jnp.mean(ckv * ckv, axis=-1, keepdims=True) + EPS) * kvn_ref[...]
    cqn_t = cqn.T.astype(BF16)
    ckvn_t = ckvn.T.astype(BF16)
    q_t = jnp.dot(wuqt_ref[...], cqn_t, preferred_element_type=F32)
    v_t = jnp.dot(wvt_ref[...], ckvn_t, preferred_element_type=F32)
    kn_all = jnp.dot(ckvn.astype(BF16), wk_ref[...], preferred_element_type=F32)

    nh = MLA_HEADS
    half = QK_ROPE // 2
    sm_scale = QK_HD ** -0.5
    cos_t, sin_t = cost_ref[...], sint_ref[...]
    kn_w, kr_w = knw_ref[:, :QK_NOPE], knw_ref[:, QK_NOPE:]
    kr = kpe * kr_w
    kr = kr * cos_ref[...] + _rope_rot(kr) * sin_ref[...]
    kpe_ss = jnp.sum(kpe * kpe, axis=-1, keepdims=True)
    zrows = jnp.zeros((QK_PAD - QK_HD, tm), F32)
    zpad = jnp.zeros((tm, QK_PAD - QK_HD), F32)
    ones_rows = (lax.broadcasted_iota(jnp.int32, (V_PAD - V_HD, tm), 0) == 0).astype(BF16)
    for hd in range(nh):
        qh = q_t[hd * QK_HD:(hd + 1) * QK_HD, :]
        q_inv = lax.rsqrt(jnp.sum(qh * qh, axis=0, keepdims=True) / QK_HD + EPS) * (sm_scale * LOG2E)
        qw = qh * qnw_ref[...]
        x1, x2 = qw[QK_NOPE:QK_NOPE + half, :], qw[QK_NOPE + half:, :]
        qt_ref[hd] = jnp.concatenate(
            [qw[:QK_NOPE, :] * q_inv, (x1 * cos_t - x2 * sin_t) * q_inv, (x2 * cos_t + x1 * sin_t) * q_inv, zrows],
            axis=0).astype(BF16)
        kn = kn_all[:, hd * QK_NOPE:(hd + 1) * QK_NOPE]
        k_ss = jnp.sum(kn * kn, axis=-1, keepdims=True) + kpe_ss
        k_inv = lax.rsqrt(k_ss / QK_HD + EPS)
        k_ref[hd] = jnp.concatenate([kn * k_inv * kn_w, kr * k_inv, zpad], axis=1).astype(BF16)
        vt_ref[hd, :V_HD, :] = v_t[hd * V_HD:(hd + 1) * V_HD, :].astype(BF16)
        vt_ref[hd, V_HD:, :] = ones_rows


def _odd_prep(x, mod, nw, w_in, w_kpe, conv_w, q_a_norm, w_uq_t, kv_norm, w_k, w_v_t, q_norm_col, k_norm_w, cos, sin,
              cos_t, sin_t, tm=1024):
    s, d = x.shape
    nh = MLA_HEADS
    half = QK_ROPE // 2
    consts = [mod, nw, w_in, w_kpe, conv_w, q_a_norm, w_uq_t, kv_norm, w_k, w_v_t, q_norm_col, k_norm_w]
    return pl.pallas_call(
        _odd_prep_kernel,
        grid=(s // tm,),
        in_specs=[pl.BlockSpec((tm, d), lambda i: (i, 0))] + [_const_spec(a.shape) for a in consts] + [
            pl.BlockSpec((tm, QK_ROPE), lambda i: (i, 0)), pl.BlockSpec((tm, QK_ROPE), lambda i: (i, 0)),
            pl.BlockSpec((half, tm), lambda i: (0, i)), pl.BlockSpec((half, tm), lambda i: (0, i))],
        out_specs=[
            pl.BlockSpec((tm, C_WIDTH), lambda i: (i, 0)),
            pl.BlockSpec((nh, QK_PAD, tm), lambda i: (0, 0, i)),
            pl.BlockSpec((nh, tm, QK_PAD), lambda i: (0, i, 0)),
            pl.BlockSpec((nh, V_PAD, tm), lambda i: (0, 0, i)),
        ],
        out_shape=[
            jax.ShapeDtypeStruct((s, C_WIDTH), BF16),
            jax.ShapeDtypeStruct((nh, QK_PAD, s), BF16),
            jax.ShapeDtypeStruct((nh, s, QK_PAD), BF16),
            jax.ShapeDtypeStruct((nh, V_PAD, s), BF16),
        ],
        scratch_shapes=[pltpu.VMEM((CONV_HALO, C_WIDTH), F32)],
        compiler_params=_params(("arbitrary",)),
        name="odd_prep",
    )(x, *consts, cos, sin, cos_t, sin_t)


def _attn_kernel(qt_ref, k_ref, vt_ref, o_ref, s0_ref, s1_ref, bm0_ref, bm1_ref, m_ref, acc_ref, *, tq, tk):
    qi = pl.program_id(1)
    m_ref[...] = jnp.full_like(m_ref, MASK_VALUE)
    acc_ref[...] = jnp.zeros_like(acc_ref)

    def scores(j, s_ref, bm_ref, diag_offset=None):
        q0 = 0 if diag_offset is None else diag_offset
        start = pl.multiple_of(j * tk, tk)
        s = jnp.dot(k_ref[0, pl.ds(start, tk), :], qt_ref[0, :, q0:], preferred_element_type=F32)
        if diag_offset is not None:
            key = lax.broadcasted_iota(jnp.int32, s.shape, 0)
            qry = lax.broadcasted_iota(jnp.int32, s.shape, 1)
            s = jnp.where(key <= qry, s, MASK_VALUE)
        s_ref[:, q0:] = s
        bm_ref[:, q0:] = jnp.max(s, axis=0, keepdims=True)

    def consume(j, s_ref, bm_ref, q0=0):
        m_prev = m_ref[:, q0:]
        m_new = jnp.maximum(m_prev, bm_ref[:, q0:])
        alpha = jnp.exp2(m_prev - m_new)
        p = jnp.exp2((s_ref[:, q0:] - m_new).astype(BF16))
        start = pl.multiple_of(j * tk, tk)
        vt = vt_ref[0, :, pl.ds(start, tk)]
        acc_ref[:, q0:] = alpha * acc_ref[:, q0:] + jnp.dot(vt, p, preferred_element_type=F32)
        m_ref[:, q0:] = m_new

    r = tq // tk
    slots = ((s0_ref, bm0_ref), (s1_ref, bm1_ref))

    def trip(j0, then_diagonal):
        for b in range(r):
            first_diag = then_diagonal and b == r - 1
            scores(j0 + b + 1, *slots[(b + 1) % 2], diag_offset=0 if first_diag else None)
            consume(j0 + b, *slots[b % 2])

    def diagonal(jd):
        for b in range(r):
            if b + 1 < r:
                scores(jd + b + 1, *slots[(b + 1) % 2], diag_offset=(b + 1) * tk)
            consume(jd + b, *slots[b % 2], q0=b * tk)

    @pl.when(qi > 0)
    def _():
        scores(0, *slots[0])

    n_plain = jnp.maximum(qi - 1, 0)

    def body(t, carry):
        trip(r * (2 * t), False)
        trip(r * (2 * t + 1), False)
        return carry

    lax.fori_loop(0, n_plain // 2, body, 0)

    @pl.when(n_plain % 2 == 1)
    def _():
        trip(r * (n_plain - 1), False)

    @pl.when(qi > 0)
    def _():
        trip(r * (qi - 1), True)
        diagonal(r * qi)

    @pl.when(qi == 0)
    def _():
        scores(0, *slots[0], diag_offset=0)
        diagonal(0)

    acc = acc_ref[...]
    out_t = acc[:V_HD, :] / acc[V_HD:V_HD + 1, :]
    o_ref[...] = out_t.T.astype(o_ref.dtype)


def _attention(qt, k, vt, tq=1024, tk=512):
    nh, _, s = qt.shape
    assert tq % (2 * tk) == 0
    return pl.pallas_call(
        functools.partial(_attn_kernel, tq=tq, tk=tk),
        grid=(nh, s // tq),
        in_specs=[
            pl.BlockSpec((1, QK_PAD, tq), lambda h, i: (h, 0, i)),
            pl.BlockSpec((1, s, QK_PAD), lambda h, i: (h, 0, 0)),
            pl.BlockSpec((1, V_PAD, s), lambda h, i: (h, 0, 0)),
        ],
        out_specs=pl.BlockSpec((tq, V_HD), lambda h, i: (i, h)),
        out_shape=jax.ShapeDtypeStruct((s, nh * V_HD), BF16),
        scratch_shapes=[pltpu.VMEM((tk, tq), F32), pltpu.VMEM((tk, tq), F32), pltpu.VMEM((1, tq), F32),
                        pltpu.VMEM((1, tq), F32), pltpu.VMEM((1, tq), F32), pltpu.VMEM((V_PAD, tq), F32)],
        compiler_params=_params(("arbitrary", "arbitrary")),
        name="mla_attention",
    )(qt, k, vt)


def _pack_bf16_pairs(x):
    n = x.shape[1] // 2
    lo = pltpu.bitcast(x[:, :n].astype(BF16).astype(F32), jnp.uint32)
    hi = pltpu.bitcast(x[:, n:].astype(BF16).astype(F32), jnp.uint32)
    return (lo >> 16) | hi


def _unpack_bf16_pairs(p):
    lo = pltpu.bitcast(p << 16, F32)
    hi = pltpu.bitcast(p & jnp.uint32(0xFFFF0000), F32)
    return jnp.concatenate([lo, hi], axis=1)


def _odd_out_kernel(x_ref, yc_ref, yd_ref, mod_ref, nw_ref, wout_ref, wr_ref, before_ref, x_out_ref, h_ref, rw_ref,
                    ridx_ref, cnt_ref):
    d = D_MODEL
    x = x_ref[...]
    gate_m = mod_ref[:, 2 * d:3 * d]
    sh, sc = mod_ref[:, 3 * d:4 * d], mod_ref[:, 4 * d:5 * d]
    mix = (jnp.dot(yc_ref[...], wout_ref[:C_WIDTH, :], preferred_element_type=F32)
           + jnp.dot(yd_ref[...], wout_ref[C_WIDTH:, :], preferred_element_type=F32))
    x1 = x + gate_m * mix
    x_out_ref[...] = x1
    h = _rms_mod(x1, nw_ref[...], sc, sh)
    h_ref[...] = _pack_bf16_pairs(h)

    h_hi = h.astype(BF16)
    h_lo = (h - h_hi.astype(F32)).astype(BF16)
    hw = jnp.dot(h_hi, wr_ref[...], preferred_element_type=F32)
    logits = hw[:, :LANES] + (hw[:, LANES:] + jnp.dot(h_lo, wr_ref[:, :LANES], preferred_element_type=F32))
    lane = lax.broadcasted_iota(jnp.int32, logits.shape, 1)
    logits = jnp.where(lane < N_EXPERTS, logits, -jnp.inf)
    m1 = jnp.max(logits, axis=-1, keepdims=True)
    i1 = jnp.min(jnp.where(logits == m1, lane, LANES), axis=-1, keepdims=True)
    rest = jnp.where(lane == i1, -jnp.inf, logits)
    m2 = jnp.max(rest, axis=-1, keepdims=True)
    i2 = jnp.min(jnp.where(rest == m2, lane, LANES), axis=-1, keepdims=True)
    e2 = jnp.exp(m2 - m1)
    w1 = 1.0 / (1.0 + e2)
    w2 = e2 / (1.0 + e2)
    rw_ref[...] = jnp.where(lane == 0, w1, jnp.where(lane == 1, w2, 0.0))

    i = pl.program_id(0)

    @pl.when(i == 0)
    def _():
        cnt_ref[...] = jnp.zeros_like(cnt_ref)

    onehot = jnp.where((lane == i1) | (lane == i2), 1.0, 0.0)
    prefix = jnp.dot(before_ref[...], onehot.astype(BF16), preferred_element_type=F32) + cnt_ref[...]
    r1 = jnp.sum(jnp.where(lane == i1, prefix, 0.0), axis=-1, keepdims=True)
    r2 = jnp.sum(jnp.where(lane == i2, prefix, 0.0), axis=-1, keepdims=True)
    cnt_ref[...] += jnp.sum(onehot, axis=0, keepdims=True)
    cols = jnp.where(lane == 0, i1.astype(F32), jnp.where(lane == 1, i2.astype(F32),
                                                          jnp.where(lane == 2, r1, jnp.where(lane == 3, r2, 0.0))))
    ridx_ref[...] = cols.T[:ridx_ref.shape[0], :].astype(jnp.int32)


def _odd_out(x, yc, yd, mod, nw, w_out, w_router, tm=1024):
    s, d = x.shape
    return pl.pallas_call(
        _odd_out_kernel,
        grid=(s // tm,),
        in_specs=[
            pl.BlockSpec((tm, d), lambda i: (i, 0)),
            pl.BlockSpec((tm, C_WIDTH), lambda i: (i, 0)),
            pl.BlockSpec((tm, MLA_HEADS * V_HD), lambda i: (i, 0)),
            _const_spec(mod.shape), _const_spec(nw.shape), _const_spec(w_out.shape), _const_spec(w_router.shape),
            _const_spec((tm, tm)),
        ],
        out_specs=[
            pl.BlockSpec((tm, d), lambda i: (i, 0)),
            pl.BlockSpec((tm, d // 2), lambda i: (i, 0)),
            pl.BlockSpec((tm, LANES), lambda i: (i, 0)),
            pl.BlockSpec((SUBLANES, tm), lambda i: (0, i)),
            pl.BlockSpec((1, LANES), lambda i: (0, 0)),
        ],
        out_shape=[
            jax.ShapeDtypeStruct((s, d), F32),
            jax.ShapeDtypeStruct((s, d // 2), jnp.uint32),
            jax.ShapeDtypeStruct((s, LANES), F32),
            jax.ShapeDtypeStruct((SUBLANES, s), jnp.int32),
            jax.ShapeDtypeStruct((1, LANES), F32),
        ],
        compiler_params=_params(("arbitrary",)),
        name="odd_out_router",
    )(x, yc, yd, mod, nw, w_out, w_router, jnp.tril(jnp.ones((tm, tm), BF16), -1))


def _sc_workers():
    info = plsc.get_sparse_core_info()
    return info.num_cores, info.num_cores * info.num_subcores


def _sc_scatter_rows(x, idx0, idx1, out_rows):
    n, w = x.shape
    nc, nw = _sc_workers()
    per_w = n // nw
    nch = per_w // SC_ROWS
    assert nch % 2 == 0 and nch >= 2
    mesh = plsc.VectorSubcoreMesh(core_axis_name="c", subcore_axis_name="s")

    @functools.partial(
        pl.kernel, mesh=mesh, out_type=jax.ShapeDtypeStruct((out_rows, w), x.dtype),
        scratch_types=[pltpu.VMEM((nch, SC_ROWS), jnp.int32), pltpu.VMEM((nch, SC_ROWS), jnp.int32),
                       pltpu.VMEM((SC_ROWS, w), x.dtype), pltpu.VMEM((SC_ROWS, w), x.dtype),
                       pltpu.SemaphoreType.DMA, pltpu.SemaphoreType.DMA, pltpu.SemaphoreType.DMA,
                       pltpu.SemaphoreType.DMA],
        name="moe_dispatch")
    def scatter(x_hbm, i0_hbm, i1_hbm, out_hbm, i0_v, i1_v, buf0, buf1, lsem0, lsem1, ssem0, ssem1):
        wid = lax.axis_index("s") * nc + lax.axis_index("c")
        pltpu.sync_copy(i0_hbm.at[wid], i0_v)
        pltpu.sync_copy(i1_hbm.at[wid], i1_v)
        base = wid * per_w

        def load(c, buf, sem):
            return pltpu.make_async_copy(x_hbm.at[pl.ds(base + c * SC_ROWS, SC_ROWS)], buf, sem)

        def put(c, buf):
            first = pltpu.make_async_copy(buf, out_hbm.at[i0_v.at[c]], ssem0)
            second = pltpu.make_async_copy(buf, out_hbm.at[i1_v.at[c]], ssem1)
            first.start()
            second.start()
            first.wait()
            second.wait()

        def pair(c, prefetch_next):
            load(c + 1, buf1, lsem1).start()
            load(c, buf0, lsem0).wait()
            put(c, buf0)
            if prefetch_next:
                load(c + 2, buf0, lsem0).start()
            load(c + 1, buf1, lsem1).wait()
            put(c + 1, buf1)

        load(0, buf0, lsem0).start()

        def body(t, carry):
            pair(2 * t, True)
            return carry

        lax.fori_loop(0, nch // 2 - 1, body, 0)
        pair(nch - 2, False)

    return scatter(x, idx0.reshape(nw, nch, SC_ROWS), idx1.reshape(nw, nch, SC_ROWS))


def _sc_gather_rows(table, idx):
    _, w = table.shape
    b = idx.shape[0]
    nc, nw = _sc_workers()
    per_w = b // nw
    nch = per_w // SC_ROWS
    mesh = plsc.VectorSubcoreMesh(core_axis_name="c", subcore_axis_name="s")

    assert nch % 2 == 0 and nch >= 2

    @functools.partial(
        pl.kernel, mesh=mesh, out_type=jax.ShapeDtypeStruct((b, w), table.dtype),
        scratch_types=[pltpu.VMEM((nch, SC_ROWS), jnp.int32), pltpu.VMEM((SC_ROWS, w), table.dtype),
                       pltpu.VMEM((SC_ROWS, w), table.dtype), pltpu.SemaphoreType.DMA, pltpu.SemaphoreType.DMA],
        name="moe_combine_gather")
    def gather(table_hbm, idx_hbm, out_hbm, idx_v, buf0, buf1, sem0, sem1):
        wid = lax.axis_index("s") * nc + lax.axis_index("c")
        pltpu.sync_copy(idx_hbm.at[wid], idx_v)
        base = wid * per_w

        def fetch(c, buf, sem):
            return pltpu.make_async_copy(table_hbm.at[idx_v.at[c]], buf, sem)

        def put(c, buf):
            pltpu.sync_copy(buf, out_hbm.at[pl.ds(base + c * SC_ROWS, SC_ROWS)])

        def pair(c, prefetch_next):
            fetch(c + 1, buf1, sem1).start()
            fetch(c, buf0, sem0).wait()
            put(c, buf0)
            if prefetch_next:
                fetch(c + 2, buf0, sem0).start()
            fetch(c + 1, buf1, sem1).wait()
            put(c + 1, buf1)

        fetch(0, buf0, sem0).start()

        def body(t, carry):
            pair(2 * t, True)
            return carry

        lax.fori_loop(0, nch // 2 - 1, body, 0)
        pair(nch - 2, False)

    return gather(table, idx.reshape(nw, nch, SC_ROWS))


def _moe_kernel(te_ref, nv_ref, xs_ref, wg_ref, wu_ref, wd_ref, ys_ref, x_scr, acc_ref):
    j = pl.program_id(0)
    f = pl.program_id(1)
    nf = pl.num_programs(1)

    @pl.when(j < nv_ref[0])
    def _():
        @pl.when(f == 0)
        def _():
            x_scr[...] = _unpack_bf16_pairs(xs_ref[...]).astype(BF16)

        h = x_scr[...]
        g = jnp.dot(h, wg_ref[0], preferred_element_type=F32)
        u = jnp.dot(h, wu_ref[0], preferred_element_type=F32)
        act = (g * jax.nn.sigmoid(g) * u).astype(BF16)
        y = jnp.dot(act, wd_ref[0], preferred_element_type=F32)

        @pl.when(f == 0)
        def _():
            acc_ref[...] = y

        @pl.when(f > 0)
        def _():
            acc_ref[...] += y

        @pl.when(f == nf - 1)
        def _():
            ys_ref[...] = _pack_bf16_pairs(acc_ref[...])


def _moe_grouped(xs, tile_expert, n_valid, w_gu, w_down, tm, tf=1792):
    p_rows, dh = xs.shape
    d = 2 * dh
    ne, _, ff2 = w_gu.shape
    ff = ff2 // 2
    nf = ff // tf

    def tile(j, nv):
        return jnp.minimum(j, nv[0] - 1)

    def chunk(j, f, nv):
        return jnp.where(j < nv[0], f, nf - 1)

    grid_spec = pltpu.PrefetchScalarGridSpec(
        num_scalar_prefetch=2,
        grid=(p_rows // tm, nf),
        in_specs=[
            pl.BlockSpec((tm, dh), lambda j, f, te, nv: (tile(j, nv), 0)),
            pl.BlockSpec((1, d, tf), lambda j, f, te, nv: (te[tile(j, nv)], 0, chunk(j, f, nv))),
            pl.BlockSpec((1, d, tf), lambda j, f, te, nv: (te[tile(j, nv)], 0, nf + chunk(j, f, nv))),
            pl.BlockSpec((1, tf, d), lambda j, f, te, nv: (te[tile(j, nv)], chunk(j, f, nv), 0)),
        ],
        out_specs=pl.BlockSpec((tm, dh), lambda j, f, te, nv: (tile(j, nv), 0)),
        scratch_shapes=[pltpu.VMEM((tm, d), BF16), pltpu.VMEM((tm, d), F32)],
    )
    return pl.pallas_call(
        _moe_kernel,
        grid_spec=grid_spec,
        out_shape=jax.ShapeDtypeStruct((p_rows, dh), jnp.uint32),
        compiler_params=_params(("arbitrary", "arbitrary")),
        name="moe_experts",
    )(tile_expert, n_valid, xs, w_gu, w_gu, w_down)


def _moe_combine_kernel(x_ref, y0_ref, y1_ref, rw_ref, mod_ref, *rest):
    o_ref = rest[-1]
    d = D_MODEL
    w1 = rw_ref[:, 0:1]
    w2 = rw_ref[:, 1:2]
    y = w1 * _unpack_bf16_pairs(y0_ref[...]) + w2 * _unpack_bf16_pairs(y1_ref[...])
    o_ref[...] = x_ref[...] + mod_ref[:, 5 * d:6 * d] * y


def _moe_combine(x, yg, rw, mod, prev, part, n_parts, tm=512):
    s, d = x.shape
    nb = s // tm // n_parts
    first = part * nb
    in_specs = [
        pl.BlockSpec((tm, d), lambda i: (first + i, 0)),
        pl.BlockSpec((tm, d // 2), lambda i: (i, 0)),
        pl.BlockSpec((tm, d // 2), lambda i: (nb + i, 0)),
        pl.BlockSpec((tm, LANES), lambda i: (first + i, 0)),
        _const_spec(mod.shape),
    ]
    args = [x, yg, yg, rw, mod]
    aliases = {}
    if prev is not None:
        in_specs.append(pl.BlockSpec(memory_space=pl.ANY))
        args.append(prev)
        aliases = {len(args) - 1: 0}
    return pl.pallas_call(
        _moe_combine_kernel,
        grid=(nb,),
        in_specs=in_specs,
        out_specs=pl.BlockSpec((tm, d), lambda i: (first + i, 0)),
        out_shape=jax.ShapeDtypeStruct((s, d), F32),
        input_output_aliases=aliases,
        compiler_params=_params(("arbitrary",)),
        name="moe_combine",
    )(*args)


def _moe_sparse(x, h_packed, rw, ridx, counts, mod, w_gu, w_down, tm=MOE_TILE):
    s = x.shape[0]
    ne = w_gu.shape[0]
    n_tiles = (2 * s) // tm + ne
    cnt = counts[0, :ne].astype(jnp.int32)
    padded = ((cnt + tm - 1) // tm) * tm
    ends = jnp.cumsum(padded)
    offs = ends - padded
    experts = jnp.arange(ne, dtype=jnp.int32)
    off_of = lambda e: jnp.sum(jnp.where(e[:, None] == experts[None, :], offs[None, :], 0), axis=1)
    pos0 = off_of(ridx[0]) + ridx[2]
    pos1 = off_of(ridx[1]) + ridx[3]
    tile_start = jnp.arange(n_tiles, dtype=jnp.int32) * tm
    tile_expert = jnp.minimum(jnp.sum(tile_start[:, None] >= ends[None, :], axis=1), ne - 1).astype(jnp.int32)
    n_valid = (ends[-1] // tm).reshape(1).astype(jnp.int32)
    xs = _sc_scatter_rows(h_packed, pos0, pos1, n_tiles * tm)
    ys = _moe_grouped(xs, tile_expert, n_valid, w_gu, w_down, tm)
    out = None
    sp = s // COMBINE_PARTS
    for part in range(COMBINE_PARTS):
        rows = slice(part * sp, (part + 1) * sp)
        yg = _sc_gather_rows(ys, jnp.concatenate([pos0[rows], pos1[rows]]))
        out = _moe_combine(x, yg, rw, mod, out, part, COMBINE_PARTS)
    return out


def kernel(x, c, positions, norm_mix_w, norm_ffn_w, ada_w, ada_b, e_w_in, a_ln_w, a_ln_b, a_w_s, a_b_s, b_w_grp,
           b_scale, e_w_out, ffn_w_gu, ffn_w_down, o_w_in, c_conv_w, q_a_norm, w_uq, kv_norm, w_ukv, q_norm_w,
           k_norm_w, o_w_out, router_w, moe_w_gu, moe_w_down):
    bsz, s, d = x.shape
    assert bsz == 1 and d == D_MODEL
    depth = ada_w.shape[0]
    nh = MLA_HEADS
    xs = x.reshape(s, d)
    mod = _ada_mod(c, ada_w, ada_b)
    cos, sin, cos_t, sin_t = _rope_tables(positions)
    row = lambda a: a.reshape(1, -1)

    for layer in range(depth):
        i = layer // 2
        m = mod[layer]
        if layer % 2 == 0:
            has_next = layer + 1 < depth
            dummy = jnp.zeros((s // 512, 16, LANES), F32)
            xs, moe_w_down_bf16 = _even_mix(
                xs, m, row(norm_mix_w[layer]), e_w_in[i].astype(BF16), row(a_ln_w[i]), row(a_ln_b[i]), a_w_s[i],
                a_b_s[i].reshape(A_GROUPS, CHUNK, 1), b_w_grp[i].astype(BF16), row(b_scale[i]),
                e_w_out[i].astype(BF16), moe_w_down[i] if has_next else dummy)
            xs, moe_w_gu_bf16 = _ffn(xs, m, row(norm_ffn_w[layer]), ffn_w_gu[i].astype(BF16),
                                     ffn_w_down[i].astype(BF16), moe_w_gu[i] if has_next else dummy)
        else:
            o2 = 3 * C_WIDTH + Q_LORA + KV_LORA
            w_in = o_w_in[i][:, :o2].astype(BF16)
            w_kpe = jnp.pad(o_w_in[i][:, o2:], ((0, 0), (0, LANES - QK_ROPE))).astype(BF16)
            wkv = w_ukv[i].reshape(KV_LORA, nh, QK_NOPE + V_HD)
            w_k = wkv[:, :, :QK_NOPE].reshape(KV_LORA, nh * QK_NOPE).astype(BF16)
            w_v_t = wkv[:, :, QK_NOPE:].reshape(KV_LORA, nh * V_HD).T.astype(BF16)
            yc, qt, k, vt = _odd_prep(
                xs, m, row(norm_mix_w[layer]), w_in, w_kpe, c_conv_w[i], row(q_a_norm[i]), w_uq[i].T.astype(BF16),
                row(kv_norm[i]), w_k, w_v_t, q_norm_w[i].reshape(QK_HD, 1), row(k_norm_w[i]), cos, sin, cos_t, sin_t)
            yd = _attention(qt, k, vt)
            wr = jnp.pad(router_w[i], ((0, 0), (0, LANES - N_EXPERTS)))
            wr_hi = lax.reduce_precision(wr, exponent_bits=8, mantissa_bits=7)
            w_router = jnp.concatenate([wr_hi, wr - wr_hi], axis=1).astype(BF16)
            xs, hp, rw, ridx, counts = _odd_out(xs, yc, yd, m, row(norm_ffn_w[layer]), o_w_out[i].astype(BF16),
                                                w_router)
            xs = _moe_sparse(xs, hp, rw, ridx, counts, m, moe_w_gu_bf16, moe_w_down_bf16)
    return xs.reshape(bsz, s, d)
```

```python
import functools

import jax
import jax.numpy as jnp
from jax import lax
from jax.experimental import pallas as pl
from jax.experimental.pallas import tpu as pltpu
from jax.experimental.pallas import tpu_sc as plsc

D_MODEL = 1024
SEQ = 16384
EPS = 1e-6
CHUNK = 128
A_WIDTH = 512
A_GROUPS = 4
B_WIDTH = 512
POOL_WINDOWS = (2, 4, 8, 16)
B_HD = 128
C_WIDTH = 512
MLA_HEADS = 4
Q_LORA = 256
KV_LORA = 256
QK_NOPE = 128
QK_ROPE = 64
QK_HD = QK_NOPE + QK_ROPE
V_HD = 128
ROPE_THETA = 10000.0
D_FF = 2816
N_EXPERTS = 8
D_FF_EXPERT = 3584

LANES = 128
SUBLANES = 8
POOL_HALO = 16
CONV_HALO = 8
QK_PAD = 256
V_PAD = 144
LOG2E = 1.4426950408889634
MASK_VALUE = -1e30
MOE_TILE = 512
MOE_ROW_STEP = 128
COMBINE_PARTS = 4
SC_ROWS = 64
VMEM_LIMIT = 56 * 1024 * 1024

F32 = jnp.float32
BF16 = jnp.bfloat16


def _params(sem, vmem=VMEM_LIMIT, flags=None):
    return pltpu.CompilerParams(dimension_semantics=sem, vmem_limit_bytes=vmem, flags=flags)


def _const_spec(shape, single=False):
    nd = len(shape)
    return pl.BlockSpec(shape, lambda *_: (0,) * nd, pipeline_mode=pl.Buffered(1) if single else None)


def _rms_mod(x, nw, sc, sh):
    ms = jnp.mean(x * x, axis=-1, keepdims=True)
    return (x * lax.rsqrt(ms + EPS)) * nw * (1.0 + sc) + sh


def _ada_kernel(c_ref, w_ref, b_ref, o_ref):
    c = c_ref[...]
    ca = c * jax.nn.sigmoid(c)
    o_ref[0] = jnp.sum(w_ref[0] * ca, axis=0, keepdims=True) + b_ref[0]


def _ada_mod(c, ada_w, ada_b):
    depth, d, n = ada_w.shape
    tn = 1536
    return pl.pallas_call(
        _ada_kernel,
        grid=(depth, n // tn),
        in_specs=[
            pl.BlockSpec((d, 1), lambda l, j: (0, 0)),
            pl.BlockSpec((1, d, tn), lambda l, j: (l, 0, j)),
            pl.BlockSpec((1, 1, tn), lambda l, j: (l, 0, j)),
        ],
        out_specs=pl.BlockSpec((1, 1, tn), lambda l, j: (l, 0, j)),
        out_shape=jax.ShapeDtypeStruct((depth, 1, n), F32),
        compiler_params=_params(("arbitrary", "arbitrary")),
        name="ada_mod",
    )(c.reshape(d, 1), ada_w, ada_b.reshape(depth, 1, n))


def _even_mix_kernel(x_ref, mod_ref, nw_ref, win_ref, lnw_ref, lnb_ref, ws_ref, bs_ref, wg_ref, bsc_ref,
                     wout_ref, side_ref, o_ref, side_out_ref, halo_ref, sv_ref, yb_ref):
    side_out_ref[...] = side_ref[...].astype(BF16)
    tm = x_ref.shape[0]
    i = pl.program_id(0)

    @pl.when(i == 0)
    def _():
        halo_ref[...] = jnp.zeros_like(halo_ref)

    d = D_MODEL
    x = x_ref[...]
    sh, sc, gate = mod_ref[:, 0:d], mod_ref[:, d:2 * d], mod_ref[:, 2 * d:3 * d]
    h = _rms_mod(x, nw_ref[...], sc, sh).astype(BF16)
    p = jnp.dot(h, win_ref[...], preferred_element_type=F32)

    gl = jax.nn.gelu(p[:, :2 * A_WIDTH])
    u = gl[:, :A_WIDTH]
    v = gl[:, A_WIDTH:]
    mu = jnp.mean(v, axis=-1, keepdims=True)
    vc = v - mu
    var = jnp.mean(vc * vc, axis=-1, keepdims=True)
    vn = (vc * lax.rsqrt(var + EPS) * lnw_ref[...] + lnb_ref[...]).astype(BF16)
    row = lax.broadcasted_iota(jnp.int32, (CHUNK, CHUNK), 0)
    col = lax.broadcasted_iota(jnp.int32, (CHUNK, CHUNK), 1)
    for g in range(A_GROUPS):
        w = jnp.where(col <= row, ws_ref[g], 0.0).astype(BF16)
        b = bs_ref[g]
        for c in range(tm // CHUNK):
            blk = vn[c * CHUNK:(c + 1) * CHUNK, g * LANES:(g + 1) * LANES]
            sv_ref[c * CHUNK:(c + 1) * CHUNK, g * LANES:(g + 1) * LANES] = (
                jnp.dot(w, blk, preferred_element_type=F32) + b)
    ya = (u * sv_ref[...]).astype(BF16)

    pb = p[:, 2 * A_WIDTH:]
    ext = jnp.concatenate([halo_ref[...], pb], axis=0)
    halo_ref[...] = pb[tm - POOL_HALO:, :]
    t_glob = i * tm + lax.broadcasted_iota(jnp.int32, (tm, 1), 0)
    s = ext
    width = 1
    for g, win in enumerate(POOL_WINDOWS):
        while width < win:
            s = s + pltpu.roll(s, width, 0)
            width *= 2
        cnt = jnp.minimum(t_glob + 1, win).astype(F32)
        sl = slice(g * B_HD, (g + 1) * B_HD)
        pooled = s[POOL_HALO:, sl] / cnt
        dg = (pooled - pb[:, sl]).astype(BF16)
        yb_ref[:, sl] = jnp.dot(dg, wg_ref[g], preferred_element_type=F32)
    yb = (yb_ref[...] * bsc_ref[...]).astype(BF16)

    mix = (jnp.dot(ya, wout_ref[:A_WIDTH, :], preferred_element_type=F32)
           + jnp.dot(yb, wout_ref[A_WIDTH:, :], preferred_element_type=F32))
    o_ref[...] = x + gate * mix


def _side_cast_spec(side, steps, axis):
    ne = side.shape[0]
    parts = steps // ne
    assert parts * ne == steps and side.shape[axis] % parts == 0
    block = list(side.shape)
    block[0] = 1
    block[axis] //= parts
    assert block[1] % 16 == 0 and block[2] % LANES == 0
    if axis == 1:
        return pl.BlockSpec(tuple(block), lambda i: (i // parts, i % parts, 0))
    return pl.BlockSpec(tuple(block), lambda i: (i // parts, 0, i % parts))


def _even_mix(x, mod, nw, w_in, ln_w, ln_b, w_s, b_s, w_grp, b_scale, w_out, side, tm=512):
    s, d = x.shape
    steps = s // tm
    side_spec = _side_cast_spec(side, steps, 1)
    return pl.pallas_call(
        _even_mix_kernel,
        grid=(steps,),
        in_specs=[
            pl.BlockSpec((tm, d), lambda i: (i, 0)),
            _const_spec(mod.shape), _const_spec(nw.shape), _const_spec(w_in.shape),
            _const_spec(ln_w.shape), _const_spec(ln_b.shape), _const_spec(w_s.shape), _const_spec(b_s.shape),
            _const_spec(w_grp.shape), _const_spec(b_scale.shape), _const_spec(w_out.shape), side_spec,
        ],
        out_specs=[pl.BlockSpec((tm, d), lambda i: (i, 0)), side_spec],
        out_shape=[jax.ShapeDtypeStruct((s, d), F32), jax.ShapeDtypeStruct(side.shape, BF16)],
        scratch_shapes=[pltpu.VMEM((POOL_HALO, B_WIDTH), F32), pltpu.VMEM((tm, A_WIDTH), F32),
                        pltpu.VMEM((tm, B_WIDTH), F32)],
        compiler_params=_params(("arbitrary",)),
        name="even_mix",
    )(x, mod, nw, w_in, ln_w, ln_b, w_s, b_s, w_grp, b_scale, w_out, side)


def _ffn_kernel(x_ref, mod_ref, nw_ref, wgu_ref, wd_ref, side_ref, o_ref, side_out_ref, *, n_chunks):
    side_out_ref[...] = side_ref[...].astype(BF16)
    d = D_MODEL
    x = x_ref[...]
    sh, sc, gate = mod_ref[:, 3 * d:4 * d], mod_ref[:, 4 * d:5 * d], mod_ref[:, 5 * d:6 * d]
    h = _rms_mod(x, nw_ref[...], sc, sh).astype(BF16)
    ff = wd_ref.shape[0]
    tf = ff // n_chunks
    acc = jnp.zeros(x.shape, F32)
    for f in range(n_chunks):
        g = jnp.dot(h, wgu_ref[:, f * tf:(f + 1) * tf], preferred_element_type=F32)
        u = jnp.dot(h, wgu_ref[:, ff + f * tf:ff + (f + 1) * tf], preferred_element_type=F32)
        act = (g * jax.nn.sigmoid(g) * u).astype(BF16)
        acc = acc + jnp.dot(act, wd_ref[f * tf:(f + 1) * tf, :], preferred_element_type=F32)
    o_ref[...] = x + gate * acc


def _ffn(x, mod, nw, w_gu, w_down, side, tm=512, n_chunks=11):
    s, d = x.shape
    steps = s // tm
    side_spec = _side_cast_spec(side, steps, 2)
    return pl.pallas_call(
        functools.partial(_ffn_kernel, n_chunks=n_chunks),
        grid=(steps,),
        in_specs=[
            pl.BlockSpec((tm, d), lambda i: (i, 0)),
            _const_spec(mod.shape), _const_spec(nw.shape), _const_spec(w_gu.shape, single=True),
            _const_spec(w_down.shape, single=True), side_spec,
        ],
        out_specs=[pl.BlockSpec((tm, d), lambda i: (i, 0)), side_spec],
        out_shape=[jax.ShapeDtypeStruct((s, d), F32), jax.ShapeDtypeStruct(side.shape, BF16)],
        compiler_params=_params(("arbitrary",)),
        name="ffn",
    )(x, mod, nw, w_gu, w_down, side)


def _rope_table_kernel(pos_ref, invf_ref, cos_ref, sin_ref):
    ang = pos_ref[...].astype(F32) * invf_ref[...]
    cos_ref[...] = jnp.cos(ang)
    sin_ref[...] = jnp.sin(ang)


def _rope_tables(positions):
    s = positions.shape[-1]
    half = QK_ROPE // 2
    inv_freq = ROPE_THETA ** (-jnp.arange(0, QK_ROPE, 2, dtype=F32) / QK_ROPE)
    cos_t, sin_t = pl.pallas_call(
        _rope_table_kernel,
        out_shape=(jax.ShapeDtypeStruct((half, s), F32), jax.ShapeDtypeStruct((half, s), F32)),
        name="rope_tables",
    )(positions.reshape(1, s), inv_freq.reshape(half, 1))
    return jnp.tile(cos_t.T, (1, 2)), jnp.tile(sin_t.T, (1, 2)), cos_t, sin_t


def _rope_rot(x):
    n = x.shape[-1]
    lane = lax.broadcasted_iota(jnp.int32, x.shape, 1)
    half = QK_ROPE // 2
    fwd = pltpu.roll(x, half, 1)
    bwd = pltpu.roll(x, n - half, 1)
    return jnp.where((lane % QK_ROPE) < half, -bwd, fwd)


def _odd_prep_kernel(x_ref, mod_ref, nw_ref, win_ref, wkpe_ref, cw_ref, qan_ref, wuqt_ref, kvn_ref, wk_ref, wvt_ref,
                     qnw_ref, knw_ref, cos_ref, sin_ref, cost_ref, sint_ref, yc_ref, qt_ref, k_ref, vt_ref,
                     halo_ref):
    tm = x_ref.shape[0]
    i = pl.program_id(0)

    @pl.when(i == 0)
    def _():
        halo_ref[...] = jnp.zeros_like(halo_ref)

    d = D_MODEL
    x = x_ref[...]
    sh, sc = mod_ref[:, 0:d], mod_ref[:, d:2 * d]
    h = _rms_mod(x, nw_ref[...], sc, sh).astype(BF16)
    p = jnp.dot(h, win_ref[...], preferred_element_type=F32)
    kpe = jnp.dot(h, wkpe_ref[...], preferred_element_type=F32)[:, :QK_ROPE]

    cw = C_WIDTH
    bg, cg, hh = p[:, 0:cw], p[:, cw:2 * cw], p[:, 2 * cw:3 * cw]
    z = cg * hh
    ext = jnp.concatenate([halo_ref[...], z], axis=0)
    halo_ref[...] = z[tm - CONV_HALO:, :]
    z1 = pltpu.roll(ext, 1, 0)[CONV_HALO:, :]
    z2 = pltpu.roll(ext, 2, 0)[CONV_HALO:, :]
    conv = cw_ref[0:1, :] * z2 + cw_ref[1:2, :] * z1 + cw_ref[2:3, :] * z
    yc_ref[...] = (bg * conv).astype(BF16)

    o0 = 3 * cw
    cq = p[:, o0:o0 + Q_LORA]
    ckv = p[:, o0 + Q_LORA:o0 + Q_LORA + KV_LORA]
    cqn = cq * lax.rsqrt(jnp.mean(cq * cq, axis=-1, keepdims=True) + EPS) * qan_ref[...]
    ckvn = ckv * lax.rsqrt(jnp.mean(ckv * ckv, axis=-1, keepdims=True) + EPS) * kvn_ref[...]
    cqn_t = cqn.T.astype(BF16)
    ckvn_t = ckvn.T.astype(BF16)
    q_t = jnp.dot(wuqt_ref[...], cqn_t, preferred_element_type=F32)
    v_t = jnp.dot(wvt_ref[...], ckvn_t, preferred_element_type=F32)
    kn_all = jnp.dot(ckvn.astype(BF16), wk_ref[...], preferred_element_type=F32)

    nh = MLA_HEADS
    half = QK_ROPE // 2
    sm_scale = QK_HD ** -0.5
    cos_t, sin_t = cost_ref[...], sint_ref[...]
    kn_w, kr_w = knw_ref[:, :QK_NOPE], knw_ref[:, QK_NOPE:]
    kr = kpe * kr_w
    kr = kr * cos_ref[...] + _rope_rot(kr) * sin_ref[...]
    kpe_ss = jnp.sum(kpe * kpe, axis=-1, keepdims=True)
    zrows = jnp.zeros((QK_PAD - QK_HD, tm), F32)
    zpad = jnp.zeros((tm, QK_PAD - QK_HD), F32)
    ones_rows = (lax.broadcasted_iota(jnp.int32, (V_PAD - V_HD, tm), 0) == 0).astype(BF16)
    for hd in range(nh):
        qh = q_t[hd * QK_HD:(hd + 1) * QK_HD, :]
        q_inv = lax.rsqrt(jnp.sum(qh * qh, axis=0, keepdims=True) / QK_HD + EPS) * (sm_scale * LOG2E)
        qw = qh * qnw_ref[...]
        x1, x2 = qw[QK_NOPE:QK_NOPE + half, :], qw[QK_NOPE + half:, :]
        qt_ref[hd] = jnp.concatenate(
            [qw[:QK_NOPE, :] * q_inv, (x1 * cos_t - x2 * sin_t) * q_inv, (x2 * cos_t + x1 * sin_t) * q_inv, zrows],
            axis=0).astype(BF16)
        kn = kn_all[:, hd * QK_NOPE:(hd + 1) * QK_NOPE]
        k_ss = jnp.sum(kn * kn, axis=-1, keepdims=True) + kpe_ss
        k_inv = lax.rsqrt(k_ss / QK_HD + EPS)
        k_ref[hd] = jnp.concatenate([kn * k_inv * kn_w, kr * k_inv, zpad], axis=1).astype(BF16)
        vt_ref[hd, :V_HD, :] = v_t[hd * V_HD:(hd + 1) * V_HD, :].astype(BF16)
        vt_ref[hd, V_HD:, :] = ones_rows


def _odd_prep(x, mod, nw, w_in, w_kpe, conv_w, q_a_norm, w_uq_t, kv_norm, w_k, w_v_t, q_norm_col, k_norm_w, cos, sin,
              cos_t, sin_t, tm=1024):
    s, d = x.shape
    nh = MLA_HEADS
    half = QK_ROPE // 2
    consts = [mod, nw, w_in, w_kpe, conv_w, q_a_norm, w_uq_t, kv_norm, w_k, w_v_t, q_norm_col, k_norm_w]
    return pl.pallas_call(
        _odd_prep_kernel,
        grid=(s // tm,),
        in_specs=[pl.BlockSpec((tm, d), lambda i: (i, 0))] + [_const_spec(a.shape) for a in consts] + [
            pl.BlockSpec((tm, QK_ROPE), lambda i: (i, 0)), pl.BlockSpec((tm, QK_ROPE), lambda i: (i, 0)),
            pl.BlockSpec((half, tm), lambda i: (0, i)), pl.BlockSpec((half, tm), lambda i: (0, i))],
        out_specs=[
            pl.BlockSpec((tm, C_WIDTH), lambda i: (i, 0)),
            pl.BlockSpec((nh, QK_PAD, tm), lambda i: (0, 0, i)),
            pl.BlockSpec((nh, tm, QK_PAD), lambda i: (0, i, 0)),
            pl.BlockSpec((nh, V_PAD, tm), lambda i: (0, 0, i)),
        ],
        out_shape=[
            jax.ShapeDtypeStruct((s, C_WIDTH), BF16),
            jax.ShapeDtypeStruct((nh, QK_PAD, s), BF16),
            jax.ShapeDtypeStruct((nh, s, QK_PAD), BF16),
            jax.ShapeDtypeStruct((nh, V_PAD, s), BF16),
        ],
        scratch_shapes=[pltpu.VMEM((CONV_HALO, C_WIDTH), F32)],
        compiler_params=_params(("arbitrary",)),
        name="odd_prep",
    )(x, *consts, cos, sin, cos_t, sin_t)


def _attn_kernel(qt_ref, k_ref, vt_ref, o_ref, s0_ref, s1_ref, bm0_ref, bm1_ref, m_ref, acc_ref, *, tq, tk):
    qi = pl.program_id(1)
    m_ref[...] = jnp.full_like(m_ref, MASK_VALUE)
    acc_ref[...] = jnp.zeros_like(acc_ref)

    def scores(j, s_ref, bm_ref, diag_offset=None):
        q0 = 0 if diag_offset is None else diag_offset
        start = pl.multiple_of(j * tk, tk)
        s = jnp.dot(k_ref[0, pl.ds(start, tk), :], qt_ref[0, :, q0:], preferred_element_type=F32)
        if diag_offset is not None:
            key = lax.broadcasted_iota(jnp.int32, s.shape, 0)
            qry = lax.broadcasted_iota(jnp.int32, s.shape, 1)
            s = jnp.where(key <= qry, s, MASK_VALUE)
        s_ref[:, q0:] = s
        bm_ref[:, q0:] = jnp.max(s, axis=0, keepdims=True)

    def consume(j, s_ref, bm_ref, q0=0):
        m_prev = m_ref[:, q0:]
        m_new = jnp.maximum(m_prev, bm_ref[:, q0:])
        alpha = jnp.exp2(m_prev - m_new)
        p = jnp.exp2((s_ref[:, q0:] - m_new).astype(BF16))
        start = pl.multiple_of(j * tk, tk)
        vt = vt_ref[0, :, pl.ds(start, tk)]
        acc_ref[:, q0:] = alpha * acc_ref[:, q0:] + jnp.dot(vt, p, preferred_element_type=F32)
        m_ref[:, q0:] = m_new

    r = tq // tk
    slots = ((s0_ref, bm0_ref), (s1_ref, bm1_ref))

    def trip(j0, then_diagonal):
        for b in range(r):
            first_diag = then_diagonal and b == r - 1
            scores(j0 + b + 1, *slots[(b + 1) % 2], diag_offset=0 if first_diag else None)
            consume(j0 + b, *slots[b % 2])

    def diagonal(jd):
        for b in range(r):
            if b + 1 < r:
                scores(jd + b + 1, *slots[(b + 1) % 2], diag_offset=(b + 1) * tk)
            consume(jd + b, *slots[b % 2], q0=b * tk)

    @pl.when(qi > 0)
    def _():
        scores(0, *slots[0])

    n_plain = jnp.maximum(qi - 1, 0)

    def body(t, carry):
        trip(r * (2 * t), False)
        trip(r * (2 * t + 1), False)
        return carry

    lax.fori_loop(0, n_plain // 2, body, 0)

    @pl.when(n_plain % 2 == 1)
    def _():
        trip(r * (n_plain - 1), False)

    @pl.when(qi > 0)
    def _():
        trip(r * (qi - 1), True)
        diagonal(r * qi)

    @pl.when(qi == 0)
    def _():
        scores(0, *slots[0], diag_offset=0)
        diagonal(0)

    acc = acc_ref[...]
    out_t = acc[:V_HD, :] / acc[V_HD:V_HD + 1, :]
    o_ref[...] = out_t.T.astype(o_ref.dtype)


def _attention(qt, k, vt, tq=1024, tk=512):
    nh, _, s = qt.shape
    assert tq % (2 * tk) == 0
    return pl.pallas_call(
        functools.partial(_attn_kernel, tq=tq, tk=tk),
        grid=(nh, s // tq),
        in_specs=[
            pl.BlockSpec((1, QK_PAD, tq), lambda h, i: (h, 0, i)),
            pl.BlockSpec((1, s, QK_PAD), lambda h, i: (h, 0, 0)),
            pl.BlockSpec((1, V_PAD, s), lambda h, i: (h, 0, 0)),
        ],
        out_specs=pl.BlockSpec((tq, V_HD), lambda h, i: (i, h)),
        out_shape=jax.ShapeDtypeStruct((s, nh * V_HD), BF16),
        scratch_shapes=[pltpu.VMEM((tk, tq), F32), pltpu.VMEM((tk, tq), F32), pltpu.VMEM((1, tq), F32),
                        pltpu.VMEM((1, tq), F32), pltpu.VMEM((1, tq), F32), pltpu.VMEM((V_PAD, tq), F32)],
        compiler_params=_params(("arbitrary", "arbitrary")),
        name="mla_attention",
    )(qt, k, vt)


def _pack_bf16_pairs(x):
    n = x.shape[1] // 2
    lo = pltpu.bitcast(x[:, :n].astype(BF16).astype(F32), jnp.uint32)
    hi = pltpu.bitcast(x[:, n:].astype(BF16).astype(F32), jnp.uint32)
    return (lo >> 16) | hi


def _unpack_bf16_pairs(p):
    lo = pltpu.bitcast(p << 16, F32)
    hi = pltpu.bitcast(p & jnp.uint32(0xFFFF0000), F32)
    return jnp.concatenate([lo, hi], axis=1)


def _odd_out_kernel(x_ref, yc_ref, yd_ref, mod_ref, nw_ref, wout_ref, wr_ref, before_ref, x_out_ref, h_ref, rw_ref,
                    ridx_ref, cnt_ref):
    d = D_MODEL
    x = x_ref[...]
    gate_m = mod_ref[:, 2 * d:3 * d]
    sh, sc = mod_ref[:, 3 * d:4 * d], mod_ref[:, 4 * d:5 * d]
    mix = (jnp.dot(yc_ref[...], wout_ref[:C_WIDTH, :], preferred_element_type=F32)
           + jnp.dot(yd_ref[...], wout_ref[C_WIDTH:, :], preferred_element_type=F32))
    x1 = x + gate_m * mix
    x_out_ref[...] = x1
    h = _rms_mod(x1, nw_ref[...], sc, sh)
    h_ref[...] = _pack_bf16_pairs(h)

    h_hi = h.astype(BF16)
    h_lo = (h - h_hi.astype(F32)).astype(BF16)
    hw = jnp.dot(h_hi, wr_ref[...], preferred_element_type=F32)
    logits = hw[:, :LANES] + (hw[:, LANES:] + jnp.dot(h_lo, wr_ref[:, :LANES], preferred_element_type=F32))
    lane = lax.broadcasted_iota(jnp.int32, logits.shape, 1)
    logits = jnp.where(lane < N_EXPERTS, logits, -jnp.inf)
    m1 = jnp.max(logits, axis=-1, keepdims=True)
    i1 = jnp.min(jnp.where(logits == m1, lane, LANES), axis=-1, keepdims=True)
    rest = jnp.where(lane == i1, -jnp.inf, logits)
    m2 = jnp.max(rest, axis=-1, keepdims=True)
    i2 = jnp.min(jnp.where(rest == m2, lane, LANES), axis=-1, keepdims=True)
    e2 = jnp.exp(m2 - m1)
    w1 = 1.0 / (1.0 + e2)
    w2 = e2 / (1.0 + e2)
    rw_ref[...] = jnp.where(lane == 0, w1, jnp.where(lane == 1, w2, 0.0))

    i = pl.program_id(0)

    @pl.when(i == 0)
    def _():
        cnt_ref[...] = jnp.zeros_like(cnt_ref)

    onehot = jnp.where((lane == i1) | (lane == i2), 1.0, 0.0)
    prefix = jnp.dot(before_ref[...], onehot.astype(BF16), preferred_element_type=F32) + cnt_ref[...]
    r1 = jnp.sum(jnp.where(lane == i1, prefix, 0.0), axis=-1, keepdims=True)
    r2 = jnp.sum(jnp.where(lane == i2, prefix, 0.0), axis=-1, keepdims=True)
    cnt_ref[...] += jnp.sum(onehot, axis=0, keepdims=True)
    cols = jnp.where(lane == 0, i1.astype(F32), jnp.where(lane == 1, i2.astype(F32),
                                                          jnp.where(lane == 2, r1, jnp.where(lane == 3, r2, 0.0))))
    ridx_ref[...] = cols.T[:ridx_ref.shape[0], :].astype(jnp.int32)


def _odd_out(x, yc, yd, mod, nw, w_out, w_router, tm=1024):
    s, d = x.shape
    return pl.pallas_call(
        _odd_out_kernel,
        grid=(s // tm,),
        in_specs=[
            pl.BlockSpec((tm, d), lambda i: (i, 0)),
            pl.BlockSpec((tm, C_WIDTH), lambda i: (i, 0)),
            pl.BlockSpec((tm, MLA_HEADS * V_HD), lambda i: (i, 0)),
            _const_spec(mod.shape), _const_spec(nw.shape), _const_spec(w_out.shape), _const_spec(w_router.shape),
            _const_spec((tm, tm)),
        ],
        out_specs=[
            pl.BlockSpec((tm, d), lambda i: (i, 0)),
            pl.BlockSpec((tm, d // 2), lambda i: (i, 0)),
            pl.BlockSpec((tm, LANES), lambda i: (i, 0)),
            pl.BlockSpec((SUBLANES, tm), lambda i: (0, i)),
            pl.BlockSpec((1, LANES), lambda i: (0, 0)),
        ],
        out_shape=[
            jax.ShapeDtypeStruct((s, d), F32),
            jax.ShapeDtypeStruct((s, d // 2), jnp.uint32),
            jax.ShapeDtypeStruct((s, LANES), F32),
            jax.ShapeDtypeStruct((SUBLANES, s), jnp.int32),
            jax.ShapeDtypeStruct((1, LANES), F32),
        ],
        compiler_params=_params(("arbitrary",)),
        name="odd_out_router",
    )(x, yc, yd, mod, nw, w_out, w_router, jnp.tril(jnp.ones((tm, tm), BF16), -1))


def _sc_workers():
    info = plsc.get_sparse_core_info()
    return info.num_cores, info.num_cores * info.num_subcores


def _sc_scatter_rows(x, idx0, idx1, out_rows):
    n, w = x.shape
    nc, nw = _sc_workers()
    per_w = n // nw
    nch = per_w // SC_ROWS
    assert nch % 2 == 0 and nch >= 2
    mesh = plsc.VectorSubcoreMesh(core_axis_name="c", subcore_axis_name="s")

    @functools.partial(
        pl.kernel, mesh=mesh, out_type=jax.ShapeDtypeStruct((out_rows, w), x.dtype),
        scratch_types=[pltpu.VMEM((nch, SC_ROWS), jnp.int32), pltpu.VMEM((nch, SC_ROWS), jnp.int32),
                       pltpu.VMEM((SC_ROWS, w), x.dtype), pltpu.VMEM((SC_ROWS, w), x.dtype),
                       pltpu.SemaphoreType.DMA, pltpu.SemaphoreType.DMA, pltpu.SemaphoreType.DMA,
                       pltpu.SemaphoreType.DMA],
        name="moe_dispatch")
    def scatter(x_hbm, i0_hbm, i1_hbm, out_hbm, i0_v, i1_v, buf0, buf1, lsem0, lsem1, ssem0, ssem1):
        wid = lax.axis_index("s") * nc + lax.axis_index("c")
        pltpu.sync_copy(i0_hbm.at[wid], i0_v)
        pltpu.sync_copy(i1_hbm.at[wid], i1_v)
        base = wid * per_w

        def load(c, buf, sem):
            return pltpu.make_async_copy(x_hbm.at[pl.ds(base + c * SC_ROWS, SC_ROWS)], buf, sem)

        def put(c, buf):
            first = pltpu.make_async_copy(buf, out_hbm.at[i0_v.at[c]], ssem0)
            second = pltpu.make_async_copy(buf, out_hbm.at[i1_v.at[c]], ssem1)
            first.start()
            second.start()
            first.wait()
            second.wait()

        def pair(c, prefetch_next):
            load(c + 1, buf1, lsem1).start()
            load(c, buf0, lsem0).wait()
            put(c, buf0)
            if prefetch_next:
                load(c + 2, buf0, lsem0).start()
            load(c + 1, buf1, lsem1).wait()
            put(c + 1, buf1)

        load(0, buf0, lsem0).start()

        def body(t, carry):
            pair(2 * t, True)
            return carry

        lax.fori_loop(0, nch // 2 - 1, body, 0)
        pair(nch - 2, False)

    return scatter(x, idx0.reshape(nw, nch, SC_ROWS), idx1.reshape(nw, nch, SC_ROWS))


def _sc_gather_rows(table, idx):
    _, w = table.shape
    b = idx.shape[0]
    nc, nw = _sc_workers()
    per_w = b // nw
    nch = per_w // SC_ROWS
    mesh = plsc.VectorSubcoreMesh(core_axis_name="c", subcore_axis_name="s")

    assert nch % 2 == 0 and nch >= 2

    @functools.partial(
        pl.kernel, mesh=mesh, out_type=jax.ShapeDtypeStruct((b, w), table.dtype),
        scratch_types=[pltpu.VMEM((nch, SC_ROWS), jnp.int32), pltpu.VMEM((SC_ROWS, w), table.dtype),
                       pltpu.VMEM((SC_ROWS, w), table.dtype), pltpu.SemaphoreType.DMA, pltpu.SemaphoreType.DMA],
        name="moe_combine_gather")
    def gather(table_hbm, idx_hbm, out_hbm, idx_v, buf0, buf1, sem0, sem1):
        wid = lax.axis_index("s") * nc + lax.axis_index("c")
        pltpu.sync_copy(idx_hbm.at[wid], idx_v)
        base = wid * per_w

        def fetch(c, buf, sem):
            return pltpu.make_async_copy(table_hbm.at[idx_v.at[c]], buf, sem)

        def put(c, buf):
            pltpu.sync_copy(buf, out_hbm.at[pl.ds(base + c * SC_ROWS, SC_ROWS)])

        def pair(c, prefetch_next):
            fetch(c + 1, buf1, sem1).start()
            fetch(c, buf0, sem0).wait()
            put(c, buf0)
            if prefetch_next:
                fetch(c + 2, buf0, sem0).start()
            fetch(c + 1, buf1, sem1).wait()
            put(c + 1, buf1)

        fetch(0, buf0, sem0).start()

        def body(t, carry):
            pair(2 * t, True)
            return carry

        lax.fori_loop(0, nch // 2 - 1, body, 0)
        pair(nch - 2, False)

    return gather(table, idx.reshape(nw, nch, SC_ROWS))


def _moe_kernel(te_ref, nv_ref, rows_ref, xs_ref, wg_ref, wu_ref, wd_ref, ys_ref, x_scr, acc_ref):
    j = pl.program_id(0)
    f = pl.program_id(1)
    nf = pl.num_programs(1)
    tm = xs_ref.shape[0]

    def expert_ffn(rows):
        h = x_scr[:rows, :]
        g = jnp.dot(h, wg_ref[0], preferred_element_type=F32)
        u = jnp.dot(h, wu_ref[0], preferred_element_type=F32)
        act = (g * jax.nn.sigmoid(g) * u).astype(BF16)
        y = jnp.dot(act, wd_ref[0], preferred_element_type=F32)

        @pl.when(f == 0)
        def _():
            acc_ref[:rows, :] = y

        @pl.when(f > 0)
        def _():
            acc_ref[:rows, :] += y

        @pl.when(f == nf - 1)
        def _():
            ys_ref[:rows, :] = _pack_bf16_pairs(acc_ref[:rows, :])

    @pl.when(j < nv_ref[0])
    def _():
        @pl.when(f == 0)
        def _():
            x_scr[...] = _unpack_bf16_pairs(xs_ref[...]).astype(BF16)

        for rows in range(MOE_ROW_STEP, tm + 1, MOE_ROW_STEP):
            pl.when(rows_ref[j] == rows)(functools.partial(expert_ffn, rows))


def _moe_grouped(xs, tile_expert, n_valid, tile_rows, w_gu, w_down, tm, tf=1792):
    p_rows, dh = xs.shape
    d = 2 * dh
    ne, _, ff2 = w_gu.shape
    ff = ff2 // 2
    nf = ff // tf

    def tile(j, nv):
        return jnp.minimum(j, nv[0] - 1)

    def chunk(j, f, nv):
        return jnp.where(j < nv[0], f, nf - 1)

    grid_spec = pltpu.PrefetchScalarGridSpec(
        num_scalar_prefetch=3,
        grid=(p_rows // tm, nf),
        in_specs=[
            pl.BlockSpec((tm, dh), lambda j, f, te, nv, tr: (tile(j, nv), 0)),
            pl.BlockSpec((1, d, tf), lambda j, f, te, nv, tr: (te[tile(j, nv)], 0, chunk(j, f, nv))),
            pl.BlockSpec((1, d, tf), lambda j, f, te, nv, tr: (te[tile(j, nv)], 0, nf + chunk(j, f, nv))),
            pl.BlockSpec((1, tf, d), lambda j, f, te, nv, tr: (te[tile(j, nv)], chunk(j, f, nv), 0)),
        ],
        out_specs=pl.BlockSpec((tm, dh), lambda j, f, te, nv, tr: (tile(j, nv), 0)),
        scratch_shapes=[pltpu.VMEM((tm, d), BF16), pltpu.VMEM((tm, d), F32)],
    )
    return pl.pallas_call(
        _moe_kernel,
        grid_spec=grid_spec,
        out_shape=jax.ShapeDtypeStruct((p_rows, dh), jnp.uint32),
        compiler_params=_params(("arbitrary", "arbitrary")),
        name="moe_experts",
    )(tile_expert, n_valid, tile_rows, xs, w_gu, w_gu, w_down)


def _moe_combine_kernel(x_ref, y0_ref, y1_ref, rw_ref, mod_ref, *rest):
    o_ref = rest[-1]
    d = D_MODEL
    w1 = rw_ref[:, 0:1]
    w2 = rw_ref[:, 1:2]
    y = w1 * _unpack_bf16_pairs(y0_ref[...]) + w2 * _unpack_bf16_pairs(y1_ref[...])
    o_ref[...] = x_ref[...] + mod_ref[:, 5 * d:6 * d] * y


def _moe_combine(x, yg, rw, mod, prev, part, n_parts, tm=512):
    s, d = x.shape
    nb = s // tm // n_parts
    first = part * nb
    in_specs = [
        pl.BlockSpec((tm, d), lambda i: (first + i, 0)),
        pl.BlockSpec((tm, d // 2), lambda i: (i, 0)),
        pl.BlockSpec((tm, d // 2), lambda i: (nb + i, 0)),
        pl.BlockSpec((tm, LANES), lambda i: (first + i, 0)),
        _const_spec(mod.shape),
    ]
    args = [x, yg, yg, rw, mod]
    aliases = {}
    if prev is not None:
        in_specs.append(pl.BlockSpec(memory_space=pl.ANY))
        args.append(prev)
        aliases = {len(args) - 1: 0}
    return pl.pallas_call(
        _moe_combine_kernel,
        grid=(nb,),
        in_specs=in_specs,
        out_specs=pl.BlockSpec((tm, d), lambda i: (first + i, 0)),
        out_shape=jax.ShapeDtypeStruct((s, d), F32),
        input_output_aliases=aliases,
        compiler_params=_params(("arbitrary",)),
        name="moe_combine",
    )(*args)


def _moe_sparse(x, h_packed, rw, ridx, counts, mod, w_gu, w_down, tm=MOE_TILE):
    s = x.shape[0]
    ne = w_gu.shape[0]
    n_tiles = (2 * s) // tm + ne
    cnt = counts[0, :ne].astype(jnp.int32)
    padded = ((cnt + tm - 1) // tm) * tm
    ends = jnp.cumsum(padded)
    offs = ends - padded
    experts = jnp.arange(ne, dtype=jnp.int32)
    off_of = lambda e: jnp.sum(jnp.where(e[:, None] == experts[None, :], offs[None, :], 0), axis=1)
    pos0 = off_of(ridx[0]) + ridx[2]
    pos1 = off_of(ridx[1]) + ridx[3]
    tile_start = jnp.arange(n_tiles, dtype=jnp.int32) * tm
    tile_expert = jnp.minimum(jnp.sum(tile_start[:, None] >= ends[None, :], axis=1), ne - 1).astype(jnp.int32)
    n_valid = (ends[-1] // tm).reshape(1).astype(jnp.int32)
    filled = jnp.clip((offs + cnt)[tile_expert] - tile_start, 0, tm)
    tile_rows = (((filled + MOE_ROW_STEP - 1) // MOE_ROW_STEP) * MOE_ROW_STEP).astype(jnp.int32)
    xs = _sc_scatter_rows(h_packed, pos0, pos1, n_tiles * tm)
    ys = _moe_grouped(xs, tile_expert, n_valid, tile_rows, w_gu, w_down, tm)
    out = None
    sp = s // COMBINE_PARTS
    for part in range(COMBINE_PARTS):
        rows = slice(part * sp, (part + 1) * sp)
        yg = _sc_gather_rows(ys, jnp.concatenate([pos0[rows], pos1[rows]]))
        out = _moe_combine(x, yg, rw, mod, out, part, COMBINE_PARTS)
    return out


def kernel(x, c, positions, norm_mix_w, norm_ffn_w, ada_w, ada_b, e_w_in, a_ln_w, a_ln_b, a_w_s, a_b_s, b_w_grp,
           b_scale, e_w_out, ffn_w_gu, ffn_w_down, o_w_in, c_conv_w, q_a_norm, w_uq, kv_norm, w_ukv, q_norm_w,
           k_norm_w, o_w_out, router_w, moe_w_gu, moe_w_down):
    bsz, s, d = x.shape
    assert bsz == 1 and d == D_MODEL
    depth = ada_w.shape[0]
    nh = MLA_HEADS
    xs = x.reshape(s, d)
    mod = _ada_mod(c, ada_w, ada_b)
    cos, sin, cos_t, sin_t = _rope_tables(positions)
    row = lambda a: a.reshape(1, -1)

    for layer in range(depth):
        i = layer // 2
        m = mod[layer]
        if layer % 2 == 0:
            has_next = layer + 1 < depth
            dummy = jnp.zeros((s // 512, 16, LANES), F32)
            xs, moe_w_down_bf16 = _even_mix(
                xs, m, row(norm_mix_w[layer]), e_w_in[i].astype(BF16), row(a_ln_w[i]), row(a_ln_b[i]), a_w_s[i],
                a_b_s[i].reshape(A_GROUPS, CHUNK, 1), b_w_grp[i].astype(BF16), row(b_scale[i]),
                e_w_out[i].astype(BF16), moe_w_down[i] if has_next else dummy)
            xs, moe_w_gu_bf16 = _ffn(xs, m, row(norm_ffn_w[layer]), ffn_w_gu[i].astype(BF16),
                                     ffn_w_down[i].astype(BF16), moe_w_gu[i] if has_next else dummy)
        else:
            o2 = 3 * C_WIDTH + Q_LORA + KV_LORA
            w_in = o_w_in[i][:, :o2].astype(BF16)
            w_kpe = jnp.pad(o_w_in[i][:, o2:], ((0, 0), (0, LANES - QK_ROPE))).astype(BF16)
            wkv = w_ukv[i].reshape(KV_LORA, nh, QK_NOPE + V_HD)
            w_k = wkv[:, :, :QK_NOPE].reshape(KV_LORA, nh * QK_NOPE).astype(BF16)
            w_v_t = wkv[:, :, QK_NOPE:].reshape(KV_LORA, nh * V_HD).T.astype(BF16)
            yc, qt, k, vt = _odd_prep(
                xs, m, row(norm_mix_w[layer]), w_in, w_kpe, c_conv_w[i], row(q_a_norm[i]), w_uq[i].T.astype(BF16),
                row(kv_norm[i]), w_k, w_v_t, q_norm_w[i].reshape(QK_HD, 1), row(k_norm_w[i]), cos, sin, cos_t, sin_t)
            yd = _attention(qt, k, vt)
            wr = jnp.pad(router_w[i], ((0, 0), (0, LANES - N_EXPERTS)))
            wr_hi = lax.reduce_precision(wr, exponent_bits=8, mantissa_bits=7)
            w_router = jnp.concatenate([wr_hi, wr - wr_hi], axis=1).astype(BF16)
            xs, hp, rw, ridx, counts = _odd_out(xs, yc, yd, m, row(norm_ffn_w[layer]), o_w_out[i].astype(BF16),
                                                w_router)
            xs = _moe_sparse(xs, hp, rw, ridx, counts, m, moe_w_gu_bf16, moe_w_down_bf16)
    return xs.reshape(bsz, s, d)
```

```python
import functools

import jax
import jax.numpy as jnp
from jax import lax
from jax.experimental import pallas as pl
from jax.experimental.pallas import tpu as pltpu
from jax.experimental.pallas import tpu_sc as plsc

D_MODEL = 1024
SEQ = 16384
EPS = 1e-6
CHUNK = 128
A_WIDTH = 512
A_GROUPS = 4
B_WIDTH = 512
POOL_WINDOWS = (2, 4, 8, 16)
B_HD = 128
C_WIDTH = 512
MLA_HEADS = 4
Q_LORA = 256
KV_LORA = 256
QK_NOPE = 128
QK_ROPE = 64
QK_HD = QK_NOPE + QK_ROPE
V_HD = 128
ROPE_THETA = 10000.0
D_FF = 2816
N_EXPERTS = 8
D_FF_EXPERT = 3584

LANES = 128
SUBLANES = 8
POOL_HALO = 16
CONV_HALO = 8
QK_PAD = 256
V_PAD = 144
LOG2E = 1.4426950408889634
MASK_VALUE = -1e30
MOE_TILE = 512
CONV_COLS = 256
ATTN_STRIP = 256
MOE_ROW_STEP = 128
COMBINE_PARTS = 4
SC_ROWS = 64
VMEM_LIMIT = 56 * 1024 * 1024

F32 = jnp.float32
BF16 = jnp.bfloat16


def _params(sem, vmem=VMEM_LIMIT, flags=None):
    return pltpu.CompilerParams(dimension_semantics=sem, vmem_limit_bytes=vmem, flags=flags)


def _const_spec(shape, single=False):
    nd = len(shape)
    return pl.BlockSpec(shape, lambda *_: (0,) * nd, pipeline_mode=pl.Buffered(1) if single else None)


def _rms_mod(x, nw, sc, sh):
    ms = jnp.mean(x * x, axis=-1, keepdims=True)
    return (x * lax.rsqrt(ms + EPS)) * nw * (1.0 + sc) + sh


def _ada_kernel(c_ref, w_ref, b_ref, o_ref):
    c = c_ref[...]
    ca = c * jax.nn.sigmoid(c)
    o_ref[0] = jnp.sum(w_ref[0] * ca, axis=0, keepdims=True) + b_ref[0]


def _ada_mod(c, ada_w, ada_b):
    depth, d, n = ada_w.shape
    tn = 1536
    return pl.pallas_call(
        _ada_kernel,
        grid=(depth, n // tn),
        in_specs=[
            pl.BlockSpec((d, 1), lambda l, j: (0, 0)),
            pl.BlockSpec((1, d, tn), lambda l, j: (l, 0, j)),
            pl.BlockSpec((1, 1, tn), lambda l, j: (l, 0, j)),
        ],
        out_specs=pl.BlockSpec((1, 1, tn), lambda l, j: (l, 0, j)),
        out_shape=jax.ShapeDtypeStruct((depth, 1, n), F32),
        compiler_params=_params(("arbitrary", "arbitrary")),
        name="ada_mod",
    )(c.reshape(d, 1), ada_w, ada_b.reshape(depth, 1, n))


def _even_mix_kernel(x_ref, mod_ref, nw_ref, win_ref, lnw_ref, lnb_ref, ws_ref, bs_ref, wg_ref, bsc_ref,
                     wout_ref, side_ref, o_ref, side_out_ref, halo_ref, sv_ref, yb_ref):
    side_out_ref[...] = side_ref[...].astype(BF16)
    tm = x_ref.shape[0]
    i = pl.program_id(0)

    @pl.when(i == 0)
    def _():
        halo_ref[...] = jnp.zeros_like(halo_ref)

    d = D_MODEL
    x = x_ref[...]
    sh, sc, gate = mod_ref[:, 0:d], mod_ref[:, d:2 * d], mod_ref[:, 2 * d:3 * d]
    h = _rms_mod(x, nw_ref[...], sc, sh).astype(BF16)
    p = jnp.dot(h, win_ref[...], preferred_element_type=F32)

    gl = jax.nn.gelu(p[:, :2 * A_WIDTH])
    u = gl[:, :A_WIDTH]
    v = gl[:, A_WIDTH:]
    mu = jnp.mean(v, axis=-1, keepdims=True)
    vc = v - mu
    var = jnp.mean(vc * vc, axis=-1, keepdims=True)
    vn = (vc * lax.rsqrt(var + EPS) * lnw_ref[...] + lnb_ref[...]).astype(BF16)
    row = lax.broadcasted_iota(jnp.int32, (CHUNK, CHUNK), 0)
    col = lax.broadcasted_iota(jnp.int32, (CHUNK, CHUNK), 1)
    for g in range(A_GROUPS):
        w = jnp.where(col <= row, ws_ref[g], 0.0).astype(BF16)
        b = bs_ref[g]
        for c in range(tm // CHUNK):
            blk = vn[c * CHUNK:(c + 1) * CHUNK, g * LANES:(g + 1) * LANES]
            sv_ref[c * CHUNK:(c + 1) * CHUNK, g * LANES:(g + 1) * LANES] = (
                jnp.dot(w, blk, preferred_element_type=F32) + b)
    ya = (u * sv_ref[...]).astype(BF16)

    pb = p[:, 2 * A_WIDTH:]
    ext = jnp.concatenate([halo_ref[...], pb], axis=0)
    halo_ref[...] = pb[tm - POOL_HALO:, :]
    t_glob = i * tm + lax.broadcasted_iota(jnp.int32, (tm, 1), 0)
    s = ext
    width = 1
    for g, win in enumerate(POOL_WINDOWS):
        while width < win:
            s = s + pltpu.roll(s, width, 0)
            width *= 2
        cnt = jnp.minimum(t_glob + 1, win).astype(F32)
        sl = slice(g * B_HD, (g + 1) * B_HD)
        pooled = s[POOL_HALO:, sl] / cnt
        dg = (pooled - pb[:, sl]).astype(BF16)
        yb_ref[:, sl] = jnp.dot(dg, wg_ref[g], preferred_element_type=F32)
    yb = (yb_ref[...] * bsc_ref[...]).astype(BF16)

    mix = (jnp.dot(ya, wout_ref[:A_WIDTH, :], preferred_element_type=F32)
           + jnp.dot(yb, wout_ref[A_WIDTH:, :], preferred_element_type=F32))
    o_ref[...] = x + gate * mix


def _side_cast_spec(side, steps, axis):
    ne = side.shape[0]
    parts = steps // ne
    assert parts * ne == steps and side.shape[axis] % parts == 0
    block = list(side.shape)
    block[0] = 1
    block[axis] //= parts
    assert block[1] % 16 == 0 and block[2] % LANES == 0
    if axis == 1:
        return pl.BlockSpec(tuple(block), lambda i: (i // parts, i % parts, 0))
    return pl.BlockSpec(tuple(block), lambda i: (i // parts, 0, i % parts))


def _even_mix(x, mod, nw, w_in, ln_w, ln_b, w_s, b_s, w_grp, b_scale, w_out, side, tm=512):
    s, d = x.shape
    steps = s // tm
    side_spec = _side_cast_spec(side, steps, 1)
    return pl.pallas_call(
        _even_mix_kernel,
        grid=(steps,),
        in_specs=[
            pl.BlockSpec((tm, d), lambda i: (i, 0)),
            _const_spec(mod.shape), _const_spec(nw.shape), _const_spec(w_in.shape),
            _const_spec(ln_w.shape), _const_spec(ln_b.shape), _const_spec(w_s.shape), _const_spec(b_s.shape),
            _const_spec(w_grp.shape), _const_spec(b_scale.shape), _const_spec(w_out.shape), side_spec,
        ],
        out_specs=[pl.BlockSpec((tm, d), lambda i: (i, 0)), side_spec],
        out_shape=[jax.ShapeDtypeStruct((s, d), F32), jax.ShapeDtypeStruct(side.shape, BF16)],
        scratch_shapes=[pltpu.VMEM((POOL_HALO, B_WIDTH), F32), pltpu.VMEM((tm, A_WIDTH), F32),
                        pltpu.VMEM((tm, B_WIDTH), F32)],
        compiler_params=_params(("arbitrary",)),
        name="even_mix",
    )(x, mod, nw, w_in, ln_w, ln_b, w_s, b_s, w_grp, b_scale, w_out, side)


def _ffn_kernel(x_ref, mod_ref, nw_ref, wgu_ref, wd_ref, side_ref, o_ref, side_out_ref, *, n_chunks):
    side_out_ref[...] = side_ref[...].astype(BF16)
    d = D_MODEL
    x = x_ref[...]
    sh, sc, gate = mod_ref[:, 3 * d:4 * d], mod_ref[:, 4 * d:5 * d], mod_ref[:, 5 * d:6 * d]
    h = _rms_mod(x, nw_ref[...], sc, sh).astype(BF16)
    ff = wd_ref.shape[0]
    tf = ff // n_chunks
    acc = jnp.zeros(x.shape, F32)
    for f in range(n_chunks):
        g = jnp.dot(h, wgu_ref[:, f * tf:(f + 1) * tf], preferred_element_type=F32)
        u = jnp.dot(h, wgu_ref[:, ff + f * tf:ff + (f + 1) * tf], preferred_element_type=F32)
        act = (g * jax.nn.sigmoid(g) * u).astype(BF16)
        acc = acc + jnp.dot(act, wd_ref[f * tf:(f + 1) * tf, :], preferred_element_type=F32)
    o_ref[...] = x + gate * acc


def _ffn(x, mod, nw, w_gu, w_down, side, tm=512, n_chunks=11):
    s, d = x.shape
    steps = s // tm
    side_spec = _side_cast_spec(side, steps, 2)
    return pl.pallas_call(
        functools.partial(_ffn_kernel, n_chunks=n_chunks),
        grid=(steps,),
        in_specs=[
            pl.BlockSpec((tm, d), lambda i: (i, 0)),
            _const_spec(mod.shape), _const_spec(nw.shape), _const_spec(w_gu.shape, single=True),
            _const_spec(w_down.shape, single=True), side_spec,
        ],
        out_specs=[pl.BlockSpec((tm, d), lambda i: (i, 0)), side_spec],
        out_shape=[jax.ShapeDtypeStruct((s, d), F32), jax.ShapeDtypeStruct(side.shape, BF16)],
        compiler_params=_params(("arbitrary",)),
        name="ffn",
    )(x, mod, nw, w_gu, w_down, side)


def _rope_table_kernel(pos_ref, invf_ref, cos_ref, sin_ref):
    ang = pos_ref[...].astype(F32) * invf_ref[...]
    cos_ref[...] = jnp.cos(ang)
    sin_ref[...] = jnp.sin(ang)


def _rope_tables(positions):
    s = positions.shape[-1]
    half = QK_ROPE // 2
    inv_freq = ROPE_THETA ** (-jnp.arange(0, QK_ROPE, 2, dtype=F32) / QK_ROPE)
    cos_t, sin_t = pl.pallas_call(
        _rope_table_kernel,
        out_shape=(jax.ShapeDtypeStruct((half, s), F32), jax.ShapeDtypeStruct((half, s), F32)),
        name="rope_tables",
    )(positions.reshape(1, s), inv_freq.reshape(half, 1))
    return jnp.tile(cos_t.T, (1, 2)), jnp.tile(sin_t.T, (1, 2)), cos_t, sin_t


def _rope_rot(x):
    n = x.shape[-1]
    lane = lax.broadcasted_iota(jnp.int32, x.shape, 1)
    half = QK_ROPE // 2
    fwd = pltpu.roll(x, half, 1)
    bwd = pltpu.roll(x, n - half, 1)
    return jnp.where((lane % QK_ROPE) < half, -bwd, fwd)


def _odd_prep_kernel(x_ref, mod_ref, nw_ref, win_ref, wkpe_ref, cw_ref, qan_ref, wuqt_ref, kvn_ref, wk_ref, wvt_ref,
                     qnw_ref, knw_ref, cos_ref, sin_ref, cost_ref, sint_ref, yc_ref, qt_ref, k_ref, vt_ref,
                     halo_ref):
    tm = x_ref.shape[0]
    i = pl.program_id(0)

    @pl.when(i == 0)
    def _():
        halo_ref[...] = jnp.zeros_like(halo_ref)

    d = D_MODEL
    x = x_ref[...]
    sh, sc = mod_ref[:, 0:d], mod_ref[:, d:2 * d]
    h = _rms_mod(x, nw_ref[...], sc, sh).astype(BF16)
    cw = C_WIDTH
    o0 = 3 * cw
    proj = lambda lo, hi: jnp.dot(h, win_ref[:, lo:hi], preferred_element_type=F32)
    kpe = jnp.dot(h, wkpe_ref[...], preferred_element_type=F32)[:, :QK_ROPE]

    for c0 in range(0, cw, CONV_COLS):
        cols = slice(c0, c0 + CONV_COLS)
        z = proj(cw + c0, cw + c0 + CONV_COLS) * proj(2 * cw + c0, 2 * cw + c0 + CONV_COLS)
        ext = jnp.concatenate([halo_ref[:, cols], z], axis=0)
        halo_ref[:, cols] = z[tm - CONV_HALO:, :]
        z1 = pltpu.roll(ext, 1, 0)[CONV_HALO:, :]
        z2 = pltpu.roll(ext, 2, 0)[CONV_HALO:, :]
        conv = cw_ref[0:1, cols] * z2 + cw_ref[1:2, cols] * z1 + cw_ref[2:3, cols] * z
        yc_ref[:, cols] = (proj(c0, c0 + CONV_COLS) * conv).astype(BF16)

    cq = proj(o0, o0 + Q_LORA)
    ckv = proj(o0 + Q_LORA, o0 + Q_LORA + KV_LORA)
    cqn = cq * lax.rsqrt(jnp.mean(cq * cq, axis=-1, keepdims=True) + EPS) * qan_ref[...]
    ckvn = ckv * lax.rsqrt(jnp.mean(ckv * ckv, axis=-1, keepdims=True) + EPS) * kvn_ref[...]
    cqn_t = cqn.T.astype(BF16)
    ckvn_t = ckvn.T.astype(BF16)
    q_t = jnp.dot(wuqt_ref[...], cqn_t, preferred_element_type=F32)
    v_t = jnp.dot(wvt_ref[...], ckvn_t, preferred_element_type=F32)
    kn_all = jnp.dot(ckvn.astype(BF16), wk_ref[...], preferred_element_type=F32)

    nh = MLA_HEADS
    half = QK_ROPE // 2
    sm_scale = QK_HD ** -0.5
    cos_t, sin_t = cost_ref[...], sint_ref[...]
    kn_w, kr_w = knw_ref[:, :QK_NOPE], knw_ref[:, QK_NOPE:]
    kr = kpe * kr_w
    kr = kr * cos_ref[...] + _rope_rot(kr) * sin_ref[...]
    kpe_ss = jnp.sum(kpe * kpe, axis=-1, keepdims=True)
    zrows = jnp.zeros((QK_PAD - QK_HD, tm), F32)
    zpad = jnp.zeros((tm, QK_PAD - QK_HD), F32)
    ones_rows = (lax.broadcasted_iota(jnp.int32, (V_PAD - V_HD, tm), 0) == 0).astype(BF16)
    for hd in range(nh):
        qh = q_t[hd * QK_HD:(hd + 1) * QK_HD, :]
        q_inv = lax.rsqrt(jnp.sum(qh * qh, axis=0, keepdims=True) / QK_HD + EPS) * (sm_scale * LOG2E)
        qw = qh * qnw_ref[...]
        x1, x2 = qw[QK_NOPE:QK_NOPE + half, :], qw[QK_NOPE + half:, :]
        qt_ref[hd] = jnp.concatenate(
            [qw[:QK_NOPE, :] * q_inv, (x1 * cos_t - x2 * sin_t) * q_inv, (x2 * cos_t + x1 * sin_t) * q_inv, zrows],
            axis=0).astype(BF16)
        kn = kn_all[:, hd * QK_NOPE:(hd + 1) * QK_NOPE]
        k_ss = jnp.sum(kn * kn, axis=-1, keepdims=True) + kpe_ss
        k_inv = lax.rsqrt(k_ss / QK_HD + EPS)
        k_ref[hd] = jnp.concatenate([kn * k_inv * kn_w, kr * k_inv, zpad], axis=1).astype(BF16)
        vt_ref[hd, :V_HD, :] = v_t[hd * V_HD:(hd + 1) * V_HD, :].astype(BF16)
        vt_ref[hd, V_HD:, :] = ones_rows


def _odd_prep(x, mod, nw, w_in, w_kpe, conv_w, q_a_norm, w_uq_t, kv_norm, w_k, w_v_t, q_norm_col, k_norm_w, cos, sin,
              cos_t, sin_t, tm=1024):
    s, d = x.shape
    nh = MLA_HEADS
    half = QK_ROPE // 2
    consts = [mod, nw, w_in, w_kpe, conv_w, q_a_norm, w_uq_t, kv_norm, w_k, w_v_t, q_norm_col, k_norm_w]
    return pl.pallas_call(
        _odd_prep_kernel,
        grid=(s // tm,),
        in_specs=[pl.BlockSpec((tm, d), lambda i: (i, 0))] + [_const_spec(a.shape) for a in consts] + [
            pl.BlockSpec((tm, QK_ROPE), lambda i: (i, 0)), pl.BlockSpec((tm, QK_ROPE), lambda i: (i, 0)),
            pl.BlockSpec((half, tm), lambda i: (0, i)), pl.BlockSpec((half, tm), lambda i: (0, i))],
        out_specs=[
            pl.BlockSpec((tm, C_WIDTH), lambda i: (i, 0)),
            pl.BlockSpec((nh, QK_PAD, tm), lambda i: (0, 0, i)),
            pl.BlockSpec((nh, tm, QK_PAD), lambda i: (0, i, 0)),
            pl.BlockSpec((nh, V_PAD, tm), lambda i: (0, 0, i)),
        ],
        out_shape=[
            jax.ShapeDtypeStruct((s, C_WIDTH), BF16),
            jax.ShapeDtypeStruct((nh, QK_PAD, s), BF16),
            jax.ShapeDtypeStruct((nh, s, QK_PAD), BF16),
            jax.ShapeDtypeStruct((nh, V_PAD, s), BF16),
        ],
        scratch_shapes=[pltpu.VMEM((CONV_HALO, C_WIDTH), F32)],
        compiler_params=_params(("arbitrary",)),
        name="odd_prep",
    )(x, *consts, cos, sin, cos_t, sin_t)


def _attn_kernel(qt_ref, k_ref, vt_ref, o_ref, s0_ref, s1_ref, bm0_ref, bm1_ref, m_ref, acc_ref, *, tq, tk):
    qi = pl.program_id(1)
    m_ref[...] = jnp.full_like(m_ref, MASK_VALUE)
    acc_ref[...] = jnp.zeros_like(acc_ref)

    def scores(j, s_ref, bm_ref, diag_offset=None):
        q0 = 0 if diag_offset is None else diag_offset
        start = pl.multiple_of(j * tk, tk)
        k = k_ref[0, pl.ds(start, tk), :]
        for c0 in range(q0, tq, ATTN_STRIP):
            cols = slice(c0, c0 + ATTN_STRIP)
            s = jnp.dot(k, qt_ref[0, :, cols], preferred_element_type=F32)
            if diag_offset is not None and c0 < q0 + tk:
                key = lax.broadcasted_iota(jnp.int32, s.shape, 0)
                qry = (c0 - q0) + lax.broadcasted_iota(jnp.int32, s.shape, 1)
                s = jnp.where(key <= qry, s, MASK_VALUE)
            s_ref[:, cols] = s
            bm_ref[:, cols] = jnp.max(s, axis=0, keepdims=True)

    def consume(j, s_ref, bm_ref, q0=0):
        start = pl.multiple_of(j * tk, tk)
        vt = vt_ref[0, :, pl.ds(start, tk)]
        for c0 in range(q0, tq, ATTN_STRIP):
            cols = slice(c0, c0 + ATTN_STRIP)
            m_prev = m_ref[:, cols]
            m_new = jnp.maximum(m_prev, bm_ref[:, cols])
            alpha = jnp.exp2(m_prev - m_new)
            p = jnp.exp2((s_ref[:, cols] - m_new).astype(BF16))
            acc_ref[:, cols] = alpha * acc_ref[:, cols] + jnp.dot(vt, p, preferred_element_type=F32)
            m_ref[:, cols] = m_new

    r = tq // tk
    slots = ((s0_ref, bm0_ref), (s1_ref, bm1_ref))

    def trip(j0, then_diagonal):
        for b in range(r):
            first_diag = then_diagonal and b == r - 1
            scores(j0 + b + 1, *slots[(b + 1) % 2], diag_offset=0 if first_diag else None)
            consume(j0 + b, *slots[b % 2])

    def diagonal(jd):
        for b in range(r):
            if b + 1 < r:
                scores(jd + b + 1, *slots[(b + 1) % 2], diag_offset=(b + 1) * tk)
            consume(jd + b, *slots[b % 2], q0=b * tk)

    @pl.when(qi > 0)
    def _():
        scores(0, *slots[0])

    n_plain = jnp.maximum(qi - 1, 0)

    def body(t, carry):
        trip(r * (2 * t), False)
        trip(r * (2 * t + 1), False)
        return carry

    lax.fori_loop(0, n_plain // 2, body, 0)

    @pl.when(n_plain % 2 == 1)
    def _():
        trip(r * (n_plain - 1), False)

    @pl.when(qi > 0)
    def _():
        trip(r * (qi - 1), True)
        diagonal(r * qi)

    @pl.when(qi == 0)
    def _():
        scores(0, *slots[0], diag_offset=0)
        diagonal(0)

    acc = acc_ref[...]
    out_t = acc[:V_HD, :] / acc[V_HD:V_HD + 1, :]
    o_ref[...] = out_t.T.astype(o_ref.dtype)


def _attention(qt, k, vt, tq=1024, tk=512):
    nh, _, s = qt.shape
    assert tq % (2 * tk) == 0
    return pl.pallas_call(
        functools.partial(_attn_kernel, tq=tq, tk=tk),
        grid=(nh, s // tq),
        in_specs=[
            pl.BlockSpec((1, QK_PAD, tq), lambda h, i: (h, 0, i)),
            pl.BlockSpec((1, s, QK_PAD), lambda h, i: (h, 0, 0)),
            pl.BlockSpec((1, V_PAD, s), lambda h, i: (h, 0, 0)),
        ],
        out_specs=pl.BlockSpec((tq, V_HD), lambda h, i: (i, h)),
        out_shape=jax.ShapeDtypeStruct((s, nh * V_HD), BF16),
        scratch_shapes=[pltpu.VMEM((tk, tq), F32), pltpu.VMEM((tk, tq), F32), pltpu.VMEM((1, tq), F32),
                        pltpu.VMEM((1, tq), F32), pltpu.VMEM((1, tq), F32), pltpu.VMEM((V_PAD, tq), F32)],
        compiler_params=_params(("arbitrary", "arbitrary")),
        name="mla_attention",
    )(qt, k, vt)


def _pack_bf16_pairs(x):
    n = x.shape[1] // 2
    lo = pltpu.bitcast(x[:, :n].astype(BF16).astype(F32), jnp.uint32)
    hi = pltpu.bitcast(x[:, n:].astype(BF16).astype(F32), jnp.uint32)
    return (lo >> 16) | hi


def _unpack_bf16_pairs(p):
    lo = pltpu.bitcast(p << 16, F32)
    hi = pltpu.bitcast(p & jnp.uint32(0xFFFF0000), F32)
    return jnp.concatenate([lo, hi], axis=1)


def _odd_out_kernel(x_ref, yc_ref, yd_ref, mod_ref, nw_ref, wout_ref, wr_ref, before_ref, x_out_ref, h_ref, rw_ref,
                    ridx_ref, cnt_ref):
    d = D_MODEL
    x = x_ref[...]
    gate_m = mod_ref[:, 2 * d:3 * d]
    sh, sc = mod_ref[:, 3 * d:4 * d], mod_ref[:, 4 * d:5 * d]
    mix = (jnp.dot(yc_ref[...], wout_ref[:C_WIDTH, :], preferred_element_type=F32)
           + jnp.dot(yd_ref[...], wout_ref[C_WIDTH:, :], preferred_element_type=F32))
    x1 = x + gate_m * mix
    x_out_ref[...] = x1
    h = _rms_mod(x1, nw_ref[...], sc, sh)
    h_ref[...] = _pack_bf16_pairs(h)

    h_hi = h.astype(BF16)
    h_lo = (h - h_hi.astype(F32)).astype(BF16)
    hw = jnp.dot(h_hi, wr_ref[...], preferred_element_type=F32)
    logits = hw[:, :LANES] + (hw[:, LANES:] + jnp.dot(h_lo, wr_ref[:, :LANES], preferred_element_type=F32))
    lane = lax.broadcasted_iota(jnp.int32, logits.shape, 1)
    logits = jnp.where(lane < N_EXPERTS, logits, -jnp.inf)
    m1 = jnp.max(logits, axis=-1, keepdims=True)
    i1 = jnp.min(jnp.where(logits == m1, lane, LANES), axis=-1, keepdims=True)
    rest = jnp.where(lane == i1, -jnp.inf, logits)
    m2 = jnp.max(rest, axis=-1, keepdims=True)
    i2 = jnp.min(jnp.where(rest == m2, lane, LANES), axis=-1, keepdims=True)
    e2 = jnp.exp(m2 - m1)
    w1 = 1.0 / (1.0 + e2)
    w2 = e2 / (1.0 + e2)
    rw_ref[...] = jnp.where(lane == 0, w1, jnp.where(lane == 1, w2, 0.0))

    i = pl.program_id(0)

    @pl.when(i == 0)
    def _():
        cnt_ref[...] = jnp.zeros_like(cnt_ref)

    onehot = jnp.where((lane == i1) | (lane == i2), 1.0, 0.0)
    prefix = jnp.dot(before_ref[...], onehot.astype(BF16), preferred_element_type=F32) + cnt_ref[...]
    r1 = jnp.sum(jnp.where(lane == i1, prefix, 0.0), axis=-1, keepdims=True)
    r2 = jnp.sum(jnp.where(lane == i2, prefix, 0.0), axis=-1, keepdims=True)
    cnt_ref[...] += jnp.sum(onehot, axis=0, keepdims=True)
    cols = jnp.where(lane == 0, i1.astype(F32), jnp.where(lane == 1, i2.astype(F32),
                                                          jnp.where(lane == 2, r1, jnp.where(lane == 3, r2, 0.0))))
    ridx_ref[...] = cols.T[:ridx_ref.shape[0], :].astype(jnp.int32)


def _odd_out(x, yc, yd, mod, nw, w_out, w_router, tm=1024):
    s, d = x.shape
    return pl.pallas_call(
        _odd_out_kernel,
        grid=(s // tm,),
        in_specs=[
            pl.BlockSpec((tm, d), lambda i: (i, 0)),
            pl.BlockSpec((tm, C_WIDTH), lambda i: (i, 0)),
            pl.BlockSpec((tm, MLA_HEADS * V_HD), lambda i: (i, 0)),
            _const_spec(mod.shape), _const_spec(nw.shape), _const_spec(w_out.shape), _const_spec(w_router.shape),
            _const_spec((tm, tm)),
        ],
        out_specs=[
            pl.BlockSpec((tm, d), lambda i: (i, 0)),
            pl.BlockSpec((tm, d // 2), lambda i: (i, 0)),
            pl.BlockSpec((tm, LANES), lambda i: (i, 0)),
            pl.BlockSpec((SUBLANES, tm), lambda i: (0, i)),
            pl.BlockSpec((1, LANES), lambda i: (0, 0)),
        ],
        out_shape=[
            jax.ShapeDtypeStruct((s, d), F32),
            jax.ShapeDtypeStruct((s, d // 2), jnp.uint32),
            jax.ShapeDtypeStruct((s, LANES), F32),
            jax.ShapeDtypeStruct((SUBLANES, s), jnp.int32),
            jax.ShapeDtypeStruct((1, LANES), F32),
        ],
        compiler_params=_params(("arbitrary",)),
        name="odd_out_router",
    )(x, yc, yd, mod, nw, w_out, w_router, jnp.tril(jnp.ones((tm, tm), BF16), -1))


def _sc_workers():
    info = plsc.get_sparse_core_info()
    return info.num_cores, info.num_cores * info.num_subcores


def _sc_scatter_rows(x, idx0, idx1, out_rows):
    n, w = x.shape
    nc, nw = _sc_workers()
    per_w = n // nw
    nch = per_w // SC_ROWS
    assert nch % 2 == 0 and nch >= 2
    mesh = plsc.VectorSubcoreMesh(core_axis_name="c", subcore_axis_name="s")

    @functools.partial(
        pl.kernel, mesh=mesh, out_type=jax.ShapeDtypeStruct((out_rows, w), x.dtype),
        scratch_types=[pltpu.VMEM((nch, SC_ROWS), jnp.int32), pltpu.VMEM((nch, SC_ROWS), jnp.int32),
                       pltpu.VMEM((SC_ROWS, w), x.dtype), pltpu.VMEM((SC_ROWS, w), x.dtype),
                       pltpu.SemaphoreType.DMA, pltpu.SemaphoreType.DMA, pltpu.SemaphoreType.DMA,
                       pltpu.SemaphoreType.DMA],
        name="moe_dispatch")
    def scatter(x_hbm, i0_hbm, i1_hbm, out_hbm, i0_v, i1_v, buf0, buf1, lsem0, lsem1, ssem0, ssem1):
        wid = lax.axis_index("s") * nc + lax.axis_index("c")
        pltpu.sync_copy(i0_hbm.at[wid], i0_v)
        pltpu.sync_copy(i1_hbm.at[wid], i1_v)
        base = wid * per_w

        def load(c, buf, sem):
            return pltpu.make_async_copy(x_hbm.at[pl.ds(base + c * SC_ROWS, SC_ROWS)], buf, sem)

        def put(c, buf):
            first = pltpu.make_async_copy(buf, out_hbm.at[i0_v.at[c]], ssem0)
            second = pltpu.make_async_copy(buf, out_hbm.at[i1_v.at[c]], ssem1)
            first.start()
            second.start()
            first.wait()
            second.wait()

        def pair(c, prefetch_next):
            load(c + 1, buf1, lsem1).start()
            load(c, buf0, lsem0).wait()
            put(c, buf0)
            if prefetch_next:
                load(c + 2, buf0, lsem0).start()
            load(c + 1, buf1, lsem1).wait()
            put(c + 1, buf1)

        load(0, buf0, lsem0).start()

        def body(t, carry):
            pair(2 * t, True)
            return carry

        lax.fori_loop(0, nch // 2 - 1, body, 0)
        pair(nch - 2, False)

    return scatter(x, idx0.reshape(nw, nch, SC_ROWS), idx1.reshape(nw, nch, SC_ROWS))


def _sc_gather_rows(table, idx):
    _, w = table.shape
    b = idx.shape[0]
    nc, nw = _sc_workers()
    per_w = b // nw
    nch = per_w // SC_ROWS
    mesh = plsc.VectorSubcoreMesh(core_axis_name="c", subcore_axis_name="s")

    assert nch % 2 == 0 and nch >= 2

    @functools.partial(
        pl.kernel, mesh=mesh, out_type=jax.ShapeDtypeStruct((b, w), table.dtype),
        scratch_types=[pltpu.VMEM((nch, SC_ROWS), jnp.int32), pltpu.VMEM((SC_ROWS, w), table.dtype),
                       pltpu.VMEM((SC_ROWS, w), table.dtype), pltpu.SemaphoreType.DMA, pltpu.SemaphoreType.DMA],
        name="moe_combine_gather")
    def gather(table_hbm, idx_hbm, out_hbm, idx_v, buf0, buf1, sem0, sem1):
        wid = lax.axis_index("s") * nc + lax.axis_index("c")
        pltpu.sync_copy(idx_hbm.at[wid], idx_v)
        base = wid * per_w

        def fetch(c, buf, sem):
            return pltpu.make_async_copy(table_hbm.at[idx_v.at[c]], buf, sem)

        def put(c, buf):
            pltpu.sync_copy(buf, out_hbm.at[pl.ds(base + c * SC_ROWS, SC_ROWS)])

        def pair(c, prefetch_next):
            fetch(c + 1, buf1, sem1).start()
            fetch(c, buf0, sem0).wait()
            put(c, buf0)
            if prefetch_next:
                fetch(c + 2, buf0, sem0).start()
            fetch(c + 1, buf1, sem1).wait()
            put(c + 1, buf1)

        fetch(0, buf0, sem0).start()

        def body(t, carry):
            pair(2 * t, True)
            return carry

        lax.fori_loop(0, nch // 2 - 1, body, 0)
        pair(nch - 2, False)

    return gather(table, idx.reshape(nw, nch, SC_ROWS))


def _moe_kernel(te_ref, nv_ref, rows_ref, xs_ref, wg_ref, wu_ref, wd_ref, ys_ref, x_scr, acc_ref):
    j = pl.program_id(0)
    f = pl.program_id(1)
    nf = pl.num_programs(1)
    tm = xs_ref.shape[0]

    def expert_ffn(rows):
        h = x_scr[:rows, :]
        g = jnp.dot(h, wg_ref[0], preferred_element_type=F32)
        u = jnp.dot(h, wu_ref[0], preferred_element_type=F32)
        act = (g * jax.nn.sigmoid(g) * u).astype(BF16)
        y = jnp.dot(act, wd_ref[0], preferred_element_type=F32)

        @pl.when(f == 0)
        def _():
            acc_ref[:rows, :] = y

        @pl.when(f > 0)
        def _():
            acc_ref[:rows, :] += y

        @pl.when(f == nf - 1)
        def _():
            ys_ref[:rows, :] = _pack_bf16_pairs(acc_ref[:rows, :])

    @pl.when(j < nv_ref[0])
    def _():
        @pl.when(f == 0)
        def _():
            x_scr[...] = _unpack_bf16_pairs(xs_ref[...]).astype(BF16)

        for rows in range(MOE_ROW_STEP, tm + 1, MOE_ROW_STEP):
            pl.when(rows_ref[j] == rows)(functools.partial(expert_ffn, rows))


def _moe_grouped(xs, tile_expert, n_valid, tile_rows, w_gu, w_down, tm, tf=1792):
    p_rows, dh = xs.shape
    d = 2 * dh
    ne, _, ff2 = w_gu.shape
    ff = ff2 // 2
    nf = ff // tf

    def tile(j, nv):
        return jnp.minimum(j, nv[0] - 1)

    def chunk(j, f, nv):
        return jnp.where(j < nv[0], f, nf - 1)

    grid_spec = pltpu.PrefetchScalarGridSpec(
        num_scalar_prefetch=3,
        grid=(p_rows // tm, nf),
        in_specs=[
            pl.BlockSpec((tm, dh), lambda j, f, te, nv, tr: (tile(j, nv), 0)),
            pl.BlockSpec((1, d, tf), lambda j, f, te, nv, tr: (te[tile(j, nv)], 0, chunk(j, f, nv))),
            pl.BlockSpec((1, d, tf), lambda j, f, te, nv, tr: (te[tile(j, nv)], 0, nf + chunk(j, f, nv))),
            pl.BlockSpec((1, tf, d), lambda j, f, te, nv, tr: (te[tile(j, nv)], chunk(j, f, nv), 0)),
        ],
        out_specs=pl.BlockSpec((tm, dh), lambda j, f, te, nv, tr: (tile(j, nv), 0)),
        scratch_shapes=[pltpu.VMEM((tm, d), BF16), pltpu.VMEM((tm, d), F32)],
    )
    return pl.pallas_call(
        _moe_kernel,
        grid_spec=grid_spec,
        out_shape=jax.ShapeDtypeStruct((p_rows, dh), jnp.uint32),
        compiler_params=_params(("arbitrary", "arbitrary")),
        name="moe_experts",
    )(tile_expert, n_valid, tile_rows, xs, w_gu, w_gu, w_down)


def _moe_combine_kernel(x_ref, y0_ref, y1_ref, rw_ref, mod_ref, *rest):
    o_ref = rest[-1]
    d = D_MODEL
    w1 = rw_ref[:, 0:1]
    w2 = rw_ref[:, 1:2]
    y = w1 * _unpack_bf16_pairs(y0_ref[...]) + w2 * _unpack_bf16_pairs(y1_ref[...])
    o_ref[...] = x_ref[...] + mod_ref[:, 5 * d:6 * d] * y


def _moe_combine(x, yg, rw, mod, prev, part, n_parts, tm=512):
    s, d = x.shape
    nb = s // tm // n_parts
    first = part * nb
    in_specs = [
        pl.BlockSpec((tm, d), lambda i: (first + i, 0)),
        pl.BlockSpec((tm, d // 2), lambda i: (i, 0)),
        pl.BlockSpec((tm, d // 2), lambda i: (nb + i, 0)),
        pl.BlockSpec((tm, LANES), lambda i: (first + i, 0)),
        _const_spec(mod.shape),
    ]
    args = [x, yg, yg, rw, mod]
    aliases = {}
    if prev is not None:
        in_specs.append(pl.BlockSpec(memory_space=pl.ANY))
        args.append(prev)
        aliases = {len(args) - 1: 0}
    return pl.pallas_call(
        _moe_combine_kernel,
        grid=(nb,),
        in_specs=in_specs,
        out_specs=pl.BlockSpec((tm, d), lambda i: (first + i, 0)),
        out_shape=jax.ShapeDtypeStruct((s, d), F32),
        input_output_aliases=aliases,
        compiler_params=_params(("arbitrary",)),
        name="moe_combine",
    )(*args)


def _moe_sparse(x, h_packed, rw, ridx, counts, mod, w_gu, w_down, tm=MOE_TILE):
    s = x.shape[0]
    ne = w_gu.shape[0]
    n_tiles = (2 * s) // tm + ne
    cnt = counts[0, :ne].astype(jnp.int32)
    padded = ((cnt + tm - 1) // tm) * tm
    ends = jnp.cumsum(padded)
    offs = ends - padded
    experts = jnp.arange(ne, dtype=jnp.int32)
    off_of = lambda e: jnp.sum(jnp.where(e[:, None] == experts[None, :], offs[None, :], 0), axis=1)
    pos0 = off_of(ridx[0]) + ridx[2]
    pos1 = off_of(ridx[1]) + ridx[3]
    tile_start = jnp.arange(n_tiles, dtype=jnp.int32) * tm
    tile_expert = jnp.minimum(jnp.sum(tile_start[:, None] >= ends[None, :], axis=1), ne - 1).astype(jnp.int32)
    n_valid = (ends[-1] // tm).reshape(1).astype(jnp.int32)
    filled = jnp.clip((offs + cnt)[tile_expert] - tile_start, 0, tm)
    tile_rows = (((filled + MOE_ROW_STEP - 1) // MOE_ROW_STEP) * MOE_ROW_STEP).astype(jnp.int32)
    xs = _sc_scatter_rows(h_packed, pos0, pos1, n_tiles * tm)
    ys = _moe_grouped(xs, tile_expert, n_valid, tile_rows, w_gu, w_down, tm)
    out = None
    sp = s // COMBINE_PARTS
    for part in range(COMBINE_PARTS):
        rows = slice(part * sp, (part + 1) * sp)
        yg = _sc_gather_rows(ys, jnp.concatenate([pos0[rows], pos1[rows]]))
        out = _moe_combine(x, yg, rw, mod, out, part, COMBINE_PARTS)
    return out


def kernel(x, c, positions, norm_mix_w, norm_ffn_w, ada_w, ada_b, e_w_in, a_ln_w, a_ln_b, a_w_s, a_b_s, b_w_grp,
           b_scale, e_w_out, ffn_w_gu, ffn_w_down, o_w_in, c_conv_w, q_a_norm, w_uq, kv_norm, w_ukv, q_norm_w,
           k_norm_w, o_w_out, router_w, moe_w_gu, moe_w_down):
    bsz, s, d = x.shape
    assert bsz == 1 and d == D_MODEL
    depth = ada_w.shape[0]
    nh = MLA_HEADS
    xs = x.reshape(s, d)
    mod = _ada_mod(c, ada_w, ada_b)
    cos, sin, cos_t, sin_t = _rope_tables(positions)
    row = lambda a: a.reshape(1, -1)

    for layer in range(depth):
        i = layer // 2
        m = mod[layer]
        if layer % 2 == 0:
            has_next = layer + 1 < depth
            dummy = jnp.zeros((s // 512, 16, LANES), F32)
            xs, moe_w_down_bf16 = _even_mix(
                xs, m, row(norm_mix_w[layer]), e_w_in[i].astype(BF16), row(a_ln_w[i]), row(a_ln_b[i]), a_w_s[i],
                a_b_s[i].reshape(A_GROUPS, CHUNK, 1), b_w_grp[i].astype(BF16), row(b_scale[i]),
                e_w_out[i].astype(BF16), moe_w_down[i] if has_next else dummy)
            xs, moe_w_gu_bf16 = _ffn(xs, m, row(norm_ffn_w[layer]), ffn_w_gu[i].astype(BF16),
                                     ffn_w_down[i].astype(BF16), moe_w_gu[i] if has_next else dummy)
        else:
            o2 = 3 * C_WIDTH + Q_LORA + KV_LORA
            w_in = o_w_in[i][:, :o2].astype(BF16)
            w_kpe = jnp.pad(o_w_in[i][:, o2:], ((0, 0), (0, LANES - QK_ROPE))).astype(BF16)
            wkv = w_ukv[i].reshape(KV_LORA, nh, QK_NOPE + V_HD)
            w_k = wkv[:, :, :QK_NOPE].reshape(KV_LORA, nh * QK_NOPE).astype(BF16)
            w_v_t = wkv[:, :, QK_NOPE:].reshape(KV_LORA, nh * V_HD).T.astype(BF16)
            yc, qt, k, vt = _odd_prep(
                xs, m, row(norm_mix_w[layer]), w_in, w_kpe, c_conv_w[i], row(q_a_norm[i]), w_uq[i].T.astype(BF16),
                row(kv_norm[i]), w_k, w_v_t, q_norm_w[i].reshape(QK_HD, 1), row(k_norm_w[i]), cos, sin, cos_t, sin_t)
            yd = _attention(qt, k, vt)
            wr = jnp.pad(router_w[i], ((0, 0), (0, LANES - N_EXPERTS)))
            wr_hi = lax.reduce_precision(wr, exponent_bits=8, mantissa_bits=7)
            w_router = jnp.concatenate([wr_hi, wr - wr_hi], axis=1).astype(BF16)
            xs, hp, rw, ridx, counts = _odd_out(xs, yc, yd, m, row(norm_ffn_w[layer]), o_w_out[i].astype(BF16),
                                                w_router)
            xs = _moe_sparse(xs, hp, rw, ridx, counts, m, moe_w_gu_bf16, moe_w_down_bf16)
    return xs.reshape(bsz, s, d)
```

```python
import functools

import jax
import jax.numpy as jnp
from jax import lax
from jax.experimental import pallas as pl
from jax.experimental.pallas import tpu as pltpu
from jax.experimental.pallas import tpu_sc as plsc

D_MODEL = 1024
SEQ = 16384
EPS = 1e-6
CHUNK = 128
A_WIDTH = 512
A_GROUPS = 4
B_WIDTH = 512
POOL_WINDOWS = (2, 4, 8, 16)
B_HD = 128
C_WIDTH = 512
MLA_HEADS = 4
Q_LORA = 256
KV_LORA = 256
QK_NOPE = 128
QK_ROPE = 64
QK_HD = QK_NOPE + QK_ROPE
V_HD = 128
ROPE_THETA = 10000.0
D_FF = 2816
N_EXPERTS = 8
D_FF_EXPERT = 3584

LANES = 128
SUBLANES = 8
POOL_HALO = 16
CONV_HALO = 8
QK_PAD = 256
V_PAD = 144
LOG2E = 1.4426950408889634
MASK_VALUE = -1e30
MOE_TILE = 512
CONV_COLS = 256
ATTN_MAX_BOUND = 48.0
MOE_ROW_STEP = 128
COMBINE_PARTS = 4
SC_ROWS = 64
VMEM_LIMIT = 56 * 1024 * 1024

F32 = jnp.float32
BF16 = jnp.bfloat16


def _params(sem, vmem=VMEM_LIMIT, flags=None):
    return pltpu.CompilerParams(dimension_semantics=sem, vmem_limit_bytes=vmem, flags=flags)


def _const_spec(shape, single=False):
    nd = len(shape)
    return pl.BlockSpec(shape, lambda *_: (0,) * nd, pipeline_mode=pl.Buffered(1) if single else None)


def _rms_mod(x, nw, sc, sh):
    ms = jnp.mean(x * x, axis=-1, keepdims=True)
    return (x * lax.rsqrt(ms + EPS)) * nw * (1.0 + sc) + sh


def _ada_kernel(c_ref, w_ref, b_ref, o_ref):
    c = c_ref[...]
    ca = c * jax.nn.sigmoid(c)
    o_ref[0] = jnp.sum(w_ref[0] * ca, axis=0, keepdims=True) + b_ref[0]


def _ada_mod(c, ada_w, ada_b):
    depth, d, n = ada_w.shape
    tn = 1536
    return pl.pallas_call(
        _ada_kernel,
        grid=(depth, n // tn),
        in_specs=[
            pl.BlockSpec((d, 1), lambda l, j: (0, 0)),
            pl.BlockSpec((1, d, tn), lambda l, j: (l, 0, j)),
            pl.BlockSpec((1, 1, tn), lambda l, j: (l, 0, j)),
        ],
        out_specs=pl.BlockSpec((1, 1, tn), lambda l, j: (l, 0, j)),
        out_shape=jax.ShapeDtypeStruct((depth, 1, n), F32),
        compiler_params=_params(("arbitrary", "arbitrary")),
        name="ada_mod",
    )(c.reshape(d, 1), ada_w, ada_b.reshape(depth, 1, n))


def _even_mix_kernel(x_ref, mod_ref, nw_ref, win_ref, lnw_ref, lnb_ref, ws_ref, bs_ref, wg_ref, bsc_ref,
                     wout_ref, side_ref, o_ref, side_out_ref, halo_ref, sv_ref, yb_ref):
    side_out_ref[...] = side_ref[...].astype(BF16)
    tm = x_ref.shape[0]
    i = pl.program_id(0)

    @pl.when(i == 0)
    def _():
        halo_ref[...] = jnp.zeros_like(halo_ref)

    d = D_MODEL
    x = x_ref[...]
    sh, sc, gate = mod_ref[:, 0:d], mod_ref[:, d:2 * d], mod_ref[:, 2 * d:3 * d]
    h = _rms_mod(x, nw_ref[...], sc, sh).astype(BF16)
    p = jnp.dot(h, win_ref[...], preferred_element_type=F32)

    gl = jax.nn.gelu(p[:, :2 * A_WIDTH])
    u = gl[:, :A_WIDTH]
    v = gl[:, A_WIDTH:]
    mu = jnp.mean(v, axis=-1, keepdims=True)
    vc = v - mu
    var = jnp.mean(vc * vc, axis=-1, keepdims=True)
    vn = (vc * lax.rsqrt(var + EPS) * lnw_ref[...] + lnb_ref[...]).astype(BF16)
    row = lax.broadcasted_iota(jnp.int32, (CHUNK, CHUNK), 0)
    col = lax.broadcasted_iota(jnp.int32, (CHUNK, CHUNK), 1)
    for g in range(A_GROUPS):
        w = jnp.where(col <= row, ws_ref[g], 0.0).astype(BF16)
        b = bs_ref[g]
        for c in range(tm // CHUNK):
            blk = vn[c * CHUNK:(c + 1) * CHUNK, g * LANES:(g + 1) * LANES]
            sv_ref[c * CHUNK:(c + 1) * CHUNK, g * LANES:(g + 1) * LANES] = (
                jnp.dot(w, blk, preferred_element_type=F32) + b)
    ya = (u * sv_ref[...]).astype(BF16)

    pb = p[:, 2 * A_WIDTH:]
    ext = jnp.concatenate([halo_ref[...], pb], axis=0)
    halo_ref[...] = pb[tm - POOL_HALO:, :]
    t_glob = i * tm + lax.broadcasted_iota(jnp.int32, (tm, 1), 0)
    s = ext
    width = 1
    for g, win in enumerate(POOL_WINDOWS):
        while width < win:
            s = s + pltpu.roll(s, width, 0)
            width *= 2
        cnt = jnp.minimum(t_glob + 1, win).astype(F32)
        sl = slice(g * B_HD, (g + 1) * B_HD)
        pooled = s[POOL_HALO:, sl] / cnt
        dg = (pooled - pb[:, sl]).astype(BF16)
        yb_ref[:, sl] = jnp.dot(dg, wg_ref[g], preferred_element_type=F32)
    yb = (yb_ref[...] * bsc_ref[...]).astype(BF16)

    mix = (jnp.dot(ya, wout_ref[:A_WIDTH, :], preferred_element_type=F32)
           + jnp.dot(yb, wout_ref[A_WIDTH:, :], preferred_element_type=F32))
    o_ref[...] = x + gate * mix


def _side_cast_spec(side, steps, axis):
    ne = side.shape[0]
    parts = steps // ne
    assert parts * ne == steps and side.shape[axis] % parts == 0
    block = list(side.shape)
    block[0] = 1
    block[axis] //= parts
    assert block[1] % 16 == 0 and block[2] % LANES == 0
    if axis == 1:
        return pl.BlockSpec(tuple(block), lambda i: (i // parts, i % parts, 0))
    return pl.BlockSpec(tuple(block), lambda i: (i // parts, 0, i % parts))


def _even_mix(x, mod, nw, w_in, ln_w, ln_b, w_s, b_s, w_grp, b_scale, w_out, side, tm=512):
    s, d = x.shape
    steps = s // tm
    side_spec = _side_cast_spec(side, steps, 1)
    return pl.pallas_call(
        _even_mix_kernel,
        grid=(steps,),
        in_specs=[
            pl.BlockSpec((tm, d), lambda i: (i, 0)),
            _const_spec(mod.shape), _const_spec(nw.shape), _const_spec(w_in.shape),
            _const_spec(ln_w.shape), _const_spec(ln_b.shape), _const_spec(w_s.shape), _const_spec(b_s.shape),
            _const_spec(w_grp.shape), _const_spec(b_scale.shape), _const_spec(w_out.shape), side_spec,
        ],
        out_specs=[pl.BlockSpec((tm, d), lambda i: (i, 0)), side_spec],
        out_shape=[jax.ShapeDtypeStruct((s, d), F32), jax.ShapeDtypeStruct(side.shape, BF16)],
        scratch_shapes=[pltpu.VMEM((POOL_HALO, B_WIDTH), F32), pltpu.VMEM((tm, A_WIDTH), F32),
                        pltpu.VMEM((tm, B_WIDTH), F32)],
        compiler_params=_params(("arbitrary",)),
        name="even_mix",
    )(x, mod, nw, w_in, ln_w, ln_b, w_s, b_s, w_grp, b_scale, w_out, side)


def _ffn_kernel(x_ref, mod_ref, nw_ref, wgu_ref, wd_ref, side_ref, o_ref, side_out_ref, *, n_chunks):
    side_out_ref[...] = side_ref[...].astype(BF16)
    d = D_MODEL
    x = x_ref[...]
    sh, sc, gate = mod_ref[:, 3 * d:4 * d], mod_ref[:, 4 * d:5 * d], mod_ref[:, 5 * d:6 * d]
    h = _rms_mod(x, nw_ref[...], sc, sh).astype(BF16)
    ff = wd_ref.shape[0]
    tf = ff // n_chunks
    acc = jnp.zeros(x.shape, F32)
    for f in range(n_chunks):
        g = jnp.dot(h, wgu_ref[:, f * tf:(f + 1) * tf], preferred_element_type=F32)
        u = jnp.dot(h, wgu_ref[:, ff + f * tf:ff + (f + 1) * tf], preferred_element_type=F32)
        act = (g * jax.nn.sigmoid(g) * u).astype(BF16)
        acc = acc + jnp.dot(act, wd_ref[f * tf:(f + 1) * tf, :], preferred_element_type=F32)
    o_ref[...] = x + gate * acc


def _ffn(x, mod, nw, w_gu, w_down, side, tm=512, n_chunks=11):
    s, d = x.shape
    steps = s // tm
    side_spec = _side_cast_spec(side, steps, 2)
    return pl.pallas_call(
        functools.partial(_ffn_kernel, n_chunks=n_chunks),
        grid=(steps,),
        in_specs=[
            pl.BlockSpec((tm, d), lambda i: (i, 0)),
            _const_spec(mod.shape), _const_spec(nw.shape), _const_spec(w_gu.shape, single=True),
            _const_spec(w_down.shape, single=True), side_spec,
        ],
        out_specs=[pl.BlockSpec((tm, d), lambda i: (i, 0)), side_spec],
        out_shape=[jax.ShapeDtypeStruct((s, d), F32), jax.ShapeDtypeStruct(side.shape, BF16)],
        compiler_params=_params(("arbitrary",)),
        name="ffn",
    )(x, mod, nw, w_gu, w_down, side)


def _rope_table_kernel(pos_ref, invf_ref, cos_ref, sin_ref):
    ang = pos_ref[...].astype(F32) * invf_ref[...]
    cos_ref[...] = jnp.cos(ang)
    sin_ref[...] = jnp.sin(ang)


def _rope_tables(positions):
    s = positions.shape[-1]
    half = QK_ROPE // 2
    inv_freq = ROPE_THETA ** (-jnp.arange(0, QK_ROPE, 2, dtype=F32) / QK_ROPE)
    cos_t, sin_t = pl.pallas_call(
        _rope_table_kernel,
        out_shape=(jax.ShapeDtypeStruct((half, s), F32), jax.ShapeDtypeStruct((half, s), F32)),
        name="rope_tables",
    )(positions.reshape(1, s), inv_freq.reshape(half, 1))
    return jnp.tile(cos_t.T, (1, 2)), jnp.tile(sin_t.T, (1, 2)), cos_t, sin_t


def _rope_rot(x):
    n = x.shape[-1]
    lane = lax.broadcasted_iota(jnp.int32, x.shape, 1)
    half = QK_ROPE // 2
    fwd = pltpu.roll(x, half, 1)
    bwd = pltpu.roll(x, n - half, 1)
    return jnp.where((lane % QK_ROPE) < half, -bwd, fwd)


def _odd_prep_kernel(x_ref, mod_ref, nw_ref, win_ref, wkpe_ref, cw_ref, qan_ref, wuqt_ref, kvn_ref, wk_ref, wvt_ref,
                     qnw_ref, knw_ref, shift_ref, cos_ref, sin_ref, cost_ref, sint_ref, yc_ref, qt_ref, k_ref, vt_ref,
                     halo_ref):
    tm = x_ref.shape[0]
    i = pl.program_id(0)

    @pl.when(i == 0)
    def _():
        halo_ref[...] = jnp.zeros_like(halo_ref)

    d = D_MODEL
    x = x_ref[...]
    sh, sc = mod_ref[:, 0:d], mod_ref[:, d:2 * d]
    h = _rms_mod(x, nw_ref[...], sc, sh).astype(BF16)
    cw = C_WIDTH
    o0 = 3 * cw
    proj = lambda lo, hi: jnp.dot(h, win_ref[:, lo:hi], preferred_element_type=F32)
    kpe = jnp.dot(h, wkpe_ref[...], preferred_element_type=F32)[:, :QK_ROPE]

    for c0 in range(0, cw, CONV_COLS):
        cols = slice(c0, c0 + CONV_COLS)
        z = proj(cw + c0, cw + c0 + CONV_COLS) * proj(2 * cw + c0, 2 * cw + c0 + CONV_COLS)
        ext = jnp.concatenate([halo_ref[:, cols], z], axis=0)
        halo_ref[:, cols] = z[tm - CONV_HALO:, :]
        z1 = pltpu.roll(ext, 1, 0)[CONV_HALO:, :]
        z2 = pltpu.roll(ext, 2, 0)[CONV_HALO:, :]
        conv = cw_ref[0:1, cols] * z2 + cw_ref[1:2, cols] * z1 + cw_ref[2:3, cols] * z
        yc_ref[:, cols] = (proj(c0, c0 + CONV_COLS) * conv).astype(BF16)

    cq = proj(o0, o0 + Q_LORA)
    ckv = proj(o0 + Q_LORA, o0 + Q_LORA + KV_LORA)
    cqn = cq * lax.rsqrt(jnp.mean(cq * cq, axis=-1, keepdims=True) + EPS) * qan_ref[...]
    ckvn = ckv * lax.rsqrt(jnp.mean(ckv * ckv, axis=-1, keepdims=True) + EPS) * kvn_ref[...]
    cqn_t = cqn.T.astype(BF16)
    ckvn_t = ckvn.T.astype(BF16)
    q_t = jnp.dot(wuqt_ref[...], cqn_t, preferred_element_type=F32)
    v_t = jnp.dot(wvt_ref[...], ckvn_t, preferred_element_type=F32)
    kn_all = jnp.dot(ckvn.astype(BF16), wk_ref[...], preferred_element_type=F32)

    nh = MLA_HEADS
    half = QK_ROPE // 2
    sm_scale = QK_HD ** -0.5
    cos_t, sin_t = cost_ref[...], sint_ref[...]
    kn_w, kr_w = knw_ref[:, :QK_NOPE], knw_ref[:, QK_NOPE:]
    kr = kpe * kr_w
    kr = kr * cos_ref[...] + _rope_rot(kr) * sin_ref[...]
    kpe_ss = jnp.sum(kpe * kpe, axis=-1, keepdims=True)
    npad = QK_PAD - QK_HD
    zrows = jnp.where(lax.broadcasted_iota(jnp.int32, (npad, tm), 0) == 0, shift_ref[:, 0:1], 0.0)
    zpad = jnp.where(lax.broadcasted_iota(jnp.int32, (tm, npad), 1) == 0, 1.0, 0.0)
    ones_rows = (lax.broadcasted_iota(jnp.int32, (V_PAD - V_HD, tm), 0) == 0).astype(BF16)
    for hd in range(nh):
        qh = q_t[hd * QK_HD:(hd + 1) * QK_HD, :]
        q_inv = lax.rsqrt(jnp.sum(qh * qh, axis=0, keepdims=True) / QK_HD + EPS) * (sm_scale * LOG2E)
        qw = qh * qnw_ref[...]
        x1, x2 = qw[QK_NOPE:QK_NOPE + half, :], qw[QK_NOPE + half:, :]
        qt_ref[hd] = jnp.concatenate(
            [qw[:QK_NOPE, :] * q_inv, (x1 * cos_t - x2 * sin_t) * q_inv, (x2 * cos_t + x1 * sin_t) * q_inv, zrows],
            axis=0).astype(BF16)
        kn = kn_all[:, hd * QK_NOPE:(hd + 1) * QK_NOPE]
        k_ss = jnp.sum(kn * kn, axis=-1, keepdims=True) + kpe_ss
        k_inv = lax.rsqrt(k_ss / QK_HD + EPS)
        k_ref[hd] = jnp.concatenate([kn * k_inv * kn_w, kr * k_inv, zpad], axis=1).astype(BF16)
        vt_ref[hd, :V_HD, :] = v_t[hd * V_HD:(hd + 1) * V_HD, :].astype(BF16)
        vt_ref[hd, V_HD:, :] = ones_rows


def _odd_prep(x, mod, nw, w_in, w_kpe, conv_w, q_a_norm, w_uq_t, kv_norm, w_k, w_v_t, q_norm_col, k_norm_w, shift,
              cos, sin, cos_t, sin_t, tm=1024):
    s, d = x.shape
    nh = MLA_HEADS
    half = QK_ROPE // 2
    consts = [mod, nw, w_in, w_kpe, conv_w, q_a_norm, w_uq_t, kv_norm, w_k, w_v_t, q_norm_col, k_norm_w, shift]
    return pl.pallas_call(
        _odd_prep_kernel,
        grid=(s // tm,),
        in_specs=[pl.BlockSpec((tm, d), lambda i: (i, 0))] + [_const_spec(a.shape) for a in consts] + [
            pl.BlockSpec((tm, QK_ROPE), lambda i: (i, 0)), pl.BlockSpec((tm, QK_ROPE), lambda i: (i, 0)),
            pl.BlockSpec((half, tm), lambda i: (0, i)), pl.BlockSpec((half, tm), lambda i: (0, i))],
        out_specs=[
            pl.BlockSpec((tm, C_WIDTH), lambda i: (i, 0)),
            pl.BlockSpec((nh, QK_PAD, tm), lambda i: (0, 0, i)),
            pl.BlockSpec((nh, tm, QK_PAD), lambda i: (0, i, 0)),
            pl.BlockSpec((nh, V_PAD, tm), lambda i: (0, 0, i)),
        ],
        out_shape=[
            jax.ShapeDtypeStruct((s, C_WIDTH), BF16),
            jax.ShapeDtypeStruct((nh, QK_PAD, s), BF16),
            jax.ShapeDtypeStruct((nh, s, QK_PAD), BF16),
            jax.ShapeDtypeStruct((nh, V_PAD, s), BF16),
        ],
        scratch_shapes=[pltpu.VMEM((CONV_HALO, C_WIDTH), F32)],
        compiler_params=_params(("arbitrary",)),
        name="odd_prep",
    )(x, *consts, cos, sin, cos_t, sin_t)


def _causal_mask(s):
    key = lax.broadcasted_iota(jnp.int32, s.shape, 0)
    qry = lax.broadcasted_iota(jnp.int32, s.shape, 1)
    return jnp.where(key <= qry, s, MASK_VALUE)


def _attn_shifted_kernel(qt_ref, k_ref, vt_ref, o_ref, acc_ref, *, tq, tk):
    qi = pl.program_id(1)
    acc_ref[...] = jnp.zeros_like(acc_ref)
    r = tq // tk

    def block(j, diag_piece=None):
        q0 = 0 if diag_piece is None else diag_piece * tk
        start = pl.multiple_of(j * tk, tk)
        s = jnp.dot(k_ref[0, pl.ds(start, tk), :], qt_ref[0, :, q0:], preferred_element_type=F32)
        if diag_piece is not None:
            s = _causal_mask(s)
        p = jnp.exp2(s).astype(BF16)
        acc_ref[:, q0:] += jnp.dot(vt_ref[0, :, pl.ds(start, tk)], p, preferred_element_type=F32)

    def body(t, carry):
        for b in range(r):
            block(r * t + b)
        return carry

    lax.fori_loop(0, qi, body, 0)
    for b in range(r):
        block(r * qi + b, diag_piece=b)
    acc = acc_ref[...]
    out_t = acc[:V_HD, :] / acc[V_HD:V_HD + 1, :]
    o_ref[...] = out_t.T.astype(o_ref.dtype)


def _attention_shifted(qt, k, vt, tq=1024, tk=512):
    nh, _, s = qt.shape
    return pl.pallas_call(
        functools.partial(_attn_shifted_kernel, tq=tq, tk=tk),
        grid=(nh, s // tq),
        in_specs=[
            pl.BlockSpec((1, QK_PAD, tq), lambda h, i: (h, 0, i)),
            pl.BlockSpec((1, s, QK_PAD), lambda h, i: (h, 0, 0)),
            pl.BlockSpec((1, V_PAD, s), lambda h, i: (h, 0, 0)),
        ],
        out_specs=pl.BlockSpec((tq, V_HD), lambda h, i: (i, h)),
        out_shape=jax.ShapeDtypeStruct((s, nh * V_HD), BF16),
        scratch_shapes=[pltpu.VMEM((V_PAD, tq), F32)],
        compiler_params=_params(("arbitrary", "arbitrary")),
        name="mla_attention_shifted",
    )(qt, k, vt)


def _attn_kernel(qt_ref, k_ref, vt_ref, o_ref, s0_ref, s1_ref, bm0_ref, bm1_ref, m_ref, acc_ref, *, tq, tk):
    qi = pl.program_id(1)
    m_ref[...] = jnp.full_like(m_ref, MASK_VALUE)
    acc_ref[...] = jnp.zeros_like(acc_ref)

    def scores(j, s_ref, bm_ref, diag_offset=None):
        q0 = 0 if diag_offset is None else diag_offset
        start = pl.multiple_of(j * tk, tk)
        s = jnp.dot(k_ref[0, pl.ds(start, tk), :], qt_ref[0, :, q0:], preferred_element_type=F32)
        if diag_offset is not None:
            s = _causal_mask(s)
        s_ref[:, q0:] = s
        bm_ref[:, q0:] = jnp.max(s, axis=0, keepdims=True)

    def consume(j, s_ref, bm_ref, q0=0):
        m_prev = m_ref[:, q0:]
        m_new = jnp.maximum(m_prev, bm_ref[:, q0:])
        alpha = jnp.exp2(m_prev - m_new)
        p = jnp.exp2((s_ref[:, q0:] - m_new).astype(BF16))
        start = pl.multiple_of(j * tk, tk)
        vt = vt_ref[0, :, pl.ds(start, tk)]
        acc_ref[:, q0:] = alpha * acc_ref[:, q0:] + jnp.dot(vt, p, preferred_element_type=F32)
        m_ref[:, q0:] = m_new

    r = tq // tk
    slots = ((s0_ref, bm0_ref), (s1_ref, bm1_ref))

    def trip(j0, then_diagonal):
        for b in range(r):
            first_diag = then_diagonal and b == r - 1
            scores(j0 + b + 1, *slots[(b + 1) % 2], diag_offset=0 if first_diag else None)
            consume(j0 + b, *slots[b % 2])

    def diagonal(jd):
        for b in range(r):
            if b + 1 < r:
                scores(jd + b + 1, *slots[(b + 1) % 2], diag_offset=(b + 1) * tk)
            consume(jd + b, *slots[b % 2], q0=b * tk)

    @pl.when(qi > 0)
    def _():
        scores(0, *slots[0])

    n_plain = jnp.maximum(qi - 1, 0)

    def body(t, carry):
        trip(r * (2 * t), False)
        trip(r * (2 * t + 1), False)
        return carry

    lax.fori_loop(0, n_plain // 2, body, 0)

    @pl.when(n_plain % 2 == 1)
    def _():
        trip(r * (n_plain - 1), False)

    @pl.when(qi > 0)
    def _():
        trip(r * (qi - 1), True)
        diagonal(r * qi)

    @pl.when(qi == 0)
    def _():
        scores(0, *slots[0], diag_offset=0)
        diagonal(0)

    acc = acc_ref[...]
    out_t = acc[:V_HD, :] / acc[V_HD:V_HD + 1, :]
    o_ref[...] = out_t.T.astype(o_ref.dtype)


def _attention(qt, k, vt, tq=1024, tk=512):
    nh, _, s = qt.shape
    assert tq % (2 * tk) == 0
    return pl.pallas_call(
        functools.partial(_attn_kernel, tq=tq, tk=tk),
        grid=(nh, s // tq),
        in_specs=[
            pl.BlockSpec((1, QK_PAD, tq), lambda h, i: (h, 0, i)),
            pl.BlockSpec((1, s, QK_PAD), lambda h, i: (h, 0, 0)),
            pl.BlockSpec((1, V_PAD, s), lambda h, i: (h, 0, 0)),
        ],
        out_specs=pl.BlockSpec((tq, V_HD), lambda h, i: (i, h)),
        out_shape=jax.ShapeDtypeStruct((s, nh * V_HD), BF16),
        scratch_shapes=[pltpu.VMEM((tk, tq), F32), pltpu.VMEM((tk, tq), F32), pltpu.VMEM((1, tq), F32),
                        pltpu.VMEM((1, tq), F32), pltpu.VMEM((1, tq), F32), pltpu.VMEM((V_PAD, tq), F32)],
        compiler_params=_params(("arbitrary", "arbitrary")),
        name="mla_attention",
    )(qt, k, vt)


def _pack_bf16_pairs(x):
    n = x.shape[1] // 2
    lo = pltpu.bitcast(x[:, :n].astype(BF16).astype(F32), jnp.uint32)
    hi = pltpu.bitcast(x[:, n:].astype(BF16).astype(F32), jnp.uint32)
    return (lo >> 16) | hi


def _unpack_bf16_pairs(p):
    lo = pltpu.bitcast(p << 16, F32)
    hi = pltpu.bitcast(p & jnp.uint32(0xFFFF0000), F32)
    return jnp.concatenate([lo, hi], axis=1)


def _odd_out_kernel(x_ref, yc_ref, yd_ref, mod_ref, nw_ref, wout_ref, wr_ref, before_ref, x_out_ref, h_ref, rw_ref,
                    ridx_ref, cnt_ref):
    d = D_MODEL
    x = x_ref[...]
    gate_m = mod_ref[:, 2 * d:3 * d]
    sh, sc = mod_ref[:, 3 * d:4 * d], mod_ref[:, 4 * d:5 * d]
    mix = (jnp.dot(yc_ref[...], wout_ref[:C_WIDTH, :], preferred_element_type=F32)
           + jnp.dot(yd_ref[...], wout_ref[C_WIDTH:, :], preferred_element_type=F32))
    x1 = x + gate_m * mix
    x_out_ref[...] = x1
    h = _rms_mod(x1, nw_ref[...], sc, sh)
    h_ref[...] = _pack_bf16_pairs(h)

    h_hi = h.astype(BF16)
    h_lo = (h - h_hi.astype(F32)).astype(BF16)
    hw = jnp.dot(h_hi, wr_ref[...], preferred_element_type=F32)
    logits = hw[:, :LANES] + (hw[:, LANES:] + jnp.dot(h_lo, wr_ref[:, :LANES], preferred_element_type=F32))
    lane = lax.broadcasted_iota(jnp.int32, logits.shape, 1)
    logits = jnp.where(lane < N_EXPERTS, logits, -jnp.inf)
    m1 = jnp.max(logits, axis=-1, keepdims=True)
    i1 = jnp.min(jnp.where(logits == m1, lane, LANES), axis=-1, keepdims=True)
    rest = jnp.where(lane == i1, -jnp.inf, logits)
    m2 = jnp.max(rest, axis=-1, keepdims=True)
    i2 = jnp.min(jnp.where(rest == m2, lane, LANES), axis=-1, keepdims=True)
    e2 = jnp.exp(m2 - m1)
    w1 = 1.0 / (1.0 + e2)
    w2 = e2 / (1.0 + e2)
    rw_ref[...] = jnp.where(lane == 0, w1, jnp.where(lane == 1, w2, 0.0))

    i = pl.program_id(0)

    @pl.when(i == 0)
    def _():
        cnt_ref[...] = jnp.zeros_like(cnt_ref)

    onehot = jnp.where((lane == i1) | (lane == i2), 1.0, 0.0)
    prefix = jnp.dot(before_ref[...], onehot.astype(BF16), preferred_element_type=F32) + cnt_ref[...]
    r1 = jnp.sum(jnp.where(lane == i1, prefix, 0.0), axis=-1, keepdims=True)
    r2 = jnp.sum(jnp.where(lane == i2, prefix, 0.0), axis=-1, keepdims=True)
    cnt_ref[...] += jnp.sum(onehot, axis=0, keepdims=True)
    cols = jnp.where(lane == 0, i1.astype(F32), jnp.where(lane == 1, i2.astype(F32),
                                                          jnp.where(lane == 2, r1, jnp.where(lane == 3, r2, 0.0))))
    ridx_ref[...] = cols.T[:ridx_ref.shape[0], :].astype(jnp.int32)


def _odd_out(x, yc, yd, mod, nw, w_out, w_router, tm=1024):
    s, d = x.shape
    return pl.pallas_call(
        _odd_out_kernel,
        grid=(s // tm,),
        in_specs=[
            pl.BlockSpec((tm, d), lambda i: (i, 0)),
            pl.BlockSpec((tm, C_WIDTH), lambda i: (i, 0)),
            pl.BlockSpec((tm, MLA_HEADS * V_HD), lambda i: (i, 0)),
            _const_spec(mod.shape), _const_spec(nw.shape), _const_spec(w_out.shape), _const_spec(w_router.shape),
            _const_spec((tm, tm)),
        ],
        out_specs=[
            pl.BlockSpec((tm, d), lambda i: (i, 0)),
            pl.BlockSpec((tm, d // 2), lambda i: (i, 0)),
            pl.BlockSpec((tm, LANES), lambda i: (i, 0)),
            pl.BlockSpec((SUBLANES, tm), lambda i: (0, i)),
            pl.BlockSpec((1, LANES), lambda i: (0, 0)),
        ],
        out_shape=[
            jax.ShapeDtypeStruct((s, d), F32),
            jax.ShapeDtypeStruct((s, d // 2), jnp.uint32),
            jax.ShapeDtypeStruct((s, LANES), F32),
            jax.ShapeDtypeStruct((SUBLANES, s), jnp.int32),
            jax.ShapeDtypeStruct((1, LANES), F32),
        ],
        compiler_params=_params(("arbitrary",)),
        name="odd_out_router",
    )(x, yc, yd, mod, nw, w_out, w_router, jnp.tril(jnp.ones((tm, tm), BF16), -1))


def _sc_workers():
    info = plsc.get_sparse_core_info()
    return info.num_cores, info.num_cores * info.num_subcores


def _sc_scatter_rows(x, idx0, idx1, out_rows):
    n, w = x.shape
    nc, nw = _sc_workers()
    per_w = n // nw
    nch = per_w // SC_ROWS
    assert nch % 2 == 0 and nch >= 2
    mesh = plsc.VectorSubcoreMesh(core_axis_name="c", subcore_axis_name="s")

    @functools.partial(
        pl.kernel, mesh=mesh, out_type=jax.ShapeDtypeStruct((out_rows, w), x.dtype),
        scratch_types=[pltpu.VMEM((nch, SC_ROWS), jnp.int32), pltpu.VMEM((nch, SC_ROWS), jnp.int32),
                       pltpu.VMEM((SC_ROWS, w), x.dtype), pltpu.VMEM((SC_ROWS, w), x.dtype),
                       pltpu.SemaphoreType.DMA, pltpu.SemaphoreType.DMA, pltpu.SemaphoreType.DMA,
                       pltpu.SemaphoreType.DMA],
        name="moe_dispatch")
    def scatter(x_hbm, i0_hbm, i1_hbm, out_hbm, i0_v, i1_v, buf0, buf1, lsem0, lsem1, ssem0, ssem1):
        wid = lax.axis_index("s") * nc + lax.axis_index("c")
        pltpu.sync_copy(i0_hbm.at[wid], i0_v)
        pltpu.sync_copy(i1_hbm.at[wid], i1_v)
        base = wid * per_w

        def load(c, buf, sem):
            return pltpu.make_async_copy(x_hbm.at[pl.ds(base + c * SC_ROWS, SC_ROWS)], buf, sem)

        def put(c, buf):
            first = pltpu.make_async_copy(buf, out_hbm.at[i0_v.at[c]], ssem0)
            second = pltpu.make_async_copy(buf, out_hbm.at[i1_v.at[c]], ssem1)
            first.start()
            second.start()
            first.wait()
            second.wait()

        def pair(c, prefetch_next):
            load(c + 1, buf1, lsem1).start()
            load(c, buf0, lsem0).wait()
            put(c, buf0)
            if prefetch_next:
                load(c + 2, buf0, lsem0).start()
            load(c + 1, buf1, lsem1).wait()
            put(c + 1, buf1)

        load(0, buf0, lsem0).start()

        def body(t, carry):
            pair(2 * t, True)
            return carry

        lax.fori_loop(0, nch // 2 - 1, body, 0)
        pair(nch - 2, False)

    return scatter(x, idx0.reshape(nw, nch, SC_ROWS), idx1.reshape(nw, nch, SC_ROWS))


def _sc_gather_rows(table, idx):
    _, w = table.shape
    b = idx.shape[0]
    nc, nw = _sc_workers()
    per_w = b // nw
    nch = per_w // SC_ROWS
    mesh = plsc.VectorSubcoreMesh(core_axis_name="c", subcore_axis_name="s")

    assert nch % 2 == 0 and nch >= 2

    @functools.partial(
        pl.kernel, mesh=mesh, out_type=jax.ShapeDtypeStruct((b, w), table.dtype),
        scratch_types=[pltpu.VMEM((nch, SC_ROWS), jnp.int32), pltpu.VMEM((SC_ROWS, w), table.dtype),
                       pltpu.VMEM((SC_ROWS, w), table.dtype), pltpu.SemaphoreType.DMA, pltpu.SemaphoreType.DMA],
        name="moe_combine_gather")
    def gather(table_hbm, idx_hbm, out_hbm, idx_v, buf0, buf1, sem0, sem1):
        wid = lax.axis_index("s") * nc + lax.axis_index("c")
        pltpu.sync_copy(idx_hbm.at[wid], idx_v)
        base = wid * per_w

        def fetch(c, buf, sem):
            return pltpu.make_async_copy(table_hbm.at[idx_v.at[c]], buf, sem)

        def put(c, buf):
            pltpu.sync_copy(buf, out_hbm.at[pl.ds(base + c * SC_ROWS, SC_ROWS)])

        def pair(c, prefetch_next):
            fetch(c + 1, buf1, sem1).start()
            fetch(c, buf0, sem0).wait()
            put(c, buf0)
            if prefetch_next:
                fetch(c + 2, buf0, sem0).start()
            fetch(c + 1, buf1, sem1).wait()
            put(c + 1, buf1)

        fetch(0, buf0, sem0).start()

        def body(t, carry):
            pair(2 * t, True)
            return carry

        lax.fori_loop(0, nch // 2 - 1, body, 0)
        pair(nch - 2, False)

    return gather(table, idx.reshape(nw, nch, SC_ROWS))


def _moe_kernel(te_ref, nv_ref, rows_ref, xs_ref, wg_ref, wu_ref, wd_ref, ys_ref, x_scr, acc_ref):
    j = pl.program_id(0)
    f = pl.program_id(1)
    nf = pl.num_programs(1)
    tm = xs_ref.shape[0]

    def expert_ffn(rows):
        h = x_scr[:rows, :]
        g = jnp.dot(h, wg_ref[0], preferred_element_type=F32)
        u = jnp.dot(h, wu_ref[0], preferred_element_type=F32)
        act = (g * jax.nn.sigmoid(g) * u).astype(BF16)
        y = jnp.dot(act, wd_ref[0], preferred_element_type=F32)

        @pl.when(f == 0)
        def _():
            acc_ref[:rows, :] = y

        @pl.when(f > 0)
        def _():
            acc_ref[:rows, :] += y

        @pl.when(f == nf - 1)
        def _():
            ys_ref[:rows, :] = _pack_bf16_pairs(acc_ref[:rows, :])

    @pl.when(j < nv_ref[0])
    def _():
        @pl.when(f == 0)
        def _():
            x_scr[...] = _unpack_bf16_pairs(xs_ref[...]).astype(BF16)

        for rows in range(MOE_ROW_STEP, tm + 1, MOE_ROW_STEP):
            pl.when(rows_ref[j] == rows)(functools.partial(expert_ffn, rows))


def _moe_grouped(xs, tile_expert, n_valid, tile_rows, w_gu, w_down, tm, tf=1792):
    p_rows, dh = xs.shape
    d = 2 * dh
    ne, _, ff2 = w_gu.shape
    ff = ff2 // 2
    nf = ff // tf

    def tile(j, nv):
        return jnp.minimum(j, nv[0] - 1)

    def chunk(j, f, nv):
        return jnp.where(j < nv[0], f, nf - 1)

    grid_spec = pltpu.PrefetchScalarGridSpec(
        num_scalar_prefetch=3,
        grid=(p_rows // tm, nf),
        in_specs=[
            pl.BlockSpec((tm, dh), lambda j, f, te, nv, tr: (tile(j, nv), 0)),
            pl.BlockSpec((1, d, tf), lambda j, f, te, nv, tr: (te[tile(j, nv)], 0, chunk(j, f, nv))),
            pl.BlockSpec((1, d, tf), lambda j, f, te, nv, tr: (te[tile(j, nv)], 0, nf + chunk(j, f, nv))),
            pl.BlockSpec((1, tf, d), lambda j, f, te, nv, tr: (te[tile(j, nv)], chunk(j, f, nv), 0)),
        ],
        out_specs=pl.BlockSpec((tm, dh), lambda j, f, te, nv, tr: (tile(j, nv), 0)),
        scratch_shapes=[pltpu.VMEM((tm, d), BF16), pltpu.VMEM((tm, d), F32)],
    )
    return pl.pallas_call(
        _moe_kernel,
        grid_spec=grid_spec,
        out_shape=jax.ShapeDtypeStruct((p_rows, dh), jnp.uint32),
        compiler_params=_params(("arbitrary", "arbitrary")),
        name="moe_experts",
    )(tile_expert, n_valid, tile_rows, xs, w_gu, w_gu, w_down)


def _moe_combine_kernel(x_ref, y0_ref, y1_ref, rw_ref, mod_ref, *rest):
    o_ref = rest[-1]
    d = D_MODEL
    w1 = rw_ref[:, 0:1]
    w2 = rw_ref[:, 1:2]
    y = w1 * _unpack_bf16_pairs(y0_ref[...]) + w2 * _unpack_bf16_pairs(y1_ref[...])
    o_ref[...] = x_ref[...] + mod_ref[:, 5 * d:6 * d] * y


def _moe_combine(x, yg, rw, mod, prev, part, n_parts, tm=512):
    s, d = x.shape
    nb = s // tm // n_parts
    first = part * nb
    in_specs = [
        pl.BlockSpec((tm, d), lambda i: (first + i, 0)),
        pl.BlockSpec((tm, d // 2), lambda i: (i, 0)),
        pl.BlockSpec((tm, d // 2), lambda i: (nb + i, 0)),
        pl.BlockSpec((tm, LANES), lambda i: (first + i, 0)),
        _const_spec(mod.shape),
    ]
    args = [x, yg, yg, rw, mod]
    aliases = {}
    if prev is not None:
        in_specs.append(pl.BlockSpec(memory_space=pl.ANY))
        args.append(prev)
        aliases = {len(args) - 1: 0}
    return pl.pallas_call(
        _moe_combine_kernel,
        grid=(nb,),
        in_specs=in_specs,
        out_specs=pl.BlockSpec((tm, d), lambda i: (first + i, 0)),
        out_shape=jax.ShapeDtypeStruct((s, d), F32),
        input_output_aliases=aliases,
        compiler_params=_params(("arbitrary",)),
        name="moe_combine",
    )(*args)


def _moe_sparse(x, h_packed, rw, ridx, counts, mod, w_gu, w_down, tm=MOE_TILE):
    s = x.shape[0]
    ne = w_gu.shape[0]
    n_tiles = (2 * s) // tm + ne
    cnt = counts[0, :ne].astype(jnp.int32)
    padded = ((cnt + tm - 1) // tm) * tm
    ends = jnp.cumsum(padded)
    offs = ends - padded
    experts = jnp.arange(ne, dtype=jnp.int32)
    off_of = lambda e: jnp.sum(jnp.where(e[:, None] == experts[None, :], offs[None, :], 0), axis=1)
    pos0 = off_of(ridx[0]) + ridx[2]
    pos1 = off_of(ridx[1]) + ridx[3]
    tile_start = jnp.arange(n_tiles, dtype=jnp.int32) * tm
    tile_expert = jnp.minimum(jnp.sum(tile_start[:, None] >= ends[None, :], axis=1), ne - 1).astype(jnp.int32)
    n_valid = (ends[-1] // tm).reshape(1).astype(jnp.int32)
    filled = jnp.clip((offs + cnt)[tile_expert] - tile_start, 0, tm)
    tile_rows = (((filled + MOE_ROW_STEP - 1) // MOE_ROW_STEP) * MOE_ROW_STEP).astype(jnp.int32)
    xs = _sc_scatter_rows(h_packed, pos0, pos1, n_tiles * tm)
    ys = _moe_grouped(xs, tile_expert, n_valid, tile_rows, w_gu, w_down, tm)
    out = None
    sp = s // COMBINE_PARTS
    for part in range(COMBINE_PARTS):
        rows = slice(part * sp, (part + 1) * sp)
        yg = _sc_gather_rows(ys, jnp.concatenate([pos0[rows], pos1[rows]]))
        out = _moe_combine(x, yg, rw, mod, out, part, COMBINE_PARTS)
    return out


def kernel(x, c, positions, norm_mix_w, norm_ffn_w, ada_w, ada_b, e_w_in, a_ln_w, a_ln_b, a_w_s, a_b_s, b_w_grp,
           b_scale, e_w_out, ffn_w_gu, ffn_w_down, o_w_in, c_conv_w, q_a_norm, w_uq, kv_norm, w_ukv, q_norm_w,
           k_norm_w, o_w_out, router_w, moe_w_gu, moe_w_down):
    bsz, s, d = x.shape
    assert bsz == 1 and d == D_MODEL
    depth = ada_w.shape[0]
    nh = MLA_HEADS
    xs = x.reshape(s, d)
    mod = _ada_mod(c, ada_w, ada_b)
    cos, sin, cos_t, sin_t = _rope_tables(positions)
    row = lambda a: a.reshape(1, -1)

    for layer in range(depth):
        i = layer // 2
        m = mod[layer]
        if layer % 2 == 0:
            has_next = layer + 1 < depth
            dummy = jnp.zeros((s // 512, 16, LANES), F32)
            xs, moe_w_down_bf16 = _even_mix(
                xs, m, row(norm_mix_w[layer]), e_w_in[i].astype(BF16), row(a_ln_w[i]), row(a_ln_b[i]), a_w_s[i],
                a_b_s[i].reshape(A_GROUPS, CHUNK, 1), b_w_grp[i].astype(BF16), row(b_scale[i]),
                e_w_out[i].astype(BF16), moe_w_down[i] if has_next else dummy)
            xs, moe_w_gu_bf16 = _ffn(xs, m, row(norm_ffn_w[layer]), ffn_w_gu[i].astype(BF16),
                                     ffn_w_down[i].astype(BF16), moe_w_gu[i] if has_next else dummy)
        else:
            o2 = 3 * C_WIDTH + Q_LORA + KV_LORA
            w_in = o_w_in[i][:, :o2].astype(BF16)
            w_kpe = jnp.pad(o_w_in[i][:, o2:], ((0, 0), (0, LANES - QK_ROPE))).astype(BF16)
            bound = (1.02 * QK_HD * QK_HD ** -0.5 * LOG2E) * jnp.max(jnp.abs(q_norm_w[i])) * jnp.max(jnp.abs(k_norm_w[i]))
            wkv = w_ukv[i].reshape(KV_LORA, nh, QK_NOPE + V_HD)
            w_k = wkv[:, :, :QK_NOPE].reshape(KV_LORA, nh * QK_NOPE).astype(BF16)
            w_v_t = wkv[:, :, QK_NOPE:].reshape(KV_LORA, nh * V_HD).T.astype(BF16)
            yc, qt, k, vt = _odd_prep(
                xs, m, row(norm_mix_w[layer]), w_in, w_kpe, c_conv_w[i], row(q_a_norm[i]), w_uq[i].T.astype(BF16),
                row(kv_norm[i]), w_k, w_v_t, q_norm_w[i].reshape(QK_HD, 1), row(k_norm_w[i]),
                jnp.full((1, LANES), -bound, F32), cos, sin, cos_t, sin_t)
            yd = lax.cond(bound <= ATTN_MAX_BOUND, _attention_shifted, _attention, qt, k, vt)
            wr = jnp.pad(router_w[i], ((0, 0), (0, LANES - N_EXPERTS)))
            wr_hi = lax.reduce_precision(wr, exponent_bits=8, mantissa_bits=7)
            w_router = jnp.concatenate([wr_hi, wr - wr_hi], axis=1).astype(BF16)
            xs, hp, rw, ridx, counts = _odd_out(xs, yc, yd, m, row(norm_ffn_w[layer]), o_w_out[i].astype(BF16),
                                                w_router)
            xs = _moe_sparse(xs, hp, rw, ridx, counts, m, moe_w_gu_bf16, moe_w_down_bf16)
    return xs.reshape(bsz, s, d)
```

```python
import functools

import jax
import jax.numpy as jnp
from jax import lax
from jax.experimental import pallas as pl
from jax.experimental.pallas import tpu as pltpu
from jax.experimental.pallas import tpu_sc as plsc

D_MODEL = 1024
SEQ = 16384
EPS = 1e-6
CHUNK = 128
A_WIDTH = 512
A_GROUPS = 4
B_WIDTH = 512
POOL_WINDOWS = (2, 4, 8, 16)
B_HD = 128
C_WIDTH = 512
MLA_HEADS = 4
Q_LORA = 256
KV_LORA = 256
QK_NOPE = 128
QK_ROPE = 64
QK_HD = QK_NOPE + QK_ROPE
V_HD = 128
ROPE_THETA = 10000.0
D_FF = 2816
N_EXPERTS = 8
D_FF_EXPERT = 3584

LANES = 128
SUBLANES = 8
POOL_HALO = 16
CONV_HALO = 8
QK_PAD = 256
V_PAD = 144
LOG2E = 1.4426950408889634
MASK_VALUE = -1e30
MOE_TILE = 512
CONV_COLS = 256
ATTN_MAX_BOUND = 48.0
MOE_ROW_STEP = 128
COMBINE_PARTS = 4
SC_ROWS = 64
VMEM_LIMIT = 56 * 1024 * 1024

F32 = jnp.float32
BF16 = jnp.bfloat16


def _params(sem, vmem=VMEM_LIMIT, flags=None):
    return pltpu.CompilerParams(dimension_semantics=sem, vmem_limit_bytes=vmem, flags=flags)


def _const_spec(shape, single=False):
    nd = len(shape)
    return pl.BlockSpec(shape, lambda *_: (0,) * nd, pipeline_mode=pl.Buffered(1) if single else None)


def _rms_mod(x, nw, sc, sh):
    ms = jnp.mean(x * x, axis=-1, keepdims=True)
    return (x * lax.rsqrt(ms + EPS)) * nw * (1.0 + sc) + sh


def _ada_kernel(c_ref, w_ref, b_ref, o_ref):
    c = c_ref[...]
    ca = c * jax.nn.sigmoid(c)
    o_ref[0] = jnp.sum(w_ref[0] * ca, axis=0, keepdims=True) + b_ref[0]


def _ada_mod(c, ada_w, ada_b):
    depth, d, n = ada_w.shape
    tn = 1536
    return pl.pallas_call(
        _ada_kernel,
        grid=(depth, n // tn),
        in_specs=[
            pl.BlockSpec((d, 1), lambda l, j: (0, 0)),
            pl.BlockSpec((1, d, tn), lambda l, j: (l, 0, j)),
            pl.BlockSpec((1, 1, tn), lambda l, j: (l, 0, j)),
        ],
        out_specs=pl.BlockSpec((1, 1, tn), lambda l, j: (l, 0, j)),
        out_shape=jax.ShapeDtypeStruct((depth, 1, n), F32),
        compiler_params=_params(("arbitrary", "arbitrary")),
        name="ada_mod",
    )(c.reshape(d, 1), ada_w, ada_b.reshape(depth, 1, n))


def _even_mix_kernel(x_ref, mod_ref, nw_ref, win_ref, lnw_ref, lnb_ref, ws_ref, bs_ref, wg_ref, bsc_ref,
                     wout_ref, side_ref, o_ref, side_out_ref, halo_ref, sv_ref, yb_ref):
    side_out_ref[...] = side_ref[...].astype(BF16)
    tm = x_ref.shape[0]
    i = pl.program_id(0)

    @pl.when(i == 0)
    def _():
        halo_ref[...] = jnp.zeros_like(halo_ref)

    d = D_MODEL
    x = x_ref[...]
    sh, sc, gate = mod_ref[:, 0:d], mod_ref[:, d:2 * d], mod_ref[:, 2 * d:3 * d]
    h = _rms_mod(x, nw_ref[...], sc, sh).astype(BF16)
    p = jnp.dot(h, win_ref[...], preferred_element_type=F32)

    gl = jax.nn.gelu(p[:, :2 * A_WIDTH])
    u = gl[:, :A_WIDTH]
    v = gl[:, A_WIDTH:]
    mu = jnp.mean(v, axis=-1, keepdims=True)
    vc = v - mu
    var = jnp.mean(vc * vc, axis=-1, keepdims=True)
    vn = (vc * lax.rsqrt(var + EPS) * lnw_ref[...] + lnb_ref[...]).astype(BF16)
    row = lax.broadcasted_iota(jnp.int32, (CHUNK, CHUNK), 0)
    col = lax.broadcasted_iota(jnp.int32, (CHUNK, CHUNK), 1)
    for g in range(A_GROUPS):
        w = jnp.where(col <= row, ws_ref[g], 0.0).astype(BF16)
        b = bs_ref[g]
        for c in range(tm // CHUNK):
            blk = vn[c * CHUNK:(c + 1) * CHUNK, g * LANES:(g + 1) * LANES]
            sv_ref[c * CHUNK:(c + 1) * CHUNK, g * LANES:(g + 1) * LANES] = (
                jnp.dot(w, blk, preferred_element_type=F32) + b)
    ya = (u * sv_ref[...]).astype(BF16)

    pb = p[:, 2 * A_WIDTH:]
    ext = jnp.concatenate([halo_ref[...], pb], axis=0)
    halo_ref[...] = pb[tm - POOL_HALO:, :]
    t_glob = i * tm + lax.broadcasted_iota(jnp.int32, (tm, 1), 0)
    s = ext
    width = 1
    for g, win in enumerate(POOL_WINDOWS):
        while width < win:
            s = s + pltpu.roll(s, width, 0)
            width *= 2
        cnt = jnp.minimum(t_glob + 1, win).astype(F32)
        sl = slice(g * B_HD, (g + 1) * B_HD)
        pooled = s[POOL_HALO:, sl] / cnt
        dg = (pooled - pb[:, sl]).astype(BF16)
        yb_ref[:, sl] = jnp.dot(dg, wg_ref[g], preferred_element_type=F32)
    yb = (yb_ref[...] * bsc_ref[...]).astype(BF16)

    mix = (jnp.dot(ya, wout_ref[:A_WIDTH, :], preferred_element_type=F32)
           + jnp.dot(yb, wout_ref[A_WIDTH:, :], preferred_element_type=F32))
    o_ref[...] = x + gate * mix


def _side_cast_spec(side, steps, axis):
    ne = side.shape[0]
    parts = steps // ne
    assert parts * ne == steps and side.shape[axis] % parts == 0
    block = list(side.shape)
    block[0] = 1
    block[axis] //= parts
    assert block[1] % 16 == 0 and block[2] % LANES == 0
    if axis == 1:
        return pl.BlockSpec(tuple(block), lambda i: (i // parts, i % parts, 0))
    return pl.BlockSpec(tuple(block), lambda i: (i // parts, 0, i % parts))


def _even_mix(x, mod, nw, w_in, ln_w, ln_b, w_s, b_s, w_grp, b_scale, w_out, side, tm=512):
    s, d = x.shape
    steps = s // tm
    side_spec = _side_cast_spec(side, steps, 1)
    return pl.pallas_call(
        _even_mix_kernel,
        grid=(steps,),
        in_specs=[
            pl.BlockSpec((tm, d), lambda i: (i, 0)),
            _const_spec(mod.shape), _const_spec(nw.shape), _const_spec(w_in.shape),
            _const_spec(ln_w.shape), _const_spec(ln_b.shape), _const_spec(w_s.shape), _const_spec(b_s.shape),
            _const_spec(w_grp.shape), _const_spec(b_scale.shape), _const_spec(w_out.shape), side_spec,
        ],
        out_specs=[pl.BlockSpec((tm, d), lambda i: (i, 0)), side_spec],
        out_shape=[jax.ShapeDtypeStruct((s, d), F32), jax.ShapeDtypeStruct(side.shape, BF16)],
        scratch_shapes=[pltpu.VMEM((POOL_HALO, B_WIDTH), F32), pltpu.VMEM((tm, A_WIDTH), F32),
                        pltpu.VMEM((tm, B_WIDTH), F32)],
        compiler_params=_params(("arbitrary",)),
        name="even_mix",
    )(x, mod, nw, w_in, ln_w, ln_b, w_s, b_s, w_grp, b_scale, w_out, side)


def _ffn_kernel(x_ref, mod_ref, nw_ref, wgu_ref, wd_ref, side_ref, o_ref, side_out_ref, *, n_chunks):
    side_out_ref[...] = side_ref[...].astype(BF16)
    d = D_MODEL
    x = x_ref[...]
    sh, sc, gate = mod_ref[:, 3 * d:4 * d], mod_ref[:, 4 * d:5 * d], mod_ref[:, 5 * d:6 * d]
    h = _rms_mod(x, nw_ref[...], sc, sh).astype(BF16)
    ff = wd_ref.shape[0]
    tf = ff // n_chunks
    acc = jnp.zeros(x.shape, F32)
    for f in range(n_chunks):
        g = jnp.dot(h, wgu_ref[:, f * tf:(f + 1) * tf], preferred_element_type=F32)
        u = jnp.dot(h, wgu_ref[:, ff + f * tf:ff + (f + 1) * tf], preferred_element_type=F32)
        act = (g * jax.nn.sigmoid(g) * u).astype(BF16)
        acc = acc + jnp.dot(act, wd_ref[f * tf:(f + 1) * tf, :], preferred_element_type=F32)
    o_ref[...] = x + gate * acc


def _ffn(x, mod, nw, w_gu, w_down, side, tm=512, n_chunks=11):
    s, d = x.shape
    steps = s // tm
    side_spec = _side_cast_spec(side, steps, 2)
    return pl.pallas_call(
        functools.partial(_ffn_kernel, n_chunks=n_chunks),
        grid=(steps,),
        in_specs=[
            pl.BlockSpec((tm, d), lambda i: (i, 0)),
            _const_spec(mod.shape), _const_spec(nw.shape), _const_spec(w_gu.shape, single=True),
            _const_spec(w_down.shape, single=True), side_spec,
        ],
        out_specs=[pl.BlockSpec((tm, d), lambda i: (i, 0)), side_spec],
        out_shape=[jax.ShapeDtypeStruct((s, d), F32), jax.ShapeDtypeStruct(side.shape, BF16)],
        compiler_params=_params(("arbitrary",)),
        name="ffn",
    )(x, mod, nw, w_gu, w_down, side)


def _rope_table_kernel(pos_ref, invf_ref, cos_ref, sin_ref):
    ang = pos_ref[...].astype(F32) * invf_ref[...]
    cos_ref[...] = jnp.cos(ang)
    sin_ref[...] = jnp.sin(ang)


def _rope_tables(positions):
    s = positions.shape[-1]
    half = QK_ROPE // 2
    inv_freq = ROPE_THETA ** (-jnp.arange(0, QK_ROPE, 2, dtype=F32) / QK_ROPE)
    cos_t, sin_t = pl.pallas_call(
        _rope_table_kernel,
        out_shape=(jax.ShapeDtypeStruct((half, s), F32), jax.ShapeDtypeStruct((half, s), F32)),
        name="rope_tables",
    )(positions.reshape(1, s), inv_freq.reshape(half, 1))
    return jnp.tile(cos_t.T, (1, 2)), jnp.tile(sin_t.T, (1, 2)), cos_t, sin_t


def _rope_rot(x):
    n = x.shape[-1]
    lane = lax.broadcasted_iota(jnp.int32, x.shape, 1)
    half = QK_ROPE // 2
    fwd = pltpu.roll(x, half, 1)
    bwd = pltpu.roll(x, n - half, 1)
    return jnp.where((lane % QK_ROPE) < half, -bwd, fwd)


def _odd_prep_kernel(x_ref, mod_ref, nw_ref, win_ref, wkpe_ref, cw_ref, qan_ref, wuqt_ref, kvn_ref, wk_ref, wvt_ref,
                     qnw_ref, knw_ref, shift_ref, cos_ref, sin_ref, cost_ref, sint_ref, yc_ref, qt_ref, k_ref, vt_ref,
                     halo_ref):
    tm = x_ref.shape[0]
    i = pl.program_id(0)

    @pl.when(i == 0)
    def _():
        halo_ref[...] = jnp.zeros_like(halo_ref)

    d = D_MODEL
    x = x_ref[...]
    sh, sc = mod_ref[:, 0:d], mod_ref[:, d:2 * d]
    h = _rms_mod(x, nw_ref[...], sc, sh).astype(BF16)
    cw = C_WIDTH
    o0 = 3 * cw
    proj = lambda lo, hi: jnp.dot(h, win_ref[:, lo:hi], preferred_element_type=F32)
    kpe = jnp.dot(h, wkpe_ref[...], preferred_element_type=F32)[:, :QK_ROPE]

    for c0 in range(0, cw, CONV_COLS):
        cols = slice(c0, c0 + CONV_COLS)
        z = proj(cw + c0, cw + c0 + CONV_COLS) * proj(2 * cw + c0, 2 * cw + c0 + CONV_COLS)
        ext = jnp.concatenate([halo_ref[:, cols], z], axis=0)
        halo_ref[:, cols] = z[tm - CONV_HALO:, :]
        z1 = pltpu.roll(ext, 1, 0)[CONV_HALO:, :]
        z2 = pltpu.roll(ext, 2, 0)[CONV_HALO:, :]
        conv = cw_ref[0:1, cols] * z2 + cw_ref[1:2, cols] * z1 + cw_ref[2:3, cols] * z
        yc_ref[:, cols] = (proj(c0, c0 + CONV_COLS) * conv).astype(BF16)

    cq = proj(o0, o0 + Q_LORA)
    ckv = proj(o0 + Q_LORA, o0 + Q_LORA + KV_LORA)
    cqn = cq * lax.rsqrt(jnp.mean(cq * cq, axis=-1, keepdims=True) + EPS) * qan_ref[...]
    ckvn = ckv * lax.rsqrt(jnp.mean(ckv * ckv, axis=-1, keepdims=True) + EPS) * kvn_ref[...]
    cqn_t = cqn.T.astype(BF16)
    ckvn_t = ckvn.T.astype(BF16)
    q_t = jnp.dot(wuqt_ref[...], cqn_t, preferred_element_type=F32)
    v_t = jnp.dot(wvt_ref[...], ckvn_t, preferred_element_type=F32)
    kn_all = jnp.dot(ckvn.astype(BF16), wk_ref[...], preferred_element_type=F32)

    nh = MLA_HEADS
    half = QK_ROPE // 2
    sm_scale = QK_HD ** -0.5
    cos_t, sin_t = cost_ref[...], sint_ref[...]
    kn_w, kr_w = knw_ref[:, :QK_NOPE], knw_ref[:, QK_NOPE:]
    kr = kpe * kr_w
    kr = kr * cos_ref[...] + _rope_rot(kr) * sin_ref[...]
    kpe_ss = jnp.sum(kpe * kpe, axis=-1, keepdims=True)
    npad = QK_PAD - QK_HD
    zrows = jnp.where(lax.broadcasted_iota(jnp.int32, (npad, tm), 0) == 0, shift_ref[:, 0:1], 0.0)
    zpad = jnp.where(lax.broadcasted_iota(jnp.int32, (tm, npad), 1) == 0, 1.0, 0.0)
    ones_rows = (lax.broadcasted_iota(jnp.int32, (V_PAD - V_HD, tm), 0) == 0).astype(BF16)
    for hd in range(nh):
        qh = q_t[hd * QK_HD:(hd + 1) * QK_HD, :]
        q_inv = lax.rsqrt(jnp.sum(qh * qh, axis=0, keepdims=True) / QK_HD + EPS) * (sm_scale * LOG2E)
        qw = qh * qnw_ref[...]
        x1, x2 = qw[QK_NOPE:QK_NOPE + half, :], qw[QK_NOPE + half:, :]
        qt_ref[hd] = jnp.concatenate(
            [qw[:QK_NOPE, :] * q_inv, (x1 * cos_t - x2 * sin_t) * q_inv, (x2 * cos_t + x1 * sin_t) * q_inv, zrows],
            axis=0).astype(BF16)
        kn = kn_all[:, hd * QK_NOPE:(hd + 1) * QK_NOPE]
        k_ss = jnp.sum(kn * kn, axis=-1, keepdims=True) + kpe_ss
        k_inv = lax.rsqrt(k_ss / QK_HD + EPS)
        k_ref[hd] = jnp.concatenate([kn * k_inv * kn_w, kr * k_inv, zpad], axis=1).astype(BF16)
        vt_ref[hd, :V_HD, :] = v_t[hd * V_HD:(hd + 1) * V_HD, :].astype(BF16)
        vt_ref[hd, V_HD:, :] = ones_rows


def _odd_prep(x, mod, nw, w_in, w_kpe, conv_w, q_a_norm, w_uq_t, kv_norm, w_k, w_v_t, q_norm_col, k_norm_w, shift,
              cos, sin, cos_t, sin_t, tm=1024):
    s, d = x.shape
    nh = MLA_HEADS
    half = QK_ROPE // 2
    consts = [mod, nw, w_in, w_kpe, conv_w, q_a_norm, w_uq_t, kv_norm, w_k, w_v_t, q_norm_col, k_norm_w, shift]
    return pl.pallas_call(
        _odd_prep_kernel,
        grid=(s // tm,),
        in_specs=[pl.BlockSpec((tm, d), lambda i: (i, 0))] + [_const_spec(a.shape) for a in consts] + [
            pl.BlockSpec((tm, QK_ROPE), lambda i: (i, 0)), pl.BlockSpec((tm, QK_ROPE), lambda i: (i, 0)),
            pl.BlockSpec((half, tm), lambda i: (0, i)), pl.BlockSpec((half, tm), lambda i: (0, i))],
        out_specs=[
            pl.BlockSpec((tm, C_WIDTH), lambda i: (i, 0)),
            pl.BlockSpec((nh, QK_PAD, tm), lambda i: (0, 0, i)),
            pl.BlockSpec((nh, tm, QK_PAD), lambda i: (0, i, 0)),
            pl.BlockSpec((nh, V_PAD, tm), lambda i: (0, 0, i)),
        ],
        out_shape=[
            jax.ShapeDtypeStruct((s, C_WIDTH), BF16),
            jax.ShapeDtypeStruct((nh, QK_PAD, s), BF16),
            jax.ShapeDtypeStruct((nh, s, QK_PAD), BF16),
            jax.ShapeDtypeStruct((nh, V_PAD, s), BF16),
        ],
        scratch_shapes=[pltpu.VMEM((CONV_HALO, C_WIDTH), F32)],
        compiler_params=_params(("arbitrary",)),
        name="odd_prep",
    )(x, *consts, cos, sin, cos_t, sin_t)


def _causal_mask(s, lead=0):
    key = lax.broadcasted_iota(jnp.int32, s.shape, 0)
    qry = lead + lax.broadcasted_iota(jnp.int32, s.shape, 1)
    return jnp.where(key <= qry, s, MASK_VALUE)


def _attn_shifted_kernel(qt_ref, k_ref, vt_ref, o_ref, acc_ref, *, tq, tk):
    qi = pl.program_id(1)
    acc_ref[...] = jnp.zeros_like(acc_ref)
    r = tq // tk

    def block(j, diag_piece=None):
        q0 = 0 if diag_piece is None else diag_piece * tk
        start = pl.multiple_of(j * tk, tk)
        s = jnp.dot(k_ref[0, pl.ds(start, tk), :], qt_ref[0, :, q0:], preferred_element_type=F32)
        if diag_piece is not None:
            s = _causal_mask(s)
        p = jnp.exp2(s).astype(BF16)
        acc_ref[:, q0:] += jnp.dot(vt_ref[0, :, pl.ds(start, tk)], p, preferred_element_type=F32)

    def body(t, carry):
        for b in range(r):
            block(r * t + b)
        return carry

    lax.fori_loop(0, qi, body, 0)
    for b in range(r):
        block(r * qi + b, diag_piece=b)
    acc = acc_ref[...]
    out_t = acc[:V_HD, :] / acc[V_HD:V_HD + 1, :]
    o_ref[...] = out_t.T.astype(o_ref.dtype)


def _attention_shifted(qt, k, vt, tq=2048, tk=512):
    nh, _, s = qt.shape
    return pl.pallas_call(
        functools.partial(_attn_shifted_kernel, tq=tq, tk=tk),
        grid=(nh, s // tq),
        in_specs=[
            pl.BlockSpec((1, QK_PAD, tq), lambda h, i: (h, 0, i)),
            pl.BlockSpec((1, s, QK_PAD), lambda h, i: (h, 0, 0)),
            pl.BlockSpec((1, V_PAD, s), lambda h, i: (h, 0, 0)),
        ],
        out_specs=pl.BlockSpec((tq, V_HD), lambda h, i: (i, h)),
        out_shape=jax.ShapeDtypeStruct((s, nh * V_HD), BF16),
        scratch_shapes=[pltpu.VMEM((V_PAD, tq), F32)],
        compiler_params=_params(("arbitrary", "arbitrary")),
        name="mla_attention_shifted",
    )(qt, k, vt)


def _attn_kernel(qt_ref, k_ref, vt_ref, o_ref, s0_ref, s1_ref, bm0_ref, bm1_ref, m_ref, acc_ref, *, tq, tk):
    qi = pl.program_id(1)
    m_ref[...] = jnp.full_like(m_ref, MASK_VALUE)
    acc_ref[...] = jnp.zeros_like(acc_ref)

    def scores(j, s_ref, bm_ref, diag_offset=None):
        q0 = 0 if diag_offset is None else diag_offset
        start = pl.multiple_of(j * tk, tk)
        s = jnp.dot(k_ref[0, pl.ds(start, tk), :], qt_ref[0, :, q0:], preferred_element_type=F32)
        if diag_offset is not None:
            s = _causal_mask(s)
        s_ref[:, q0:] = s
        bm_ref[:, q0:] = jnp.max(s, axis=0, keepdims=True)

    def consume(j, s_ref, bm_ref, q0=0):
        m_prev = m_ref[:, q0:]
        m_new = jnp.maximum(m_prev, bm_ref[:, q0:])
        alpha = jnp.exp2(m_prev - m_new)
        p = jnp.exp2((s_ref[:, q0:] - m_new).astype(BF16))
        start = pl.multiple_of(j * tk, tk)
        vt = vt_ref[0, :, pl.ds(start, tk)]
        acc_ref[:, q0:] = alpha * acc_ref[:, q0:] + jnp.dot(vt, p, preferred_element_type=F32)
        m_ref[:, q0:] = m_new

    r = tq // tk
    slots = ((s0_ref, bm0_ref), (s1_ref, bm1_ref))

    def trip(j0, then_diagonal):
        for b in range(r):
            first_diag = then_diagonal and b == r - 1
            scores(j0 + b + 1, *slots[(b + 1) % 2], diag_offset=0 if first_diag else None)
            consume(j0 + b, *slots[b % 2])

    def diagonal(jd):
        for b in range(r):
            if b + 1 < r:
                scores(jd + b + 1, *slots[(b + 1) % 2], diag_offset=(b + 1) * tk)
            consume(jd + b, *slots[b % 2], q0=b * tk)

    @pl.when(qi > 0)
    def _():
        scores(0, *slots[0])

    n_plain = jnp.maximum(qi - 1, 0)

    def body(t, carry):
        trip(r * (2 * t), False)
        trip(r * (2 * t + 1), False)
        return carry

    lax.fori_loop(0, n_plain // 2, body, 0)

    @pl.when(n_plain % 2 == 1)
    def _():
        trip(r * (n_plain - 1), False)

    @pl.when(qi > 0)
    def _():
        trip(r * (qi - 1), True)
        diagonal(r * qi)

    @pl.when(qi == 0)
    def _():
        scores(0, *slots[0], diag_offset=0)
        diagonal(0)

    acc = acc_ref[...]
    out_t = acc[:V_HD, :] / acc[V_HD:V_HD + 1, :]
    o_ref[...] = out_t.T.astype(o_ref.dtype)


def _attention(qt, k, vt, tq=1024, tk=512):
    nh, _, s = qt.shape
    assert tq % (2 * tk) == 0
    return pl.pallas_call(
        functools.partial(_attn_kernel, tq=tq, tk=tk),
        grid=(nh, s // tq),
        in_specs=[
            pl.BlockSpec((1, QK_PAD, tq), lambda h, i: (h, 0, i)),
            pl.BlockSpec((1, s, QK_PAD), lambda h, i: (h, 0, 0)),
            pl.BlockSpec((1, V_PAD, s), lambda h, i: (h, 0, 0)),
        ],
        out_specs=pl.BlockSpec((tq, V_HD), lambda h, i: (i, h)),
        out_shape=jax.ShapeDtypeStruct((s, nh * V_HD), BF16),
        scratch_shapes=[pltpu.VMEM((tk, tq), F32), pltpu.VMEM((tk, tq), F32), pltpu.VMEM((1, tq), F32),
                        pltpu.VMEM((1, tq), F32), pltpu.VMEM((1, tq), F32), pltpu.VMEM((V_PAD, tq), F32)],
        compiler_params=_params(("arbitrary", "arbitrary")),
        name="mla_attention",
    )(qt, k, vt)


def _pack_bf16_pairs(x):
    n = x.shape[1] // 2
    lo = pltpu.bitcast(x[:, :n].astype(BF16).astype(F32), jnp.uint32)
    hi = pltpu.bitcast(x[:, n:].astype(BF16).astype(F32), jnp.uint32)
    return (lo >> 16) | hi


def _unpack_bf16_pairs(p):
    lo = pltpu.bitcast(p << 16, F32)
    hi = pltpu.bitcast(p & jnp.uint32(0xFFFF0000), F32)
    return jnp.concatenate([lo, hi], axis=1)


def _odd_out_kernel(x_ref, yc_ref, yd_ref, mod_ref, nw_ref, wout_ref, wr_ref, before_ref, x_out_ref, h_ref, rw_ref,
                    ridx_ref, cnt_ref):
    d = D_MODEL
    x = x_ref[...]
    gate_m = mod_ref[:, 2 * d:3 * d]
    sh, sc = mod_ref[:, 3 * d:4 * d], mod_ref[:, 4 * d:5 * d]
    mix = (jnp.dot(yc_ref[...], wout_ref[:C_WIDTH, :], preferred_element_type=F32)
           + jnp.dot(yd_ref[...], wout_ref[C_WIDTH:, :], preferred_element_type=F32))
    x1 = x + gate_m * mix
    x_out_ref[...] = x1
    h = _rms_mod(x1, nw_ref[...], sc, sh)
    h_ref[...] = _pack_bf16_pairs(h)

    h_hi = h.astype(BF16)
    h_lo = (h - h_hi.astype(F32)).astype(BF16)
    hw = jnp.dot(h_hi, wr_ref[...], preferred_element_type=F32)
    logits = hw[:, :LANES] + (hw[:, LANES:] + jnp.dot(h_lo, wr_ref[:, :LANES], preferred_element_type=F32))
    lane = lax.broadcasted_iota(jnp.int32, logits.shape, 1)
    logits = jnp.where(lane < N_EXPERTS, logits, -jnp.inf)
    m1 = jnp.max(logits, axis=-1, keepdims=True)
    i1 = jnp.min(jnp.where(logits == m1, lane, LANES), axis=-1, keepdims=True)
    rest = jnp.where(lane == i1, -jnp.inf, logits)
    m2 = jnp.max(rest, axis=-1, keepdims=True)
    i2 = jnp.min(jnp.where(rest == m2, lane, LANES), axis=-1, keepdims=True)
    e2 = jnp.exp(m2 - m1)
    w1 = 1.0 / (1.0 + e2)
    w2 = e2 / (1.0 + e2)
    rw_ref[...] = jnp.where(lane == 0, w1, jnp.where(lane == 1, w2, 0.0))

    i = pl.program_id(0)

    @pl.when(i == 0)
    def _():
        cnt_ref[...] = jnp.zeros_like(cnt_ref)

    onehot = jnp.where((lane == i1) | (lane == i2), 1.0, 0.0)
    prefix = jnp.dot(before_ref[...], onehot.astype(BF16), preferred_element_type=F32) + cnt_ref[...]
    r1 = jnp.sum(jnp.where(lane == i1, prefix, 0.0), axis=-1, keepdims=True)
    r2 = jnp.sum(jnp.where(lane == i2, prefix, 0.0), axis=-1, keepdims=True)
    cnt_ref[...] += jnp.sum(onehot, axis=0, keepdims=True)
    cols = jnp.where(lane == 0, i1.astype(F32), jnp.where(lane == 1, i2.astype(F32),
                                                          jnp.where(lane == 2, r1, jnp.where(lane == 3, r2, 0.0))))
    ridx_ref[...] = cols.T[:ridx_ref.shape[0], :].astype(jnp.int32)


def _odd_out(x, yc, yd, mod, nw, w_out, w_router, tm=1024):
    s, d = x.shape
    return pl.pallas_call(
        _odd_out_kernel,
        grid=(s // tm,),
        in_specs=[
            pl.BlockSpec((tm, d), lambda i: (i, 0)),
            pl.BlockSpec((tm, C_WIDTH), lambda i: (i, 0)),
            pl.BlockSpec((tm, MLA_HEADS * V_HD), lambda i: (i, 0)),
            _const_spec(mod.shape), _const_spec(nw.shape), _const_spec(w_out.shape), _const_spec(w_router.shape),
            _const_spec((tm, tm)),
        ],
        out_specs=[
            pl.BlockSpec((tm, d), lambda i: (i, 0)),
            pl.BlockSpec((tm, d // 2), lambda i: (i, 0)),
            pl.BlockSpec((tm, LANES), lambda i: (i, 0)),
            pl.BlockSpec((SUBLANES, tm), lambda i: (0, i)),
            pl.BlockSpec((1, LANES), lambda i: (0, 0)),
        ],
        out_shape=[
            jax.ShapeDtypeStruct((s, d), F32),
            jax.ShapeDtypeStruct((s, d // 2), jnp.uint32),
            jax.ShapeDtypeStruct((s, LANES), F32),
            jax.ShapeDtypeStruct((SUBLANES, s), jnp.int32),
            jax.ShapeDtypeStruct((1, LANES), F32),
        ],
        compiler_params=_params(("arbitrary",)),
        name="odd_out_router",
    )(x, yc, yd, mod, nw, w_out, w_router, jnp.tril(jnp.ones((tm, tm), BF16), -1))


def _sc_workers():
    info = plsc.get_sparse_core_info()
    return info.num_cores, info.num_cores * info.num_subcores


def _sc_scatter_rows(x, idx0, idx1, out_rows):
    n, w = x.shape
    nc, nw = _sc_workers()
    per_w = n // nw
    nch = per_w // SC_ROWS
    assert nch % 2 == 0 and nch >= 2
    mesh = plsc.VectorSubcoreMesh(core_axis_name="c", subcore_axis_name="s")

    @functools.partial(
        pl.kernel, mesh=mesh, out_type=jax.ShapeDtypeStruct((out_rows, w), x.dtype),
        scratch_types=[pltpu.VMEM((nch, SC_ROWS), jnp.int32), pltpu.VMEM((nch, SC_ROWS), jnp.int32),
                       pltpu.VMEM((SC_ROWS, w), x.dtype), pltpu.VMEM((SC_ROWS, w), x.dtype),
                       pltpu.SemaphoreType.DMA, pltpu.SemaphoreType.DMA, pltpu.SemaphoreType.DMA,
                       pltpu.SemaphoreType.DMA],
        name="moe_dispatch")
    def scatter(x_hbm, i0_hbm, i1_hbm, out_hbm, i0_v, i1_v, buf0, buf1, lsem0, lsem1, ssem0, ssem1):
        wid = lax.axis_index("s") * nc + lax.axis_index("c")
        pltpu.sync_copy(i0_hbm.at[wid], i0_v)
        pltpu.sync_copy(i1_hbm.at[wid], i1_v)
        base = wid * per_w

        def load(c, buf, sem):
            return pltpu.make_async_copy(x_hbm.at[pl.ds(base + c * SC_ROWS, SC_ROWS)], buf, sem)

        def put(c, buf):
            first = pltpu.make_async_copy(buf, out_hbm.at[i0_v.at[c]], ssem0)
            second = pltpu.make_async_copy(buf, out_hbm.at[i1_v.at[c]], ssem1)
            first.start()
            second.start()
            first.wait()
            second.wait()

        def pair(c, prefetch_next):
            load(c + 1, buf1, lsem1).start()
            load(c, buf0, lsem0).wait()
            put(c, buf0)
            if prefetch_next:
                load(c + 2, buf0, lsem0).start()
            load(c + 1, buf1, lsem1).wait()
            put(c + 1, buf1)

        load(0, buf0, lsem0).start()

        def body(t, carry):
            pair(2 * t, True)
            return carry

        lax.fori_loop(0, nch // 2 - 1, body, 0)
        pair(nch - 2, False)

    return scatter(x, idx0.reshape(nw, nch, SC_ROWS), idx1.reshape(nw, nch, SC_ROWS))


def _sc_gather_rows(table, idx):
    _, w = table.shape
    b = idx.shape[0]
    nc, nw = _sc_workers()
    per_w = b // nw
    nch = per_w // SC_ROWS
    mesh = plsc.VectorSubcoreMesh(core_axis_name="c", subcore_axis_name="s")

    assert nch % 2 == 0 and nch >= 2

    @functools.partial(
        pl.kernel, mesh=mesh, out_type=jax.ShapeDtypeStruct((b, w), table.dtype),
        scratch_types=[pltpu.VMEM((nch, SC_ROWS), jnp.int32), pltpu.VMEM((SC_ROWS, w), table.dtype),
                       pltpu.VMEM((SC_ROWS, w), table.dtype), pltpu.SemaphoreType.DMA, pltpu.SemaphoreType.DMA],
        name="moe_combine_gather")
    def gather(table_hbm, idx_hbm, out_hbm, idx_v, buf0, buf1, sem0, sem1):
        wid = lax.axis_index("s") * nc + lax.axis_index("c")
        pltpu.sync_copy(idx_hbm.at[wid], idx_v)
        base = wid * per_w

        def fetch(c, buf, sem):
            return pltpu.make_async_copy(table_hbm.at[idx_v.at[c]], buf, sem)

        def put(c, buf):
            pltpu.sync_copy(buf, out_hbm.at[pl.ds(base + c * SC_ROWS, SC_ROWS)])

        def pair(c, prefetch_next):
            fetch(c + 1, buf1, sem1).start()
            fetch(c, buf0, sem0).wait()
            put(c, buf0)
            if prefetch_next:
                fetch(c + 2, buf0, sem0).start()
            fetch(c + 1, buf1, sem1).wait()
            put(c + 1, buf1)

        fetch(0, buf0, sem0).start()

        def body(t, carry):
            pair(2 * t, True)
            return carry

        lax.fori_loop(0, nch // 2 - 1, body, 0)
        pair(nch - 2, False)

    return gather(table, idx.reshape(nw, nch, SC_ROWS))


def _moe_kernel(te_ref, nv_ref, rows_ref, xs_ref, wg_ref, wu_ref, wd_ref, ys_ref, x_scr, acc_ref):
    j = pl.program_id(0)
    f = pl.program_id(1)
    nf = pl.num_programs(1)
    tm = xs_ref.shape[0]

    def expert_ffn(rows):
        h = x_scr[:rows, :]
        g = jnp.dot(h, wg_ref[0], preferred_element_type=F32)
        u = jnp.dot(h, wu_ref[0], preferred_element_type=F32)
        act = (g * jax.nn.sigmoid(g) * u).astype(BF16)
        y = jnp.dot(act, wd_ref[0], preferred_element_type=F32)

        @pl.when(f == 0)
        def _():
            acc_ref[:rows, :] = y

        @pl.when(f > 0)
        def _():
            acc_ref[:rows, :] += y

        @pl.when(f == nf - 1)
        def _():
            ys_ref[:rows, :] = _pack_bf16_pairs(acc_ref[:rows, :])

    @pl.when(j < nv_ref[0])
    def _():
        @pl.when(f == 0)
        def _():
            x_scr[...] = _unpack_bf16_pairs(xs_ref[...]).astype(BF16)

        for rows in range(MOE_ROW_STEP, tm + 1, MOE_ROW_STEP):
            pl.when(rows_ref[j] == rows)(functools.partial(expert_ffn, rows))


def _moe_grouped(xs, tile_expert, n_valid, tile_rows, w_gu, w_down, tm, tf=1792):
    p_rows, dh = xs.shape
    d = 2 * dh
    ne, _, ff2 = w_gu.shape
    ff = ff2 // 2
    nf = ff // tf

    def tile(j, nv):
        return jnp.minimum(j, nv[0] - 1)

    def chunk(j, f, nv):
        return jnp.where(j < nv[0], f, nf - 1)

    grid_spec = pltpu.PrefetchScalarGridSpec(
        num_scalar_prefetch=3,
        grid=(p_rows // tm, nf),
        in_specs=[
            pl.BlockSpec((tm, dh), lambda j, f, te, nv, tr: (tile(j, nv), 0)),
            pl.BlockSpec((1, d, tf), lambda j, f, te, nv, tr: (te[tile(j, nv)], 0, chunk(j, f, nv))),
            pl.BlockSpec((1, d, tf), lambda j, f, te, nv, tr: (te[tile(j, nv)], 0, nf + chunk(j, f, nv))),
            pl.BlockSpec((1, tf, d), lambda j, f, te, nv, tr: (te[tile(j, nv)], chunk(j, f, nv), 0)),
        ],
        out_specs=pl.BlockSpec((tm, dh), lambda j, f, te, nv, tr: (tile(j, nv), 0)),
        scratch_shapes=[pltpu.VMEM((tm, d), BF16), pltpu.VMEM((tm, d), F32)],
    )
    return pl.pallas_call(
        _moe_kernel,
        grid_spec=grid_spec,
        out_shape=jax.ShapeDtypeStruct((p_rows, dh), jnp.uint32),
        compiler_params=_params(("arbitrary", "arbitrary")),
        name="moe_experts",
    )(tile_expert, n_valid, tile_rows, xs, w_gu, w_gu, w_down)


def _moe_combine_kernel(x_ref, y0_ref, y1_ref, rw_ref, mod_ref, *rest):
    o_ref = rest[-1]
    d = D_MODEL
    w1 = rw_ref[:, 0:1]
    w2 = rw_ref[:, 1:2]
    y = w1 * _unpack_bf16_pairs(y0_ref[...]) + w2 * _unpack_bf16_pairs(y1_ref[...])
    o_ref[...] = x_ref[...] + mod_ref[:, 5 * d:6 * d] * y


def _moe_combine(x, yg, rw, mod, prev, part, n_parts, tm=512):
    s, d = x.shape
    nb = s // tm // n_parts
    first = part * nb
    in_specs = [
        pl.BlockSpec((tm, d), lambda i: (first + i, 0)),
        pl.BlockSpec((tm, d // 2), lambda i: (i, 0)),
        pl.BlockSpec((tm, d // 2), lambda i: (nb + i, 0)),
        pl.BlockSpec((tm, LANES), lambda i: (first + i, 0)),
        _const_spec(mod.shape),
    ]
    args = [x, yg, yg, rw, mod]
    aliases = {}
    if prev is not None:
        in_specs.append(pl.BlockSpec(memory_space=pl.ANY))
        args.append(prev)
        aliases = {len(args) - 1: 0}
    return pl.pallas_call(
        _moe_combine_kernel,
        grid=(nb,),
        in_specs=in_specs,
        out_specs=pl.BlockSpec((tm, d), lambda i: (first + i, 0)),
        out_shape=jax.ShapeDtypeStruct((s, d), F32),
        input_output_aliases=aliases,
        compiler_params=_params(("arbitrary",)),
        name="moe_combine",
    )(*args)


def _moe_sparse(x, h_packed, rw, ridx, counts, mod, w_gu, w_down, tm=MOE_TILE):
    s = x.shape[0]
    ne = w_gu.shape[0]
    n_tiles = (2 * s) // tm + ne
    cnt = counts[0, :ne].astype(jnp.int32)
    padded = ((cnt + tm - 1) // tm) * tm
    ends = jnp.cumsum(padded)
    offs = ends - padded
    experts = jnp.arange(ne, dtype=jnp.int32)
    off_of = lambda e: jnp.sum(jnp.where(e[:, None] == experts[None, :], offs[None, :], 0), axis=1)
    pos0 = off_of(ridx[0]) + ridx[2]
    pos1 = off_of(ridx[1]) + ridx[3]
    tile_start = jnp.arange(n_tiles, dtype=jnp.int32) * tm
    tile_expert = jnp.minimum(jnp.sum(tile_start[:, None] >= ends[None, :], axis=1), ne - 1).astype(jnp.int32)
    n_valid = (ends[-1] // tm).reshape(1).astype(jnp.int32)
    filled = jnp.clip((offs + cnt)[tile_expert] - tile_start, 0, tm)
    tile_rows = (((filled + MOE_ROW_STEP - 1) // MOE_ROW_STEP) * MOE_ROW_STEP).astype(jnp.int32)
    xs = _sc_scatter_rows(h_packed, pos0, pos1, n_tiles * tm)
    ys = _moe_grouped(xs, tile_expert, n_valid, tile_rows, w_gu, w_down, tm)
    out = None
    sp = s // COMBINE_PARTS
    for part in range(COMBINE_PARTS):
        rows = slice(part * sp, (part + 1) * sp)
        yg = _sc_gather_rows(ys, jnp.concatenate([pos0[rows], pos1[rows]]))
        out = _moe_combine(x, yg, rw, mod, out, part, COMBINE_PARTS)
    return out


def kernel(x, c, positions, norm_mix_w, norm_ffn_w, ada_w, ada_b, e_w_in, a_ln_w, a_ln_b, a_w_s, a_b_s, b_w_grp,
           b_scale, e_w_out, ffn_w_gu, ffn_w_down, o_w_in, c_conv_w, q_a_norm, w_uq, kv_norm, w_ukv, q_norm_w,
           k_norm_w, o_w_out, router_w, moe_w_gu, moe_w_down):
    bsz, s, d = x.shape
    assert bsz == 1 and d == D_MODEL
    depth = ada_w.shape[0]
    nh = MLA_HEADS
    xs = x.reshape(s, d)
    mod = _ada_mod(c, ada_w, ada_b)
    cos, sin, cos_t, sin_t = _rope_tables(positions)
    row = lambda a: a.reshape(1, -1)

    for layer in range(depth):
        i = layer // 2
        m = mod[layer]
        if layer % 2 == 0:
            has_next = layer + 1 < depth
            dummy = jnp.zeros((s // 512, 16, LANES), F32)
            xs, moe_w_down_bf16 = _even_mix(
                xs, m, row(norm_mix_w[layer]), e_w_in[i].astype(BF16), row(a_ln_w[i]), row(a_ln_b[i]), a_w_s[i],
                a_b_s[i].reshape(A_GROUPS, CHUNK, 1), b_w_grp[i].astype(BF16), row(b_scale[i]),
                e_w_out[i].astype(BF16), moe_w_down[i] if has_next else dummy)
            xs, moe_w_gu_bf16 = _ffn(xs, m, row(norm_ffn_w[layer]), ffn_w_gu[i].astype(BF16),
                                     ffn_w_down[i].astype(BF16), moe_w_gu[i] if has_next else dummy)
        else:
            o2 = 3 * C_WIDTH + Q_LORA + KV_LORA
            w_in = o_w_in[i][:, :o2].astype(BF16)
            w_kpe = jnp.pad(o_w_in[i][:, o2:], ((0, 0), (0, LANES - QK_ROPE))).astype(BF16)
            bound = (1.02 * QK_HD * QK_HD ** -0.5 * LOG2E) * jnp.max(jnp.abs(q_norm_w[i])) * jnp.max(jnp.abs(k_norm_w[i]))
            wkv = w_ukv[i].reshape(KV_LORA, nh, QK_NOPE + V_HD)
            w_k = wkv[:, :, :QK_NOPE].reshape(KV_LORA, nh * QK_NOPE).astype(BF16)
            w_v_t = wkv[:, :, QK_NOPE:].reshape(KV_LORA, nh * V_HD).T.astype(BF16)
            yc, qt, k, vt = _odd_prep(
                xs, m, row(norm_mix_w[layer]), w_in, w_kpe, c_conv_w[i], row(q_a_norm[i]), w_uq[i].T.astype(BF16),
                row(kv_norm[i]), w_k, w_v_t, q_norm_w[i].reshape(QK_HD, 1), row(k_norm_w[i]),
                jnp.full((1, LANES), -bound, F32), cos, sin, cos_t, sin_t)
            yd = lax.cond(bound <= ATTN_MAX_BOUND, _attention_shifted, _attention, qt, k, vt)
            wr = jnp.pad(router_w[i], ((0, 0), (0, LANES - N_EXPERTS)))
            wr_hi = lax.reduce_precision(wr, exponent_bits=8, mantissa_bits=7)
            w_router = jnp.concatenate([wr_hi, wr - wr_hi], axis=1).astype(BF16)
            xs, hp, rw, ridx, counts = _odd_out(xs, yc, yd, m, row(norm_ffn_w[layer]), o_w_out[i].astype(BF16),
                                                w_router)
            xs = _moe_sparse(xs, hp, rw, ridx, counts, m, moe_w_gu_bf16, moe_w_down_bf16)
    return xs.reshape(bsz, s, d)
```

```python
import functools

import jax
import jax.numpy as jnp
from jax import lax
from jax.experimental import pallas as pl
from jax.experimental.pallas import tpu as pltpu
from jax.experimental.pallas import tpu_sc as plsc

D_MODEL = 1024
SEQ = 16384
EPS = 1e-6
CHUNK = 128
A_WIDTH = 512
A_GROUPS = 4
B_WIDTH = 512
POOL_WINDOWS = (2, 4, 8, 16)
B_HD = 128
C_WIDTH = 512
MLA_HEADS = 4
Q_LORA = 256
KV_LORA = 256
QK_NOPE = 128
QK_ROPE = 64
QK_HD = QK_NOPE + QK_ROPE
V_HD = 128
ROPE_THETA = 10000.0
D_FF = 2816
N_EXPERTS = 8
D_FF_EXPERT = 3584

LANES = 128
SUBLANES = 8
POOL_HALO = 16
CONV_HALO = 8
QK_PAD = 256
V_PAD = 144
LOG2E = 1.4426950408889634
MASK_VALUE = -1e30
MOE_TILE = 512
CONV_COLS = 256
ATTN_MAX_BOUND = 48.0
MOE_ROW_STEP = 128
COMBINE_PARTS = 4
SC_ROWS = 64
VMEM_LIMIT = 56 * 1024 * 1024

F32 = jnp.float32
BF16 = jnp.bfloat16


def _params(sem, vmem=VMEM_LIMIT, flags=None):
    return pltpu.CompilerParams(dimension_semantics=sem, vmem_limit_bytes=vmem, flags=flags)


def _const_spec(shape, single=False):
    nd = len(shape)
    return pl.BlockSpec(shape, lambda *_: (0,) * nd, pipeline_mode=pl.Buffered(1) if single else None)


def _rms_mod(x, nw, sc, sh):
    ms = jnp.mean(x * x, axis=-1, keepdims=True)
    return (x * lax.rsqrt(ms + EPS)) * nw * (1.0 + sc) + sh


def _ada_kernel(c_ref, w_ref, b_ref, o_ref):
    c = c_ref[...]
    ca = c * jax.nn.sigmoid(c)
    o_ref[0] = jnp.sum(w_ref[0] * ca, axis=0, keepdims=True) + b_ref[0]


def _ada_mod(c, ada_w, ada_b):
    depth, d, n = ada_w.shape
    tn = 1536
    return pl.pallas_call(
        _ada_kernel,
        grid=(depth, n // tn),
        in_specs=[
            pl.BlockSpec((d, 1), lambda l, j: (0, 0)),
            pl.BlockSpec((1, d, tn), lambda l, j: (l, 0, j)),
            pl.BlockSpec((1, 1, tn), lambda l, j: (l, 0, j)),
        ],
        out_specs=pl.BlockSpec((1, 1, tn), lambda l, j: (l, 0, j)),
        out_shape=jax.ShapeDtypeStruct((depth, 1, n), F32),
        compiler_params=_params(("arbitrary", "arbitrary")),
        name="ada_mod",
    )(c.reshape(d, 1), ada_w, ada_b.reshape(depth, 1, n))


def _even_mix_kernel(x_ref, mod_ref, nw_ref, win_ref, lnw_ref, lnb_ref, ws_ref, bs_ref, wg_ref, bsc_ref,
                     wout_ref, *rest):
    n_sides = (len(rest) - 4) // 2
    o_ref = rest[n_sides]
    halo_ref, sv_ref, yb_ref = rest[-3:]
    for side_ref, side_out_ref in zip(rest[:n_sides], rest[n_sides + 1:2 * n_sides + 1]):
        side_out_ref[...] = side_ref[...].astype(BF16)
    tm = x_ref.shape[0]
    i = pl.program_id(0)

    @pl.when(i == 0)
    def _():
        halo_ref[...] = jnp.zeros_like(halo_ref)

    d = D_MODEL
    x = x_ref[...]
    sh, sc, gate = mod_ref[:, 0:d], mod_ref[:, d:2 * d], mod_ref[:, 2 * d:3 * d]
    h = _rms_mod(x, nw_ref[...], sc, sh).astype(BF16)
    p = jnp.dot(h, win_ref[...], preferred_element_type=F32)

    gl = jax.nn.gelu(p[:, :2 * A_WIDTH])
    u = gl[:, :A_WIDTH]
    v = gl[:, A_WIDTH:]
    mu = jnp.mean(v, axis=-1, keepdims=True)
    vc = v - mu
    var = jnp.mean(vc * vc, axis=-1, keepdims=True)
    vn = (vc * lax.rsqrt(var + EPS) * lnw_ref[...] + lnb_ref[...]).astype(BF16)
    row = lax.broadcasted_iota(jnp.int32, (CHUNK, CHUNK), 0)
    col = lax.broadcasted_iota(jnp.int32, (CHUNK, CHUNK), 1)
    for g in range(A_GROUPS):
        w = jnp.where(col <= row, ws_ref[g], 0.0).astype(BF16)
        b = bs_ref[g]
        for c in range(tm // CHUNK):
            blk = vn[c * CHUNK:(c + 1) * CHUNK, g * LANES:(g + 1) * LANES]
            sv_ref[c * CHUNK:(c + 1) * CHUNK, g * LANES:(g + 1) * LANES] = (
                jnp.dot(w, blk, preferred_element_type=F32) + b)
    ya = (u * sv_ref[...]).astype(BF16)

    pb = p[:, 2 * A_WIDTH:]
    ext = jnp.concatenate([halo_ref[...], pb], axis=0)
    halo_ref[...] = pb[tm - POOL_HALO:, :]
    t_glob = i * tm + lax.broadcasted_iota(jnp.int32, (tm, 1), 0)
    s = ext
    width = 1
    for g, win in enumerate(POOL_WINDOWS):
        while width < win:
            s = s + pltpu.roll(s, width, 0)
            width *= 2
        cnt = jnp.minimum(t_glob + 1, win).astype(F32)
        sl = slice(g * B_HD, (g + 1) * B_HD)
        pooled = s[POOL_HALO:, sl] / cnt
        dg = (pooled - pb[:, sl]).astype(BF16)
        yb_ref[:, sl] = jnp.dot(dg, wg_ref[g], preferred_element_type=F32)
    yb = (yb_ref[...] * bsc_ref[...]).astype(BF16)

    mix = (jnp.dot(ya, wout_ref[:A_WIDTH, :], preferred_element_type=F32)
           + jnp.dot(yb, wout_ref[A_WIDTH:, :], preferred_element_type=F32))
    o_ref[...] = x + gate * mix


def _side_cast_spec(side, steps, axis):
    ne = side.shape[0]
    parts = steps // ne
    assert parts * ne == steps and side.shape[axis] % parts == 0
    block = list(side.shape)
    block[0] = 1
    block[axis] //= parts
    assert block[1] % 16 == 0 and block[2] % LANES == 0
    if axis == 1:
        return pl.BlockSpec(tuple(block), lambda i: (i // parts, i % parts, 0))
    return pl.BlockSpec(tuple(block), lambda i: (i // parts, 0, i % parts))


def _row_cast_spec(w, steps):
    rows = -(-w.shape[0] // steps)
    rows = -(-rows // 16) * 16
    last = -(-w.shape[0] // rows) - 1
    return pl.BlockSpec((rows, w.shape[1]), lambda i: (jnp.minimum(i, last), 0))


def _even_mix(x, mod, nw, w_in, ln_w, ln_b, w_s, b_s, w_grp, b_scale, w_out, side, row_sides, tm=512):
    s, d = x.shape
    steps = s // tm
    sides = [side, *row_sides]
    side_specs = [_side_cast_spec(side, steps, 1)] + [_row_cast_spec(w, steps) for w in row_sides]
    outs = pl.pallas_call(
        _even_mix_kernel,
        grid=(steps,),
        in_specs=[
            pl.BlockSpec((tm, d), lambda i: (i, 0)),
            _const_spec(mod.shape), _const_spec(nw.shape), _const_spec(w_in.shape),
            _const_spec(ln_w.shape), _const_spec(ln_b.shape), _const_spec(w_s.shape), _const_spec(b_s.shape),
            _const_spec(w_grp.shape), _const_spec(b_scale.shape), _const_spec(w_out.shape), *side_specs,
        ],
        out_specs=[pl.BlockSpec((tm, d), lambda i: (i, 0)), *side_specs],
        out_shape=[jax.ShapeDtypeStruct((s, d), F32)] + [jax.ShapeDtypeStruct(w.shape, BF16) for w in sides],
        scratch_shapes=[pltpu.VMEM((POOL_HALO, B_WIDTH), F32), pltpu.VMEM((tm, A_WIDTH), F32),
                        pltpu.VMEM((tm, B_WIDTH), F32)],
        compiler_params=_params(("arbitrary",)),
        name="even_mix",
    )(x, mod, nw, w_in, ln_w, ln_b, w_s, b_s, w_grp, b_scale, w_out, *sides)
    return outs[0], outs[1], outs[2:]


def _ffn_kernel(x_ref, mod_ref, nw_ref, wgu_ref, wd_ref, side_ref, o_ref, side_out_ref, *, n_chunks):
    side_out_ref[...] = side_ref[...].astype(BF16)
    d = D_MODEL
    x = x_ref[...]
    sh, sc, gate = mod_ref[:, 3 * d:4 * d], mod_ref[:, 4 * d:5 * d], mod_ref[:, 5 * d:6 * d]
    h = _rms_mod(x, nw_ref[...], sc, sh).astype(BF16)
    ff = wd_ref.shape[0]
    tf = ff // n_chunks
    acc = jnp.zeros(x.shape, F32)
    for f in range(n_chunks):
        g = jnp.dot(h, wgu_ref[:, f * tf:(f + 1) * tf], preferred_element_type=F32)
        u = jnp.dot(h, wgu_ref[:, ff + f * tf:ff + (f + 1) * tf], preferred_element_type=F32)
        act = (g * jax.nn.sigmoid(g) * u).astype(BF16)
        acc = acc + jnp.dot(act, wd_ref[f * tf:(f + 1) * tf, :], preferred_element_type=F32)
    o_ref[...] = x + gate * acc


def _ffn(x, mod, nw, w_gu, w_down, side, tm=512, n_chunks=11):
    s, d = x.shape
    steps = s // tm
    side_spec = _side_cast_spec(side, steps, 2)
    return pl.pallas_call(
        functools.partial(_ffn_kernel, n_chunks=n_chunks),
        grid=(steps,),
        in_specs=[
            pl.BlockSpec((tm, d), lambda i: (i, 0)),
            _const_spec(mod.shape), _const_spec(nw.shape), _const_spec(w_gu.shape, single=True),
            _const_spec(w_down.shape, single=True), side_spec,
        ],
        out_specs=[pl.BlockSpec((tm, d), lambda i: (i, 0)), side_spec],
        out_shape=[jax.ShapeDtypeStruct((s, d), F32), jax.ShapeDtypeStruct(side.shape, BF16)],
        compiler_params=_params(("arbitrary",)),
        name="ffn",
    )(x, mod, nw, w_gu, w_down, side)


def _rope_table_kernel(pos_ref, invf_ref, cos_ref, sin_ref):
    ang = pos_ref[...].astype(F32) * invf_ref[...]
    cos_ref[...] = jnp.cos(ang)
    sin_ref[...] = jnp.sin(ang)


def _rope_tables(positions):
    s = positions.shape[-1]
    half = QK_ROPE // 2
    inv_freq = ROPE_THETA ** (-jnp.arange(0, QK_ROPE, 2, dtype=F32) / QK_ROPE)
    return pl.pallas_call(
        _rope_table_kernel,
        out_shape=(jax.ShapeDtypeStruct((half, s), F32), jax.ShapeDtypeStruct((half, s), F32)),
        name="rope_tables",
    )(positions.reshape(1, s), inv_freq.reshape(half, 1))


def _odd_prep_kernel(x_ref, mod_ref, nw_ref, win_ref, wkpe_ref, cw_ref, qan_ref, wuqt_ref, kvn_ref, wk_ref, wvt_ref,
                     qnw_ref, knw_ref, knwc_ref, shift_ref, cost_ref, sint_ref, yc_ref, qt_ref, k_ref, vt_ref, halo_ref):
    tm = x_ref.shape[0]
    i = pl.program_id(0)

    @pl.when(i == 0)
    def _():
        halo_ref[...] = jnp.zeros_like(halo_ref)

    d = D_MODEL
    x = x_ref[...]
    sh, sc = mod_ref[:, 0:d], mod_ref[:, d:2 * d]
    h = _rms_mod(x, nw_ref[...], sc, sh).astype(BF16)
    cw = C_WIDTH
    o0 = 3 * cw
    proj = lambda lo, hi: jnp.dot(h, win_ref[:, lo:hi], preferred_element_type=F32)
    kpe_pad = jnp.dot(h, wkpe_ref[...], preferred_element_type=F32)
    kpe = kpe_pad[:, :QK_ROPE]

    for c0 in range(0, cw, CONV_COLS):
        cols = slice(c0, c0 + CONV_COLS)
        z = proj(cw + c0, cw + c0 + CONV_COLS) * proj(2 * cw + c0, 2 * cw + c0 + CONV_COLS)
        ext = jnp.concatenate([halo_ref[:, cols], z], axis=0)
        halo_ref[:, cols] = z[tm - CONV_HALO:, :]
        z1 = pltpu.roll(ext, 1, 0)[CONV_HALO:, :]
        z2 = pltpu.roll(ext, 2, 0)[CONV_HALO:, :]
        conv = cw_ref[0:1, cols] * z2 + cw_ref[1:2, cols] * z1 + cw_ref[2:3, cols] * z
        yc_ref[:, cols] = (proj(c0, c0 + CONV_COLS) * conv).astype(BF16)

    cq = proj(o0, o0 + Q_LORA)
    ckv = proj(o0 + Q_LORA, o0 + Q_LORA + KV_LORA)
    cqn = cq * lax.rsqrt(jnp.mean(cq * cq, axis=-1, keepdims=True) + EPS) * qan_ref[...]
    ckvn = ckv * lax.rsqrt(jnp.mean(ckv * ckv, axis=-1, keepdims=True) + EPS) * kvn_ref[...]
    cqn_t = cqn.T.astype(BF16)
    ckvn_t = ckvn.T.astype(BF16)
    q_t = jnp.dot(wuqt_ref[...], cqn_t, preferred_element_type=F32)
    v_t = jnp.dot(wvt_ref[...], ckvn_t, preferred_element_type=F32)
    kn_all = jnp.dot(ckvn.astype(BF16), wk_ref[...], preferred_element_type=F32)

    nh = MLA_HEADS
    half = QK_ROPE // 2
    sm_scale = QK_HD ** -0.5
    cos_t, sin_t = cost_ref[...], sint_ref[...]
    kn_w = knw_ref[:, :QK_NOPE]
    kw_t = kpe_pad.T[:QK_ROPE, :] * knwc_ref[QK_NOPE:, :]
    k1, k2 = kw_t[:half, :], kw_t[half:, :]
    kr_t = jnp.concatenate([k1 * cos_t - k2 * sin_t, k2 * cos_t + k1 * sin_t,
                            jnp.zeros((LANES - QK_ROPE, tm), F32)], axis=0)
    kr = kr_t.T[:, :QK_ROPE]
    kpe_ss = jnp.sum(kpe * kpe, axis=-1, keepdims=True)
    npad = QK_PAD - QK_HD
    zrows = jnp.where(lax.broadcasted_iota(jnp.int32, (npad, tm), 0) == 0, shift_ref[:, 0:1], 0.0)
    zpad = jnp.where(lax.broadcasted_iota(jnp.int32, (tm, npad), 1) == 0, 1.0, 0.0)
    ones_rows = (lax.broadcasted_iota(jnp.int32, (V_PAD - V_HD, tm), 0) == 0).astype(BF16)
    for hd in range(nh):
        qh = q_t[hd * QK_HD:(hd + 1) * QK_HD, :]
        q_inv = lax.rsqrt(jnp.sum(qh * qh, axis=0, keepdims=True) / QK_HD + EPS) * (sm_scale * LOG2E)
        qw = qh * qnw_ref[...]
        x1, x2 = qw[QK_NOPE:QK_NOPE + half, :], qw[QK_NOPE + half:, :]
        qt_ref[hd] = jnp.concatenate(
            [qw[:QK_NOPE, :] * q_inv, (x1 * cos_t - x2 * sin_t) * q_inv, (x2 * cos_t + x1 * sin_t) * q_inv, zrows],
            axis=0).astype(BF16)
        kn = kn_all[:, hd * QK_NOPE:(hd + 1) * QK_NOPE]
        k_ss = jnp.sum(kn * kn, axis=-1, keepdims=True) + kpe_ss
        k_inv = lax.rsqrt(k_ss / QK_HD + EPS)
        k_ref[hd] = jnp.concatenate([kn * k_inv * kn_w, kr * k_inv, zpad], axis=1).astype(BF16)
        vt_ref[hd, :V_HD, :] = v_t[hd * V_HD:(hd + 1) * V_HD, :].astype(BF16)
        vt_ref[hd, V_HD:, :] = ones_rows


def _odd_prep(x, mod, nw, w_in, w_kpe, conv_w, q_a_norm, w_uq_t, kv_norm, w_k, w_v_t, q_norm_col, k_norm_w,
              k_norm_col, shift, cos_t, sin_t, tm=1024):
    s, d = x.shape
    nh = MLA_HEADS
    half = QK_ROPE // 2
    consts = [mod, nw, w_in, w_kpe, conv_w, q_a_norm, w_uq_t, kv_norm, w_k, w_v_t, q_norm_col, k_norm_w, k_norm_col,
              shift]
    return pl.pallas_call(
        _odd_prep_kernel,
        grid=(s // tm,),
        in_specs=[pl.BlockSpec((tm, d), lambda i: (i, 0))] + [_const_spec(a.shape) for a in consts] + [
            pl.BlockSpec((half, tm), lambda i: (0, i)), pl.BlockSpec((half, tm), lambda i: (0, i))],
        out_specs=[
            pl.BlockSpec((tm, C_WIDTH), lambda i: (i, 0)),
            pl.BlockSpec((nh, QK_PAD, tm), lambda i: (0, 0, i)),
            pl.BlockSpec((nh, tm, QK_PAD), lambda i: (0, i, 0)),
            pl.BlockSpec((nh, V_PAD, tm), lambda i: (0, 0, i)),
        ],
        out_shape=[
            jax.ShapeDtypeStruct((s, C_WIDTH), BF16),
            jax.ShapeDtypeStruct((nh, QK_PAD, s), BF16),
            jax.ShapeDtypeStruct((nh, s, QK_PAD), BF16),
            jax.ShapeDtypeStruct((nh, V_PAD, s), BF16),
        ],
        scratch_shapes=[pltpu.VMEM((CONV_HALO, C_WIDTH), F32)],
        compiler_params=_params(("arbitrary",)),
        name="odd_prep",
    )(x, *consts, cos_t, sin_t)


def _causal_mask(s, lead=0):
    key = lax.broadcasted_iota(jnp.int32, s.shape, 0)
    qry = lead + lax.broadcasted_iota(jnp.int32, s.shape, 1)
    return jnp.where(key <= qry, s, MASK_VALUE)


def _attn_shifted_kernel(qt_ref, k_ref, vt_ref, o_ref, acc_ref, *, tq, tk):
    qi = pl.program_id(1)
    acc_ref[...] = jnp.zeros_like(acc_ref)
    r = tq // tk

    def block(j, diag_piece=None):
        q0 = 0 if diag_piece is None else diag_piece * tk
        start = pl.multiple_of(j * tk, tk)
        s = jnp.dot(k_ref[0, pl.ds(start, tk), :], qt_ref[0, :, q0:], preferred_element_type=F32)
        if diag_piece is not None:
            s = _causal_mask(s)
        p = jnp.exp2(s).astype(BF16)
        acc_ref[:, q0:] += jnp.dot(vt_ref[0, :, pl.ds(start, tk)], p, preferred_element_type=F32)

    def body(t, carry):
        for b in range(r):
            block(r * t + b)
        return carry

    lax.fori_loop(0, qi, body, 0)
    for b in range(r):
        block(r * qi + b, diag_piece=b)
    acc = acc_ref[...]
    out_t = acc[:V_HD, :] / acc[V_HD:V_HD + 1, :]
    o_ref[...] = out_t.T.astype(o_ref.dtype)


def _attention_shifted(qt, k, vt, tq=2048, tk=512):
    nh, _, s = qt.shape
    return pl.pallas_call(
        functools.partial(_attn_shifted_kernel, tq=tq, tk=tk),
        grid=(nh, s // tq),
        in_specs=[
            pl.BlockSpec((1, QK_PAD, tq), lambda h, i: (h, 0, i)),
            pl.BlockSpec((1, s, QK_PAD), lambda h, i: (h, 0, 0)),
            pl.BlockSpec((1, V_PAD, s), lambda h, i: (h, 0, 0)),
        ],
        out_specs=pl.BlockSpec((tq, V_HD), lambda h, i: (i, h)),
        out_shape=jax.ShapeDtypeStruct((s, nh * V_HD), BF16),
        scratch_shapes=[pltpu.VMEM((V_PAD, tq), F32)],
        compiler_params=_params(("arbitrary", "arbitrary")),
        name="mla_attention_shifted",
    )(qt, k, vt)


def _attn_kernel(qt_ref, k_ref, vt_ref, o_ref, s0_ref, s1_ref, bm0_ref, bm1_ref, m_ref, acc_ref, *, tq, tk):
    qi = pl.program_id(1)
    m_ref[...] = jnp.full_like(m_ref, MASK_VALUE)
    acc_ref[...] = jnp.zeros_like(acc_ref)

    def scores(j, s_ref, bm_ref, diag_offset=None):
        q0 = 0 if diag_offset is None else diag_offset
        start = pl.multiple_of(j * tk, tk)
        s = jnp.dot(k_ref[0, pl.ds(start, tk), :], qt_ref[0, :, q0:], preferred_element_type=F32)
        if diag_offset is not None:
            s = _causal_mask(s)
        s_ref[:, q0:] = s
        bm_ref[:, q0:] = jnp.max(s, axis=0, keepdims=True)

    def consume(j, s_ref, bm_ref, q0=0):
        m_prev = m_ref[:, q0:]
        m_new = jnp.maximum(m_prev, bm_ref[:, q0:])
        alpha = jnp.exp2(m_prev - m_new)
        p = jnp.exp2((s_ref[:, q0:] - m_new).astype(BF16))
        start = pl.multiple_of(j * tk, tk)
        vt = vt_ref[0, :, pl.ds(start, tk)]
        acc_ref[:, q0:] = alpha * acc_ref[:, q0:] + jnp.dot(vt, p, preferred_element_type=F32)
        m_ref[:, q0:] = m_new

    r = tq // tk
    slots = ((s0_ref, bm0_ref), (s1_ref, bm1_ref))

    def trip(j0, then_diagonal):
        for b in range(r):
            first_diag = then_diagonal and b == r - 1
            scores(j0 + b + 1, *slots[(b + 1) % 2], diag_offset=0 if first_diag else None)
            consume(j0 + b, *slots[b % 2])

    def diagonal(jd):
        for b in range(r):
            if b + 1 < r:
                scores(jd + b + 1, *slots[(b + 1) % 2], diag_offset=(b + 1) * tk)
            consume(jd + b, *slots[b % 2], q0=b * tk)

    @pl.when(qi > 0)
    def _():
        scores(0, *slots[0])

    n_plain = jnp.maximum(qi - 1, 0)

    def body(t, carry):
        trip(r * (2 * t), False)
        trip(r * (2 * t + 1), False)
        return carry

    lax.fori_loop(0, n_plain // 2, body, 0)

    @pl.when(n_plain % 2 == 1)
    def _():
        trip(r * (n_plain - 1), False)

    @pl.when(qi > 0)
    def _():
        trip(r * (qi - 1), True)
        diagonal(r * qi)

    @pl.when(qi == 0)
    def _():
        scores(0, *slots[0], diag_offset=0)
        diagonal(0)

    acc = acc_ref[...]
    out_t = acc[:V_HD, :] / acc[V_HD:V_HD + 1, :]
    o_ref[...] = out_t.T.astype(o_ref.dtype)


def _attention(qt, k, vt, tq=1024, tk=512):
    nh, _, s = qt.shape
    assert tq % (2 * tk) == 0
    return pl.pallas_call(
        functools.partial(_attn_kernel, tq=tq, tk=tk),
        grid=(nh, s // tq),
        in_specs=[
            pl.BlockSpec((1, QK_PAD, tq), lambda h, i: (h, 0, i)),
            pl.BlockSpec((1, s, QK_PAD), lambda h, i: (h, 0, 0)),
            pl.BlockSpec((1, V_PAD, s), lambda h, i: (h, 0, 0)),
        ],
        out_specs=pl.BlockSpec((tq, V_HD), lambda h, i: (i, h)),
        out_shape=jax.ShapeDtypeStruct((s, nh * V_HD), BF16),
        scratch_shapes=[pltpu.VMEM((tk, tq), F32), pltpu.VMEM((tk, tq), F32), pltpu.VMEM((1, tq), F32),
                        pltpu.VMEM((1, tq), F32), pltpu.VMEM((1, tq), F32), pltpu.VMEM((V_PAD, tq), F32)],
        compiler_params=_params(("arbitrary", "arbitrary")),
        name="mla_attention",
    )(qt, k, vt)


def _pack_bf16_pairs(x):
    n = x.shape[1] // 2
    lo = pltpu.bitcast(x[:, :n].astype(BF16).astype(F32), jnp.uint32)
    hi = pltpu.bitcast(x[:, n:].astype(BF16).astype(F32), jnp.uint32)
    return (lo >> 16) | hi


def _unpack_bf16_pairs(p):
    lo = pltpu.bitcast(p << 16, F32)
    hi = pltpu.bitcast(p & jnp.uint32(0xFFFF0000), F32)
    return jnp.concatenate([lo, hi], axis=1)


def _odd_out_kernel(x_ref, yc_ref, yd_ref, mod_ref, nw_ref, wout_ref, wr_ref, before_ref, x_out_ref, h_ref, rw_ref,
                    ridx_ref, cnt_ref):
    d = D_MODEL
    x = x_ref[...]
    gate_m = mod_ref[:, 2 * d:3 * d]
    sh, sc = mod_ref[:, 3 * d:4 * d], mod_ref[:, 4 * d:5 * d]
    mix = (jnp.dot(yc_ref[...], wout_ref[:C_WIDTH, :], preferred_element_type=F32)
           + jnp.dot(yd_ref[...], wout_ref[C_WIDTH:, :], preferred_element_type=F32))
    x1 = x + gate_m * mix
    x_out_ref[...] = x1
    h = _rms_mod(x1, nw_ref[...], sc, sh)
    h_ref[...] = _pack_bf16_pairs(h)

    h_hi = h.astype(BF16)
    h_lo = (h - h_hi.astype(F32)).astype(BF16)
    hw = jnp.dot(h_hi, wr_ref[...], preferred_element_type=F32)
    logits = hw[:, :LANES] + (hw[:, LANES:] + jnp.dot(h_lo, wr_ref[:, :LANES], preferred_element_type=F32))
    lt = logits.T[:N_EXPERTS, :]
    ex = lax.broadcasted_iota(jnp.int32, lt.shape, 0)
    m1 = jnp.max(lt, axis=0, keepdims=True)
    i1 = jnp.min(jnp.where(lt == m1, ex, N_EXPERTS), axis=0, keepdims=True)
    rest = jnp.where(ex == i1, -jnp.inf, lt)
    m2 = jnp.max(rest, axis=0, keepdims=True)
    i2 = jnp.min(jnp.where(rest == m2, ex, N_EXPERTS), axis=0, keepdims=True)
    e2 = jnp.exp(m2 - m1)
    w1 = 1.0 / (1.0 + e2)
    w2 = e2 / (1.0 + e2)
    row = lax.broadcasted_iota(jnp.int32, (LANES, lt.shape[1]), 0)
    rw_ref[...] = jnp.where(row == 0, w1, jnp.where(row == 1, w2, 0.0)).T

    i = pl.program_id(0)

    @pl.when(i == 0)
    def _():
        cnt_ref[...] = jnp.zeros_like(cnt_ref)

    onehot = jnp.where((ex == i1) | (ex == i2), 1.0, 0.0)
    prefix = jnp.dot(onehot.astype(BF16), before_ref[...], preferred_element_type=F32) + cnt_ref[...]
    r1 = jnp.sum(jnp.where(ex == i1, prefix, 0.0), axis=0, keepdims=True)
    r2 = jnp.sum(jnp.where(ex == i2, prefix, 0.0), axis=0, keepdims=True)
    cnt_ref[...] += jnp.sum(onehot, axis=1, keepdims=True)
    ridx_ref[...] = jnp.where(ex == 0, i1, jnp.where(ex == 1, i2, jnp.where(
        ex == 2, r1.astype(jnp.int32), jnp.where(ex == 3, r2.astype(jnp.int32), 0))))


def _odd_out(x, yc, yd, mod, nw, w_out, w_router, tm=1024):
    s, d = x.shape
    return pl.pallas_call(
        _odd_out_kernel,
        grid=(s // tm,),
        in_specs=[
            pl.BlockSpec((tm, d), lambda i: (i, 0)),
            pl.BlockSpec((tm, C_WIDTH), lambda i: (i, 0)),
            pl.BlockSpec((tm, MLA_HEADS * V_HD), lambda i: (i, 0)),
            _const_spec(mod.shape), _const_spec(nw.shape), _const_spec(w_out.shape), _const_spec(w_router.shape),
            _const_spec((tm, tm)),
        ],
        out_specs=[
            pl.BlockSpec((tm, d), lambda i: (i, 0)),
            pl.BlockSpec((tm, d // 2), lambda i: (i, 0)),
            pl.BlockSpec((tm, LANES), lambda i: (i, 0)),
            pl.BlockSpec((SUBLANES, tm), lambda i: (0, i)),
            pl.BlockSpec((N_EXPERTS, 1), lambda i: (0, 0)),
        ],
        out_shape=[
            jax.ShapeDtypeStruct((s, d), F32),
            jax.ShapeDtypeStruct((s, d // 2), jnp.uint32),
            jax.ShapeDtypeStruct((s, LANES), F32),
            jax.ShapeDtypeStruct((SUBLANES, s), jnp.int32),
            jax.ShapeDtypeStruct((N_EXPERTS, 1), F32),
        ],
        compiler_params=_params(("arbitrary",)),
        name="odd_out_router",
    )(x, yc, yd, mod, nw, w_out, w_router, jnp.triu(jnp.ones((tm, tm), BF16), 1))


def _sc_workers():
    info = plsc.get_sparse_core_info()
    return info.num_cores, info.num_cores * info.num_subcores


def _sc_scatter_rows(x, idx0, idx1, out_rows):
    n, w = x.shape
    nc, nw = _sc_workers()
    per_w = n // nw
    nch = per_w // SC_ROWS
    assert nch % 2 == 0 and nch >= 2
    mesh = plsc.VectorSubcoreMesh(core_axis_name="c", subcore_axis_name="s")

    @functools.partial(
        pl.kernel, mesh=mesh, out_type=jax.ShapeDtypeStruct((out_rows, w), x.dtype),
        scratch_types=[pltpu.VMEM((nch, SC_ROWS), jnp.int32), pltpu.VMEM((nch, SC_ROWS), jnp.int32),
                       pltpu.VMEM((SC_ROWS, w), x.dtype), pltpu.VMEM((SC_ROWS, w), x.dtype),
                       pltpu.SemaphoreType.DMA, pltpu.SemaphoreType.DMA, pltpu.SemaphoreType.DMA,
                       pltpu.SemaphoreType.DMA],
        name="moe_dispatch")
    def scatter(x_hbm, i0_hbm, i1_hbm, out_hbm, i0_v, i1_v, buf0, buf1, lsem0, lsem1, ssem0, ssem1):
        wid = lax.axis_index("s") * nc + lax.axis_index("c")
        pltpu.sync_copy(i0_hbm.at[wid], i0_v)
        pltpu.sync_copy(i1_hbm.at[wid], i1_v)
        base = wid * per_w

        def load(c, buf, sem):
            return pltpu.make_async_copy(x_hbm.at[pl.ds(base + c * SC_ROWS, SC_ROWS)], buf, sem)

        def put(c, buf):
            first = pltpu.make_async_copy(buf, out_hbm.at[i0_v.at[c]], ssem0)
            second = pltpu.make_async_copy(buf, out_hbm.at[i1_v.at[c]], ssem1)
            first.start()
            second.start()
            first.wait()
            second.wait()

        def pair(c, prefetch_next):
            load(c + 1, buf1, lsem1).start()
            load(c, buf0, lsem0).wait()
            put(c, buf0)
            if prefetch_next:
                load(c + 2, buf0, lsem0).start()
            load(c + 1, buf1, lsem1).wait()
            put(c + 1, buf1)

        load(0, buf0, lsem0).start()

        def body(t, carry):
            pair(2 * t, True)
            return carry

        lax.fori_loop(0, nch // 2 - 1, body, 0)
        pair(nch - 2, False)

    return scatter(x, idx0.reshape(nw, nch, SC_ROWS), idx1.reshape(nw, nch, SC_ROWS))


def _sc_gather_rows(table, idx):
    _, w = table.shape
    b = idx.shape[0]
    nc, nw = _sc_workers()
    per_w = b // nw
    nch = per_w // SC_ROWS
    mesh = plsc.VectorSubcoreMesh(core_axis_name="c", subcore_axis_name="s")

    assert nch % 2 == 0 and nch >= 2

    @functools.partial(
        pl.kernel, mesh=mesh, out_type=jax.ShapeDtypeStruct((b, w), table.dtype),
        scratch_types=[pltpu.VMEM((nch, SC_ROWS), jnp.int32), pltpu.VMEM((SC_ROWS, w), table.dtype),
                       pltpu.VMEM((SC_ROWS, w), table.dtype), pltpu.SemaphoreType.DMA, pltpu.SemaphoreType.DMA],
        name="moe_combine_gather")
    def gather(table_hbm, idx_hbm, out_hbm, idx_v, buf0, buf1, sem0, sem1):
        wid = lax.axis_index("s") * nc + lax.axis_index("c")
        pltpu.sync_copy(idx_hbm.at[wid], idx_v)
        base = wid * per_w

        def fetch(c, buf, sem):
            return pltpu.make_async_copy(table_hbm.at[idx_v.at[c]], buf, sem)

        def put(c, buf):
            pltpu.sync_copy(buf, out_hbm.at[pl.ds(base + c * SC_ROWS, SC_ROWS)])

        def pair(c, prefetch_next):
            fetch(c + 1, buf1, sem1).start()
            fetch(c, buf0, sem0).wait()
            put(c, buf0)
            if prefetch_next:
                fetch(c + 2, buf0, sem0).start()
            fetch(c + 1, buf1, sem1).wait()
            put(c + 1, buf1)

        fetch(0, buf0, sem0).start()

        def body(t, carry):
            pair(2 * t, True)
            return carry

        lax.fori_loop(0, nch // 2 - 1, body, 0)
        pair(nch - 2, False)

    return gather(table, idx.reshape(nw, nch, SC_ROWS))


def _moe_kernel(te_ref, nv_ref, rows_ref, xs_ref, wg_ref, wu_ref, wd_ref, ys_ref, x_scr, acc_ref):
    j = pl.program_id(0)
    f = pl.program_id(1)
    nf = pl.num_programs(1)
    tm = xs_ref.shape[0]

    def expert_ffn(rows):
        h = x_scr[:rows, :]
        g = jnp.dot(h, wg_ref[0], preferred_element_type=F32)
        u = jnp.dot(h, wu_ref[0], preferred_element_type=F32)
        act = (g * jax.nn.sigmoid(g) * u).astype(BF16)
        y = jnp.dot(act, wd_ref[0], preferred_element_type=F32)

        @pl.when(f == 0)
        def _():
            acc_ref[:rows, :] = y

        @pl.when(f > 0)
        def _():
            acc_ref[:rows, :] += y

        @pl.when(f == nf - 1)
        def _():
            ys_ref[:rows, :] = _pack_bf16_pairs(acc_ref[:rows, :])

    @pl.when(j < nv_ref[0])
    def _():
        @pl.when(f == 0)
        def _():
            x_scr[...] = _unpack_bf16_pairs(xs_ref[...]).astype(BF16)

        for rows in range(MOE_ROW_STEP, tm + 1, MOE_ROW_STEP):
            pl.when(rows_ref[j] == rows)(functools.partial(expert_ffn, rows))


def _moe_grouped(xs, tile_expert, n_valid, tile_rows, w_gu, w_down, tm, tf=1792):
    p_rows, dh = xs.shape
    d = 2 * dh
    ne, _, ff2 = w_gu.shape
    ff = ff2 // 2
    nf = ff // tf

    def tile(j, nv):
        return jnp.minimum(j, nv[0] - 1)

    def chunk(j, f, nv):
        return jnp.where(j < nv[0], f, nf - 1)

    grid_spec = pltpu.PrefetchScalarGridSpec(
        num_scalar_prefetch=3,
        grid=(p_rows // tm, nf),
        in_specs=[
            pl.BlockSpec((tm, dh), lambda j, f, te, nv, tr: (tile(j, nv), 0)),
            pl.BlockSpec((1, d, tf), lambda j, f, te, nv, tr: (te[tile(j, nv)], 0, chunk(j, f, nv))),
            pl.BlockSpec((1, d, tf), lambda j, f, te, nv, tr: (te[tile(j, nv)], 0, nf + chunk(j, f, nv))),
            pl.BlockSpec((1, tf, d), lambda j, f, te, nv, tr: (te[tile(j, nv)], chunk(j, f, nv), 0)),
        ],
        out_specs=pl.BlockSpec((tm, dh), lambda j, f, te, nv, tr: (tile(j, nv), 0)),
        scratch_shapes=[pltpu.VMEM((tm, d), BF16), pltpu.VMEM((tm, d), F32)],
    )
    return pl.pallas_call(
        _moe_kernel,
        grid_spec=grid_spec,
        out_shape=jax.ShapeDtypeStruct((p_rows, dh), jnp.uint32),
        compiler_params=_params(("arbitrary", "arbitrary")),
        name="moe_experts",
    )(tile_expert, n_valid, tile_rows, xs, w_gu, w_gu, w_down)


def _moe_combine_kernel(x_ref, y0_ref, y1_ref, rw_ref, mod_ref, *rest):
    o_ref = rest[-1]
    d = D_MODEL
    w1 = rw_ref[:, 0:1]
    w2 = rw_ref[:, 1:2]
    y = w1 * _unpack_bf16_pairs(y0_ref[...]) + w2 * _unpack_bf16_pairs(y1_ref[...])
    o_ref[...] = x_ref[...] + mod_ref[:, 5 * d:6 * d] * y


def _moe_combine(x, yg, rw, mod, prev, part, n_parts, tm=512):
    s, d = x.shape
    nb = s // tm // n_parts
    first = part * nb
    in_specs = [
        pl.BlockSpec((tm, d), lambda i: (first + i, 0)),
        pl.BlockSpec((tm, d // 2), lambda i: (i, 0)),
        pl.BlockSpec((tm, d // 2), lambda i: (nb + i, 0)),
        pl.BlockSpec((tm, LANES), lambda i: (first + i, 0)),
        _const_spec(mod.shape),
    ]
    args = [x, yg, yg, rw, mod]
    aliases = {}
    if prev is not None:
        in_specs.append(pl.BlockSpec(memory_space=pl.ANY))
        args.append(prev)
        aliases = {len(args) - 1: 0}
    return pl.pallas_call(
        _moe_combine_kernel,
        grid=(nb,),
        in_specs=in_specs,
        out_specs=pl.BlockSpec((tm, d), lambda i: (first + i, 0)),
        out_shape=jax.ShapeDtypeStruct((s, d), F32),
        input_output_aliases=aliases,
        compiler_params=_params(("arbitrary",)),
        name="moe_combine",
    )(*args)


def _moe_sparse(x, h_packed, rw, ridx, counts, mod, w_gu, w_down, tm=MOE_TILE):
    s = x.shape[0]
    ne = w_gu.shape[0]
    n_tiles = (2 * s) // tm + ne
    cnt = counts[:ne, 0].astype(jnp.int32)
    padded = ((cnt + tm - 1) // tm) * tm
    ends = jnp.cumsum(padded)
    offs = ends - padded
    experts = jnp.arange(ne, dtype=jnp.int32)
    off_of = lambda e: jnp.sum(jnp.where(e[:, None] == experts[None, :], offs[None, :], 0), axis=1)
    pos0 = off_of(ridx[0]) + ridx[2]
    pos1 = off_of(ridx[1]) + ridx[3]
    tile_start = jnp.arange(n_tiles, dtype=jnp.int32) * tm
    tile_expert = jnp.minimum(jnp.sum(tile_start[:, None] >= ends[None, :], axis=1), ne - 1).astype(jnp.int32)
    n_valid = (ends[-1] // tm).reshape(1).astype(jnp.int32)
    filled = jnp.clip((offs + cnt)[tile_expert] - tile_start, 0, tm)
    tile_rows = (((filled + MOE_ROW_STEP - 1) // MOE_ROW_STEP) * MOE_ROW_STEP).astype(jnp.int32)
    xs = _sc_scatter_rows(h_packed, pos0, pos1, n_tiles * tm)
    ys = _moe_grouped(xs, tile_expert, n_valid, tile_rows, w_gu, w_down, tm)
    out = None
    sp = s // COMBINE_PARTS
    for part in range(COMBINE_PARTS):
        rows = slice(part * sp, (part + 1) * sp)
        yg = _sc_gather_rows(ys, jnp.concatenate([pos0[rows], pos1[rows]]))
        out = _moe_combine(x, yg, rw, mod, out, part, COMBINE_PARTS)
    return out


def kernel(x, c, positions, norm_mix_w, norm_ffn_w, ada_w, ada_b, e_w_in, a_ln_w, a_ln_b, a_w_s, a_b_s, b_w_grp,
           b_scale, e_w_out, ffn_w_gu, ffn_w_down, o_w_in, c_conv_w, q_a_norm, w_uq, kv_norm, w_ukv, q_norm_w,
           k_norm_w, o_w_out, router_w, moe_w_gu, moe_w_down):
    bsz, s, d = x.shape
    assert bsz == 1 and d == D_MODEL
    depth = ada_w.shape[0]
    nh = MLA_HEADS
    xs = x.reshape(s, d)
    mod = _ada_mod(c, ada_w, ada_b)
    cos_t, sin_t = _rope_tables(positions)
    row = lambda a: a.reshape(1, -1)

    for layer in range(depth):
        i = layer // 2
        m = mod[layer]
        if layer % 2 == 0:
            has_next = layer + 1 < depth
            dummy = jnp.zeros((s // 512, 16, LANES), F32)
            xs, moe_w_down_bf16, (ffn_w_gu_bf16, ffn_w_down_bf16) = _even_mix(
                xs, m, row(norm_mix_w[layer]), e_w_in[i].astype(BF16), row(a_ln_w[i]), row(a_ln_b[i]), a_w_s[i],
                a_b_s[i].reshape(A_GROUPS, CHUNK, 1), b_w_grp[i].astype(BF16), row(b_scale[i]),
                e_w_out[i].astype(BF16), moe_w_down[i] if has_next else dummy, [ffn_w_gu[i], ffn_w_down[i]])
            xs, moe_w_gu_bf16 = _ffn(xs, m, row(norm_ffn_w[layer]), ffn_w_gu_bf16, ffn_w_down_bf16,
                                     moe_w_gu[i] if has_next else dummy)
        else:
            o2 = 3 * C_WIDTH + Q_LORA + KV_LORA
            w_in = o_w_in[i][:, :o2].astype(BF16)
            w_kpe = jnp.pad(o_w_in[i][:, o2:], ((0, 0), (0, LANES - QK_ROPE))).astype(BF16)
            bound = (1.02 * QK_HD * QK_HD ** -0.5 * LOG2E) * jnp.max(jnp.abs(q_norm_w[i])) * jnp.max(jnp.abs(k_norm_w[i]))
            wkv = w_ukv[i].reshape(KV_LORA, nh, QK_NOPE + V_HD)
            w_k = wkv[:, :, :QK_NOPE].reshape(KV_LORA, nh * QK_NOPE).astype(BF16)
            w_v_t = wkv[:, :, QK_NOPE:].reshape(KV_LORA, nh * V_HD).T.astype(BF16)
            yc, qt, k, vt = _odd_prep(
                xs, m, row(norm_mix_w[layer]), w_in, w_kpe, c_conv_w[i], row(q_a_norm[i]), w_uq[i].T.astype(BF16),
                row(kv_norm[i]), w_k, w_v_t, q_norm_w[i].reshape(QK_HD, 1), row(k_norm_w[i]),
                k_norm_w[i].reshape(QK_HD, 1), jnp.full((1, LANES), -bound, F32), cos_t, sin_t)
            yd = lax.cond(bound <= ATTN_MAX_BOUND, _attention_shifted, _attention, qt, k, vt)
            wr = jnp.pad(router_w[i], ((0, 0), (0, LANES - N_EXPERTS)))
            wr_hi = lax.reduce_precision(wr, exponent_bits=8, mantissa_bits=7)
            w_router = jnp.concatenate([wr_hi, wr - wr_hi], axis=1).astype(BF16)
            xs, hp, rw, ridx, counts = _odd_out(xs, yc, yd, m, row(norm_ffn_w[layer]), o_w_out[i].astype(BF16),
                                                w_router)
            xs = _moe_sparse(xs, hp, rw, ridx, counts, m, moe_w_gu_bf16, moe_w_down_bf16)
    return xs.reshape(bsz, s, d)
```

```python
import functools

import jax
import jax.numpy as jnp
from jax import lax
from jax.experimental import pallas as pl
from jax.experimental.pallas import tpu as pltpu
from jax.experimental.pallas import tpu_sc as plsc

D_MODEL = 1024
SEQ = 16384
EPS = 1e-6
CHUNK = 128
A_WIDTH = 512
A_GROUPS = 4
B_WIDTH = 512
POOL_WINDOWS = (2, 4, 8, 16)
B_HD = 128
C_WIDTH = 512
MLA_HEADS = 4
Q_LORA = 256
KV_LORA = 256
QK_NOPE = 128
QK_ROPE = 64
QK_HD = QK_NOPE + QK_ROPE
V_HD = 128
ROPE_THETA = 10000.0
D_FF = 2816
N_EXPERTS = 8
D_FF_EXPERT = 3584

LANES = 128
SUBLANES = 8
POOL_HALO = 16
CONV_HALO = 8
QK_PAD = 256
V_PAD = 144
LOG2E = 1.4426950408889634
MASK_VALUE = -1e30
MOE_TILE = 512
CONV_COLS = 256
ATTN_MAX_BOUND = 48.0
MOE_ROW_STEP = 128
COMBINE_PARTS = 4
SC_ROWS = 64
VMEM_LIMIT = 56 * 1024 * 1024

F32 = jnp.float32
BF16 = jnp.bfloat16


def _params(sem, vmem=VMEM_LIMIT, flags=None):
    return pltpu.CompilerParams(dimension_semantics=sem, vmem_limit_bytes=vmem, flags=flags)


def _const_spec(shape, single=False):
    nd = len(shape)
    return pl.BlockSpec(shape, lambda *_: (0,) * nd, pipeline_mode=pl.Buffered(1) if single else None)


def _rms_mod(x, nw, sc, sh):
    ms = jnp.mean(x * x, axis=-1, keepdims=True)
    return (x * lax.rsqrt(ms + EPS)) * nw * (1.0 + sc) + sh


def _ada_kernel(c_ref, w_ref, b_ref, o_ref):
    c = c_ref[...]
    ca = c * jax.nn.sigmoid(c)
    o_ref[0] = jnp.sum(w_ref[0] * ca, axis=0, keepdims=True) + b_ref[0]


def _ada_mod(c, ada_w, ada_b):
    depth, d, n = ada_w.shape
    tn = 3072
    return pl.pallas_call(
        _ada_kernel,
        grid=(depth, n // tn),
        in_specs=[
            pl.BlockSpec((d, 1), lambda l, j: (0, 0)),
            pl.BlockSpec((1, d, tn), lambda l, j: (l, 0, j)),
            pl.BlockSpec((1, 1, tn), lambda l, j: (l, 0, j)),
        ],
        out_specs=pl.BlockSpec((1, 1, tn), lambda l, j: (l, 0, j)),
        out_shape=jax.ShapeDtypeStruct((depth, 1, n), F32),
        compiler_params=_params(("arbitrary", "arbitrary")),
        name="ada_mod",
    )(c.reshape(d, 1), ada_w, ada_b.reshape(depth, 1, n))


def _even_mix_kernel(x_ref, mod_ref, nw_ref, win_ref, lnw_ref, lnb_ref, ws_ref, bs_ref, wg_ref, bsc_ref,
                     wout_ref, *rest):
    n_sides = (len(rest) - 4) // 2
    o_ref = rest[n_sides]
    halo_ref, sv_ref, yb_ref = rest[-3:]
    for side_ref, side_out_ref in zip(rest[:n_sides], rest[n_sides + 1:2 * n_sides + 1]):
        side_out_ref[...] = side_ref[...].astype(BF16)
    tm = x_ref.shape[0]
    i = pl.program_id(0)

    @pl.when(i == 0)
    def _():
        halo_ref[...] = jnp.zeros_like(halo_ref)

    d = D_MODEL
    x = x_ref[...]
    sh, sc, gate = mod_ref[:, 0:d], mod_ref[:, d:2 * d], mod_ref[:, 2 * d:3 * d]
    h = _rms_mod(x, nw_ref[...], sc, sh).astype(BF16)
    p = jnp.dot(h, win_ref[...], preferred_element_type=F32)

    gl = jax.nn.gelu(p[:, :2 * A_WIDTH])
    u = gl[:, :A_WIDTH]
    v = gl[:, A_WIDTH:]
    mu = jnp.mean(v, axis=-1, keepdims=True)
    vc = v - mu
    var = jnp.mean(vc * vc, axis=-1, keepdims=True)
    vn = (vc * lax.rsqrt(var + EPS) * lnw_ref[...] + lnb_ref[...]).astype(BF16)
    row = lax.broadcasted_iota(jnp.int32, (CHUNK, CHUNK), 0)
    col = lax.broadcasted_iota(jnp.int32, (CHUNK, CHUNK), 1)
    for g in range(A_GROUPS):
        w = jnp.where(col <= row, ws_ref[g], 0.0).astype(BF16)
        b = bs_ref[g]
        for c in range(tm // CHUNK):
            blk = vn[c * CHUNK:(c + 1) * CHUNK, g * LANES:(g + 1) * LANES]
            sv_ref[c * CHUNK:(c + 1) * CHUNK, g * LANES:(g + 1) * LANES] = (
                jnp.dot(w, blk, preferred_element_type=F32) + b)
    ya = (u * sv_ref[...]).astype(BF16)

    pb = p[:, 2 * A_WIDTH:]
    ext = jnp.concatenate([halo_ref[...], pb], axis=0)
    halo_ref[...] = pb[tm - POOL_HALO:, :]
    t_glob = i * tm + lax.broadcasted_iota(jnp.int32, (tm, 1), 0)
    s = ext
    width = 1
    for g, win in enumerate(POOL_WINDOWS):
        while width < win:
            s = s + pltpu.roll(s, width, 0)
            width *= 2
        cnt = jnp.minimum(t_glob + 1, win).astype(F32)
        sl = slice(g * B_HD, (g + 1) * B_HD)
        pooled = s[POOL_HALO:, sl] / cnt
        dg = (pooled - pb[:, sl]).astype(BF16)
        yb_ref[:, sl] = jnp.dot(dg, wg_ref[g], preferred_element_type=F32)
    yb = (yb_ref[...] * bsc_ref[...]).astype(BF16)

    mix = (jnp.dot(ya, wout_ref[:A_WIDTH, :], preferred_element_type=F32)
           + jnp.dot(yb, wout_ref[A_WIDTH:, :], preferred_element_type=F32))
    o_ref[...] = x + gate * mix


def _side_cast_spec(side, steps, axis):
    ne = side.shape[0]
    parts = steps // ne
    assert parts * ne == steps and side.shape[axis] % parts == 0
    block = list(side.shape)
    block[0] = 1
    block[axis] //= parts
    assert block[1] % 16 == 0 and block[2] % LANES == 0
    if axis == 1:
        return pl.BlockSpec(tuple(block), lambda i: (i // parts, i % parts, 0))
    return pl.BlockSpec(tuple(block), lambda i: (i // parts, 0, i % parts))


def _row_cast_spec(w, steps):
    rows = -(-w.shape[0] // steps)
    rows = -(-rows // 16) * 16
    last = -(-w.shape[0] // rows) - 1
    return pl.BlockSpec((rows, w.shape[1]), lambda i: (jnp.minimum(i, last), 0))


def _even_mix(x, mod, nw, w_in, ln_w, ln_b, w_s, b_s, w_grp, b_scale, w_out, side, row_sides, tm=512):
    s, d = x.shape
    steps = s // tm
    sides = [side, *row_sides]
    side_specs = [_side_cast_spec(side, steps, 1)] + [_row_cast_spec(w, steps) for w in row_sides]
    outs = pl.pallas_call(
        _even_mix_kernel,
        grid=(steps,),
        in_specs=[
            pl.BlockSpec((tm, d), lambda i: (i, 0)),
            _const_spec(mod.shape), _const_spec(nw.shape), _const_spec(w_in.shape),
            _const_spec(ln_w.shape), _const_spec(ln_b.shape), _const_spec(w_s.shape), _const_spec(b_s.shape),
            _const_spec(w_grp.shape), _const_spec(b_scale.shape), _const_spec(w_out.shape), *side_specs,
        ],
        out_specs=[pl.BlockSpec((tm, d), lambda i: (i, 0)), *side_specs],
        out_shape=[jax.ShapeDtypeStruct((s, d), F32)] + [jax.ShapeDtypeStruct(w.shape, BF16) for w in sides],
        scratch_shapes=[pltpu.VMEM((POOL_HALO, B_WIDTH), F32), pltpu.VMEM((tm, A_WIDTH), F32),
                        pltpu.VMEM((tm, B_WIDTH), F32)],
        compiler_params=_params(("arbitrary",)),
        name="even_mix",
    )(x, mod, nw, w_in, ln_w, ln_b, w_s, b_s, w_grp, b_scale, w_out, *sides)
    return outs[0], outs[1], outs[2:]


def _ffn_kernel(x_ref, mod_ref, nw_ref, wgu_ref, wd_ref, side_ref, o_ref, side_out_ref, *, n_chunks):
    side_out_ref[...] = side_ref[...].astype(BF16)
    d = D_MODEL
    x = x_ref[...]
    sh, sc, gate = mod_ref[:, 3 * d:4 * d], mod_ref[:, 4 * d:5 * d], mod_ref[:, 5 * d:6 * d]
    h = _rms_mod(x, nw_ref[...], sc, sh).astype(BF16)
    ff = wd_ref.shape[0]
    tf = ff // n_chunks
    acc = jnp.zeros(x.shape, F32)
    for f in range(n_chunks):
        g = jnp.dot(h, wgu_ref[:, f * tf:(f + 1) * tf], preferred_element_type=F32)
        u = jnp.dot(h, wgu_ref[:, ff + f * tf:ff + (f + 1) * tf], preferred_element_type=F32)
        act = (g * jax.nn.sigmoid(g) * u).astype(BF16)
        acc = acc + jnp.dot(act, wd_ref[f * tf:(f + 1) * tf, :], preferred_element_type=F32)
    o_ref[...] = x + gate * acc


def _ffn(x, mod, nw, w_gu, w_down, side, tm=512, n_chunks=11):
    s, d = x.shape
    steps = s // tm
    side_spec = _side_cast_spec(side, steps, 2)
    return pl.pallas_call(
        functools.partial(_ffn_kernel, n_chunks=n_chunks),
        grid=(steps,),
        in_specs=[
            pl.BlockSpec((tm, d), lambda i: (i, 0)),
            _const_spec(mod.shape), _const_spec(nw.shape), _const_spec(w_gu.shape, single=True),
            _const_spec(w_down.shape, single=True), side_spec,
        ],
        out_specs=[pl.BlockSpec((tm, d), lambda i: (i, 0)), side_spec],
        out_shape=[jax.ShapeDtypeStruct((s, d), F32), jax.ShapeDtypeStruct(side.shape, BF16)],
        compiler_params=_params(("arbitrary",)),
        name="ffn",
    )(x, mod, nw, w_gu, w_down, side)


def _rope_table_kernel(pos_ref, invf_ref, cos_ref, sin_ref):
    ang = pos_ref[...].astype(F32) * invf_ref[...]
    cos_ref[...] = jnp.cos(ang)
    sin_ref[...] = jnp.sin(ang)


def _rope_tables(positions):
    s = positions.shape[-1]
    half = QK_ROPE // 2
    inv_freq = ROPE_THETA ** (-jnp.arange(0, QK_ROPE, 2, dtype=F32) / QK_ROPE)
    return pl.pallas_call(
        _rope_table_kernel,
        out_shape=(jax.ShapeDtypeStruct((half, s), F32), jax.ShapeDtypeStruct((half, s), F32)),
        name="rope_tables",
    )(positions.reshape(1, s), inv_freq.reshape(half, 1))


def _odd_prep_kernel(x_ref, mod_ref, nw_ref, win_ref, wkpe_ref, cw_ref, qan_ref, wuqt_ref, kvn_ref, wk_ref, wvt_ref,
                     qnw_ref, knw_ref, knwc_ref, shift_ref, cost_ref, sint_ref, yc_ref, qt_ref, k_ref, vt_ref, halo_ref):
    tm = x_ref.shape[0]
    i = pl.program_id(0)

    @pl.when(i == 0)
    def _():
        halo_ref[...] = jnp.zeros_like(halo_ref)

    d = D_MODEL
    x = x_ref[...]
    sh, sc = mod_ref[:, 0:d], mod_ref[:, d:2 * d]
    h = _rms_mod(x, nw_ref[...], sc, sh).astype(BF16)
    cw = C_WIDTH
    o0 = 3 * cw
    proj = lambda lo, hi: jnp.dot(h, win_ref[:, lo:hi], preferred_element_type=F32)
    kpe_pad = jnp.dot(h, wkpe_ref[...], preferred_element_type=F32)
    kpe = kpe_pad[:, :QK_ROPE]

    for c0 in range(0, cw, CONV_COLS):
        cols = slice(c0, c0 + CONV_COLS)
        z = proj(cw + c0, cw + c0 + CONV_COLS) * proj(2 * cw + c0, 2 * cw + c0 + CONV_COLS)
        ext = jnp.concatenate([halo_ref[:, cols], z], axis=0)
        halo_ref[:, cols] = z[tm - CONV_HALO:, :]
        z1 = pltpu.roll(ext, 1, 0)[CONV_HALO:, :]
        z2 = pltpu.roll(ext, 2, 0)[CONV_HALO:, :]
        conv = cw_ref[0:1, cols] * z2 + cw_ref[1:2, cols] * z1 + cw_ref[2:3, cols] * z
        yc_ref[:, cols] = (proj(c0, c0 + CONV_COLS) * conv).astype(BF16)

    cq = proj(o0, o0 + Q_LORA)
    ckv = proj(o0 + Q_LORA, o0 + Q_LORA + KV_LORA)
    cqn = cq * lax.rsqrt(jnp.mean(cq * cq, axis=-1, keepdims=True) + EPS) * qan_ref[...]
    ckvn = ckv * lax.rsqrt(jnp.mean(ckv * ckv, axis=-1, keepdims=True) + EPS) * kvn_ref[...]
    cqn_t = cqn.T.astype(BF16)
    ckvn_t = ckvn.T.astype(BF16)
    q_t = jnp.dot(wuqt_ref[...], cqn_t, preferred_element_type=F32)
    v_t = jnp.dot(wvt_ref[...], ckvn_t, preferred_element_type=F32)
    kn_all = jnp.dot(ckvn.astype(BF16), wk_ref[...], preferred_element_type=F32)

    nh = MLA_HEADS
    half = QK_ROPE // 2
    sm_scale = QK_HD ** -0.5
    cos_t, sin_t = cost_ref[...], sint_ref[...]
    kn_w = knw_ref[:, :QK_NOPE]
    kw_t = kpe_pad.T[:QK_ROPE, :] * knwc_ref[QK_NOPE:, :]
    k1, k2 = kw_t[:half, :], kw_t[half:, :]
    kr_t = jnp.concatenate([k1 * cos_t - k2 * sin_t, k2 * cos_t + k1 * sin_t,
                            jnp.zeros((LANES - QK_ROPE, tm), F32)], axis=0)
    kr = kr_t.T[:, :QK_ROPE]
    kpe_ss = jnp.sum(kpe * kpe, axis=-1, keepdims=True)
    npad = QK_PAD - QK_HD
    zrows = jnp.where(lax.broadcasted_iota(jnp.int32, (npad, tm), 0) == 0, shift_ref[:, 0:1], 0.0)
    zpad = jnp.where(lax.broadcasted_iota(jnp.int32, (tm, npad), 1) == 0, 1.0, 0.0)
    ones_rows = (lax.broadcasted_iota(jnp.int32, (V_PAD - V_HD, tm), 0) == 0).astype(BF16)
    for hd in range(nh):
        qh = q_t[hd * QK_HD:(hd + 1) * QK_HD, :]
        q_inv = lax.rsqrt(jnp.sum(qh * qh, axis=0, keepdims=True) / QK_HD + EPS) * (sm_scale * LOG2E)
        qw = qh * qnw_ref[...]
        x1, x2 = qw[QK_NOPE:QK_NOPE + half, :], qw[QK_NOPE + half:, :]
        qt_ref[hd] = jnp.concatenate(
            [qw[:QK_NOPE, :] * q_inv, (x1 * cos_t - x2 * sin_t) * q_inv, (x2 * cos_t + x1 * sin_t) * q_inv, zrows],
            axis=0).astype(BF16)
        kn = kn_all[:, hd * QK_NOPE:(hd + 1) * QK_NOPE]
        k_ss = jnp.sum(kn * kn, axis=-1, keepdims=True) + kpe_ss
        k_inv = lax.rsqrt(k_ss / QK_HD + EPS)
        k_ref[hd] = jnp.concatenate([kn * k_inv * kn_w, kr * k_inv, zpad], axis=1).astype(BF16)
        vt_ref[hd, :V_HD, :] = v_t[hd * V_HD:(hd + 1) * V_HD, :].astype(BF16)
        vt_ref[hd, V_HD:, :] = ones_rows


def _odd_prep(x, mod, nw, w_in, w_kpe, conv_w, q_a_norm, w_uq_t, kv_norm, w_k, w_v_t, q_norm_col, k_norm_w,
              k_norm_col, shift, cos_t, sin_t, tm=1024):
    s, d = x.shape
    nh = MLA_HEADS
    half = QK_ROPE // 2
    consts = [mod, nw, w_in, w_kpe, conv_w, q_a_norm, w_uq_t, kv_norm, w_k, w_v_t, q_norm_col, k_norm_w, k_norm_col,
              shift]
    return pl.pallas_call(
        _odd_prep_kernel,
        grid=(s // tm,),
        in_specs=[pl.BlockSpec((tm, d), lambda i: (i, 0))] + [_const_spec(a.shape) for a in consts] + [
            pl.BlockSpec((half, tm), lambda i: (0, i)), pl.BlockSpec((half, tm), lambda i: (0, i))],
        out_specs=[
            pl.BlockSpec((tm, C_WIDTH), lambda i: (i, 0)),
            pl.BlockSpec((nh, QK_PAD, tm), lambda i: (0, 0, i)),
            pl.BlockSpec((nh, tm, QK_PAD), lambda i: (0, i, 0)),
            pl.BlockSpec((nh, V_PAD, tm), lambda i: (0, 0, i)),
        ],
        out_shape=[
            jax.ShapeDtypeStruct((s, C_WIDTH), BF16),
            jax.ShapeDtypeStruct((nh, QK_PAD, s), BF16),
            jax.ShapeDtypeStruct((nh, s, QK_PAD), BF16),
            jax.ShapeDtypeStruct((nh, V_PAD, s), BF16),
        ],
        scratch_shapes=[pltpu.VMEM((CONV_HALO, C_WIDTH), F32)],
        compiler_params=_params(("arbitrary",)),
        name="odd_prep",
    )(x, *consts, cos_t, sin_t)


def _causal_mask(s, lead=0):
    key = lax.broadcasted_iota(jnp.int32, s.shape, 0)
    qry = lead + lax.broadcasted_iota(jnp.int32, s.shape, 1)
    return jnp.where(key <= qry, s, MASK_VALUE)


def _attn_shifted_kernel(qt_ref, k_ref, vt_ref, o_ref, acc_ref, l_ref, *, tq, tk):
    qi = pl.program_id(1)
    acc_ref[...] = jnp.zeros_like(acc_ref)
    l_ref[...] = jnp.zeros_like(l_ref)
    r = tq // tk

    def block(j, diag_piece=None):
        q0 = 0 if diag_piece is None else diag_piece * tk
        start = pl.multiple_of(j * tk, tk)
        s = jnp.dot(k_ref[0, pl.ds(start, tk), :], qt_ref[0, :, q0:], preferred_element_type=F32)
        if diag_piece is not None:
            s = _causal_mask(s)
        p = jnp.exp2(s)
        l_ref[:, q0:] += jnp.sum(p.reshape(tk // SUBLANES, SUBLANES, p.shape[1]), axis=0)
        acc_ref[:, q0:] += jnp.dot(vt_ref[0, :V_HD, pl.ds(start, tk)], p.astype(BF16), preferred_element_type=F32)

    def body(t, carry):
        for b in range(r):
            block(r * t + b)
        return carry

    lax.fori_loop(0, qi, body, 0)
    for b in range(r):
        block(r * qi + b, diag_piece=b)
    out_t = acc_ref[...] / jnp.sum(l_ref[...], axis=0, keepdims=True)
    o_ref[...] = out_t.T.astype(o_ref.dtype)


def _attention_shifted(qt, k, vt, tq=2048, tk=512):
    nh, _, s = qt.shape
    return pl.pallas_call(
        functools.partial(_attn_shifted_kernel, tq=tq, tk=tk),
        grid=(nh, s // tq),
        in_specs=[
            pl.BlockSpec((1, QK_PAD, tq), lambda h, i: (h, 0, i)),
            pl.BlockSpec((1, s, QK_PAD), lambda h, i: (h, 0, 0)),
            pl.BlockSpec((1, V_PAD, s), lambda h, i: (h, 0, 0)),
        ],
        out_specs=pl.BlockSpec((tq, V_HD), lambda h, i: (i, h)),
        out_shape=jax.ShapeDtypeStruct((s, nh * V_HD), BF16),
        scratch_shapes=[pltpu.VMEM((V_HD, tq), F32), pltpu.VMEM((SUBLANES, tq), F32)],
        compiler_params=_params(("arbitrary", "arbitrary")),
        name="mla_attention_shifted",
    )(qt, k, vt)


def _attn_kernel(qt_ref, k_ref, vt_ref, o_ref, s0_ref, s1_ref, bm0_ref, bm1_ref, m_ref, acc_ref, *, tq, tk):
    qi = pl.program_id(1)
    m_ref[...] = jnp.full_like(m_ref, MASK_VALUE)
    acc_ref[...] = jnp.zeros_like(acc_ref)

    def scores(j, s_ref, bm_ref, diag_offset=None):
        q0 = 0 if diag_offset is None else diag_offset
        start = pl.multiple_of(j * tk, tk)
        s = jnp.dot(k_ref[0, pl.ds(start, tk), :], qt_ref[0, :, q0:], preferred_element_type=F32)
        if diag_offset is not None:
            s = _causal_mask(s)
        s_ref[:, q0:] = s
        bm_ref[:, q0:] = jnp.max(s, axis=0, keepdims=True)

    def consume(j, s_ref, bm_ref, q0=0):
        m_prev = m_ref[:, q0:]
        m_new = jnp.maximum(m_prev, bm_ref[:, q0:])
        alpha = jnp.exp2(m_prev - m_new)
        p = jnp.exp2((s_ref[:, q0:] - m_new).astype(BF16))
        start = pl.multiple_of(j * tk, tk)
        vt = vt_ref[0, :, pl.ds(start, tk)]
        acc_ref[:, q0:] = alpha * acc_ref[:, q0:] + jnp.dot(vt, p, preferred_element_type=F32)
        m_ref[:, q0:] = m_new

    r = tq // tk
    slots = ((s0_ref, bm0_ref), (s1_ref, bm1_ref))

    def trip(j0, then_diagonal):
        for b in range(r):
            first_diag = then_diagonal and b == r - 1
            scores(j0 + b + 1, *slots[(b + 1) % 2], diag_offset=0 if first_diag else None)
            consume(j0 + b, *slots[b % 2])

    def diagonal(jd):
        for b in range(r):
            if b + 1 < r:
                scores(jd + b + 1, *slots[(b + 1) % 2], diag_offset=(b + 1) * tk)
            consume(jd + b, *slots[b % 2], q0=b * tk)

    @pl.when(qi > 0)
    def _():
        scores(0, *slots[0])

    n_plain = jnp.maximum(qi - 1, 0)

    def body(t, carry):
        trip(r * (2 * t), False)
        trip(r * (2 * t + 1), False)
        return carry

    lax.fori_loop(0, n_plain // 2, body, 0)

    @pl.when(n_plain % 2 == 1)
    def _():
        trip(r * (n_plain - 1), False)

    @pl.when(qi > 0)
    def _():
        trip(r * (qi - 1), True)
        diagonal(r * qi)

    @pl.when(qi == 0)
    def _():
        scores(0, *slots[0], diag_offset=0)
        diagonal(0)

    acc = acc_ref[...]
    out_t = acc[:V_HD, :] / acc[V_HD:V_HD + 1, :]
    o_ref[...] = out_t.T.astype(o_ref.dtype)


def _attention(qt, k, vt, tq=1024, tk=512):
    nh, _, s = qt.shape
    assert tq % (2 * tk) == 0
    return pl.pallas_call(
        functools.partial(_attn_kernel, tq=tq, tk=tk),
        grid=(nh, s // tq),
        in_specs=[
            pl.BlockSpec((1, QK_PAD, tq), lambda h, i: (h, 0, i)),
            pl.BlockSpec((1, s, QK_PAD), lambda h, i: (h, 0, 0)),
            pl.BlockSpec((1, V_PAD, s), lambda h, i: (h, 0, 0)),
        ],
        out_specs=pl.BlockSpec((tq, V_HD), lambda h, i: (i, h)),
        out_shape=jax.ShapeDtypeStruct((s, nh * V_HD), BF16),
        scratch_shapes=[pltpu.VMEM((tk, tq), F32), pltpu.VMEM((tk, tq), F32), pltpu.VMEM((1, tq), F32),
                        pltpu.VMEM((1, tq), F32), pltpu.VMEM((1, tq), F32), pltpu.VMEM((V_PAD, tq), F32)],
        compiler_params=_params(("arbitrary", "arbitrary")),
        name="mla_attention",
    )(qt, k, vt)


def _pack_bf16_pairs(x):
    n = x.shape[1] // 2
    lo = pltpu.bitcast(x[:, :n].astype(BF16).astype(F32), jnp.uint32)
    hi = pltpu.bitcast(x[:, n:].astype(BF16).astype(F32), jnp.uint32)
    return (lo >> 16) | hi


def _unpack_bf16_pairs(p):
    lo = pltpu.bitcast(p << 16, F32)
    hi = pltpu.bitcast(p & jnp.uint32(0xFFFF0000), F32)
    return jnp.concatenate([lo, hi], axis=1)


def _odd_out_kernel(x_ref, yc_ref, yd_ref, mod_ref, nw_ref, wout_ref, wr_ref, before_ref, x_out_ref, h_ref, rw_ref,
                    ridx_ref, cnt_ref):
    d = D_MODEL
    x = x_ref[...]
    gate_m = mod_ref[:, 2 * d:3 * d]
    sh, sc = mod_ref[:, 3 * d:4 * d], mod_ref[:, 4 * d:5 * d]
    mix = (jnp.dot(yc_ref[...], wout_ref[:C_WIDTH, :], preferred_element_type=F32)
           + jnp.dot(yd_ref[...], wout_ref[C_WIDTH:, :], preferred_element_type=F32))
    x1 = x + gate_m * mix
    x_out_ref[...] = x1
    h = _rms_mod(x1, nw_ref[...], sc, sh)
    h_ref[...] = _pack_bf16_pairs(h)

    h_hi = h.astype(BF16)
    h_lo = (h - h_hi.astype(F32)).astype(BF16)
    hw = jnp.dot(h_hi, wr_ref[...], preferred_element_type=F32)
    logits = hw[:, :LANES] + (hw[:, LANES:] + jnp.dot(h_lo, wr_ref[:, :LANES], preferred_element_type=F32))
    lt = logits.T[:N_EXPERTS, :]
    ex = lax.broadcasted_iota(jnp.int32, lt.shape, 0)
    m1 = jnp.max(lt, axis=0, keepdims=True)
    i1 = jnp.min(jnp.where(lt == m1, ex, N_EXPERTS), axis=0, keepdims=True)
    rest = jnp.where(ex == i1, -jnp.inf, lt)
    m2 = jnp.max(rest, axis=0, keepdims=True)
    i2 = jnp.min(jnp.where(rest == m2, ex, N_EXPERTS), axis=0, keepdims=True)
    e2 = jnp.exp(m2 - m1)
    w1 = 1.0 / (1.0 + e2)
    w2 = e2 / (1.0 + e2)
    row = lax.broadcasted_iota(jnp.int32, (LANES, lt.shape[1]), 0)
    rw_ref[...] = jnp.where(row == 0, w1, jnp.where(row == 1, w2, 0.0)).T

    i = pl.program_id(0)

    @pl.when(i == 0)
    def _():
        cnt_ref[...] = jnp.zeros_like(cnt_ref)

    onehot = jnp.where((ex == i1) | (ex == i2), 1.0, 0.0)
    prefix = jnp.dot(onehot.astype(BF16), before_ref[...], preferred_element_type=F32) + cnt_ref[...]
    r1 = jnp.sum(jnp.where(ex == i1, prefix, 0.0), axis=0, keepdims=True)
    r2 = jnp.sum(jnp.where(ex == i2, prefix, 0.0), axis=0, keepdims=True)
    cnt_ref[...] += jnp.sum(onehot, axis=1, keepdims=True)
    ridx_ref[...] = jnp.where(ex == 0, i1, jnp.where(ex == 1, i2, jnp.where(
        ex == 2, r1.astype(jnp.int32), jnp.where(ex == 3, r2.astype(jnp.int32), 0))))


def _odd_out(x, yc, yd, mod, nw, w_out, w_router, tm=1024):
    s, d = x.shape
    return pl.pallas_call(
        _odd_out_kernel,
        grid=(s // tm,),
        in_specs=[
            pl.BlockSpec((tm, d), lambda i: (i, 0)),
            pl.BlockSpec((tm, C_WIDTH), lambda i: (i, 0)),
            pl.BlockSpec((tm, MLA_HEADS * V_HD), lambda i: (i, 0)),
            _const_spec(mod.shape), _const_spec(nw.shape), _const_spec(w_out.shape), _const_spec(w_router.shape),
            _const_spec((tm, tm)),
        ],
        out_specs=[
            pl.BlockSpec((tm, d), lambda i: (i, 0)),
            pl.BlockSpec((tm, d // 2), lambda i: (i, 0)),
            pl.BlockSpec((tm, LANES), lambda i: (i, 0)),
            pl.BlockSpec((SUBLANES, tm), lambda i: (0, i)),
            pl.BlockSpec((N_EXPERTS, 1), lambda i: (0, 0)),
        ],
        out_shape=[
            jax.ShapeDtypeStruct((s, d), F32),
            jax.ShapeDtypeStruct((s, d // 2), jnp.uint32),
            jax.ShapeDtypeStruct((s, LANES), F32),
            jax.ShapeDtypeStruct((SUBLANES, s), jnp.int32),
            jax.ShapeDtypeStruct((N_EXPERTS, 1), F32),
        ],
        compiler_params=_params(("arbitrary",)),
        name="odd_out_router",
    )(x, yc, yd, mod, nw, w_out, w_router, jnp.triu(jnp.ones((tm, tm), BF16), 1))


def _sc_workers():
    info = plsc.get_sparse_core_info()
    return info.num_cores, info.num_cores * info.num_subcores


def _sc_scatter_rows(x, idx0, idx1, out_rows):
    n, w = x.shape
    nc, nw = _sc_workers()
    per_w = n // nw
    nch = per_w // SC_ROWS
    assert nch % 2 == 0 and nch >= 2
    mesh = plsc.VectorSubcoreMesh(core_axis_name="c", subcore_axis_name="s")

    @functools.partial(
        pl.kernel, mesh=mesh, out_type=jax.ShapeDtypeStruct((out_rows, w), x.dtype),
        scratch_types=[pltpu.VMEM((nch, SC_ROWS), jnp.int32), pltpu.VMEM((nch, SC_ROWS), jnp.int32),
                       pltpu.VMEM((SC_ROWS, w), x.dtype), pltpu.VMEM((SC_ROWS, w), x.dtype),
                       pltpu.SemaphoreType.DMA, pltpu.SemaphoreType.DMA, pltpu.SemaphoreType.DMA,
                       pltpu.SemaphoreType.DMA],
        name="moe_dispatch")
    def scatter(x_hbm, i0_hbm, i1_hbm, out_hbm, i0_v, i1_v, buf0, buf1, lsem0, lsem1, ssem0, ssem1):
        wid = lax.axis_index("s") * nc + lax.axis_index("c")
        pltpu.sync_copy(i0_hbm.at[wid], i0_v)
        pltpu.sync_copy(i1_hbm.at[wid], i1_v)
        base = wid * per_w

        def load(c, buf, sem):
            return pltpu.make_async_copy(x_hbm.at[pl.ds(base + c * SC_ROWS, SC_ROWS)], buf, sem)

        def put(c, buf):
            first = pltpu.make_async_copy(buf, out_hbm.at[i0_v.at[c]], ssem0)
            second = pltpu.make_async_copy(buf, out_hbm.at[i1_v.at[c]], ssem1)
            first.start()
            second.start()
            first.wait()
            second.wait()

        def pair(c, prefetch_next):
            load(c + 1, buf1, lsem1).start()
            load(c, buf0, lsem0).wait()
            put(c, buf0)
            if prefetch_next:
                load(c + 2, buf0, lsem0).start()
            load(c + 1, buf1, lsem1).wait()
            put(c + 1, buf1)

        load(0, buf0, lsem0).start()

        def body(t, carry):
            pair(2 * t, True)
            return carry

        lax.fori_loop(0, nch // 2 - 1, body, 0)
        pair(nch - 2, False)

    return scatter(x, idx0.reshape(nw, nch, SC_ROWS), idx1.reshape(nw, nch, SC_ROWS))


def _sc_gather_rows(table, idx):
    _, w = table.shape
    b = idx.shape[0]
    nc, nw = _sc_workers()
    per_w = b // nw
    nch = per_w // SC_ROWS
    mesh = plsc.VectorSubcoreMesh(core_axis_name="c", subcore_axis_name="s")

    assert nch % 2 == 0 and nch >= 2

    @functools.partial(
        pl.kernel, mesh=mesh, out_type=jax.ShapeDtypeStruct((b, w), table.dtype),
        scratch_types=[pltpu.VMEM((nch, SC_ROWS), jnp.int32), pltpu.VMEM((SC_ROWS, w), table.dtype),
                       pltpu.VMEM((SC_ROWS, w), table.dtype), pltpu.SemaphoreType.DMA, pltpu.SemaphoreType.DMA],
        name="moe_combine_gather")
    def gather(table_hbm, idx_hbm, out_hbm, idx_v, buf0, buf1, sem0, sem1):
        wid = lax.axis_index("s") * nc + lax.axis_index("c")
        pltpu.sync_copy(idx_hbm.at[wid], idx_v)
        base = wid * per_w

        def fetch(c, buf, sem):
            return pltpu.make_async_copy(table_hbm.at[idx_v.at[c]], buf, sem)

        def put(c, buf):
            pltpu.sync_copy(buf, out_hbm.at[pl.ds(base + c * SC_ROWS, SC_ROWS)])

        def pair(c, prefetch_next):
            fetch(c + 1, buf1, sem1).start()
            fetch(c, buf0, sem0).wait()
            put(c, buf0)
            if prefetch_next:
                fetch(c + 2, buf0, sem0).start()
            fetch(c + 1, buf1, sem1).wait()
            put(c + 1, buf1)

        fetch(0, buf0, sem0).start()

        def body(t, carry):
            pair(2 * t, True)
            return carry

        lax.fori_loop(0, nch // 2 - 1, body, 0)
        pair(nch - 2, False)

    return gather(table, idx.reshape(nw, nch, SC_ROWS))


def _moe_kernel(te_ref, nv_ref, rows_ref, xs_ref, wg_ref, wu_ref, wd_ref, ys_ref, x_scr, acc_ref):
    j = pl.program_id(0)
    f = pl.program_id(1)
    nf = pl.num_programs(1)
    tm = xs_ref.shape[0]

    def expert_ffn(rows):
        h = x_scr[:rows, :]
        g = jnp.dot(h, wg_ref[0], preferred_element_type=F32)
        u = jnp.dot(h, wu_ref[0], preferred_element_type=F32)
        act = (g * jax.nn.sigmoid(g) * u).astype(BF16)
        y = jnp.dot(act, wd_ref[0], preferred_element_type=F32)

        @pl.when(f == 0)
        def _():
            acc_ref[:rows, :] = y

        @pl.when(f > 0)
        def _():
            acc_ref[:rows, :] += y

        @pl.when(f == nf - 1)
        def _():
            ys_ref[:rows, :] = _pack_bf16_pairs(acc_ref[:rows, :])

    @pl.when(j < nv_ref[0])
    def _():
        @pl.when(f == 0)
        def _():
            x_scr[...] = _unpack_bf16_pairs(xs_ref[...]).astype(BF16)

        for rows in range(MOE_ROW_STEP, tm + 1, MOE_ROW_STEP):
            pl.when(rows_ref[j] == rows)(functools.partial(expert_ffn, rows))


def _moe_grouped(xs, tile_expert, n_valid, tile_rows, w_gu, w_down, tm, tf=1792):
    p_rows, dh = xs.shape
    d = 2 * dh
    ne, _, ff2 = w_gu.shape
    ff = ff2 // 2
    nf = ff // tf

    def tile(j, nv):
        return jnp.minimum(j, nv[0] - 1)

    def chunk(j, f, nv):
        return jnp.where(j < nv[0], f, nf - 1)

    grid_spec = pltpu.PrefetchScalarGridSpec(
        num_scalar_prefetch=3,
        grid=(p_rows // tm, nf),
        in_specs=[
            pl.BlockSpec((tm, dh), lambda j, f, te, nv, tr: (tile(j, nv), 0)),
            pl.BlockSpec((1, d, tf), lambda j, f, te, nv, tr: (te[tile(j, nv)], 0, chunk(j, f, nv))),
            pl.BlockSpec((1, d, tf), lambda j, f, te, nv, tr: (te[tile(j, nv)], 0, nf + chunk(j, f, nv))),
            pl.BlockSpec((1, tf, d), lambda j, f, te, nv, tr: (te[tile(j, nv)], chunk(j, f, nv), 0)),
        ],
        out_specs=pl.BlockSpec((tm, dh), lambda j, f, te, nv, tr: (tile(j, nv), 0)),
        scratch_shapes=[pltpu.VMEM((tm, d), BF16), pltpu.VMEM((tm, d), F32)],
    )
    return pl.pallas_call(
        _moe_kernel,
        grid_spec=grid_spec,
        out_shape=jax.ShapeDtypeStruct((p_rows, dh), jnp.uint32),
        compiler_params=_params(("arbitrary", "arbitrary")),
        name="moe_experts",
    )(tile_expert, n_valid, tile_rows, xs, w_gu, w_gu, w_down)


def _moe_combine_kernel(x_ref, y0_ref, y1_ref, rw_ref, mod_ref, *rest):
    o_ref = rest[-1]
    d = D_MODEL
    w1 = rw_ref[:, 0:1]
    w2 = rw_ref[:, 1:2]
    y = w1 * _unpack_bf16_pairs(y0_ref[...]) + w2 * _unpack_bf16_pairs(y1_ref[...])
    o_ref[...] = x_ref[...] + mod_ref[:, 5 * d:6 * d] * y


def _moe_combine(x, yg, rw, mod, prev, part, n_parts, tm=512):
    s, d = x.shape
    nb = s // tm // n_parts
    first = part * nb
    in_specs = [
        pl.BlockSpec((tm, d), lambda i: (first + i, 0)),
        pl.BlockSpec((tm, d // 2), lambda i: (i, 0)),
        pl.BlockSpec((tm, d // 2), lambda i: (nb + i, 0)),
        pl.BlockSpec((tm, LANES), lambda i: (first + i, 0)),
        _const_spec(mod.shape),
    ]
    args = [x, yg, yg, rw, mod]
    aliases = {}
    if prev is not None:
        in_specs.append(pl.BlockSpec(memory_space=pl.ANY))
        args.append(prev)
        aliases = {len(args) - 1: 0}
    return pl.pallas_call(
        _moe_combine_kernel,
        grid=(nb,),
        in_specs=in_specs,
        out_specs=pl.BlockSpec((tm, d), lambda i: (first + i, 0)),
        out_shape=jax.ShapeDtypeStruct((s, d), F32),
        input_output_aliases=aliases,
        compiler_params=_params(("arbitrary",)),
        name="moe_combine",
    )(*args)


def _moe_sparse(x, h_packed, rw, ridx, counts, mod, w_gu, w_down, tm=MOE_TILE):
    s = x.shape[0]
    ne = w_gu.shape[0]
    n_tiles = (2 * s) // tm + ne
    cnt = counts[:ne, 0].astype(jnp.int32)
    padded = ((cnt + tm - 1) // tm) * tm
    ends = jnp.cumsum(padded)
    offs = ends - padded
    experts = jnp.arange(ne, dtype=jnp.int32)
    off_of = lambda e: jnp.sum(jnp.where(e[:, None] == experts[None, :], offs[None, :], 0), axis=1)
    pos0 = off_of(ridx[0]) + ridx[2]
    pos1 = off_of(ridx[1]) + ridx[3]
    tile_start = jnp.arange(n_tiles, dtype=jnp.int32) * tm
    tile_expert = jnp.minimum(jnp.sum(tile_start[:, None] >= ends[None, :], axis=1), ne - 1).astype(jnp.int32)
    n_valid = (ends[-1] // tm).reshape(1).astype(jnp.int32)
    filled = jnp.clip((offs + cnt)[tile_expert] - tile_start, 0, tm)
    tile_rows = (((filled + MOE_ROW_STEP - 1) // MOE_ROW_STEP) * MOE_ROW_STEP).astype(jnp.int32)
    xs = _sc_scatter_rows(h_packed, pos0, pos1, n_tiles * tm)
    ys = _moe_grouped(xs, tile_expert, n_valid, tile_rows, w_gu, w_down, tm)
    out = None
    sp = s // COMBINE_PARTS
    for part in range(COMBINE_PARTS):
        rows = slice(part * sp, (part + 1) * sp)
        yg = _sc_gather_rows(ys, jnp.concatenate([pos0[rows], pos1[rows]]))
        out = _moe_combine(x, yg, rw, mod, out, part, COMBINE_PARTS)
    return out


def kernel(x, c, positions, norm_mix_w, norm_ffn_w, ada_w, ada_b, e_w_in, a_ln_w, a_ln_b, a_w_s, a_b_s, b_w_grp,
           b_scale, e_w_out, ffn_w_gu, ffn_w_down, o_w_in, c_conv_w, q_a_norm, w_uq, kv_norm, w_ukv, q_norm_w,
           k_norm_w, o_w_out, router_w, moe_w_gu, moe_w_down):
    bsz, s, d = x.shape
    assert bsz == 1 and d == D_MODEL
    depth = ada_w.shape[0]
    nh = MLA_HEADS
    xs = x.reshape(s, d)
    mod = _ada_mod(c, ada_w, ada_b)
    cos_t, sin_t = _rope_tables(positions)
    row = lambda a: a.reshape(1, -1)

    for layer in range(depth):
        i = layer // 2
        m = mod[layer]
        if layer % 2 == 0:
            has_next = layer + 1 < depth
            dummy = jnp.zeros((s // 512, 16, LANES), F32)
            xs, moe_w_down_bf16, (ffn_w_gu_bf16, ffn_w_down_bf16) = _even_mix(
                xs, m, row(norm_mix_w[layer]), e_w_in[i].astype(BF16), row(a_ln_w[i]), row(a_ln_b[i]), a_w_s[i],
                a_b_s[i].reshape(A_GROUPS, CHUNK, 1), b_w_grp[i].astype(BF16), row(b_scale[i]),
                e_w_out[i].astype(BF16), moe_w_down[i] if has_next else dummy, [ffn_w_gu[i], ffn_w_down[i]])
            xs, moe_w_gu_bf16 = _ffn(xs, m, row(norm_ffn_w[layer]), ffn_w_gu_bf16, ffn_w_down_bf16,
                                     moe_w_gu[i] if has_next else dummy)
        else:
            o2 = 3 * C_WIDTH + Q_LORA + KV_LORA
            w_in = o_w_in[i][:, :o2].astype(BF16)
            w_kpe = jnp.pad(o_w_in[i][:, o2:], ((0, 0), (0, LANES - QK_ROPE))).astype(BF16)
            bound = (1.02 * QK_HD * QK_HD ** -0.5 * LOG2E) * jnp.max(jnp.abs(q_norm_w[i])) * jnp.max(jnp.abs(k_norm_w[i]))
            wkv = w_ukv[i].reshape(KV_LORA, nh, QK_NOPE + V_HD)
            w_k = wkv[:, :, :QK_NOPE].reshape(KV_LORA, nh * QK_NOPE).astype(BF16)
            w_v_t = wkv[:, :, QK_NOPE:].reshape(KV_LORA, nh * V_HD).T.astype(BF16)
            yc, qt, k, vt = _odd_prep(
                xs, m, row(norm_mix_w[layer]), w_in, w_kpe, c_conv_w[i], row(q_a_norm[i]), w_uq[i].T.astype(BF16),
                row(kv_norm[i]), w_k, w_v_t, q_norm_w[i].reshape(QK_HD, 1), row(k_norm_w[i]),
                k_norm_w[i].reshape(QK_HD, 1), jnp.full((1, LANES), -bound, F32), cos_t, sin_t)
            yd = lax.cond(bound <= ATTN_MAX_BOUND, _attention_shifted, _attention, qt, k, vt)
            wr = jnp.pad(router_w[i], ((0, 0), (0, LANES - N_EXPERTS)))
            wr_hi = lax.reduce_precision(wr, exponent_bits=8, mantissa_bits=7)
            w_router = jnp.concatenate([wr_hi, wr - wr_hi], axis=1).astype(BF16)
            xs, hp, rw, ridx, counts = _odd_out(xs, yc, yd, m, row(norm_ffn_w[layer]), o_w_out[i].astype(BF16),
                                                w_router)
            xs = _moe_sparse(xs, hp, rw, ridx, counts, m, moe_w_gu_bf16, moe_w_down_bf16)
    return xs.reshape(bsz, s, d)
```

```python
import functools

import jax
import jax.numpy as jnp
from jax import lax
from jax.experimental import pallas as pl
from jax.experimental.pallas import tpu as pltpu
from jax.experimental.pallas import tpu_sc as plsc

D_MODEL = 1024
SEQ = 16384
EPS = 1e-6
CHUNK = 128
A_WIDTH = 512
A_GROUPS = 4
B_WIDTH = 512
POOL_WINDOWS = (2, 4, 8, 16)
B_HD = 128
C_WIDTH = 512
MLA_HEADS = 4
Q_LORA = 256
KV_LORA = 256
QK_NOPE = 128
QK_ROPE = 64
QK_HD = QK_NOPE + QK_ROPE
V_HD = 128
ROPE_THETA = 10000.0
D_FF = 2816
N_EXPERTS = 8
D_FF_EXPERT = 3584

LANES = 128
SUBLANES = 8
POOL_HALO = 16
CONV_HALO = 8
QK_PAD = 256
V_PAD = 144
LOG2E = 1.4426950408889634
MASK_VALUE = -1e30
MOE_TILE = 512
CONV_COLS = 256
ATTN_MAX_BOUND = 48.0
MOE_ROW_STEP = 128
COMBINE_PARTS = 4
SC_ROWS = 64
VMEM_LIMIT = 56 * 1024 * 1024

F32 = jnp.float32
BF16 = jnp.bfloat16


def _params(sem, vmem=VMEM_LIMIT, flags=None):
    return pltpu.CompilerParams(dimension_semantics=sem, vmem_limit_bytes=vmem, flags=flags)


def _const_spec(shape, single=False):
    nd = len(shape)
    return pl.BlockSpec(shape, lambda *_: (0,) * nd, pipeline_mode=pl.Buffered(1) if single else None)


def _rms_mod(x, nw, sc, sh):
    ms = jnp.mean(x * x, axis=-1, keepdims=True)
    return (x * lax.rsqrt(ms + EPS)) * (nw * (1.0 + sc)) + sh


def _ada_kernel(c_ref, w_ref, b_ref, o_ref):
    c = c_ref[...]
    ca = c * jax.nn.sigmoid(c)
    o_ref[0] = jnp.sum(w_ref[0] * ca, axis=0, keepdims=True) + b_ref[0]


def _ada_mod(c, ada_w, ada_b):
    depth, d, n = ada_w.shape
    tn = 3072
    return pl.pallas_call(
        _ada_kernel,
        grid=(depth, n // tn),
        in_specs=[
            pl.BlockSpec((d, 1), lambda l, j: (0, 0)),
            pl.BlockSpec((1, d, tn), lambda l, j: (l, 0, j)),
            pl.BlockSpec((1, 1, tn), lambda l, j: (l, 0, j)),
        ],
        out_specs=pl.BlockSpec((1, 1, tn), lambda l, j: (l, 0, j)),
        out_shape=jax.ShapeDtypeStruct((depth, 1, n), F32),
        compiler_params=_params(("arbitrary", "arbitrary")),
        name="ada_mod",
    )(c.reshape(d, 1), ada_w, ada_b.reshape(depth, 1, n))


def _even_mix_kernel(x_ref, mod_ref, nw_ref, win_ref, lnw_ref, lnb_ref, ws_ref, bs_ref, wg_ref, bsc_ref,
                     wout_ref, *rest):
    n_sides = (len(rest) - 4) // 2
    o_ref = rest[n_sides]
    halo_ref, sv_ref, yb_ref = rest[-3:]
    for side_ref, side_out_ref in zip(rest[:n_sides], rest[n_sides + 1:2 * n_sides + 1]):
        side_out_ref[...] = side_ref[...].astype(BF16)
    tm = x_ref.shape[0]
    i = pl.program_id(0)

    @pl.when(i == 0)
    def _():
        halo_ref[...] = jnp.zeros_like(halo_ref)

    d = D_MODEL
    x = x_ref[...]
    sh, sc, gate = mod_ref[:, 0:d], mod_ref[:, d:2 * d], mod_ref[:, 2 * d:3 * d]
    h = _rms_mod(x, nw_ref[...], sc, sh).astype(BF16)
    p = jnp.dot(h, win_ref[...], preferred_element_type=F32)

    gl = jax.nn.gelu(p[:, :2 * A_WIDTH])
    u = gl[:, :A_WIDTH]
    v = gl[:, A_WIDTH:]
    mu = jnp.mean(v, axis=-1, keepdims=True)
    vc = v - mu
    var = jnp.mean(vc * vc, axis=-1, keepdims=True)
    vn = (vc * lax.rsqrt(var + EPS) * lnw_ref[...] + lnb_ref[...]).astype(BF16)
    row = lax.broadcasted_iota(jnp.int32, (CHUNK, CHUNK), 0)
    col = lax.broadcasted_iota(jnp.int32, (CHUNK, CHUNK), 1)
    for g in range(A_GROUPS):
        w = jnp.where(col <= row, ws_ref[g], 0.0).astype(BF16)
        b = bs_ref[g]
        for c in range(tm // CHUNK):
            blk = vn[c * CHUNK:(c + 1) * CHUNK, g * LANES:(g + 1) * LANES]
            sv_ref[c * CHUNK:(c + 1) * CHUNK, g * LANES:(g + 1) * LANES] = (
                jnp.dot(w, blk, preferred_element_type=F32) + b)
    ya = (u * sv_ref[...]).astype(BF16)

    pb = p[:, 2 * A_WIDTH:]
    ext = jnp.concatenate([halo_ref[...], pb], axis=0)
    halo_ref[...] = pb[tm - POOL_HALO:, :]
    t_glob = i * tm + lax.broadcasted_iota(jnp.int32, (tm, 1), 0)
    s = ext
    width = 1
    for g, win in enumerate(POOL_WINDOWS):
        while width < win:
            s = s + pltpu.roll(s, width, 0)
            width *= 2
        cnt = jnp.minimum(t_glob + 1, win).astype(F32)
        sl = slice(g * B_HD, (g + 1) * B_HD)
        pooled = s[POOL_HALO:, sl] / cnt
        dg = (pooled - pb[:, sl]).astype(BF16)
        yb_ref[:, sl] = jnp.dot(dg, wg_ref[g], preferred_element_type=F32)
    yb = (yb_ref[...] * bsc_ref[...]).astype(BF16)

    mix = (jnp.dot(ya, wout_ref[:A_WIDTH, :], preferred_element_type=F32)
           + jnp.dot(yb, wout_ref[A_WIDTH:, :], preferred_element_type=F32))
    o_ref[...] = x + gate * mix


def _side_cast_spec(side, steps, axis):
    ne = side.shape[0]
    parts = steps // ne
    assert parts * ne == steps and side.shape[axis] % parts == 0
    block = list(side.shape)
    block[0] = 1
    block[axis] //= parts
    assert block[1] % 16 == 0 and block[2] % LANES == 0
    if axis == 1:
        return pl.BlockSpec(tuple(block), lambda i: (i // parts, i % parts, 0))
    return pl.BlockSpec(tuple(block), lambda i: (i // parts, 0, i % parts))


def _row_cast_spec(w, steps):
    rows = -(-w.shape[0] // steps)
    rows = -(-rows // 16) * 16
    last = -(-w.shape[0] // rows) - 1
    return pl.BlockSpec((rows, w.shape[1]), lambda i: (jnp.minimum(i, last), 0))


def _even_mix(x, mod, nw, w_in, ln_w, ln_b, w_s, b_s, w_grp, b_scale, w_out, side, row_sides, tm=512):
    s, d = x.shape
    steps = s // tm
    sides = [side, *row_sides]
    side_specs = [_side_cast_spec(side, steps, 1)] + [_row_cast_spec(w, steps) for w in row_sides]
    outs = pl.pallas_call(
        _even_mix_kernel,
        grid=(steps,),
        in_specs=[
            pl.BlockSpec((tm, d), lambda i: (i, 0)),
            _const_spec(mod.shape), _const_spec(nw.shape), _const_spec(w_in.shape),
            _const_spec(ln_w.shape), _const_spec(ln_b.shape), _const_spec(w_s.shape), _const_spec(b_s.shape),
            _const_spec(w_grp.shape), _const_spec(b_scale.shape), _const_spec(w_out.shape), *side_specs,
        ],
        out_specs=[pl.BlockSpec((tm, d), lambda i: (i, 0)), *side_specs],
        out_shape=[jax.ShapeDtypeStruct((s, d), F32)] + [jax.ShapeDtypeStruct(w.shape, BF16) for w in sides],
        scratch_shapes=[pltpu.VMEM((POOL_HALO, B_WIDTH), F32), pltpu.VMEM((tm, A_WIDTH), F32),
                        pltpu.VMEM((tm, B_WIDTH), F32)],
        compiler_params=_params(("arbitrary",)),
        name="even_mix",
    )(x, mod, nw, w_in, ln_w, ln_b, w_s, b_s, w_grp, b_scale, w_out, *sides)
    return outs[0], outs[1], outs[2:]


def _ffn_kernel(x_ref, mod_ref, nw_ref, wgu_ref, wd_ref, side_ref, o_ref, side_out_ref, *, n_chunks):
    side_out_ref[...] = side_ref[...].astype(BF16)
    d = D_MODEL
    x = x_ref[...]
    sh, sc, gate = mod_ref[:, 3 * d:4 * d], mod_ref[:, 4 * d:5 * d], mod_ref[:, 5 * d:6 * d]
    h = _rms_mod(x, nw_ref[...], sc, sh).astype(BF16)
    ff = wd_ref.shape[0]
    tf = ff // n_chunks
    acc = jnp.zeros(x.shape, F32)
    for f in range(n_chunks):
        g = jnp.dot(h, wgu_ref[:, f * tf:(f + 1) * tf], preferred_element_type=F32)
        u = jnp.dot(h, wgu_ref[:, ff + f * tf:ff + (f + 1) * tf], preferred_element_type=F32)
        act = (g * jax.nn.sigmoid(g) * u).astype(BF16)
        acc = acc + jnp.dot(act, wd_ref[f * tf:(f + 1) * tf, :], preferred_element_type=F32)
    o_ref[...] = x + gate * acc


def _ffn(x, mod, nw, w_gu, w_down, side, tm=512, n_chunks=11):
    s, d = x.shape
    steps = s // tm
    side_spec = _side_cast_spec(side, steps, 2)
    return pl.pallas_call(
        functools.partial(_ffn_kernel, n_chunks=n_chunks),
        grid=(steps,),
        in_specs=[
            pl.BlockSpec((tm, d), lambda i: (i, 0)),
            _const_spec(mod.shape), _const_spec(nw.shape), _const_spec(w_gu.shape, single=True),
            _const_spec(w_down.shape, single=True), side_spec,
        ],
        out_specs=[pl.BlockSpec((tm, d), lambda i: (i, 0)), side_spec],
        out_shape=[jax.ShapeDtypeStruct((s, d), F32), jax.ShapeDtypeStruct(side.shape, BF16)],
        compiler_params=_params(("arbitrary",)),
        name="ffn",
    )(x, mod, nw, w_gu, w_down, side)


def _rope_table_kernel(pos_ref, invf_ref, cos_ref, sin_ref):
    ang = pos_ref[...].astype(F32) * invf_ref[...]
    cos_ref[...] = jnp.cos(ang)
    sin_ref[...] = jnp.sin(ang)


def _rope_tables(positions):
    s = positions.shape[-1]
    half = QK_ROPE // 2
    inv_freq = ROPE_THETA ** (-jnp.arange(0, QK_ROPE, 2, dtype=F32) / QK_ROPE)
    return pl.pallas_call(
        _rope_table_kernel,
        out_shape=(jax.ShapeDtypeStruct((half, s), F32), jax.ShapeDtypeStruct((half, s), F32)),
        name="rope_tables",
    )(positions.reshape(1, s), inv_freq.reshape(half, 1))


def _odd_prep_kernel(x_ref, mod_ref, nw_ref, win_ref, wkpe_ref, cw_ref, qan_ref, wuqt_ref, kvn_ref, wk_ref, wvt_ref,
                     qnw_ref, knw_ref, knwc_ref, shift_ref, cost_ref, sint_ref, yc_ref, qt_ref, k_ref, vt_ref, halo_ref):
    tm = x_ref.shape[0]
    i = pl.program_id(0)

    @pl.when(i == 0)
    def _():
        halo_ref[...] = jnp.zeros_like(halo_ref)

    d = D_MODEL
    x = x_ref[...]
    sh, sc = mod_ref[:, 0:d], mod_ref[:, d:2 * d]
    h = _rms_mod(x, nw_ref[...], sc, sh).astype(BF16)
    cw = C_WIDTH
    o0 = 3 * cw
    proj = lambda lo, hi: jnp.dot(h, win_ref[:, lo:hi], preferred_element_type=F32)
    kpe_pad = jnp.dot(h, wkpe_ref[...], preferred_element_type=F32)
    kpe = kpe_pad[:, :QK_ROPE]

    for c0 in range(0, cw, CONV_COLS):
        cols = slice(c0, c0 + CONV_COLS)
        z = proj(cw + c0, cw + c0 + CONV_COLS) * proj(2 * cw + c0, 2 * cw + c0 + CONV_COLS)
        ext = jnp.concatenate([halo_ref[:, cols], z], axis=0)
        halo_ref[:, cols] = z[tm - CONV_HALO:, :]
        z1 = pltpu.roll(ext, 1, 0)[CONV_HALO:, :]
        z2 = pltpu.roll(ext, 2, 0)[CONV_HALO:, :]
        conv = cw_ref[0:1, cols] * z2 + cw_ref[1:2, cols] * z1 + cw_ref[2:3, cols] * z
        yc_ref[:, cols] = (proj(c0, c0 + CONV_COLS) * conv).astype(BF16)

    cq = proj(o0, o0 + Q_LORA)
    ckv = proj(o0 + Q_LORA, o0 + Q_LORA + KV_LORA)
    cqn = cq * lax.rsqrt(jnp.mean(cq * cq, axis=-1, keepdims=True) + EPS) * qan_ref[...]
    ckvn = ckv * lax.rsqrt(jnp.mean(ckv * ckv, axis=-1, keepdims=True) + EPS) * kvn_ref[...]
    cqn_t = cqn.T.astype(BF16)
    ckvn_t = ckvn.T.astype(BF16)
    q_t = jnp.dot(wuqt_ref[...], cqn_t, preferred_element_type=F32)
    v_t = jnp.dot(wvt_ref[...], ckvn_t, preferred_element_type=F32)
    kn_all = jnp.dot(ckvn.astype(BF16), wk_ref[...], preferred_element_type=F32)

    nh = MLA_HEADS
    half = QK_ROPE // 2
    sm_scale = QK_HD ** -0.5
    cos_t, sin_t = cost_ref[...], sint_ref[...]
    kn_w = knw_ref[:, :QK_NOPE]
    kw_t = kpe_pad.T[:QK_ROPE, :] * knwc_ref[QK_NOPE:, :]
    k1, k2 = kw_t[:half, :], kw_t[half:, :]
    kr_t = jnp.concatenate([k1 * cos_t - k2 * sin_t, k2 * cos_t + k1 * sin_t,
                            jnp.zeros((LANES - QK_ROPE, tm), F32)], axis=0)
    kr = kr_t.T[:, :QK_ROPE]
    kpe_ss = jnp.sum(kpe * kpe, axis=-1, keepdims=True)
    npad = QK_PAD - QK_HD
    zrows = jnp.where(lax.broadcasted_iota(jnp.int32, (npad, tm), 0) == 0, shift_ref[:, 0:1], 0.0)
    zpad = jnp.where(lax.broadcasted_iota(jnp.int32, (tm, npad), 1) == 0, 1.0, 0.0)
    ones_rows = (lax.broadcasted_iota(jnp.int32, (V_PAD - V_HD, tm), 0) == 0).astype(BF16)
    for hd in range(nh):
        qh = q_t[hd * QK_HD:(hd + 1) * QK_HD, :]
        q_inv = lax.rsqrt(jnp.sum(qh * qh, axis=0, keepdims=True) / QK_HD + EPS) * (sm_scale * LOG2E)
        qw = qh * qnw_ref[...]
        x1, x2 = qw[QK_NOPE:QK_NOPE + half, :], qw[QK_NOPE + half:, :]
        qt_ref[hd] = jnp.concatenate(
            [qw[:QK_NOPE, :] * q_inv, (x1 * cos_t - x2 * sin_t) * q_inv, (x2 * cos_t + x1 * sin_t) * q_inv, zrows],
            axis=0).astype(BF16)
        kn = kn_all[:, hd * QK_NOPE:(hd + 1) * QK_NOPE]
        k_ss = jnp.sum(kn * kn, axis=-1, keepdims=True) + kpe_ss
        k_inv = lax.rsqrt(k_ss / QK_HD + EPS)
        k_ref[hd] = jnp.concatenate([kn * k_inv * kn_w, kr * k_inv, zpad], axis=1).astype(BF16)
        vt_ref[hd, :V_HD, :] = v_t[hd * V_HD:(hd + 1) * V_HD, :].astype(BF16)
        vt_ref[hd, V_HD:, :] = ones_rows


def _odd_prep(x, mod, nw, w_in, w_kpe, conv_w, q_a_norm, w_uq_t, kv_norm, w_k, w_v_t, q_norm_col, k_norm_w,
              k_norm_col, shift, cos_t, sin_t, tm=1024):
    s, d = x.shape
    nh = MLA_HEADS
    half = QK_ROPE // 2
    consts = [mod, nw, w_in, w_kpe, conv_w, q_a_norm, w_uq_t, kv_norm, w_k, w_v_t, q_norm_col, k_norm_w, k_norm_col,
              shift]
    return pl.pallas_call(
        _odd_prep_kernel,
        grid=(s // tm,),
        in_specs=[pl.BlockSpec((tm, d), lambda i: (i, 0))] + [_const_spec(a.shape) for a in consts] + [
            pl.BlockSpec((half, tm), lambda i: (0, i)), pl.BlockSpec((half, tm), lambda i: (0, i))],
        out_specs=[
            pl.BlockSpec((tm, C_WIDTH), lambda i: (i, 0)),
            pl.BlockSpec((nh, QK_PAD, tm), lambda i: (0, 0, i)),
            pl.BlockSpec((nh, tm, QK_PAD), lambda i: (0, i, 0)),
            pl.BlockSpec((nh, V_PAD, tm), lambda i: (0, 0, i)),
        ],
        out_shape=[
            jax.ShapeDtypeStruct((s, C_WIDTH), BF16),
            jax.ShapeDtypeStruct((nh, QK_PAD, s), BF16),
            jax.ShapeDtypeStruct((nh, s, QK_PAD), BF16),
            jax.ShapeDtypeStruct((nh, V_PAD, s), BF16),
        ],
        scratch_shapes=[pltpu.VMEM((CONV_HALO, C_WIDTH), F32)],
        compiler_params=_params(("arbitrary",)),
        name="odd_prep",
    )(x, *consts, cos_t, sin_t)


def _causal_mask(s, lead=0):
    key = lax.broadcasted_iota(jnp.int32, s.shape, 0)
    qry = lead + lax.broadcasted_iota(jnp.int32, s.shape, 1)
    return jnp.where(key <= qry, s, MASK_VALUE)


def _attn_shifted_kernel(qt_ref, k_ref, vt_ref, o_ref, acc_ref, l_ref, *, tq, tk):
    qi = pl.program_id(1)
    acc_ref[...] = jnp.zeros_like(acc_ref)
    l_ref[...] = jnp.zeros_like(l_ref)
    r = tq // tk

    def block(j, diag_piece=None):
        q0 = 0 if diag_piece is None else diag_piece * tk
        start = pl.multiple_of(j * tk, tk)
        s = jnp.dot(k_ref[0, pl.ds(start, tk), :], qt_ref[0, :, q0:], preferred_element_type=F32)
        if diag_piece is not None:
            s = _causal_mask(s)
        p = jnp.exp2(s)
        l_ref[:, q0:] += jnp.sum(p.reshape(tk // SUBLANES, SUBLANES, p.shape[1]), axis=0)
        acc_ref[:, q0:] += jnp.dot(vt_ref[0, :V_HD, pl.ds(start, tk)], p.astype(BF16), preferred_element_type=F32)

    def body(t, carry):
        for b in range(r):
            block(r * t + b)
        return carry

    lax.fori_loop(0, qi, body, 0)
    for b in range(r):
        block(r * qi + b, diag_piece=b)
    out_t = acc_ref[...] / jnp.sum(l_ref[...], axis=0, keepdims=True)
    o_ref[...] = out_t.T.astype(o_ref.dtype)


def _attention_shifted(qt, k, vt, tq=2048, tk=512):
    nh, _, s = qt.shape
    return pl.pallas_call(
        functools.partial(_attn_shifted_kernel, tq=tq, tk=tk),
        grid=(nh, s // tq),
        in_specs=[
            pl.BlockSpec((1, QK_PAD, tq), lambda h, i: (h, 0, i)),
            pl.BlockSpec((1, s, QK_PAD), lambda h, i: (h, 0, 0)),
            pl.BlockSpec((1, V_PAD, s), lambda h, i: (h, 0, 0)),
        ],
        out_specs=pl.BlockSpec((tq, V_HD), lambda h, i: (i, h)),
        out_shape=jax.ShapeDtypeStruct((s, nh * V_HD), BF16),
        scratch_shapes=[pltpu.VMEM((V_HD, tq), F32), pltpu.VMEM((SUBLANES, tq), F32)],
        compiler_params=_params(("arbitrary", "arbitrary")),
        name="mla_attention_shifted",
    )(qt, k, vt)


def _attn_kernel(qt_ref, k_ref, vt_ref, o_ref, s0_ref, s1_ref, bm0_ref, bm1_ref, m_ref, acc_ref, *, tq, tk):
    qi = pl.program_id(1)
    m_ref[...] = jnp.full_like(m_ref, MASK_VALUE)
    acc_ref[...] = jnp.zeros_like(acc_ref)

    def scores(j, s_ref, bm_ref, diag_offset=None):
        q0 = 0 if diag_offset is None else diag_offset
        start = pl.multiple_of(j * tk, tk)
        s = jnp.dot(k_ref[0, pl.ds(start, tk), :], qt_ref[0, :, q0:], preferred_element_type=F32)
        if diag_offset is not None:
            s = _causal_mask(s)
        s_ref[:, q0:] = s
        bm_ref[:, q0:] = jnp.max(s, axis=0, keepdims=True)

    def consume(j, s_ref, bm_ref, q0=0):
        m_prev = m_ref[:, q0:]
        m_new = jnp.maximum(m_prev, bm_ref[:, q0:])
        alpha = jnp.exp2(m_prev - m_new)
        p = jnp.exp2((s_ref[:, q0:] - m_new).astype(BF16))
        start = pl.multiple_of(j * tk, tk)
        vt = vt_ref[0, :, pl.ds(start, tk)]
        acc_ref[:, q0:] = alpha * acc_ref[:, q0:] + jnp.dot(vt, p, preferred_element_type=F32)
        m_ref[:, q0:] = m_new

    r = tq // tk
    slots = ((s0_ref, bm0_ref), (s1_ref, bm1_ref))

    def trip(j0, then_diagonal):
        for b in range(r):
            first_diag = then_diagonal and b == r - 1
            scores(j0 + b + 1, *slots[(b + 1) % 2], diag_offset=0 if first_diag else None)
            consume(j0 + b, *slots[b % 2])

    def diagonal(jd):
        for b in range(r):
            if b + 1 < r:
                scores(jd + b + 1, *slots[(b + 1) % 2], diag_offset=(b + 1) * tk)
            consume(jd + b, *slots[b % 2], q0=b * tk)

    @pl.when(qi > 0)
    def _():
        scores(0, *slots[0])

    n_plain = jnp.maximum(qi - 1, 0)

    def body(t, carry):
        trip(r * (2 * t), False)
        trip(r * (2 * t + 1), False)
        return carry

    lax.fori_loop(0, n_plain // 2, body, 0)

    @pl.when(n_plain % 2 == 1)
    def _():
        trip(r * (n_plain - 1), False)

    @pl.when(qi > 0)
    def _():
        trip(r * (qi - 1), True)
        diagonal(r * qi)

    @pl.when(qi == 0)
    def _():
        scores(0, *slots[0], diag_offset=0)
        diagonal(0)

    acc = acc_ref[...]
    out_t = acc[:V_HD, :] / acc[V_HD:V_HD + 1, :]
    o_ref[...] = out_t.T.astype(o_ref.dtype)


def _attention(qt, k, vt, tq=1024, tk=512):
    nh, _, s = qt.shape
    assert tq % (2 * tk) == 0
    return pl.pallas_call(
        functools.partial(_attn_kernel, tq=tq, tk=tk),
        grid=(nh, s // tq),
        in_specs=[
            pl.BlockSpec((1, QK_PAD, tq), lambda h, i: (h, 0, i)),
            pl.BlockSpec((1, s, QK_PAD), lambda h, i: (h, 0, 0)),
            pl.BlockSpec((1, V_PAD, s), lambda h, i: (h, 0, 0)),
        ],
        out_specs=pl.BlockSpec((tq, V_HD), lambda h, i: (i, h)),
        out_shape=jax.ShapeDtypeStruct((s, nh * V_HD), BF16),
        scratch_shapes=[pltpu.VMEM((tk, tq), F32), pltpu.VMEM((tk, tq), F32), pltpu.VMEM((1, tq), F32),
                        pltpu.VMEM((1, tq), F32), pltpu.VMEM((1, tq), F32), pltpu.VMEM((V_PAD, tq), F32)],
        compiler_params=_params(("arbitrary", "arbitrary")),
        name="mla_attention",
    )(qt, k, vt)


def _pack_bf16_pairs(x):
    n = x.shape[1] // 2
    lo = pltpu.bitcast(x[:, :n].astype(BF16).astype(F32), jnp.uint32)
    hi = pltpu.bitcast(x[:, n:].astype(BF16).astype(F32), jnp.uint32)
    return (lo >> 16) | hi


def _unpack_bf16_pairs(p):
    lo = pltpu.bitcast(p << 16, F32)
    hi = pltpu.bitcast(p & jnp.uint32(0xFFFF0000), F32)
    return jnp.concatenate([lo, hi], axis=1)


def _odd_out_kernel(x_ref, yc_ref, yd_ref, mod_ref, nw_ref, wout_ref, wr_ref, before_ref, x_out_ref, h_ref, rw_ref,
                    ridx_ref, cnt_ref):
    d = D_MODEL
    x = x_ref[...]
    gate_m = mod_ref[:, 2 * d:3 * d]
    sh, sc = mod_ref[:, 3 * d:4 * d], mod_ref[:, 4 * d:5 * d]
    mix = (jnp.dot(yc_ref[...], wout_ref[:C_WIDTH, :], preferred_element_type=F32)
           + jnp.dot(yd_ref[...], wout_ref[C_WIDTH:, :], preferred_element_type=F32))
    x1 = x + gate_m * mix
    x_out_ref[...] = x1
    h = _rms_mod(x1, nw_ref[...], sc, sh)
    h_ref[...] = _pack_bf16_pairs(h)

    h_hi = h.astype(BF16)
    h_lo = (h - h_hi.astype(F32)).astype(BF16)
    hw = jnp.dot(h_hi, wr_ref[...], preferred_element_type=F32)
    logits = hw[:, :LANES] + (hw[:, LANES:] + jnp.dot(h_lo, wr_ref[:, :LANES], preferred_element_type=F32))
    lt = logits.T[:N_EXPERTS, :]
    ex = lax.broadcasted_iota(jnp.int32, lt.shape, 0)
    m1 = jnp.max(lt, axis=0, keepdims=True)
    i1 = jnp.min(jnp.where(lt == m1, ex, N_EXPERTS), axis=0, keepdims=True)
    rest = jnp.where(ex == i1, -jnp.inf, lt)
    m2 = jnp.max(rest, axis=0, keepdims=True)
    i2 = jnp.min(jnp.where(rest == m2, ex, N_EXPERTS), axis=0, keepdims=True)
    e2 = jnp.exp(m2 - m1)
    w1 = 1.0 / (1.0 + e2)
    w2 = e2 / (1.0 + e2)
    row = lax.broadcasted_iota(jnp.int32, (LANES, lt.shape[1]), 0)
    rw_ref[...] = jnp.where(row == 0, w1, jnp.where(row == 1, w2, 0.0)).T

    i = pl.program_id(0)

    @pl.when(i == 0)
    def _():
        cnt_ref[...] = jnp.zeros_like(cnt_ref)

    onehot = jnp.where((ex == i1) | (ex == i2), 1.0, 0.0)
    prefix = jnp.dot(onehot.astype(BF16), before_ref[...], preferred_element_type=F32) + cnt_ref[...]
    r1 = jnp.sum(jnp.where(ex == i1, prefix, 0.0), axis=0, keepdims=True)
    r2 = jnp.sum(jnp.where(ex == i2, prefix, 0.0), axis=0, keepdims=True)
    cnt_ref[...] += jnp.sum(onehot, axis=1, keepdims=True)
    ridx_ref[...] = jnp.where(ex == 0, i1, jnp.where(ex == 1, i2, jnp.where(
        ex == 2, r1.astype(jnp.int32), jnp.where(ex == 3, r2.astype(jnp.int32), 0))))


def _odd_out(x, yc, yd, mod, nw, w_out, w_router, tm=1024):
    s, d = x.shape
    return pl.pallas_call(
        _odd_out_kernel,
        grid=(s // tm,),
        in_specs=[
            pl.BlockSpec((tm, d), lambda i: (i, 0)),
            pl.BlockSpec((tm, C_WIDTH), lambda i: (i, 0)),
            pl.BlockSpec((tm, MLA_HEADS * V_HD), lambda i: (i, 0)),
            _const_spec(mod.shape), _const_spec(nw.shape), _const_spec(w_out.shape), _const_spec(w_router.shape),
            _const_spec((tm, tm)),
        ],
        out_specs=[
            pl.BlockSpec((tm, d), lambda i: (i, 0)),
            pl.BlockSpec((tm, d // 2), lambda i: (i, 0)),
            pl.BlockSpec((tm, LANES), lambda i: (i, 0)),
            pl.BlockSpec((SUBLANES, tm), lambda i: (0, i)),
            pl.BlockSpec((N_EXPERTS, 1), lambda i: (0, 0)),
        ],
        out_shape=[
            jax.ShapeDtypeStruct((s, d), F32),
            jax.ShapeDtypeStruct((s, d // 2), jnp.uint32),
            jax.ShapeDtypeStruct((s, LANES), F32),
            jax.ShapeDtypeStruct((SUBLANES, s), jnp.int32),
            jax.ShapeDtypeStruct((N_EXPERTS, 1), F32),
        ],
        compiler_params=_params(("arbitrary",)),
        name="odd_out_router",
    )(x, yc, yd, mod, nw, w_out, w_router, jnp.triu(jnp.ones((tm, tm), BF16), 1))


def _sc_workers():
    info = plsc.get_sparse_core_info()
    return info.num_cores, info.num_cores * info.num_subcores


def _sc_scatter_rows(x, idx0, idx1, out_rows):
    n, w = x.shape
    nc, nw = _sc_workers()
    per_w = n // nw
    nch = per_w // SC_ROWS
    assert nch % 2 == 0 and nch >= 2
    mesh = plsc.VectorSubcoreMesh(core_axis_name="c", subcore_axis_name="s")

    @functools.partial(
        pl.kernel, mesh=mesh, out_type=jax.ShapeDtypeStruct((out_rows, w), x.dtype),
        scratch_types=[pltpu.VMEM((nch, SC_ROWS), jnp.int32), pltpu.VMEM((nch, SC_ROWS), jnp.int32),
                       pltpu.VMEM((SC_ROWS, w), x.dtype), pltpu.VMEM((SC_ROWS, w), x.dtype),
                       pltpu.SemaphoreType.DMA, pltpu.SemaphoreType.DMA, pltpu.SemaphoreType.DMA,
                       pltpu.SemaphoreType.DMA],
        name="moe_dispatch")
    def scatter(x_hbm, i0_hbm, i1_hbm, out_hbm, i0_v, i1_v, buf0, buf1, lsem0, lsem1, ssem0, ssem1):
        wid = lax.axis_index("s") * nc + lax.axis_index("c")
        pltpu.sync_copy(i0_hbm.at[wid], i0_v)
        pltpu.sync_copy(i1_hbm.at[wid], i1_v)
        base = wid * per_w

        def load(c, buf, sem):
            return pltpu.make_async_copy(x_hbm.at[pl.ds(base + c * SC_ROWS, SC_ROWS)], buf, sem)

        def put(c, buf):
            first = pltpu.make_async_copy(buf, out_hbm.at[i0_v.at[c]], ssem0)
            second = pltpu.make_async_copy(buf, out_hbm.at[i1_v.at[c]], ssem1)
            first.start()
            second.start()
            first.wait()
            second.wait()

        def pair(c, prefetch_next):
            load(c + 1, buf1, lsem1).start()
            load(c, buf0, lsem0).wait()
            put(c, buf0)
            if prefetch_next:
                load(c + 2, buf0, lsem0).start()
            load(c + 1, buf1, lsem1).wait()
            put(c + 1, buf1)

        load(0, buf0, lsem0).start()

        def body(t, carry):
            pair(2 * t, True)
            return carry

        lax.fori_loop(0, nch // 2 - 1, body, 0)
        pair(nch - 2, False)

    return scatter(x, idx0.reshape(nw, nch, SC_ROWS), idx1.reshape(nw, nch, SC_ROWS))


def _sc_gather_rows(table, idx):
    _, w = table.shape
    b = idx.shape[0]
    nc, nw = _sc_workers()
    per_w = b // nw
    nch = per_w // SC_ROWS
    mesh = plsc.VectorSubcoreMesh(core_axis_name="c", subcore_axis_name="s")

    assert nch % 2 == 0 and nch >= 2

    @functools.partial(
        pl.kernel, mesh=mesh, out_type=jax.ShapeDtypeStruct((b, w), table.dtype),
        scratch_types=[pltpu.VMEM((nch, SC_ROWS), jnp.int32), pltpu.VMEM((SC_ROWS, w), table.dtype),
                       pltpu.VMEM((SC_ROWS, w), table.dtype), pltpu.SemaphoreType.DMA, pltpu.SemaphoreType.DMA],
        name="moe_combine_gather")
    def gather(table_hbm, idx_hbm, out_hbm, idx_v, buf0, buf1, sem0, sem1):
        wid = lax.axis_index("s") * nc + lax.axis_index("c")
        pltpu.sync_copy(idx_hbm.at[wid], idx_v)
        base = wid * per_w

        def fetch(c, buf, sem):
            return pltpu.make_async_copy(table_hbm.at[idx_v.at[c]], buf, sem)

        def put(c, buf):
            pltpu.sync_copy(buf, out_hbm.at[pl.ds(base + c * SC_ROWS, SC_ROWS)])

        def pair(c, prefetch_next):
            fetch(c + 1, buf1, sem1).start()
            fetch(c, buf0, sem0).wait()
            put(c, buf0)
            if prefetch_next:
                fetch(c + 2, buf0, sem0).start()
            fetch(c + 1, buf1, sem1).wait()
            put(c + 1, buf1)

        fetch(0, buf0, sem0).start()

        def body(t, carry):
            pair(2 * t, True)
            return carry

        lax.fori_loop(0, nch // 2 - 1, body, 0)
        pair(nch - 2, False)

    return gather(table, idx.reshape(nw, nch, SC_ROWS))


def _moe_kernel(te_ref, nv_ref, rows_ref, xs_ref, wg_ref, wu_ref, wd_ref, ys_ref, x_scr, acc_ref):
    j = pl.program_id(0)
    f = pl.program_id(1)
    nf = pl.num_programs(1)
    tm = xs_ref.shape[0]

    def expert_ffn(rows):
        h = x_scr[:rows, :]
        g = jnp.dot(h, wg_ref[0], preferred_element_type=F32)
        u = jnp.dot(h, wu_ref[0], preferred_element_type=F32)
        act = (g * jax.nn.sigmoid(g) * u).astype(BF16)
        y = jnp.dot(act, wd_ref[0], preferred_element_type=F32)

        @pl.when(f == 0)
        def _():
            acc_ref[:rows, :] = y

        @pl.when((f > 0) & (f < nf - 1))
        def _():
            acc_ref[:rows, :] += y

        @pl.when(f == nf - 1)
        def _():
            ys_ref[:rows, :] = _pack_bf16_pairs(acc_ref[:rows, :] + y)

    @pl.when(j < nv_ref[0])
    def _():
        @pl.when(f == 0)
        def _():
            x_scr[...] = _unpack_bf16_pairs(xs_ref[...]).astype(BF16)

        for rows in range(MOE_ROW_STEP, tm + 1, MOE_ROW_STEP):
            pl.when(rows_ref[j] == rows)(functools.partial(expert_ffn, rows))


def _moe_grouped(xs, tile_expert, n_valid, tile_rows, w_gu, w_down, tm, tf=1792):
    p_rows, dh = xs.shape
    d = 2 * dh
    ne, _, ff2 = w_gu.shape
    ff = ff2 // 2
    nf = ff // tf
    assert nf >= 2 and nf * tf == ff

    def tile(j, nv):
        return jnp.minimum(j, nv[0] - 1)

    def chunk(j, f, nv):
        return jnp.where(j < nv[0], f, nf - 1)

    grid_spec = pltpu.PrefetchScalarGridSpec(
        num_scalar_prefetch=3,
        grid=(p_rows // tm, nf),
        in_specs=[
            pl.BlockSpec((tm, dh), lambda j, f, te, nv, tr: (tile(j, nv), 0)),
            pl.BlockSpec((1, d, tf), lambda j, f, te, nv, tr: (te[tile(j, nv)], 0, chunk(j, f, nv))),
            pl.BlockSpec((1, d, tf), lambda j, f, te, nv, tr: (te[tile(j, nv)], 0, nf + chunk(j, f, nv))),
            pl.BlockSpec((1, tf, d), lambda j, f, te, nv, tr: (te[tile(j, nv)], chunk(j, f, nv), 0)),
        ],
        out_specs=pl.BlockSpec((tm, dh), lambda j, f, te, nv, tr: (tile(j, nv), 0)),
        scratch_shapes=[pltpu.VMEM((tm, d), BF16), pltpu.VMEM((tm, d), F32)],
    )
    return pl.pallas_call(
        _moe_kernel,
        grid_spec=grid_spec,
        out_shape=jax.ShapeDtypeStruct((p_rows, dh), jnp.uint32),
        compiler_params=_params(("arbitrary", "arbitrary")),
        name="moe_experts",
    )(tile_expert, n_valid, tile_rows, xs, w_gu, w_gu, w_down)


def _moe_combine_kernel(x_ref, y0_ref, y1_ref, rw_ref, mod_ref, *rest):
    o_ref = rest[-1]
    d = D_MODEL
    w1 = rw_ref[:, 0:1]
    w2 = rw_ref[:, 1:2]
    y = w1 * _unpack_bf16_pairs(y0_ref[...]) + w2 * _unpack_bf16_pairs(y1_ref[...])
    o_ref[...] = x_ref[...] + mod_ref[:, 5 * d:6 * d] * y


def _moe_combine(x, yg, rw, mod, prev, part, n_parts, tm=512):
    s, d = x.shape
    nb = s // tm // n_parts
    first = part * nb
    in_specs = [
        pl.BlockSpec((tm, d), lambda i: (first + i, 0)),
        pl.BlockSpec((tm, d // 2), lambda i: (i, 0)),
        pl.BlockSpec((tm, d // 2), lambda i: (nb + i, 0)),
        pl.BlockSpec((tm, LANES), lambda i: (first + i, 0)),
        _const_spec(mod.shape),
    ]
    args = [x, yg, yg, rw, mod]
    aliases = {}
    if prev is not None:
        in_specs.append(pl.BlockSpec(memory_space=pl.ANY))
        args.append(prev)
        aliases = {len(args) - 1: 0}
    return pl.pallas_call(
        _moe_combine_kernel,
        grid=(nb,),
        in_specs=in_specs,
        out_specs=pl.BlockSpec((tm, d), lambda i: (first + i, 0)),
        out_shape=jax.ShapeDtypeStruct((s, d), F32),
        input_output_aliases=aliases,
        compiler_params=_params(("arbitrary",)),
        name="moe_combine",
    )(*args)


def _moe_sparse(x, h_packed, rw, ridx, counts, mod, w_gu, w_down, tm=MOE_TILE):
    s = x.shape[0]
    ne = w_gu.shape[0]
    n_tiles = (2 * s) // tm + ne
    cnt = counts[:ne, 0].astype(jnp.int32)
    padded = ((cnt + tm - 1) // tm) * tm
    ends = jnp.cumsum(padded)
    offs = ends - padded
    experts = jnp.arange(ne, dtype=jnp.int32)
    off_of = lambda e: jnp.sum(jnp.where(e[:, None] == experts[None, :], offs[None, :], 0), axis=1)
    pos0 = off_of(ridx[0]) + ridx[2]
    pos1 = off_of(ridx[1]) + ridx[3]
    tile_start = jnp.arange(n_tiles, dtype=jnp.int32) * tm
    tile_expert = jnp.minimum(jnp.sum(tile_start[:, None] >= ends[None, :], axis=1), ne - 1).astype(jnp.int32)
    n_valid = (ends[-1] // tm).reshape(1).astype(jnp.int32)
    filled = jnp.clip((offs + cnt)[tile_expert] - tile_start, 0, tm)
    tile_rows = (((filled + MOE_ROW_STEP - 1) // MOE_ROW_STEP) * MOE_ROW_STEP).astype(jnp.int32)
    xs = _sc_scatter_rows(h_packed, pos0, pos1, n_tiles * tm)
    ys = _moe_grouped(xs, tile_expert, n_valid, tile_rows, w_gu, w_down, tm)
    out = None
    sp = s // COMBINE_PARTS
    for part in range(COMBINE_PARTS):
        rows = slice(part * sp, (part + 1) * sp)
        yg = _sc_gather_rows(ys, jnp.concatenate([pos0[rows], pos1[rows]]))
        out = _moe_combine(x, yg, rw, mod, out, part, COMBINE_PARTS)
    return out


def kernel(x, c, positions, norm_mix_w, norm_ffn_w, ada_w, ada_b, e_w_in, a_ln_w, a_ln_b, a_w_s, a_b_s, b_w_grp,
           b_scale, e_w_out, ffn_w_gu, ffn_w_down, o_w_in, c_conv_w, q_a_norm, w_uq, kv_norm, w_ukv, q_norm_w,
           k_norm_w, o_w_out, router_w, moe_w_gu, moe_w_down):
    bsz, s, d = x.shape
    assert bsz == 1 and d == D_MODEL
    depth = ada_w.shape[0]
    nh = MLA_HEADS
    xs = x.reshape(s, d)
    mod = _ada_mod(c, ada_w, ada_b)
    cos_t, sin_t = _rope_tables(positions)
    row = lambda a: a.reshape(1, -1)

    for layer in range(depth):
        i = layer // 2
        m = mod[layer]
        if layer % 2 == 0:
            has_next = layer + 1 < depth
            dummy = jnp.zeros((s // 512, 16, LANES), F32)
            xs, moe_w_down_bf16, (ffn_w_gu_bf16, ffn_w_down_bf16) = _even_mix(
                xs, m, row(norm_mix_w[layer]), e_w_in[i].astype(BF16), row(a_ln_w[i]), row(a_ln_b[i]), a_w_s[i],
                a_b_s[i].reshape(A_GROUPS, CHUNK, 1), b_w_grp[i].astype(BF16), row(b_scale[i]),
                e_w_out[i].astype(BF16), moe_w_down[i] if has_next else dummy, [ffn_w_gu[i], ffn_w_down[i]])
            xs, moe_w_gu_bf16 = _ffn(xs, m, row(norm_ffn_w[layer]), ffn_w_gu_bf16, ffn_w_down_bf16,
                                     moe_w_gu[i] if has_next else dummy)
        else:
            o2 = 3 * C_WIDTH + Q_LORA + KV_LORA
            w_in = o_w_in[i][:, :o2].astype(BF16)
            w_kpe = jnp.pad(o_w_in[i][:, o2:], ((0, 0), (0, LANES - QK_ROPE))).astype(BF16)
            bound = (1.02 * QK_HD * QK_HD ** -0.5 * LOG2E) * jnp.max(jnp.abs(q_norm_w[i])) * jnp.max(jnp.abs(k_norm_w[i]))
            wkv = w_ukv[i].reshape(KV_LORA, nh, QK_NOPE + V_HD)
            w_k = wkv[:, :, :QK_NOPE].reshape(KV_LORA, nh * QK_NOPE).astype(BF16)
            w_v_t = wkv[:, :, QK_NOPE:].reshape(KV_LORA, nh * V_HD).T.astype(BF16)
            yc, qt, k, vt = _odd_prep(
                xs, m, row(norm_mix_w[layer]), w_in, w_kpe, c_conv_w[i], row(q_a_norm[i]), w_uq[i].T.astype(BF16),
                row(kv_norm[i]), w_k, w_v_t, q_norm_w[i].reshape(QK_HD, 1), row(k_norm_w[i]),
                k_norm_w[i].reshape(QK_HD, 1), jnp.full((1, LANES), -bound, F32), cos_t, sin_t)
            yd = lax.cond(bound <= ATTN_MAX_BOUND, _attention_shifted, _attention, qt, k, vt)
            wr = jnp.pad(router_w[i], ((0, 0), (0, LANES - N_EXPERTS)))
            wr_hi = lax.reduce_precision(wr, exponent_bits=8, mantissa_bits=7)
            w_router = jnp.concatenate([wr_hi, wr - wr_hi], axis=1).astype(BF16)
            xs, hp, rw, ridx, counts = _odd_out(xs, yc, yd, m, row(norm_ffn_w[layer]), o_w_out[i].astype(BF16),
                                                w_router)
            xs = _moe_sparse(xs, hp, rw, ridx, counts, m, moe_w_gu_bf16, moe_w_down_bf16)
    return xs.reshape(bsz, s, d)
```

```python
import functools

import jax
import jax.numpy as jnp
from jax import lax
from jax.experimental import pallas as pl
from jax.experimental.pallas import tpu as pltpu
from jax.experimental.pallas import tpu_sc as plsc

D_MODEL = 1024
SEQ = 16384
EPS = 1e-6
CHUNK = 128
A_WIDTH = 512
A_GROUPS = 4
B_WIDTH = 512
POOL_WINDOWS = (2, 4, 8, 16)
B_HD = 128
C_WIDTH = 512
MLA_HEADS = 4
Q_LORA = 256
KV_LORA = 256
QK_NOPE = 128
QK_ROPE = 64
QK_HD = QK_NOPE + QK_ROPE
V_HD = 128
ROPE_THETA = 10000.0
D_FF = 2816
N_EXPERTS = 8
D_FF_EXPERT = 3584

LANES = 128
SUBLANES = 8
POOL_HALO = 16
CONV_HALO = 8
QK_PAD = 256
V_PAD = 144
LOG2E = 1.4426950408889634
MASK_VALUE = -1e30
MOE_TILE = 512
CONV_COLS = 256
ATTN_MAX_BOUND = 48.0
MOE_ROW_STEP = 128
COMBINE_PARTS = 4
SC_ROWS = 64
VMEM_LIMIT = 56 * 1024 * 1024
MOE_VMEM_LIMIT = 62 * 1024 * 1024

F32 = jnp.float32
BF16 = jnp.bfloat16


def _params(sem, vmem=VMEM_LIMIT, flags=None):
    return pltpu.CompilerParams(dimension_semantics=sem, vmem_limit_bytes=vmem, flags=flags)


def _const_spec(shape, single=False):
    nd = len(shape)
    return pl.BlockSpec(shape, lambda *_: (0,) * nd, pipeline_mode=pl.Buffered(1) if single else None)


def _rms_mod(x, nw, sc, sh):
    ms = jnp.mean(x * x, axis=-1, keepdims=True)
    return (x * lax.rsqrt(ms + EPS)) * (nw * (1.0 + sc)) + sh


def _ada_kernel(c_ref, w_ref, b_ref, o_ref):
    c = c_ref[...]
    ca = c * jax.nn.sigmoid(c)
    o_ref[0] = jnp.sum(w_ref[0] * ca, axis=0, keepdims=True) + b_ref[0]


def _ada_mod(c, ada_w, ada_b):
    depth, d, n = ada_w.shape
    tn = 3072
    return pl.pallas_call(
        _ada_kernel,
        grid=(depth, n // tn),
        in_specs=[
            pl.BlockSpec((d, 1), lambda l, j: (0, 0)),
            pl.BlockSpec((1, d, tn), lambda l, j: (l, 0, j)),
            pl.BlockSpec((1, 1, tn), lambda l, j: (l, 0, j)),
        ],
        out_specs=pl.BlockSpec((1, 1, tn), lambda l, j: (l, 0, j)),
        out_shape=jax.ShapeDtypeStruct((depth, 1, n), F32),
        compiler_params=_params(("arbitrary", "arbitrary")),
        name="ada_mod",
    )(c.reshape(d, 1), ada_w, ada_b.reshape(depth, 1, n))


def _even_mix_kernel(x_ref, mod_ref, nw_ref, win_ref, lnw_ref, lnb_ref, ws_ref, bs_ref, wg_ref, bsc_ref,
                     wout_ref, *rest):
    n_sides = (len(rest) - 4) // 2
    o_ref = rest[n_sides]
    halo_ref, sv_ref, yb_ref = rest[-3:]
    for side_ref, side_out_ref in zip(rest[:n_sides], rest[n_sides + 1:2 * n_sides + 1]):
        side_out_ref[...] = side_ref[...].astype(BF16)
    tm = x_ref.shape[0]
    i = pl.program_id(0)

    @pl.when(i == 0)
    def _():
        halo_ref[...] = jnp.zeros_like(halo_ref)

    d = D_MODEL
    x = x_ref[...]
    sh, sc, gate = mod_ref[:, 0:d], mod_ref[:, d:2 * d], mod_ref[:, 2 * d:3 * d]
    h = _rms_mod(x, nw_ref[...], sc, sh).astype(BF16)
    p = jnp.dot(h, win_ref[...], preferred_element_type=F32)

    gl = jax.nn.gelu(p[:, :2 * A_WIDTH])
    u = gl[:, :A_WIDTH]
    v = gl[:, A_WIDTH:]
    mu = jnp.mean(v, axis=-1, keepdims=True)
    vc = v - mu
    var = jnp.mean(vc * vc, axis=-1, keepdims=True)
    vn = (vc * lax.rsqrt(var + EPS) * lnw_ref[...] + lnb_ref[...]).astype(BF16)
    row = lax.broadcasted_iota(jnp.int32, (CHUNK, CHUNK), 0)
    col = lax.broadcasted_iota(jnp.int32, (CHUNK, CHUNK), 1)
    for g in range(A_GROUPS):
        w = jnp.where(col <= row, ws_ref[g], 0.0).astype(BF16)
        b = bs_ref[g]
        for c in range(tm // CHUNK):
            blk = vn[c * CHUNK:(c + 1) * CHUNK, g * LANES:(g + 1) * LANES]
            sv_ref[c * CHUNK:(c + 1) * CHUNK, g * LANES:(g + 1) * LANES] = (
                jnp.dot(w, blk, preferred_element_type=F32) + b)
    ya = (u * sv_ref[...]).astype(BF16)

    pb = p[:, 2 * A_WIDTH:]
    ext = jnp.concatenate([halo_ref[...], pb], axis=0)
    halo_ref[...] = pb[tm - POOL_HALO:, :]
    t_glob = i * tm + lax.broadcasted_iota(jnp.int32, (tm, 1), 0)
    s = ext
    width = 1
    for g, win in enumerate(POOL_WINDOWS):
        while width < win:
            s = s + pltpu.roll(s, width, 0)
            width *= 2
        cnt = jnp.minimum(t_glob + 1, win).astype(F32)
        sl = slice(g * B_HD, (g + 1) * B_HD)
        pooled = s[POOL_HALO:, sl] / cnt
        dg = (pooled - pb[:, sl]).astype(BF16)
        yb_ref[:, sl] = jnp.dot(dg, wg_ref[g], preferred_element_type=F32)
    yb = (yb_ref[...] * bsc_ref[...]).astype(BF16)

    mix = (jnp.dot(ya, wout_ref[:A_WIDTH, :], preferred_element_type=F32)
           + jnp.dot(yb, wout_ref[A_WIDTH:, :], preferred_element_type=F32))
    o_ref[...] = x + gate * mix


def _side_cast_spec(side, steps, axis):
    ne = side.shape[0]
    parts = steps // ne
    assert parts * ne == steps and side.shape[axis] % parts == 0
    block = list(side.shape)
    block[0] = 1
    block[axis] //= parts
    assert block[1] % 16 == 0 and block[2] % LANES == 0
    if axis == 1:
        return pl.BlockSpec(tuple(block), lambda i: (i // parts, i % parts, 0))
    return pl.BlockSpec(tuple(block), lambda i: (i // parts, 0, i % parts))


def _row_cast_spec(w, steps):
    rows = -(-w.shape[0] // steps)
    rows = -(-rows // 16) * 16
    last = -(-w.shape[0] // rows) - 1
    return pl.BlockSpec((rows, w.shape[1]), lambda i: (jnp.minimum(i, last), 0))


def _even_mix(x, mod, nw, w_in, ln_w, ln_b, w_s, b_s, w_grp, b_scale, w_out, side, row_sides, tm=512):
    s, d = x.shape
    steps = s // tm
    sides = [side, *row_sides]
    side_specs = [_side_cast_spec(side, steps, 1)] + [_row_cast_spec(w, steps) for w in row_sides]
    outs = pl.pallas_call(
        _even_mix_kernel,
        grid=(steps,),
        in_specs=[
            pl.BlockSpec((tm, d), lambda i: (i, 0)),
            _const_spec(mod.shape), _const_spec(nw.shape), _const_spec(w_in.shape),
            _const_spec(ln_w.shape), _const_spec(ln_b.shape), _const_spec(w_s.shape), _const_spec(b_s.shape),
            _const_spec(w_grp.shape), _const_spec(b_scale.shape), _const_spec(w_out.shape), *side_specs,
        ],
        out_specs=[pl.BlockSpec((tm, d), lambda i: (i, 0)), *side_specs],
        out_shape=[jax.ShapeDtypeStruct((s, d), F32)] + [jax.ShapeDtypeStruct(w.shape, BF16) for w in sides],
        scratch_shapes=[pltpu.VMEM((POOL_HALO, B_WIDTH), F32), pltpu.VMEM((tm, A_WIDTH), F32),
                        pltpu.VMEM((tm, B_WIDTH), F32)],
        compiler_params=_params(("arbitrary",)),
        name="even_mix",
    )(x, mod, nw, w_in, ln_w, ln_b, w_s, b_s, w_grp, b_scale, w_out, *sides)
    return outs[0], outs[1], outs[2:]


def _ffn_kernel(x_ref, mod_ref, nw_ref, wgu_ref, wd_ref, side_ref, o_ref, side_out_ref, *, n_chunks):
    side_out_ref[...] = side_ref[...].astype(BF16)
    d = D_MODEL
    x = x_ref[...]
    sh, sc, gate = mod_ref[:, 3 * d:4 * d], mod_ref[:, 4 * d:5 * d], mod_ref[:, 5 * d:6 * d]
    h = _rms_mod(x, nw_ref[...], sc, sh).astype(BF16)
    ff = wd_ref.shape[0]
    tf = ff // n_chunks
    acc = jnp.zeros(x.shape, F32)
    for f in range(n_chunks):
        g = jnp.dot(h, wgu_ref[:, f * tf:(f + 1) * tf], preferred_element_type=F32)
        u = jnp.dot(h, wgu_ref[:, ff + f * tf:ff + (f + 1) * tf], preferred_element_type=F32)
        act = (g * jax.nn.sigmoid(g) * u).astype(BF16)
        acc = acc + jnp.dot(act, wd_ref[f * tf:(f + 1) * tf, :], preferred_element_type=F32)
    o_ref[...] = x + gate * acc


def _ffn(x, mod, nw, w_gu, w_down, side, tm=512, n_chunks=11):
    s, d = x.shape
    steps = s // tm
    side_spec = _side_cast_spec(side, steps, 2)
    return pl.pallas_call(
        functools.partial(_ffn_kernel, n_chunks=n_chunks),
        grid=(steps,),
        in_specs=[
            pl.BlockSpec((tm, d), lambda i: (i, 0)),
            _const_spec(mod.shape), _const_spec(nw.shape), _const_spec(w_gu.shape, single=True),
            _const_spec(w_down.shape, single=True), side_spec,
        ],
        out_specs=[pl.BlockSpec((tm, d), lambda i: (i, 0)), side_spec],
        out_shape=[jax.ShapeDtypeStruct((s, d), F32), jax.ShapeDtypeStruct(side.shape, BF16)],
        compiler_params=_params(("arbitrary",)),
        name="ffn",
    )(x, mod, nw, w_gu, w_down, side)


def _rope_table_kernel(pos_ref, invf_ref, cos_ref, sin_ref):
    ang = pos_ref[...].astype(F32) * invf_ref[...]
    cos_ref[...] = jnp.cos(ang)
    sin_ref[...] = jnp.sin(ang)


def _rope_tables(positions):
    s = positions.shape[-1]
    half = QK_ROPE // 2
    inv_freq = ROPE_THETA ** (-jnp.arange(0, QK_ROPE, 2, dtype=F32) / QK_ROPE)
    return pl.pallas_call(
        _rope_table_kernel,
        out_shape=(jax.ShapeDtypeStruct((half, s), F32), jax.ShapeDtypeStruct((half, s), F32)),
        name="rope_tables",
    )(positions.reshape(1, s), inv_freq.reshape(half, 1))


def _odd_prep_kernel(x_ref, mod_ref, nw_ref, win_ref, wkpe_ref, cw_ref, qan_ref, wuqt_ref, kvn_ref, wk_ref, wvt_ref,
                     qnw_ref, knw_ref, knwc_ref, shift_ref, cost_ref, sint_ref, yc_ref, qt_ref, k_ref, vt_ref, halo_ref):
    tm = x_ref.shape[0]
    i = pl.program_id(0)

    @pl.when(i == 0)
    def _():
        halo_ref[...] = jnp.zeros_like(halo_ref)

    d = D_MODEL
    x = x_ref[...]
    sh, sc = mod_ref[:, 0:d], mod_ref[:, d:2 * d]
    h = _rms_mod(x, nw_ref[...], sc, sh).astype(BF16)
    cw = C_WIDTH
    o0 = 3 * cw
    proj = lambda lo, hi: jnp.dot(h, win_ref[:, lo:hi], preferred_element_type=F32)
    kpe_pad = jnp.dot(h, wkpe_ref[...], preferred_element_type=F32)
    kpe = kpe_pad[:, :QK_ROPE]

    for c0 in range(0, cw, CONV_COLS):
        cols = slice(c0, c0 + CONV_COLS)
        z = proj(cw + c0, cw + c0 + CONV_COLS) * proj(2 * cw + c0, 2 * cw + c0 + CONV_COLS)
        ext = jnp.concatenate([halo_ref[:, cols], z], axis=0)
        halo_ref[:, cols] = z[tm - CONV_HALO:, :]
        z1 = pltpu.roll(ext, 1, 0)[CONV_HALO:, :]
        z2 = pltpu.roll(ext, 2, 0)[CONV_HALO:, :]
        conv = cw_ref[0:1, cols] * z2 + cw_ref[1:2, cols] * z1 + cw_ref[2:3, cols] * z
        yc_ref[:, cols] = (proj(c0, c0 + CONV_COLS) * conv).astype(BF16)

    cq = proj(o0, o0 + Q_LORA)
    ckv = proj(o0 + Q_LORA, o0 + Q_LORA + KV_LORA)
    cqn = cq * lax.rsqrt(jnp.mean(cq * cq, axis=-1, keepdims=True) + EPS) * qan_ref[...]
    ckvn = ckv * lax.rsqrt(jnp.mean(ckv * ckv, axis=-1, keepdims=True) + EPS) * kvn_ref[...]
    cqn_t = cqn.T.astype(BF16)
    ckvn_t = ckvn.T.astype(BF16)
    q_t = jnp.dot(wuqt_ref[...], cqn_t, preferred_element_type=F32)
    v_t = jnp.dot(wvt_ref[...], ckvn_t, preferred_element_type=F32)
    kn_all = jnp.dot(ckvn.astype(BF16), wk_ref[...], preferred_element_type=F32)

    nh = MLA_HEADS
    half = QK_ROPE // 2
    sm_scale = QK_HD ** -0.5
    cos_t, sin_t = cost_ref[...], sint_ref[...]
    kn_w = knw_ref[:, :QK_NOPE]
    kw_t = kpe_pad.T[:QK_ROPE, :] * knwc_ref[QK_NOPE:, :]
    k1, k2 = kw_t[:half, :], kw_t[half:, :]
    kr_t = jnp.concatenate([k1 * cos_t - k2 * sin_t, k2 * cos_t + k1 * sin_t,
                            jnp.zeros((LANES - QK_ROPE, tm), F32)], axis=0)
    kr = kr_t.T[:, :QK_ROPE]
    kpe_ss = jnp.sum(kpe * kpe, axis=-1, keepdims=True)
    npad = QK_PAD - QK_HD
    zrows = jnp.where(lax.broadcasted_iota(jnp.int32, (npad, tm), 0) == 0, shift_ref[:, 0:1], 0.0)
    zpad = jnp.where(lax.broadcasted_iota(jnp.int32, (tm, npad), 1) == 0, 1.0, 0.0)
    ones_rows = (lax.broadcasted_iota(jnp.int32, (V_PAD - V_HD, tm), 0) == 0).astype(BF16)
    for hd in range(nh):
        qh = q_t[hd * QK_HD:(hd + 1) * QK_HD, :]
        q_inv = lax.rsqrt(jnp.sum(qh * qh, axis=0, keepdims=True) / QK_HD + EPS) * (sm_scale * LOG2E)
        qw = qh * qnw_ref[...]
        x1, x2 = qw[QK_NOPE:QK_NOPE + half, :], qw[QK_NOPE + half:, :]
        qt_ref[hd] = jnp.concatenate(
            [qw[:QK_NOPE, :] * q_inv, (x1 * cos_t - x2 * sin_t) * q_inv, (x2 * cos_t + x1 * sin_t) * q_inv, zrows],
            axis=0).astype(BF16)
        kn = kn_all[:, hd * QK_NOPE:(hd + 1) * QK_NOPE]
        k_ss = jnp.sum(kn * kn, axis=-1, keepdims=True) + kpe_ss
        k_inv = lax.rsqrt(k_ss / QK_HD + EPS)
        k_ref[hd] = jnp.concatenate([kn * k_inv * kn_w, kr * k_inv, zpad], axis=1).astype(BF16)
        vt_ref[hd, :V_HD, :] = v_t[hd * V_HD:(hd + 1) * V_HD, :].astype(BF16)
        vt_ref[hd, V_HD:, :] = ones_rows


def _odd_prep(x, mod, nw, w_in, w_kpe, conv_w, q_a_norm, w_uq_t, kv_norm, w_k, w_v_t, q_norm_col, k_norm_w,
              k_norm_col, shift, cos_t, sin_t, tm=1024):
    s, d = x.shape
    nh = MLA_HEADS
    half = QK_ROPE // 2
    consts = [mod, nw, w_in, w_kpe, conv_w, q_a_norm, w_uq_t, kv_norm, w_k, w_v_t, q_norm_col, k_norm_w, k_norm_col,
              shift]
    return pl.pallas_call(
        _odd_prep_kernel,
        grid=(s // tm,),
        in_specs=[pl.BlockSpec((tm, d), lambda i: (i, 0))] + [_const_spec(a.shape) for a in consts] + [
            pl.BlockSpec((half, tm), lambda i: (0, i)), pl.BlockSpec((half, tm), lambda i: (0, i))],
        out_specs=[
            pl.BlockSpec((tm, C_WIDTH), lambda i: (i, 0)),
            pl.BlockSpec((nh, QK_PAD, tm), lambda i: (0, 0, i)),
            pl.BlockSpec((nh, tm, QK_PAD), lambda i: (0, i, 0)),
            pl.BlockSpec((nh, V_PAD, tm), lambda i: (0, 0, i)),
        ],
        out_shape=[
            jax.ShapeDtypeStruct((s, C_WIDTH), BF16),
            jax.ShapeDtypeStruct((nh, QK_PAD, s), BF16),
            jax.ShapeDtypeStruct((nh, s, QK_PAD), BF16),
            jax.ShapeDtypeStruct((nh, V_PAD, s), BF16),
        ],
        scratch_shapes=[pltpu.VMEM((CONV_HALO, C_WIDTH), F32)],
        compiler_params=_params(("arbitrary",)),
        name="odd_prep",
    )(x, *consts, cos_t, sin_t)


def _causal_mask(s, lead=0):
    key = lax.broadcasted_iota(jnp.int32, s.shape, 0)
    qry = lead + lax.broadcasted_iota(jnp.int32, s.shape, 1)
    return jnp.where(key <= qry, s, MASK_VALUE)


def _attn_shifted_kernel(qt_ref, k_ref, vt_ref, o_ref, acc_ref, l_ref, *, tq, tk):
    qi = pl.program_id(1)
    acc_ref[...] = jnp.zeros_like(acc_ref)
    l_ref[...] = jnp.zeros_like(l_ref)
    r = tq // tk

    def block(j, diag_piece=None):
        q0 = 0 if diag_piece is None else diag_piece * tk
        start = pl.multiple_of(j * tk, tk)
        s = jnp.dot(k_ref[0, pl.ds(start, tk), :], qt_ref[0, :, q0:], preferred_element_type=F32)
        if diag_piece is not None:
            s = _causal_mask(s)
        p = jnp.exp2(s)
        l_ref[:, q0:] += jnp.sum(p.reshape(tk // SUBLANES, SUBLANES, p.shape[1]), axis=0)
        acc_ref[:, q0:] += jnp.dot(vt_ref[0, :V_HD, pl.ds(start, tk)], p.astype(BF16), preferred_element_type=F32)

    def body(t, carry):
        for b in range(r):
            block(r * t + b)
        return carry

    lax.fori_loop(0, qi, body, 0)
    for b in range(r):
        block(r * qi + b, diag_piece=b)
    out_t = acc_ref[...] / jnp.sum(l_ref[...], axis=0, keepdims=True)
    o_ref[...] = out_t.T.astype(o_ref.dtype)


def _attention_shifted(qt, k, vt, tq=2048, tk=512):
    nh, _, s = qt.shape
    return pl.pallas_call(
        functools.partial(_attn_shifted_kernel, tq=tq, tk=tk),
        grid=(nh, s // tq),
        in_specs=[
            pl.BlockSpec((1, QK_PAD, tq), lambda h, i: (h, 0, i)),
            pl.BlockSpec((1, s, QK_PAD), lambda h, i: (h, 0, 0)),
            pl.BlockSpec((1, V_PAD, s), lambda h, i: (h, 0, 0)),
        ],
        out_specs=pl.BlockSpec((tq, V_HD), lambda h, i: (i, h)),
        out_shape=jax.ShapeDtypeStruct((s, nh * V_HD), BF16),
        scratch_shapes=[pltpu.VMEM((V_HD, tq), F32), pltpu.VMEM((SUBLANES, tq), F32)],
        compiler_params=_params(("arbitrary", "arbitrary")),
        name="mla_attention_shifted",
    )(qt, k, vt)


def _attn_kernel(qt_ref, k_ref, vt_ref, o_ref, s0_ref, s1_ref, bm0_ref, bm1_ref, m_ref, acc_ref, *, tq, tk):
    qi = pl.program_id(1)
    m_ref[...] = jnp.full_like(m_ref, MASK_VALUE)
    acc_ref[...] = jnp.zeros_like(acc_ref)

    def scores(j, s_ref, bm_ref, diag_offset=None):
        q0 = 0 if diag_offset is None else diag_offset
        start = pl.multiple_of(j * tk, tk)
        s = jnp.dot(k_ref[0, pl.ds(start, tk), :], qt_ref[0, :, q0:], preferred_element_type=F32)
        if diag_offset is not None:
            s = _causal_mask(s)
        s_ref[:, q0:] = s
        bm_ref[:, q0:] = jnp.max(s, axis=0, keepdims=True)

    def consume(j, s_ref, bm_ref, q0=0):
        m_prev = m_ref[:, q0:]
        m_new = jnp.maximum(m_prev, bm_ref[:, q0:])
        alpha = jnp.exp2(m_prev - m_new)
        p = jnp.exp2((s_ref[:, q0:] - m_new).astype(BF16))
        start = pl.multiple_of(j * tk, tk)
        vt = vt_ref[0, :, pl.ds(start, tk)]
        acc_ref[:, q0:] = alpha * acc_ref[:, q0:] + jnp.dot(vt, p, preferred_element_type=F32)
        m_ref[:, q0:] = m_new

    r = tq // tk
    slots = ((s0_ref, bm0_ref), (s1_ref, bm1_ref))

    def trip(j0, then_diagonal):
        for b in range(r):
            first_diag = then_diagonal and b == r - 1
            scores(j0 + b + 1, *slots[(b + 1) % 2], diag_offset=0 if first_diag else None)
            consume(j0 + b, *slots[b % 2])

    def diagonal(jd):
        for b in range(r):
            if b + 1 < r:
                scores(jd + b + 1, *slots[(b + 1) % 2], diag_offset=(b + 1) * tk)
            consume(jd + b, *slots[b % 2], q0=b * tk)

    @pl.when(qi > 0)
    def _():
        scores(0, *slots[0])

    n_plain = jnp.maximum(qi - 1, 0)

    def body(t, carry):
        trip(r * (2 * t), False)
        trip(r * (2 * t + 1), False)
        return carry

    lax.fori_loop(0, n_plain // 2, body, 0)

    @pl.when(n_plain % 2 == 1)
    def _():
        trip(r * (n_plain - 1), False)

    @pl.when(qi > 0)
    def _():
        trip(r * (qi - 1), True)
        diagonal(r * qi)

    @pl.when(qi == 0)
    def _():
        scores(0, *slots[0], diag_offset=0)
        diagonal(0)

    acc = acc_ref[...]
    out_t = acc[:V_HD, :] / acc[V_HD:V_HD + 1, :]
    o_ref[...] = out_t.T.astype(o_ref.dtype)


def _attention(qt, k, vt, tq=1024, tk=512):
    nh, _, s = qt.shape
    assert tq % (2 * tk) == 0
    return pl.pallas_call(
        functools.partial(_attn_kernel, tq=tq, tk=tk),
        grid=(nh, s // tq),
        in_specs=[
            pl.BlockSpec((1, QK_PAD, tq), lambda h, i: (h, 0, i)),
            pl.BlockSpec((1, s, QK_PAD), lambda h, i: (h, 0, 0)),
            pl.BlockSpec((1, V_PAD, s), lambda h, i: (h, 0, 0)),
        ],
        out_specs=pl.BlockSpec((tq, V_HD), lambda h, i: (i, h)),
        out_shape=jax.ShapeDtypeStruct((s, nh * V_HD), BF16),
        scratch_shapes=[pltpu.VMEM((tk, tq), F32), pltpu.VMEM((tk, tq), F32), pltpu.VMEM((1, tq), F32),
                        pltpu.VMEM((1, tq), F32), pltpu.VMEM((1, tq), F32), pltpu.VMEM((V_PAD, tq), F32)],
        compiler_params=_params(("arbitrary", "arbitrary")),
        name="mla_attention",
    )(qt, k, vt)


def _pack_bf16_pairs(x):
    n = x.shape[1] // 2
    lo = pltpu.bitcast(x[:, :n].astype(BF16).astype(F32), jnp.uint32)
    hi = pltpu.bitcast(x[:, n:].astype(BF16).astype(F32), jnp.uint32)
    return (lo >> 16) | hi


def _unpack_bf16_pairs(p):
    lo = pltpu.bitcast(p << 16, F32)
    hi = pltpu.bitcast(p & jnp.uint32(0xFFFF0000), F32)
    return jnp.concatenate([lo, hi], axis=1)


def _odd_out_kernel(x_ref, yc_ref, yd_ref, mod_ref, nw_ref, wout_ref, wr_ref, before_ref, x_out_ref, h_ref, rw_ref,
                    ridx_ref, cnt_ref):
    d = D_MODEL
    x = x_ref[...]
    gate_m = mod_ref[:, 2 * d:3 * d]
    sh, sc = mod_ref[:, 3 * d:4 * d], mod_ref[:, 4 * d:5 * d]
    mix = (jnp.dot(yc_ref[...], wout_ref[:C_WIDTH, :], preferred_element_type=F32)
           + jnp.dot(yd_ref[...], wout_ref[C_WIDTH:, :], preferred_element_type=F32))
    x1 = x + gate_m * mix
    x_out_ref[...] = x1
    h = _rms_mod(x1, nw_ref[...], sc, sh)
    h_ref[...] = _pack_bf16_pairs(h)

    h_hi = h.astype(BF16)
    h_lo = (h - h_hi.astype(F32)).astype(BF16)
    hw = jnp.dot(h_hi, wr_ref[...], preferred_element_type=F32)
    logits = hw[:, :LANES] + (hw[:, LANES:] + jnp.dot(h_lo, wr_ref[:, :LANES], preferred_element_type=F32))
    lt = logits.T[:N_EXPERTS, :]
    ex = lax.broadcasted_iota(jnp.int32, lt.shape, 0)
    m1 = jnp.max(lt, axis=0, keepdims=True)
    i1 = jnp.min(jnp.where(lt == m1, ex, N_EXPERTS), axis=0, keepdims=True)
    rest = jnp.where(ex == i1, -jnp.inf, lt)
    m2 = jnp.max(rest, axis=0, keepdims=True)
    i2 = jnp.min(jnp.where(rest == m2, ex, N_EXPERTS), axis=0, keepdims=True)
    e2 = jnp.exp(m2 - m1)
    w1 = 1.0 / (1.0 + e2)
    w2 = e2 / (1.0 + e2)
    row = lax.broadcasted_iota(jnp.int32, (LANES, lt.shape[1]), 0)
    rw_ref[...] = jnp.where(row == 0, w1, jnp.where(row == 1, w2, 0.0)).T

    i = pl.program_id(0)

    @pl.when(i == 0)
    def _():
        cnt_ref[...] = jnp.zeros_like(cnt_ref)

    onehot = jnp.where((ex == i1) | (ex == i2), 1.0, 0.0)
    prefix = jnp.dot(onehot.astype(BF16), before_ref[...], preferred_element_type=F32) + cnt_ref[...]
    r1 = jnp.sum(jnp.where(ex == i1, prefix, 0.0), axis=0, keepdims=True)
    r2 = jnp.sum(jnp.where(ex == i2, prefix, 0.0), axis=0, keepdims=True)
    cnt_ref[...] += jnp.sum(onehot, axis=1, keepdims=True)
    ridx_ref[...] = jnp.where(ex == 0, i1, jnp.where(ex == 1, i2, jnp.where(
        ex == 2, r1.astype(jnp.int32), jnp.where(ex == 3, r2.astype(jnp.int32), 0))))


def _odd_out(x, yc, yd, mod, nw, w_out, w_router, tm=1024):
    s, d = x.shape
    return pl.pallas_call(
        _odd_out_kernel,
        grid=(s // tm,),
        in_specs=[
            pl.BlockSpec((tm, d), lambda i: (i, 0)),
            pl.BlockSpec((tm, C_WIDTH), lambda i: (i, 0)),
            pl.BlockSpec((tm, MLA_HEADS * V_HD), lambda i: (i, 0)),
            _const_spec(mod.shape), _const_spec(nw.shape), _const_spec(w_out.shape), _const_spec(w_router.shape),
            _const_spec((tm, tm)),
        ],
        out_specs=[
            pl.BlockSpec((tm, d), lambda i: (i, 0)),
            pl.BlockSpec((tm, d // 2), lambda i: (i, 0)),
            pl.BlockSpec((tm, LANES), lambda i: (i, 0)),
            pl.BlockSpec((SUBLANES, tm), lambda i: (0, i)),
            pl.BlockSpec((N_EXPERTS, 1), lambda i: (0, 0)),
        ],
        out_shape=[
            jax.ShapeDtypeStruct((s, d), F32),
            jax.ShapeDtypeStruct((s, d // 2), jnp.uint32),
            jax.ShapeDtypeStruct((s, LANES), F32),
            jax.ShapeDtypeStruct((SUBLANES, s), jnp.int32),
            jax.ShapeDtypeStruct((N_EXPERTS, 1), F32),
        ],
        compiler_params=_params(("arbitrary",)),
        name="odd_out_router",
    )(x, yc, yd, mod, nw, w_out, w_router, jnp.triu(jnp.ones((tm, tm), BF16), 1))


def _sc_workers():
    info = plsc.get_sparse_core_info()
    return info.num_cores, info.num_cores * info.num_subcores


def _sc_scatter_rows(x, idx0, idx1, out_rows):
    n, w = x.shape
    nc, nw = _sc_workers()
    per_w = n // nw
    nch = per_w // SC_ROWS
    assert nch % 2 == 0 and nch >= 2
    mesh = plsc.VectorSubcoreMesh(core_axis_name="c", subcore_axis_name="s")

    @functools.partial(
        pl.kernel, mesh=mesh, out_type=jax.ShapeDtypeStruct((out_rows, w), x.dtype),
        scratch_types=[pltpu.VMEM((nch, SC_ROWS), jnp.int32), pltpu.VMEM((nch, SC_ROWS), jnp.int32),
                       pltpu.VMEM((SC_ROWS, w), x.dtype), pltpu.VMEM((SC_ROWS, w), x.dtype),
                       pltpu.SemaphoreType.DMA, pltpu.SemaphoreType.DMA, pltpu.SemaphoreType.DMA,
                       pltpu.SemaphoreType.DMA],
        name="moe_dispatch")
    def scatter(x_hbm, i0_hbm, i1_hbm, out_hbm, i0_v, i1_v, buf0, buf1, lsem0, lsem1, ssem0, ssem1):
        wid = lax.axis_index("s") * nc + lax.axis_index("c")
        pltpu.sync_copy(i0_hbm.at[wid], i0_v)
        pltpu.sync_copy(i1_hbm.at[wid], i1_v)
        base = wid * per_w

        def load(c, buf, sem):
            return pltpu.make_async_copy(x_hbm.at[pl.ds(base + c * SC_ROWS, SC_ROWS)], buf, sem)

        def put(c, buf):
            first = pltpu.make_async_copy(buf, out_hbm.at[i0_v.at[c]], ssem0)
            second = pltpu.make_async_copy(buf, out_hbm.at[i1_v.at[c]], ssem1)
            first.start()
            second.start()
            first.wait()
            second.wait()

        def pair(c, prefetch_next):
            load(c + 1, buf1, lsem1).start()
            load(c, buf0, lsem0).wait()
            put(c, buf0)
            if prefetch_next:
                load(c + 2, buf0, lsem0).start()
            load(c + 1, buf1, lsem1).wait()
            put(c + 1, buf1)

        load(0, buf0, lsem0).start()

        def body(t, carry):
            pair(2 * t, True)
            return carry

        lax.fori_loop(0, nch // 2 - 1, body, 0)
        pair(nch - 2, False)

    return scatter(x, idx0.reshape(nw, nch, SC_ROWS), idx1.reshape(nw, nch, SC_ROWS))


def _sc_gather_rows(table, idx):
    _, w = table.shape
    b = idx.shape[0]
    nc, nw = _sc_workers()
    per_w = b // nw
    nch = per_w // SC_ROWS
    mesh = plsc.VectorSubcoreMesh(core_axis_name="c", subcore_axis_name="s")

    assert nch % 2 == 0 and nch >= 2

    @functools.partial(
        pl.kernel, mesh=mesh, out_type=jax.ShapeDtypeStruct((b, w), table.dtype),
        scratch_types=[pltpu.VMEM((nch, SC_ROWS), jnp.int32), pltpu.VMEM((SC_ROWS, w), table.dtype),
                       pltpu.VMEM((SC_ROWS, w), table.dtype), pltpu.SemaphoreType.DMA, pltpu.SemaphoreType.DMA],
        name="moe_combine_gather")
    def gather(table_hbm, idx_hbm, out_hbm, idx_v, buf0, buf1, sem0, sem1):
        wid = lax.axis_index("s") * nc + lax.axis_index("c")
        pltpu.sync_copy(idx_hbm.at[wid], idx_v)
        base = wid * per_w

        def fetch(c, buf, sem):
            return pltpu.make_async_copy(table_hbm.at[idx_v.at[c]], buf, sem)

        def put(c, buf):
            pltpu.sync_copy(buf, out_hbm.at[pl.ds(base + c * SC_ROWS, SC_ROWS)])

        def pair(c, prefetch_next):
            fetch(c + 1, buf1, sem1).start()
            fetch(c, buf0, sem0).wait()
            put(c, buf0)
            if prefetch_next:
                fetch(c + 2, buf0, sem0).start()
            fetch(c + 1, buf1, sem1).wait()
            put(c + 1, buf1)

        fetch(0, buf0, sem0).start()

        def body(t, carry):
            pair(2 * t, True)
            return carry

        lax.fori_loop(0, nch // 2 - 1, body, 0)
        pair(nch - 2, False)

    return gather(table, idx.reshape(nw, nch, SC_ROWS))


def _moe_kernel(te_ref, nv_ref, rows_ref, xs_ref, wg_ref, wu_ref, wd_ref, ys_ref, *, n_chunks):
    j = pl.program_id(0)
    tm = xs_ref.shape[0]
    tf = wd_ref.shape[1] // n_chunks

    def expert_ffn(rows):
        h = _unpack_bf16_pairs(xs_ref[:rows, :]).astype(BF16)
        y = None
        for c in range(n_chunks):
            cols = slice(c * tf, (c + 1) * tf)
            g = jnp.dot(h, wg_ref[0, :, cols], preferred_element_type=F32)
            u = jnp.dot(h, wu_ref[0, :, cols], preferred_element_type=F32)
            act = (g * jax.nn.sigmoid(g) * u).astype(BF16)
            yc = jnp.dot(act, wd_ref[0, cols, :], preferred_element_type=F32)
            y = yc if y is None else y + yc
        ys_ref[:rows, :] = _pack_bf16_pairs(y)

    @pl.when(j < nv_ref[0])
    def _():
        for rows in range(MOE_ROW_STEP, tm + 1, MOE_ROW_STEP):
            pl.when(rows_ref[j] == rows)(functools.partial(expert_ffn, rows))


def _moe_grouped(xs, tile_expert, n_valid, tile_rows, w_gu, w_down, tm, n_chunks=2):
    p_rows, dh = xs.shape
    d = 2 * dh
    ne, _, ff2 = w_gu.shape
    ff = ff2 // 2
    assert ff % (n_chunks * LANES) == 0

    def tile(j, nv):
        return jnp.minimum(j, nv[0] - 1)

    grid_spec = pltpu.PrefetchScalarGridSpec(
        num_scalar_prefetch=3,
        grid=(p_rows // tm,),
        in_specs=[
            pl.BlockSpec((tm, dh), lambda j, te, nv, tr: (tile(j, nv), 0)),
            pl.BlockSpec((1, d, ff), lambda j, te, nv, tr: (te[tile(j, nv)], 0, 0)),
            pl.BlockSpec((1, d, ff), lambda j, te, nv, tr: (te[tile(j, nv)], 0, 1)),
            pl.BlockSpec((1, ff, d), lambda j, te, nv, tr: (te[tile(j, nv)], 0, 0)),
        ],
        out_specs=pl.BlockSpec((tm, dh), lambda j, te, nv, tr: (tile(j, nv), 0)),
    )
    return pl.pallas_call(
        functools.partial(_moe_kernel, n_chunks=n_chunks),
        grid_spec=grid_spec,
        out_shape=jax.ShapeDtypeStruct((p_rows, dh), jnp.uint32),
        compiler_params=_params(("arbitrary",), vmem=MOE_VMEM_LIMIT),
        name="moe_experts",
    )(tile_expert, n_valid, tile_rows, xs, w_gu, w_gu, w_down)


def _moe_combine_kernel(x_ref, y0_ref, y1_ref, rw_ref, mod_ref, *rest):
    o_ref = rest[-1]
    d = D_MODEL
    w1 = rw_ref[:, 0:1]
    w2 = rw_ref[:, 1:2]
    y = w1 * _unpack_bf16_pairs(y0_ref[...]) + w2 * _unpack_bf16_pairs(y1_ref[...])
    o_ref[...] = x_ref[...] + mod_ref[:, 5 * d:6 * d] * y


def _moe_combine(x, yg, rw, mod, prev, part, n_parts, tm=512):
    s, d = x.shape
    nb = s // tm // n_parts
    first = part * nb
    in_specs = [
        pl.BlockSpec((tm, d), lambda i: (first + i, 0)),
        pl.BlockSpec((tm, d // 2), lambda i: (i, 0)),
        pl.BlockSpec((tm, d // 2), lambda i: (nb + i, 0)),
        pl.BlockSpec((tm, LANES), lambda i: (first + i, 0)),
        _const_spec(mod.shape),
    ]
    args = [x, yg, yg, rw, mod]
    aliases = {}
    if prev is not None:
        in_specs.append(pl.BlockSpec(memory_space=pl.ANY))
        args.append(prev)
        aliases = {len(args) - 1: 0}
    return pl.pallas_call(
        _moe_combine_kernel,
        grid=(nb,),
        in_specs=in_specs,
        out_specs=pl.BlockSpec((tm, d), lambda i: (first + i, 0)),
        out_shape=jax.ShapeDtypeStruct((s, d), F32),
        input_output_aliases=aliases,
        compiler_params=_params(("arbitrary",)),
        name="moe_combine",
    )(*args)


def _moe_sparse(x, h_packed, rw, ridx, counts, mod, w_gu, w_down, tm=MOE_TILE):
    s = x.shape[0]
    ne = w_gu.shape[0]
    n_tiles = (2 * s) // tm + ne
    cnt = counts[:ne, 0].astype(jnp.int32)
    padded = ((cnt + tm - 1) // tm) * tm
    ends = jnp.cumsum(padded)
    offs = ends - padded
    experts = jnp.arange(ne, dtype=jnp.int32)
    off_of = lambda e: jnp.sum(jnp.where(e[:, None] == experts[None, :], offs[None, :], 0), axis=1)
    pos0 = off_of(ridx[0]) + ridx[2]
    pos1 = off_of(ridx[1]) + ridx[3]
    tile_start = jnp.arange(n_tiles, dtype=jnp.int32) * tm
    tile_expert = jnp.minimum(jnp.sum(tile_start[:, None] >= ends[None, :], axis=1), ne - 1).astype(jnp.int32)
    n_valid = (ends[-1] // tm).reshape(1).astype(jnp.int32)
    filled = jnp.clip((offs + cnt)[tile_expert] - tile_start, 0, tm)
    tile_rows = (((filled + MOE_ROW_STEP - 1) // MOE_ROW_STEP) * MOE_ROW_STEP).astype(jnp.int32)
    xs = _sc_scatter_rows(h_packed, pos0, pos1, n_tiles * tm)
    ys = _moe_grouped(xs, tile_expert, n_valid, tile_rows, w_gu, w_down, tm)
    out = None
    sp = s // COMBINE_PARTS
    for part in range(COMBINE_PARTS):
        rows = slice(part * sp, (part + 1) * sp)
        yg = _sc_gather_rows(ys, jnp.concatenate([pos0[rows], pos1[rows]]))
        out = _moe_combine(x, yg, rw, mod, out, part, COMBINE_PARTS)
    return out


def kernel(x, c, positions, norm_mix_w, norm_ffn_w, ada_w, ada_b, e_w_in, a_ln_w, a_ln_b, a_w_s, a_b_s, b_w_grp,
           b_scale, e_w_out, ffn_w_gu, ffn_w_down, o_w_in, c_conv_w, q_a_norm, w_uq, kv_norm, w_ukv, q_norm_w,
           k_norm_w, o_w_out, router_w, moe_w_gu, moe_w_down):
    bsz, s, d = x.shape
    assert bsz == 1 and d == D_MODEL
    depth = ada_w.shape[0]
    nh = MLA_HEADS
    xs = x.reshape(s, d)
    mod = _ada_mod(c, ada_w, ada_b)
    cos_t, sin_t = _rope_tables(positions)
    row = lambda a: a.reshape(1, -1)

    for layer in range(depth):
        i = layer // 2
        m = mod[layer]
        if layer % 2 == 0:
            has_next = layer + 1 < depth
            dummy = jnp.zeros((s // 512, 16, LANES), F32)
            xs, moe_w_down_bf16, (ffn_w_gu_bf16, ffn_w_down_bf16) = _even_mix(
                xs, m, row(norm_mix_w[layer]), e_w_in[i].astype(BF16), row(a_ln_w[i]), row(a_ln_b[i]), a_w_s[i],
                a_b_s[i].reshape(A_GROUPS, CHUNK, 1), b_w_grp[i].astype(BF16), row(b_scale[i]),
                e_w_out[i].astype(BF16), moe_w_down[i] if has_next else dummy, [ffn_w_gu[i], ffn_w_down[i]])
            xs, moe_w_gu_bf16 = _ffn(xs, m, row(norm_ffn_w[layer]), ffn_w_gu_bf16, ffn_w_down_bf16,
                                     moe_w_gu[i] if has_next else dummy)
        else:
            o2 = 3 * C_WIDTH + Q_LORA + KV_LORA
            w_in = o_w_in[i][:, :o2].astype(BF16)
            w_kpe = jnp.pad(o_w_in[i][:, o2:], ((0, 0), (0, LANES - QK_ROPE))).astype(BF16)
            bound = (1.02 * QK_HD * QK_HD ** -0.5 * LOG2E) * jnp.max(jnp.abs(q_norm_w[i])) * jnp.max(jnp.abs(k_norm_w[i]))
            wkv = w_ukv[i].reshape(KV_LORA, nh, QK_NOPE + V_HD)
            w_k = wkv[:, :, :QK_NOPE].reshape(KV_LORA, nh * QK_NOPE).astype(BF16)
            w_v_t = wkv[:, :, QK_NOPE:].reshape(KV_LORA, nh * V_HD).T.astype(BF16)
            yc, qt, k, vt = _odd_prep(
                xs, m, row(norm_mix_w[layer]), w_in, w_kpe, c_conv_w[i], row(q_a_norm[i]), w_uq[i].T.astype(BF16),
                row(kv_norm[i]), w_k, w_v_t, q_norm_w[i].reshape(QK_HD, 1), row(k_norm_w[i]),
                k_norm_w[i].reshape(QK_HD, 1), jnp.full((1, LANES), -bound, F32), cos_t, sin_t)
            yd = lax.cond(bound <= ATTN_MAX_BOUND, _attention_shifted, _attention, qt, k, vt)
            wr = jnp.pad(router_w[i], ((0, 0), (0, LANES - N_EXPERTS)))
            wr_hi = lax.reduce_precision(wr, exponent_bits=8, mantissa_bits=7)
            w_router = jnp.concatenate([wr_hi, wr - wr_hi], axis=1).astype(BF16)
            xs, hp, rw, ridx, counts = _odd_out(xs, yc, yd, m, row(norm_ffn_w[layer]), o_w_out[i].astype(BF16),
                                                w_router)
            xs = _moe_sparse(xs, hp, rw, ridx, counts, m, moe_w_gu_bf16, moe_w_down_bf16)
    return xs.reshape(bsz, s, d)
```

```python
import functools

import jax
import jax.numpy as jnp
from jax import lax
from jax.experimental import pallas as pl
from jax.experimental.pallas import tpu as pltpu
from jax.experimental.pallas import tpu_sc as plsc

D_MODEL = 1024
SEQ = 16384
EPS = 1e-6
CHUNK = 128
A_WIDTH = 512
A_GROUPS = 4
B_WIDTH = 512
POOL_WINDOWS = (2, 4, 8, 16)
B_HD = 128
C_WIDTH = 512
MLA_HEADS = 4
Q_LORA = 256
KV_LORA = 256
QK_NOPE = 128
QK_ROPE = 64
QK_HD = QK_NOPE + QK_ROPE
V_HD = 128
ROPE_THETA = 10000.0
D_FF = 2816
N_EXPERTS = 8
D_FF_EXPERT = 3584

LANES = 128
SUBLANES = 8
POOL_HALO = 16
CONV_HALO = 8
QK_PAD = 256
V_PAD = 144
LOG2E = 1.4426950408889634
MASK_VALUE = -1e30
MOE_TILE = 512
CONV_COLS = 256
ATTN_MAX_BOUND = 48.0
MOE_ROW_STEP = 128
COMBINE_PARTS = 4
SC_ROWS = 64
VMEM_LIMIT = 56 * 1024 * 1024
MOE_VMEM_LIMIT = 62 * 1024 * 1024

F32 = jnp.float32
BF16 = jnp.bfloat16


def _params(sem, vmem=VMEM_LIMIT, flags=None):
    return pltpu.CompilerParams(dimension_semantics=sem, vmem_limit_bytes=vmem, flags=flags)


def _const_spec(shape, single=False):
    nd = len(shape)
    return pl.BlockSpec(shape, lambda *_: (0,) * nd, pipeline_mode=pl.Buffered(1) if single else None)


def _rms_mod(x, nw, sc, sh):
    ms = jnp.mean(x * x, axis=-1, keepdims=True)
    return (x * lax.rsqrt(ms + EPS)) * (nw * (1.0 + sc)) + sh


def _ada_kernel(c_ref, w_ref, b_ref, o_ref):
    c = c_ref[...]
    ca = c * jax.nn.sigmoid(c)
    o_ref[0] = jnp.sum(w_ref[0] * ca, axis=0, keepdims=True) + b_ref[0]


def _ada_mod(c, ada_w, ada_b):
    depth, d, n = ada_w.shape
    tn = 3072
    return pl.pallas_call(
        _ada_kernel,
        grid=(depth, n // tn),
        in_specs=[
            pl.BlockSpec((d, 1), lambda l, j: (0, 0)),
            pl.BlockSpec((1, d, tn), lambda l, j: (l, 0, j)),
            pl.BlockSpec((1, 1, tn), lambda l, j: (l, 0, j)),
        ],
        out_specs=pl.BlockSpec((1, 1, tn), lambda l, j: (l, 0, j)),
        out_shape=jax.ShapeDtypeStruct((depth, 1, n), F32),
        compiler_params=_params(("arbitrary", "arbitrary")),
        name="ada_mod",
    )(c.reshape(d, 1), ada_w, ada_b.reshape(depth, 1, n))


def _even_mix_kernel(x_ref, mod_ref, nw_ref, win_ref, lnw_ref, lnb_ref, ws_ref, bs_ref, wg_ref, bsc_ref,
                     wout_ref, *rest):
    n_sides = (len(rest) - 4) // 2
    o_ref = rest[n_sides]
    halo_ref, sv_ref, yb_ref = rest[-3:]
    for side_ref, side_out_ref in zip(rest[:n_sides], rest[n_sides + 1:2 * n_sides + 1]):
        side_out_ref[...] = side_ref[...].astype(BF16)
    tm = x_ref.shape[0]
    i = pl.program_id(0)

    @pl.when(i == 0)
    def _():
        halo_ref[...] = jnp.zeros_like(halo_ref)

    d = D_MODEL
    x = x_ref[...]
    sh, sc, gate = mod_ref[:, 0:d], mod_ref[:, d:2 * d], mod_ref[:, 2 * d:3 * d]
    h = _rms_mod(x, nw_ref[...], sc, sh).astype(BF16)
    p = jnp.dot(h, win_ref[...], preferred_element_type=F32)

    gl = jax.nn.gelu(p[:, :2 * A_WIDTH])
    u = gl[:, :A_WIDTH]
    v = gl[:, A_WIDTH:]
    mu = jnp.mean(v, axis=-1, keepdims=True)
    vc = v - mu
    var = jnp.mean(vc * vc, axis=-1, keepdims=True)
    vn = (vc * lax.rsqrt(var + EPS) * lnw_ref[...] + lnb_ref[...]).astype(BF16)
    row = lax.broadcasted_iota(jnp.int32, (CHUNK, CHUNK), 0)
    col = lax.broadcasted_iota(jnp.int32, (CHUNK, CHUNK), 1)
    for g in range(A_GROUPS):
        w = jnp.where(col <= row, ws_ref[g], 0.0).astype(BF16)
        b = bs_ref[g]
        for c in range(tm // CHUNK):
            blk = vn[c * CHUNK:(c + 1) * CHUNK, g * LANES:(g + 1) * LANES]
            sv_ref[c * CHUNK:(c + 1) * CHUNK, g * LANES:(g + 1) * LANES] = (
                jnp.dot(w, blk, preferred_element_type=F32) + b)
    ya = (u * sv_ref[...]).astype(BF16)

    pb = p[:, 2 * A_WIDTH:]
    ext = jnp.concatenate([halo_ref[...], pb], axis=0)
    halo_ref[...] = pb[tm - POOL_HALO:, :]
    t_glob = i * tm + lax.broadcasted_iota(jnp.int32, (tm, 1), 0)
    s = ext
    width = 1
    for g, win in enumerate(POOL_WINDOWS):
        while width < win:
            s = s + pltpu.roll(s, width, 0)
            width *= 2
        cnt = jnp.minimum(t_glob + 1, win).astype(F32)
        sl = slice(g * B_HD, (g + 1) * B_HD)
        pooled = s[POOL_HALO:, sl] / cnt
        dg = (pooled - pb[:, sl]).astype(BF16)
        yb_ref[:, sl] = jnp.dot(dg, wg_ref[g], preferred_element_type=F32)
    yb = (yb_ref[...] * bsc_ref[...]).astype(BF16)

    mix = (jnp.dot(ya, wout_ref[:A_WIDTH, :], preferred_element_type=F32)
           + jnp.dot(yb, wout_ref[A_WIDTH:, :], preferred_element_type=F32))
    o_ref[...] = x + gate * mix


def _side_cast_spec(side, steps, axis):
    ne = side.shape[0]
    parts = steps // ne
    assert parts * ne == steps and side.shape[axis] % parts == 0
    block = list(side.shape)
    block[0] = 1
    block[axis] //= parts
    assert block[1] % 16 == 0 and block[2] % LANES == 0
    if axis == 1:
        return pl.BlockSpec(tuple(block), lambda i: (i // parts, i % parts, 0))
    return pl.BlockSpec(tuple(block), lambda i: (i // parts, 0, i % parts))


def _row_cast_spec(w, steps):
    rows = -(-w.shape[0] // steps)
    rows = -(-rows // 16) * 16
    last = -(-w.shape[0] // rows) - 1
    return pl.BlockSpec((rows, w.shape[1]), lambda i: (jnp.minimum(i, last), 0))


def _even_mix(x, mod, nw, w_in, ln_w, ln_b, w_s, b_s, w_grp, b_scale, w_out, row_sides, tm=512):
    s, d = x.shape
    steps = s // tm
    sides = list(row_sides)
    side_specs = [_row_cast_spec(w, steps) for w in row_sides]
    outs = pl.pallas_call(
        _even_mix_kernel,
        grid=(steps,),
        in_specs=[
            pl.BlockSpec((tm, d), lambda i: (i, 0)),
            _const_spec(mod.shape), _const_spec(nw.shape), _const_spec(w_in.shape),
            _const_spec(ln_w.shape), _const_spec(ln_b.shape), _const_spec(w_s.shape), _const_spec(b_s.shape),
            _const_spec(w_grp.shape), _const_spec(b_scale.shape), _const_spec(w_out.shape), *side_specs,
        ],
        out_specs=[pl.BlockSpec((tm, d), lambda i: (i, 0)), *side_specs],
        out_shape=[jax.ShapeDtypeStruct((s, d), F32)] + [jax.ShapeDtypeStruct(w.shape, BF16) for w in sides],
        scratch_shapes=[pltpu.VMEM((POOL_HALO, B_WIDTH), F32), pltpu.VMEM((tm, A_WIDTH), F32),
                        pltpu.VMEM((tm, B_WIDTH), F32)],
        compiler_params=_params(("arbitrary",)),
        name="even_mix",
    )(x, mod, nw, w_in, ln_w, ln_b, w_s, b_s, w_grp, b_scale, w_out, *sides)
    return outs[0], outs[1:]


def _ffn_kernel(x_ref, mod_ref, nw_ref, wgu_ref, wd_ref, side_ref, o_ref, side_out_ref, *, n_chunks):
    side_out_ref[...] = side_ref[...].astype(BF16)
    d = D_MODEL
    x = x_ref[...]
    sh, sc, gate = mod_ref[:, 3 * d:4 * d], mod_ref[:, 4 * d:5 * d], mod_ref[:, 5 * d:6 * d]
    h = _rms_mod(x, nw_ref[...], sc, sh).astype(BF16)
    ff = wd_ref.shape[0]
    tf = ff // n_chunks
    acc = jnp.zeros(x.shape, F32)
    for f in range(n_chunks):
        g = jnp.dot(h, wgu_ref[:, f * tf:(f + 1) * tf], preferred_element_type=F32)
        u = jnp.dot(h, wgu_ref[:, ff + f * tf:ff + (f + 1) * tf], preferred_element_type=F32)
        act = (g * jax.nn.sigmoid(g) * u).astype(BF16)
        acc = acc + jnp.dot(act, wd_ref[f * tf:(f + 1) * tf, :], preferred_element_type=F32)
    o_ref[...] = x + gate * acc


def _ffn(x, mod, nw, w_gu, w_down, side, tm=512, n_chunks=11):
    s, d = x.shape
    steps = s // tm
    side_spec = _side_cast_spec(side, steps, 2)
    return pl.pallas_call(
        functools.partial(_ffn_kernel, n_chunks=n_chunks),
        grid=(steps,),
        in_specs=[
            pl.BlockSpec((tm, d), lambda i: (i, 0)),
            _const_spec(mod.shape), _const_spec(nw.shape), _const_spec(w_gu.shape, single=True),
            _const_spec(w_down.shape, single=True), side_spec,
        ],
        out_specs=[pl.BlockSpec((tm, d), lambda i: (i, 0)), side_spec],
        out_shape=[jax.ShapeDtypeStruct((s, d), F32), jax.ShapeDtypeStruct(side.shape, BF16)],
        compiler_params=_params(("arbitrary",)),
        name="ffn",
    )(x, mod, nw, w_gu, w_down, side)


def _rope_table_kernel(pos_ref, invf_ref, cos_ref, sin_ref):
    ang = pos_ref[...].astype(F32) * invf_ref[...]
    cos_ref[...] = jnp.cos(ang)
    sin_ref[...] = jnp.sin(ang)


def _rope_tables(positions):
    s = positions.shape[-1]
    half = QK_ROPE // 2
    inv_freq = ROPE_THETA ** (-jnp.arange(0, QK_ROPE, 2, dtype=F32) / QK_ROPE)
    return pl.pallas_call(
        _rope_table_kernel,
        out_shape=(jax.ShapeDtypeStruct((half, s), F32), jax.ShapeDtypeStruct((half, s), F32)),
        name="rope_tables",
    )(positions.reshape(1, s), inv_freq.reshape(half, 1))


def _odd_prep_kernel(x_ref, mod_ref, nw_ref, win_ref, wkpe_ref, cw_ref, qan_ref, wuqt_ref, kvn_ref, wk_ref, wvt_ref,
                     qnw_ref, knw_ref, knwc_ref, shift_ref, cost_ref, sint_ref, side_ref, yc_ref, qt_ref, k_ref, vt_ref,
                     side_out_ref, halo_ref):
    side_out_ref[...] = side_ref[...].astype(BF16)
    tm = x_ref.shape[0]
    i = pl.program_id(0)

    @pl.when(i == 0)
    def _():
        halo_ref[...] = jnp.zeros_like(halo_ref)

    d = D_MODEL
    x = x_ref[...]
    sh, sc = mod_ref[:, 0:d], mod_ref[:, d:2 * d]
    h = _rms_mod(x, nw_ref[...], sc, sh).astype(BF16)
    cw = C_WIDTH
    o0 = 3 * cw
    proj = lambda lo, hi: jnp.dot(h, win_ref[:, lo:hi], preferred_element_type=F32)
    kpe_pad = jnp.dot(h, wkpe_ref[...], preferred_element_type=F32)
    kpe = kpe_pad[:, :QK_ROPE]

    for c0 in range(0, cw, CONV_COLS):
        cols = slice(c0, c0 + CONV_COLS)
        z = proj(cw + c0, cw + c0 + CONV_COLS) * proj(2 * cw + c0, 2 * cw + c0 + CONV_COLS)
        ext = jnp.concatenate([halo_ref[:, cols], z], axis=0)
        halo_ref[:, cols] = z[tm - CONV_HALO:, :]
        z1 = pltpu.roll(ext, 1, 0)[CONV_HALO:, :]
        z2 = pltpu.roll(ext, 2, 0)[CONV_HALO:, :]
        conv = cw_ref[0:1, cols] * z2 + cw_ref[1:2, cols] * z1 + cw_ref[2:3, cols] * z
        yc_ref[:, cols] = (proj(c0, c0 + CONV_COLS) * conv).astype(BF16)

    cq = proj(o0, o0 + Q_LORA)
    ckv = proj(o0 + Q_LORA, o0 + Q_LORA + KV_LORA)
    cqn = cq * lax.rsqrt(jnp.mean(cq * cq, axis=-1, keepdims=True) + EPS) * qan_ref[...]
    ckvn = ckv * lax.rsqrt(jnp.mean(ckv * ckv, axis=-1, keepdims=True) + EPS) * kvn_ref[...]
    cqn_t = cqn.T.astype(BF16)
    ckvn_t = ckvn.T.astype(BF16)
    q_t = jnp.dot(wuqt_ref[...], cqn_t, preferred_element_type=F32)
    v_t = jnp.dot(wvt_ref[...], ckvn_t, preferred_element_type=F32)
    kn_all = jnp.dot(ckvn.astype(BF16), wk_ref[...], preferred_element_type=F32)

    nh = MLA_HEADS
    half = QK_ROPE // 2
    sm_scale = QK_HD ** -0.5
    cos_t, sin_t = cost_ref[...], sint_ref[...]
    kn_w = knw_ref[:, :QK_NOPE]
    kw_t = kpe_pad.T[:QK_ROPE, :] * knwc_ref[QK_NOPE:, :]
    k1, k2 = kw_t[:half, :], kw_t[half:, :]
    kr_t = jnp.concatenate([k1 * cos_t - k2 * sin_t, k2 * cos_t + k1 * sin_t,
                            jnp.zeros((LANES - QK_ROPE, tm), F32)], axis=0)
    kr = kr_t.T[:, :QK_ROPE]
    kpe_ss = jnp.sum(kpe * kpe, axis=-1, keepdims=True)
    npad = QK_PAD - QK_HD
    zrows = jnp.where(lax.broadcasted_iota(jnp.int32, (npad, tm), 0) == 0, shift_ref[:, 0:1], 0.0)
    zpad = jnp.where(lax.broadcasted_iota(jnp.int32, (tm, npad), 1) == 0, 1.0, 0.0)
    ones_rows = (lax.broadcasted_iota(jnp.int32, (V_PAD - V_HD, tm), 0) == 0).astype(BF16)
    for hd in range(nh):
        qh = q_t[hd * QK_HD:(hd + 1) * QK_HD, :]
        q_inv = lax.rsqrt(jnp.sum(qh * qh, axis=0, keepdims=True) / QK_HD + EPS) * (sm_scale * LOG2E)
        qw = qh * qnw_ref[...]
        x1, x2 = qw[QK_NOPE:QK_NOPE + half, :], qw[QK_NOPE + half:, :]
        qt_ref[hd] = jnp.concatenate(
            [qw[:QK_NOPE, :] * q_inv, (x1 * cos_t - x2 * sin_t) * q_inv, (x2 * cos_t + x1 * sin_t) * q_inv, zrows],
            axis=0).astype(BF16)
        kn = kn_all[:, hd * QK_NOPE:(hd + 1) * QK_NOPE]
        k_ss = jnp.sum(kn * kn, axis=-1, keepdims=True) + kpe_ss
        k_inv = lax.rsqrt(k_ss / QK_HD + EPS)
        k_ref[hd] = jnp.concatenate([kn * k_inv * kn_w, kr * k_inv, zpad], axis=1).astype(BF16)
        vt_ref[hd, :V_HD, :] = v_t[hd * V_HD:(hd + 1) * V_HD, :].astype(BF16)
        vt_ref[hd, V_HD:, :] = ones_rows


def _odd_prep(x, mod, nw, w_in, w_kpe, conv_w, q_a_norm, w_uq_t, kv_norm, w_k, w_v_t, q_norm_col, k_norm_w,
              k_norm_col, shift, cos_t, sin_t, side, tm=1024):
    s, d = x.shape
    nh = MLA_HEADS
    half = QK_ROPE // 2
    steps = s // tm
    side_spec = _side_cast_spec(side, steps, 1)
    consts = [mod, nw, w_in, w_kpe, conv_w, q_a_norm, w_uq_t, kv_norm, w_k, w_v_t, q_norm_col, k_norm_w, k_norm_col,
              shift]
    return pl.pallas_call(
        _odd_prep_kernel,
        grid=(steps,),
        in_specs=[pl.BlockSpec((tm, d), lambda i: (i, 0))] + [_const_spec(a.shape) for a in consts] + [
            pl.BlockSpec((half, tm), lambda i: (0, i)), pl.BlockSpec((half, tm), lambda i: (0, i)), side_spec],
        out_specs=[
            pl.BlockSpec((tm, C_WIDTH), lambda i: (i, 0)),
            pl.BlockSpec((nh, QK_PAD, tm), lambda i: (0, 0, i)),
            pl.BlockSpec((nh, tm, QK_PAD), lambda i: (0, i, 0)),
            pl.BlockSpec((nh, V_PAD, tm), lambda i: (0, 0, i)),
            side_spec,
        ],
        out_shape=[
            jax.ShapeDtypeStruct((s, C_WIDTH), BF16),
            jax.ShapeDtypeStruct((nh, QK_PAD, s), BF16),
            jax.ShapeDtypeStruct((nh, s, QK_PAD), BF16),
            jax.ShapeDtypeStruct((nh, V_PAD, s), BF16),
            jax.ShapeDtypeStruct(side.shape, BF16),
        ],
        scratch_shapes=[pltpu.VMEM((CONV_HALO, C_WIDTH), F32)],
        compiler_params=_params(("arbitrary",)),
        name="odd_prep",
    )(x, *consts, cos_t, sin_t, side)


def _causal_mask(s, lead=0):
    key = lax.broadcasted_iota(jnp.int32, s.shape, 0)
    qry = lead + lax.broadcasted_iota(jnp.int32, s.shape, 1)
    return jnp.where(key <= qry, s, MASK_VALUE)


def _attn_shifted_kernel(qt_ref, k_ref, vt_ref, o_ref, acc_ref, l_ref, *, tq, tk):
    qi = pl.program_id(1)
    acc_ref[...] = jnp.zeros_like(acc_ref)
    l_ref[...] = jnp.zeros_like(l_ref)
    r = tq // tk

    def block(j, diag_piece=None):
        q0 = 0 if diag_piece is None else diag_piece * tk
        start = pl.multiple_of(j * tk, tk)
        s = jnp.dot(k_ref[0, pl.ds(start, tk), :], qt_ref[0, :, q0:], preferred_element_type=F32)
        if diag_piece is not None:
            s = _causal_mask(s)
        p = jnp.exp2(s)
        l_ref[:, q0:] += jnp.sum(p.reshape(tk // SUBLANES, SUBLANES, p.shape[1]), axis=0)
        acc_ref[:, q0:] += jnp.dot(vt_ref[0, :V_HD, pl.ds(start, tk)], p.astype(BF16), preferred_element_type=F32)

    def body(t, carry):
        for b in range(r):
            block(r * t + b)
        return carry

    lax.fori_loop(0, qi, body, 0)
    for b in range(r):
        block(r * qi + b, diag_piece=b)
    out_t = acc_ref[...] / jnp.sum(l_ref[...], axis=0, keepdims=True)
    o_ref[...] = out_t.T.astype(o_ref.dtype)


def _attention_shifted(qt, k, vt, tq=2048, tk=512):
    nh, _, s = qt.shape
    return pl.pallas_call(
        functools.partial(_attn_shifted_kernel, tq=tq, tk=tk),
        grid=(nh, s // tq),
        in_specs=[
            pl.BlockSpec((1, QK_PAD, tq), lambda h, i: (h, 0, i)),
            pl.BlockSpec((1, s, QK_PAD), lambda h, i: (h, 0, 0)),
            pl.BlockSpec((1, V_PAD, s), lambda h, i: (h, 0, 0)),
        ],
        out_specs=pl.BlockSpec((tq, V_HD), lambda h, i: (i, h)),
        out_shape=jax.ShapeDtypeStruct((s, nh * V_HD), BF16),
        scratch_shapes=[pltpu.VMEM((V_HD, tq), F32), pltpu.VMEM((SUBLANES, tq), F32)],
        compiler_params=_params(("arbitrary", "arbitrary")),
        name="mla_attention_shifted",
    )(qt, k, vt)


def _attn_kernel(qt_ref, k_ref, vt_ref, o_ref, s0_ref, s1_ref, bm0_ref, bm1_ref, m_ref, acc_ref, *, tq, tk):
    qi = pl.program_id(1)
    m_ref[...] = jnp.full_like(m_ref, MASK_VALUE)
    acc_ref[...] = jnp.zeros_like(acc_ref)

    def scores(j, s_ref, bm_ref, diag_offset=None):
        q0 = 0 if diag_offset is None else diag_offset
        start = pl.multiple_of(j * tk, tk)
        s = jnp.dot(k_ref[0, pl.ds(start, tk), :], qt_ref[0, :, q0:], preferred_element_type=F32)
        if diag_offset is not None:
            s = _causal_mask(s)
        s_ref[:, q0:] = s
        bm_ref[:, q0:] = jnp.max(s, axis=0, keepdims=True)

    def consume(j, s_ref, bm_ref, q0=0):
        m_prev = m_ref[:, q0:]
        m_new = jnp.maximum(m_prev, bm_ref[:, q0:])
        alpha = jnp.exp2(m_prev - m_new)
        p = jnp.exp2((s_ref[:, q0:] - m_new).astype(BF16))
        start = pl.multiple_of(j * tk, tk)
        vt = vt_ref[0, :, pl.ds(start, tk)]
        acc_ref[:, q0:] = alpha * acc_ref[:, q0:] + jnp.dot(vt, p, preferred_element_type=F32)
        m_ref[:, q0:] = m_new

    r = tq // tk
    slots = ((s0_ref, bm0_ref), (s1_ref, bm1_ref))

    def trip(j0, then_diagonal):
        for b in range(r):
            first_diag = then_diagonal and b == r - 1
            scores(j0 + b + 1, *slots[(b + 1) % 2], diag_offset=0 if first_diag else None)
            consume(j0 + b, *slots[b % 2])

    def diagonal(jd):
        for b in range(r):
            if b + 1 < r:
                scores(jd + b + 1, *slots[(b + 1) % 2], diag_offset=(b + 1) * tk)
            consume(jd + b, *slots[b % 2], q0=b * tk)

    @pl.when(qi > 0)
    def _():
        scores(0, *slots[0])

    n_plain = jnp.maximum(qi - 1, 0)

    def body(t, carry):
        trip(r * (2 * t), False)
        trip(r * (2 * t + 1), False)
        return carry

    lax.fori_loop(0, n_plain // 2, body, 0)

    @pl.when(n_plain % 2 == 1)
    def _():
        trip(r * (n_plain - 1), False)

    @pl.when(qi > 0)
    def _():
        trip(r * (qi - 1), True)
        diagonal(r * qi)

    @pl.when(qi == 0)
    def _():
        scores(0, *slots[0], diag_offset=0)
        diagonal(0)

    acc = acc_ref[...]
    out_t = acc[:V_HD, :] / acc[V_HD:V_HD + 1, :]
    o_ref[...] = out_t.T.astype(o_ref.dtype)


def _attention(qt, k, vt, tq=1024, tk=512):
    nh, _, s = qt.shape
    assert tq % (2 * tk) == 0
    return pl.pallas_call(
        functools.partial(_attn_kernel, tq=tq, tk=tk),
        grid=(nh, s // tq),
        in_specs=[
            pl.BlockSpec((1, QK_PAD, tq), lambda h, i: (h, 0, i)),
            pl.BlockSpec((1, s, QK_PAD), lambda h, i: (h, 0, 0)),
            pl.BlockSpec((1, V_PAD, s), lambda h, i: (h, 0, 0)),
        ],
        out_specs=pl.BlockSpec((tq, V_HD), lambda h, i: (i, h)),
        out_shape=jax.ShapeDtypeStruct((s, nh * V_HD), BF16),
        scratch_shapes=[pltpu.VMEM((tk, tq), F32), pltpu.VMEM((tk, tq), F32), pltpu.VMEM((1, tq), F32),
                        pltpu.VMEM((1, tq), F32), pltpu.VMEM((1, tq), F32), pltpu.VMEM((V_PAD, tq), F32)],
        compiler_params=_params(("arbitrary", "arbitrary")),
        name="mla_attention",
    )(qt, k, vt)


def _pack_bf16_pairs(x):
    n = x.shape[1] // 2
    lo = pltpu.bitcast(x[:, :n].astype(BF16).astype(F32), jnp.uint32)
    hi = pltpu.bitcast(x[:, n:].astype(BF16).astype(F32), jnp.uint32)
    return (lo >> 16) | hi


def _unpack_bf16_pairs(p):
    lo = pltpu.bitcast(p << 16, F32)
    hi = pltpu.bitcast(p & jnp.uint32(0xFFFF0000), F32)
    return jnp.concatenate([lo, hi], axis=1)


def _odd_out_kernel(x_ref, yc_ref, yd_ref, mod_ref, nw_ref, wout_ref, wr_ref, before_ref, x_out_ref, h_ref, rw_ref,
                    ridx_ref, cnt_ref):
    d = D_MODEL
    x = x_ref[...]
    gate_m = mod_ref[:, 2 * d:3 * d]
    sh, sc = mod_ref[:, 3 * d:4 * d], mod_ref[:, 4 * d:5 * d]
    mix = (jnp.dot(yc_ref[...], wout_ref[:C_WIDTH, :], preferred_element_type=F32)
           + jnp.dot(yd_ref[...], wout_ref[C_WIDTH:, :], preferred_element_type=F32))
    x1 = x + gate_m * mix
    x_out_ref[...] = x1
    h = _rms_mod(x1, nw_ref[...], sc, sh)
    h_ref[...] = _pack_bf16_pairs(h)

    h_hi = h.astype(BF16)
    h_lo = (h - h_hi.astype(F32)).astype(BF16)
    hw = jnp.dot(h_hi, wr_ref[...], preferred_element_type=F32)
    logits = hw[:, :LANES] + (hw[:, LANES:] + jnp.dot(h_lo, wr_ref[:, :LANES], preferred_element_type=F32))
    lt = logits.T[:N_EXPERTS, :]
    ex = lax.broadcasted_iota(jnp.int32, lt.shape, 0)
    m1 = jnp.max(lt, axis=0, keepdims=True)
    i1 = jnp.min(jnp.where(lt == m1, ex, N_EXPERTS), axis=0, keepdims=True)
    rest = jnp.where(ex == i1, -jnp.inf, lt)
    m2 = jnp.max(rest, axis=0, keepdims=True)
    i2 = jnp.min(jnp.where(rest == m2, ex, N_EXPERTS), axis=0, keepdims=True)
    e2 = jnp.exp(m2 - m1)
    w1 = 1.0 / (1.0 + e2)
    w2 = e2 / (1.0 + e2)
    row = lax.broadcasted_iota(jnp.int32, (LANES, lt.shape[1]), 0)
    rw_ref[...] = jnp.where(row == 0, w1, jnp.where(row == 1, w2, 0.0)).T

    i = pl.program_id(0)

    @pl.when(i == 0)
    def _():
        cnt_ref[...] = jnp.zeros_like(cnt_ref)

    onehot = jnp.where((ex == i1) | (ex == i2), 1.0, 0.0)
    prefix = jnp.dot(onehot.astype(BF16), before_ref[...], preferred_element_type=F32) + cnt_ref[...]
    r1 = jnp.sum(jnp.where(ex == i1, prefix, 0.0), axis=0, keepdims=True)
    r2 = jnp.sum(jnp.where(ex == i2, prefix, 0.0), axis=0, keepdims=True)
    cnt_ref[...] += jnp.sum(onehot, axis=1, keepdims=True)
    ridx_ref[...] = jnp.where(ex == 0, i1, jnp.where(ex == 1, i2, jnp.where(
        ex == 2, r1.astype(jnp.int32), jnp.where(ex == 3, r2.astype(jnp.int32), 0))))


def _odd_out(x, yc, yd, mod, nw, w_out, w_router, tm=1024):
    s, d = x.shape
    return pl.pallas_call(
        _odd_out_kernel,
        grid=(s // tm,),
        in_specs=[
            pl.BlockSpec((tm, d), lambda i: (i, 0)),
            pl.BlockSpec((tm, C_WIDTH), lambda i: (i, 0)),
            pl.BlockSpec((tm, MLA_HEADS * V_HD), lambda i: (i, 0)),
            _const_spec(mod.shape), _const_spec(nw.shape), _const_spec(w_out.shape), _const_spec(w_router.shape),
            _const_spec((tm, tm)),
        ],
        out_specs=[
            pl.BlockSpec((tm, d), lambda i: (i, 0)),
            pl.BlockSpec((tm, d // 2), lambda i: (i, 0)),
            pl.BlockSpec((tm, LANES), lambda i: (i, 0)),
            pl.BlockSpec((SUBLANES, tm), lambda i: (0, i)),
            pl.BlockSpec((N_EXPERTS, 1), lambda i: (0, 0)),
        ],
        out_shape=[
            jax.ShapeDtypeStruct((s, d), F32),
            jax.ShapeDtypeStruct((s, d // 2), jnp.uint32),
            jax.ShapeDtypeStruct((s, LANES), F32),
            jax.ShapeDtypeStruct((SUBLANES, s), jnp.int32),
            jax.ShapeDtypeStruct((N_EXPERTS, 1), F32),
        ],
        compiler_params=_params(("arbitrary",)),
        name="odd_out_router",
    )(x, yc, yd, mod, nw, w_out, w_router, jnp.triu(jnp.ones((tm, tm), BF16), 1))


def _sc_workers():
    info = plsc.get_sparse_core_info()
    return info.num_cores, info.num_cores * info.num_subcores


def _sc_scatter_rows(x, idx0, idx1, out_rows):
    n, w = x.shape
    nc, nw = _sc_workers()
    per_w = n // nw
    nch = per_w // SC_ROWS
    assert nch % 2 == 0 and nch >= 2
    mesh = plsc.VectorSubcoreMesh(core_axis_name="c", subcore_axis_name="s")

    @functools.partial(
        pl.kernel, mesh=mesh, out_type=jax.ShapeDtypeStruct((out_rows, w), x.dtype),
        scratch_types=[pltpu.VMEM((nch, SC_ROWS), jnp.int32), pltpu.VMEM((nch, SC_ROWS), jnp.int32),
                       pltpu.VMEM((SC_ROWS, w), x.dtype), pltpu.VMEM((SC_ROWS, w), x.dtype),
                       pltpu.SemaphoreType.DMA, pltpu.SemaphoreType.DMA, pltpu.SemaphoreType.DMA,
                       pltpu.SemaphoreType.DMA],
        name="moe_dispatch")
    def scatter(x_hbm, i0_hbm, i1_hbm, out_hbm, i0_v, i1_v, buf0, buf1, lsem0, lsem1, ssem0, ssem1):
        wid = lax.axis_index("s") * nc + lax.axis_index("c")
        pltpu.sync_copy(i0_hbm.at[wid], i0_v)
        pltpu.sync_copy(i1_hbm.at[wid], i1_v)
        base = wid * per_w

        def load(c, buf, sem):
            return pltpu.make_async_copy(x_hbm.at[pl.ds(base + c * SC_ROWS, SC_ROWS)], buf, sem)

        def put(c, buf):
            first = pltpu.make_async_copy(buf, out_hbm.at[i0_v.at[c]], ssem0)
            second = pltpu.make_async_copy(buf, out_hbm.at[i1_v.at[c]], ssem1)
            first.start()
            second.start()
            first.wait()
            second.wait()

        def pair(c, prefetch_next):
            load(c + 1, buf1, lsem1).start()
            load(c, buf0, lsem0).wait()
            put(c, buf0)
            if prefetch_next:
                load(c + 2, buf0, lsem0).start()
            load(c + 1, buf1, lsem1).wait()
            put(c + 1, buf1)

        load(0, buf0, lsem0).start()

        def body(t, carry):
            pair(2 * t, True)
            return carry

        lax.fori_loop(0, nch // 2 - 1, body, 0)
        pair(nch - 2, False)

    return scatter(x, idx0.reshape(nw, nch, SC_ROWS), idx1.reshape(nw, nch, SC_ROWS))


def _sc_gather_rows(table, idx):
    _, w = table.shape
    b = idx.shape[0]
    nc, nw = _sc_workers()
    per_w = b // nw
    nch = per_w // SC_ROWS
    mesh = plsc.VectorSubcoreMesh(core_axis_name="c", subcore_axis_name="s")

    assert nch % 2 == 0 and nch >= 2

    @functools.partial(
        pl.kernel, mesh=mesh, out_type=jax.ShapeDtypeStruct((b, w), table.dtype),
        scratch_types=[pltpu.VMEM((nch, SC_ROWS), jnp.int32), pltpu.VMEM((SC_ROWS, w), table.dtype),
                       pltpu.VMEM((SC_ROWS, w), table.dtype), pltpu.SemaphoreType.DMA, pltpu.SemaphoreType.DMA],
        name="moe_combine_gather")
    def gather(table_hbm, idx_hbm, out_hbm, idx_v, buf0, buf1, sem0, sem1):
        wid = lax.axis_index("s") * nc + lax.axis_index("c")
        pltpu.sync_copy(idx_hbm.at[wid], idx_v)
        base = wid * per_w

        def fetch(c, buf, sem):
            return pltpu.make_async_copy(table_hbm.at[idx_v.at[c]], buf, sem)

        def put(c, buf):
            pltpu.sync_copy(buf, out_hbm.at[pl.ds(base + c * SC_ROWS, SC_ROWS)])

        def pair(c, prefetch_next):
            fetch(c + 1, buf1, sem1).start()
            fetch(c, buf0, sem0).wait()
            put(c, buf0)
            if prefetch_next:
                fetch(c + 2, buf0, sem0).start()
            fetch(c + 1, buf1, sem1).wait()
            put(c + 1, buf1)

        fetch(0, buf0, sem0).start()

        def body(t, carry):
            pair(2 * t, True)
            return carry

        lax.fori_loop(0, nch // 2 - 1, body, 0)
        pair(nch - 2, False)

    return gather(table, idx.reshape(nw, nch, SC_ROWS))


def _moe_kernel(te_ref, nv_ref, rows_ref, xs_ref, wg_ref, wu_ref, wd_ref, ys_ref, *, n_chunks):
    j = pl.program_id(0)
    tm = xs_ref.shape[0]
    tf = wd_ref.shape[1] // n_chunks

    def expert_ffn(rows):
        h = _unpack_bf16_pairs(xs_ref[:rows, :]).astype(BF16)
        y = None
        for c in range(n_chunks):
            cols = slice(c * tf, (c + 1) * tf)
            g = jnp.dot(h, wg_ref[0, :, cols], preferred_element_type=F32)
            u = jnp.dot(h, wu_ref[0, :, cols], preferred_element_type=F32)
            act = (g * jax.nn.sigmoid(g) * u).astype(BF16)
            yc = jnp.dot(act, wd_ref[0, cols, :], preferred_element_type=F32)
            y = yc if y is None else y + yc
        ys_ref[:rows, :] = _pack_bf16_pairs(y)

    @pl.when(j < nv_ref[0])
    def _():
        for rows in range(MOE_ROW_STEP, tm + 1, MOE_ROW_STEP):
            pl.when(rows_ref[j] == rows)(functools.partial(expert_ffn, rows))


def _moe_grouped(xs, tile_expert, n_valid, tile_rows, w_gu, w_down, tm, n_chunks=2):
    p_rows, dh = xs.shape
    d = 2 * dh
    ne, _, ff2 = w_gu.shape
    ff = ff2 // 2
    assert ff % (n_chunks * LANES) == 0

    def tile(j, nv):
        return jnp.minimum(j, nv[0] - 1)

    grid_spec = pltpu.PrefetchScalarGridSpec(
        num_scalar_prefetch=3,
        grid=(p_rows // tm,),
        in_specs=[
            pl.BlockSpec((tm, dh), lambda j, te, nv, tr: (tile(j, nv), 0)),
            pl.BlockSpec((1, d, ff), lambda j, te, nv, tr: (te[tile(j, nv)], 0, 0)),
            pl.BlockSpec((1, d, ff), lambda j, te, nv, tr: (te[tile(j, nv)], 0, 1)),
            pl.BlockSpec((1, ff, d), lambda j, te, nv, tr: (te[tile(j, nv)], 0, 0)),
        ],
        out_specs=pl.BlockSpec((tm, dh), lambda j, te, nv, tr: (tile(j, nv), 0)),
    )
    return pl.pallas_call(
        functools.partial(_moe_kernel, n_chunks=n_chunks),
        grid_spec=grid_spec,
        out_shape=jax.ShapeDtypeStruct((p_rows, dh), jnp.uint32),
        compiler_params=_params(("arbitrary",), vmem=MOE_VMEM_LIMIT),
        name="moe_experts",
    )(tile_expert, n_valid, tile_rows, xs, w_gu, w_gu, w_down)


def _moe_combine_kernel(x_ref, y0_ref, y1_ref, rw_ref, mod_ref, *rest):
    o_ref = rest[-1]
    d = D_MODEL
    w1 = rw_ref[:, 0:1]
    w2 = rw_ref[:, 1:2]
    y = w1 * _unpack_bf16_pairs(y0_ref[...]) + w2 * _unpack_bf16_pairs(y1_ref[...])
    o_ref[...] = x_ref[...] + mod_ref[:, 5 * d:6 * d] * y


def _moe_combine(x, yg, rw, mod, prev, part, n_parts, tm=512):
    s, d = x.shape
    nb = s // tm // n_parts
    first = part * nb
    in_specs = [
        pl.BlockSpec((tm, d), lambda i: (first + i, 0)),
        pl.BlockSpec((tm, d // 2), lambda i: (i, 0)),
        pl.BlockSpec((tm, d // 2), lambda i: (nb + i, 0)),
        pl.BlockSpec((tm, LANES), lambda i: (first + i, 0)),
        _const_spec(mod.shape),
    ]
    args = [x, yg, yg, rw, mod]
    aliases = {}
    if prev is not None:
        in_specs.append(pl.BlockSpec(memory_space=pl.ANY))
        args.append(prev)
        aliases = {len(args) - 1: 0}
    return pl.pallas_call(
        _moe_combine_kernel,
        grid=(nb,),
        in_specs=in_specs,
        out_specs=pl.BlockSpec((tm, d), lambda i: (first + i, 0)),
        out_shape=jax.ShapeDtypeStruct((s, d), F32),
        input_output_aliases=aliases,
        compiler_params=_params(("arbitrary",)),
        name="moe_combine",
    )(*args)


def _moe_sparse(x, h_packed, rw, ridx, counts, mod, w_gu, w_down, tm=MOE_TILE):
    s = x.shape[0]
    ne = w_gu.shape[0]
    n_tiles = (2 * s) // tm + ne
    cnt = counts[:ne, 0].astype(jnp.int32)
    padded = ((cnt + tm - 1) // tm) * tm
    ends = jnp.cumsum(padded)
    offs = ends - padded
    experts = jnp.arange(ne, dtype=jnp.int32)
    off_of = lambda e: jnp.sum(jnp.where(e[:, None] == experts[None, :], offs[None, :], 0), axis=1)
    pos0 = off_of(ridx[0]) + ridx[2]
    pos1 = off_of(ridx[1]) + ridx[3]
    tile_start = jnp.arange(n_tiles, dtype=jnp.int32) * tm
    tile_expert = jnp.minimum(jnp.sum(tile_start[:, None] >= ends[None, :], axis=1), ne - 1).astype(jnp.int32)
    n_valid = (ends[-1] // tm).reshape(1).astype(jnp.int32)
    filled = jnp.clip((offs + cnt)[tile_expert] - tile_start, 0, tm)
    tile_rows = (((filled + MOE_ROW_STEP - 1) // MOE_ROW_STEP) * MOE_ROW_STEP).astype(jnp.int32)
    xs = _sc_scatter_rows(h_packed, pos0, pos1, n_tiles * tm)
    ys = _moe_grouped(xs, tile_expert, n_valid, tile_rows, w_gu, w_down, tm)
    out = None
    sp = s // COMBINE_PARTS
    for part in range(COMBINE_PARTS):
        rows = slice(part * sp, (part + 1) * sp)
        yg = _sc_gather_rows(ys, jnp.concatenate([pos0[rows], pos1[rows]]))
        out = _moe_combine(x, yg, rw, mod, out, part, COMBINE_PARTS)
    return out


def kernel(x, c, positions, norm_mix_w, norm_ffn_w, ada_w, ada_b, e_w_in, a_ln_w, a_ln_b, a_w_s, a_b_s, b_w_grp,
           b_scale, e_w_out, ffn_w_gu, ffn_w_down, o_w_in, c_conv_w, q_a_norm, w_uq, kv_norm, w_ukv, q_norm_w,
           k_norm_w, o_w_out, router_w, moe_w_gu, moe_w_down):
    bsz, s, d = x.shape
    assert bsz == 1 and d == D_MODEL
    depth = ada_w.shape[0]
    nh = MLA_HEADS
    xs = x.reshape(s, d)
    mod = _ada_mod(c, ada_w, ada_b)
    cos_t, sin_t = _rope_tables(positions)
    row = lambda a: a.reshape(1, -1)

    for layer in range(depth):
        i = layer // 2
        m = mod[layer]
        if layer % 2 == 0:
            has_next = layer + 1 < depth
            dummy = jnp.zeros((s // 512, 16, LANES), F32)
            xs, (ffn_w_gu_bf16, ffn_w_down_bf16) = _even_mix(
                xs, m, row(norm_mix_w[layer]), e_w_in[i].astype(BF16), row(a_ln_w[i]), row(a_ln_b[i]), a_w_s[i],
                a_b_s[i].reshape(A_GROUPS, CHUNK, 1), b_w_grp[i].astype(BF16), row(b_scale[i]),
                e_w_out[i].astype(BF16), [ffn_w_gu[i], ffn_w_down[i]])
            xs, moe_w_gu_bf16 = _ffn(xs, m, row(norm_ffn_w[layer]), ffn_w_gu_bf16, ffn_w_down_bf16,
                                     moe_w_gu[i] if has_next else dummy)
        else:
            o2 = 3 * C_WIDTH + Q_LORA + KV_LORA
            w_in = o_w_in[i][:, :o2].astype(BF16)
            w_kpe = jnp.pad(o_w_in[i][:, o2:], ((0, 0), (0, LANES - QK_ROPE))).astype(BF16)
            bound = (1.02 * QK_HD * QK_HD ** -0.5 * LOG2E) * jnp.max(jnp.abs(q_norm_w[i])) * jnp.max(jnp.abs(k_norm_w[i]))
            wkv = w_ukv[i].reshape(KV_LORA, nh, QK_NOPE + V_HD)
            w_k = wkv[:, :, :QK_NOPE].reshape(KV_LORA, nh * QK_NOPE).astype(BF16)
            w_v_t = wkv[:, :, QK_NOPE:].reshape(KV_LORA, nh * V_HD).T.astype(BF16)
            yc, qt, k, vt, moe_w_down_bf16 = _odd_prep(
                xs, m, row(norm_mix_w[layer]), w_in, w_kpe, c_conv_w[i], row(q_a_norm[i]), w_uq[i].T.astype(BF16),
                row(kv_norm[i]), w_k, w_v_t, q_norm_w[i].reshape(QK_HD, 1), row(k_norm_w[i]),
                k_norm_w[i].reshape(QK_HD, 1), jnp.full((1, LANES), -bound, F32), cos_t, sin_t, moe_w_down[i])
            yd = lax.cond(bound <= ATTN_MAX_BOUND, _attention_shifted, _attention, qt, k, vt)
            wr = jnp.pad(router_w[i], ((0, 0), (0, LANES - N_EXPERTS)))
            wr_hi = lax.reduce_precision(wr, exponent_bits=8, mantissa_bits=7)
            w_router = jnp.concatenate([wr_hi, wr - wr_hi], axis=1).astype(BF16)
            xs, hp, rw, ridx, counts = _odd_out(xs, yc, yd, m, row(norm_ffn_w[layer]), o_w_out[i].astype(BF16),
                                                w_router)
            xs = _moe_sparse(xs, hp, rw, ridx, counts, m, moe_w_gu_bf16, moe_w_down_bf16)
    return xs.reshape(bsz, s, d)
```

```python
import functools

import jax
import jax.numpy as jnp
from jax import lax
from jax.experimental import pallas as pl
from jax.experimental.pallas import tpu as pltpu
from jax.experimental.pallas import tpu_sc as plsc

D_MODEL = 1024
SEQ = 16384
EPS = 1e-6
CHUNK = 128
A_WIDTH = 512
A_GROUPS = 4
B_WIDTH = 512
POOL_WINDOWS = (2, 4, 8, 16)
B_HD = 128
C_WIDTH = 512
MLA_HEADS = 4
Q_LORA = 256
KV_LORA = 256
QK_NOPE = 128
QK_ROPE = 64
QK_HD = QK_NOPE + QK_ROPE
V_HD = 128
ROPE_THETA = 10000.0
D_FF = 2816
N_EXPERTS = 8
D_FF_EXPERT = 3584

LANES = 128
SUBLANES = 8
POOL_HALO = 16
CONV_HALO = 8
QK_PAD = 256
V_PAD = 144
LOG2E = 1.4426950408889634
MASK_VALUE = -1e30
MOE_TILE = 512
CONV_COLS = 256
ATTN_MAX_BOUND = 48.0
ADA_COLS = 256
MOE_ROW_STEP = 128
COMBINE_PARTS = 4
SC_ROWS = 64
VMEM_LIMIT = 56 * 1024 * 1024
MOE_VMEM_LIMIT = 62 * 1024 * 1024

F32 = jnp.float32
BF16 = jnp.bfloat16


def _params(sem, vmem=VMEM_LIMIT, flags=None):
    return pltpu.CompilerParams(dimension_semantics=sem, vmem_limit_bytes=vmem, flags=flags)


def _const_spec(shape, single=False):
    nd = len(shape)
    return pl.BlockSpec(shape, lambda *_: (0,) * nd, pipeline_mode=pl.Buffered(1) if single else None)


def _rms_mod(x, nw, sc, sh):
    ms = jnp.mean(x * x, axis=-1, keepdims=True)
    return (x * lax.rsqrt(ms + EPS)) * (nw * (1.0 + sc)) + sh


def _ada_kernel(c_ref, w_ref, b_ref, o_ref):
    c = c_ref[...]
    ca = c * jax.nn.sigmoid(c)
    o_ref[0] = jnp.sum(w_ref[0] * ca, axis=0, keepdims=True) + b_ref[0]


def _ada_mod_even(c, ada_w, ada_b):
    depth, d, n = ada_w.shape
    n_even = (depth + 1) // 2
    tn = 3072
    return pl.pallas_call(
        _ada_kernel,
        grid=(n_even, n // tn),
        in_specs=[
            pl.BlockSpec((d, 1), lambda l, j: (0, 0)),
            pl.BlockSpec((1, d, tn), lambda l, j: (2 * l, 0, j)),
            pl.BlockSpec((1, 1, tn), lambda l, j: (2 * l, 0, j)),
        ],
        out_specs=pl.BlockSpec((1, 1, tn), lambda l, j: (l, 0, j)),
        out_shape=jax.ShapeDtypeStruct((n_even, 1, n), F32),
        compiler_params=_params(("arbitrary", "arbitrary")),
        name="ada_mod",
    )(c.reshape(d, 1), ada_w, ada_b.reshape(depth, 1, n))


def _even_mix_kernel(x_ref, mod_ref, nw_ref, win_ref, lnw_ref, lnb_ref, ws_ref, bs_ref, wg_ref, bsc_ref,
                     wout_ref, *rest):
    n_sides = (len(rest) - 4) // 2
    o_ref = rest[n_sides]
    halo_ref, sv_ref, yb_ref = rest[-3:]
    for side_ref, side_out_ref in zip(rest[:n_sides], rest[n_sides + 1:2 * n_sides + 1]):
        side_out_ref[...] = side_ref[...].astype(BF16)
    tm = x_ref.shape[0]
    i = pl.program_id(0)

    @pl.when(i == 0)
    def _():
        halo_ref[...] = jnp.zeros_like(halo_ref)

    d = D_MODEL
    x = x_ref[...]
    sh, sc, gate = mod_ref[:, 0:d], mod_ref[:, d:2 * d], mod_ref[:, 2 * d:3 * d]
    h = _rms_mod(x, nw_ref[...], sc, sh).astype(BF16)
    p = jnp.dot(h, win_ref[...], preferred_element_type=F32)

    gl = jax.nn.gelu(p[:, :2 * A_WIDTH])
    u = gl[:, :A_WIDTH]
    v = gl[:, A_WIDTH:]
    mu = jnp.mean(v, axis=-1, keepdims=True)
    vc = v - mu
    var = jnp.mean(vc * vc, axis=-1, keepdims=True)
    vn = (vc * lax.rsqrt(var + EPS) * lnw_ref[...] + lnb_ref[...]).astype(BF16)
    row = lax.broadcasted_iota(jnp.int32, (CHUNK, CHUNK), 0)
    col = lax.broadcasted_iota(jnp.int32, (CHUNK, CHUNK), 1)
    for g in range(A_GROUPS):
        w = jnp.where(col <= row, ws_ref[g], 0.0).astype(BF16)
        b = bs_ref[g]
        for c in range(tm // CHUNK):
            blk = vn[c * CHUNK:(c + 1) * CHUNK, g * LANES:(g + 1) * LANES]
            sv_ref[c * CHUNK:(c + 1) * CHUNK, g * LANES:(g + 1) * LANES] = (
                jnp.dot(w, blk, preferred_element_type=F32) + b)
    ya = (u * sv_ref[...]).astype(BF16)

    pb = p[:, 2 * A_WIDTH:]
    ext = jnp.concatenate([halo_ref[...], pb], axis=0)
    halo_ref[...] = pb[tm - POOL_HALO:, :]
    t_glob = i * tm + lax.broadcasted_iota(jnp.int32, (tm, 1), 0)
    s = ext
    width = 1
    for g, win in enumerate(POOL_WINDOWS):
        while width < win:
            s = s + pltpu.roll(s, width, 0)
            width *= 2
        cnt = jnp.minimum(t_glob + 1, win).astype(F32)
        sl = slice(g * B_HD, (g + 1) * B_HD)
        pooled = s[POOL_HALO:, sl] / cnt
        dg = (pooled - pb[:, sl]).astype(BF16)
        yb_ref[:, sl] = jnp.dot(dg, wg_ref[g], preferred_element_type=F32)
    yb = (yb_ref[...] * bsc_ref[...]).astype(BF16)

    mix = (jnp.dot(ya, wout_ref[:A_WIDTH, :], preferred_element_type=F32)
           + jnp.dot(yb, wout_ref[A_WIDTH:, :], preferred_element_type=F32))
    o_ref[...] = x + gate * mix


def _side_cast_spec(side, steps, axis):
    ne = side.shape[0]
    parts = steps // ne
    assert parts * ne == steps and side.shape[axis] % parts == 0
    block = list(side.shape)
    block[0] = 1
    block[axis] //= parts
    assert block[1] % 16 == 0 and block[2] % LANES == 0
    if axis == 1:
        return pl.BlockSpec(tuple(block), lambda i: (i // parts, i % parts, 0))
    return pl.BlockSpec(tuple(block), lambda i: (i // parts, 0, i % parts))


def _row_cast_spec(w, steps):
    rows = -(-w.shape[0] // steps)
    rows = -(-rows // 16) * 16
    last = -(-w.shape[0] // rows) - 1
    return pl.BlockSpec((rows, w.shape[1]), lambda i: (jnp.minimum(i, last), 0))


def _even_mix(x, mod, nw, w_in, ln_w, ln_b, w_s, b_s, w_grp, b_scale, w_out, row_sides, tm=512):
    s, d = x.shape
    steps = s // tm
    sides = list(row_sides)
    side_specs = [_row_cast_spec(w, steps) for w in row_sides]
    outs = pl.pallas_call(
        _even_mix_kernel,
        grid=(steps,),
        in_specs=[
            pl.BlockSpec((tm, d), lambda i: (i, 0)),
            _const_spec(mod.shape), _const_spec(nw.shape), _const_spec(w_in.shape),
            _const_spec(ln_w.shape), _const_spec(ln_b.shape), _const_spec(w_s.shape), _const_spec(b_s.shape),
            _const_spec(w_grp.shape), _const_spec(b_scale.shape), _const_spec(w_out.shape), *side_specs,
        ],
        out_specs=[pl.BlockSpec((tm, d), lambda i: (i, 0)), *side_specs],
        out_shape=[jax.ShapeDtypeStruct((s, d), F32)] + [jax.ShapeDtypeStruct(w.shape, BF16) for w in sides],
        scratch_shapes=[pltpu.VMEM((POOL_HALO, B_WIDTH), F32), pltpu.VMEM((tm, A_WIDTH), F32),
                        pltpu.VMEM((tm, B_WIDTH), F32)],
        compiler_params=_params(("arbitrary",)),
        name="even_mix",
    )(x, mod, nw, w_in, ln_w, ln_b, w_s, b_s, w_grp, b_scale, w_out, *sides)
    return outs[0], outs[1:]


def _ffn_kernel(x_ref, mod_ref, nw_ref, wgu_ref, wd_ref, side_ref, c_ref, aw_ref, ab_ref, o_ref, side_out_ref,
                mod_next_ref, *, n_chunks):
    side_out_ref[...] = side_ref[...].astype(BF16)
    c = c_ref[...]
    mod_next_ref[...] = jnp.sum(aw_ref[0] * (c * jax.nn.sigmoid(c)), axis=0, keepdims=True) + ab_ref[0]
    d = D_MODEL
    x = x_ref[...]
    sh, sc, gate = mod_ref[:, 3 * d:4 * d], mod_ref[:, 4 * d:5 * d], mod_ref[:, 5 * d:6 * d]
    h = _rms_mod(x, nw_ref[...], sc, sh).astype(BF16)
    ff = wd_ref.shape[0]
    tf = ff // n_chunks
    acc = jnp.zeros(x.shape, F32)
    for f in range(n_chunks):
        g = jnp.dot(h, wgu_ref[:, f * tf:(f + 1) * tf], preferred_element_type=F32)
        u = jnp.dot(h, wgu_ref[:, ff + f * tf:ff + (f + 1) * tf], preferred_element_type=F32)
        act = (g * jax.nn.sigmoid(g) * u).astype(BF16)
        acc = acc + jnp.dot(act, wd_ref[f * tf:(f + 1) * tf, :], preferred_element_type=F32)
    o_ref[...] = x + gate * acc


def _ffn(x, mod, nw, w_gu, w_down, side, c, ada_w, ada_b, next_layer, tm=512, n_chunks=11):
    s, d = x.shape
    steps = s // tm
    depth, _, n_mod = ada_w.shape
    last_col = n_mod // ADA_COLS - 1
    assert last_col < steps and n_mod % ADA_COLS == 0
    side_spec = _side_cast_spec(side, steps, 2)
    return pl.pallas_call(
        functools.partial(_ffn_kernel, n_chunks=n_chunks),
        grid=(steps,),
        in_specs=[
            pl.BlockSpec((tm, d), lambda i: (i, 0)),
            _const_spec(mod.shape), _const_spec(nw.shape), _const_spec(w_gu.shape, single=True),
            _const_spec(w_down.shape, single=True), side_spec,
            _const_spec((d, 1)),
            pl.BlockSpec((1, d, ADA_COLS), lambda i: (next_layer, 0, jnp.minimum(i, last_col))),
            pl.BlockSpec((1, 1, ADA_COLS), lambda i: (next_layer, 0, jnp.minimum(i, last_col))),
        ],
        out_specs=[pl.BlockSpec((tm, d), lambda i: (i, 0)), side_spec,
                   pl.BlockSpec((1, ADA_COLS), lambda i: (0, jnp.minimum(i, last_col)))],
        out_shape=[jax.ShapeDtypeStruct((s, d), F32), jax.ShapeDtypeStruct(side.shape, BF16),
                   jax.ShapeDtypeStruct((1, n_mod), F32)],
        compiler_params=_params(("arbitrary",)),
        name="ffn",
    )(x, mod, nw, w_gu, w_down, side, c.reshape(d, 1), ada_w, ada_b.reshape(depth, 1, n_mod))


def _rope_table_kernel(pos_ref, invf_ref, cos_ref, sin_ref):
    ang = pos_ref[...].astype(F32) * invf_ref[...]
    cos_ref[...] = jnp.cos(ang)
    sin_ref[...] = jnp.sin(ang)


def _rope_tables(positions):
    s = positions.shape[-1]
    half = QK_ROPE // 2
    inv_freq = ROPE_THETA ** (-jnp.arange(0, QK_ROPE, 2, dtype=F32) / QK_ROPE)
    return pl.pallas_call(
        _rope_table_kernel,
        out_shape=(jax.ShapeDtypeStruct((half, s), F32), jax.ShapeDtypeStruct((half, s), F32)),
        name="rope_tables",
    )(positions.reshape(1, s), inv_freq.reshape(half, 1))


def _odd_prep_kernel(x_ref, mod_ref, nw_ref, win_ref, wkpe_ref, cw_ref, qan_ref, wuqt_ref, kvn_ref, wk_ref, wvt_ref,
                     qnw_ref, knw_ref, knwc_ref, shift_ref, cost_ref, sint_ref, side_ref, yc_ref, qt_ref, k_ref, vt_ref,
                     side_out_ref, halo_ref):
    side_out_ref[...] = side_ref[...].astype(BF16)
    tm = x_ref.shape[0]
    i = pl.program_id(0)

    @pl.when(i == 0)
    def _():
        halo_ref[...] = jnp.zeros_like(halo_ref)

    d = D_MODEL
    x = x_ref[...]
    sh, sc = mod_ref[:, 0:d], mod_ref[:, d:2 * d]
    h = _rms_mod(x, nw_ref[...], sc, sh).astype(BF16)
    cw = C_WIDTH
    o0 = 3 * cw
    proj = lambda lo, hi: jnp.dot(h, win_ref[:, lo:hi], preferred_element_type=F32)
    kpe_pad = jnp.dot(h, wkpe_ref[...], preferred_element_type=F32)
    kpe = kpe_pad[:, :QK_ROPE]

    for c0 in range(0, cw, CONV_COLS):
        cols = slice(c0, c0 + CONV_COLS)
        z = proj(cw + c0, cw + c0 + CONV_COLS) * proj(2 * cw + c0, 2 * cw + c0 + CONV_COLS)
        ext = jnp.concatenate([halo_ref[:, cols], z], axis=0)
        halo_ref[:, cols] = z[tm - CONV_HALO:, :]
        z1 = pltpu.roll(ext, 1, 0)[CONV_HALO:, :]
        z2 = pltpu.roll(ext, 2, 0)[CONV_HALO:, :]
        conv = cw_ref[0:1, cols] * z2 + cw_ref[1:2, cols] * z1 + cw_ref[2:3, cols] * z
        yc_ref[:, cols] = (proj(c0, c0 + CONV_COLS) * conv).astype(BF16)

    cq = proj(o0, o0 + Q_LORA)
    ckv = proj(o0 + Q_LORA, o0 + Q_LORA + KV_LORA)
    cqn = cq * lax.rsqrt(jnp.mean(cq * cq, axis=-1, keepdims=True) + EPS) * qan_ref[...]
    ckvn = ckv * lax.rsqrt(jnp.mean(ckv * ckv, axis=-1, keepdims=True) + EPS) * kvn_ref[...]
    cqn_t = cqn.T.astype(BF16)
    ckvn_t = ckvn.T.astype(BF16)
    q_t = jnp.dot(wuqt_ref[...], cqn_t, preferred_element_type=F32)
    v_t = jnp.dot(wvt_ref[...], ckvn_t, preferred_element_type=F32)
    kn_all = jnp.dot(ckvn.astype(BF16), wk_ref[...], preferred_element_type=F32)

    nh = MLA_HEADS
    half = QK_ROPE // 2
    sm_scale = QK_HD ** -0.5
    cos_t, sin_t = cost_ref[...], sint_ref[...]
    kn_w = knw_ref[:, :QK_NOPE]
    kw_t = kpe_pad.T[:QK_ROPE, :] * knwc_ref[QK_NOPE:, :]
    k1, k2 = kw_t[:half, :], kw_t[half:, :]
    kr_t = jnp.concatenate([k1 * cos_t - k2 * sin_t, k2 * cos_t + k1 * sin_t,
                            jnp.zeros((LANES - QK_ROPE, tm), F32)], axis=0)
    kr = kr_t.T[:, :QK_ROPE]
    kpe_ss = jnp.sum(kpe * kpe, axis=-1, keepdims=True)
    npad = QK_PAD - QK_HD
    zrows = jnp.where(lax.broadcasted_iota(jnp.int32, (npad, tm), 0) == 0, shift_ref[:, 0:1], 0.0)
    zpad = jnp.where(lax.broadcasted_iota(jnp.int32, (tm, npad), 1) == 0, 1.0, 0.0)
    ones_rows = (lax.broadcasted_iota(jnp.int32, (V_PAD - V_HD, tm), 0) == 0).astype(BF16)
    for hd in range(nh):
        qh = q_t[hd * QK_HD:(hd + 1) * QK_HD, :]
        q_inv = lax.rsqrt(jnp.sum(qh * qh, axis=0, keepdims=True) / QK_HD + EPS) * (sm_scale * LOG2E)
        qw = qh * qnw_ref[...]
        x1, x2 = qw[QK_NOPE:QK_NOPE + half, :], qw[QK_NOPE + half:, :]
        qt_ref[hd] = jnp.concatenate(
            [qw[:QK_NOPE, :] * q_inv, (x1 * cos_t - x2 * sin_t) * q_inv, (x2 * cos_t + x1 * sin_t) * q_inv, zrows],
            axis=0).astype(BF16)
        kn = kn_all[:, hd * QK_NOPE:(hd + 1) * QK_NOPE]
        k_ss = jnp.sum(kn * kn, axis=-1, keepdims=True) + kpe_ss
        k_inv = lax.rsqrt(k_ss / QK_HD + EPS)
        k_ref[hd] = jnp.concatenate([kn * k_inv * kn_w, kr * k_inv, zpad], axis=1).astype(BF16)
        vt_ref[hd, :V_HD, :] = v_t[hd * V_HD:(hd + 1) * V_HD, :].astype(BF16)
        vt_ref[hd, V_HD:, :] = ones_rows


def _odd_prep(x, mod, nw, w_in, w_kpe, conv_w, q_a_norm, w_uq_t, kv_norm, w_k, w_v_t, q_norm_col, k_norm_w,
              k_norm_col, shift, cos_t, sin_t, side, tm=1024):
    s, d = x.shape
    nh = MLA_HEADS
    half = QK_ROPE // 2
    steps = s // tm
    side_spec = _side_cast_spec(side, steps, 1)
    consts = [mod, nw, w_in, w_kpe, conv_w, q_a_norm, w_uq_t, kv_norm, w_k, w_v_t, q_norm_col, k_norm_w, k_norm_col,
              shift]
    return pl.pallas_call(
        _odd_prep_kernel,
        grid=(steps,),
        in_specs=[pl.BlockSpec((tm, d), lambda i: (i, 0))] + [_const_spec(a.shape) for a in consts] + [
            pl.BlockSpec((half, tm), lambda i: (0, i)), pl.BlockSpec((half, tm), lambda i: (0, i)), side_spec],
        out_specs=[
            pl.BlockSpec((tm, C_WIDTH), lambda i: (i, 0)),
            pl.BlockSpec((nh, QK_PAD, tm), lambda i: (0, 0, i)),
            pl.BlockSpec((nh, tm, QK_PAD), lambda i: (0, i, 0)),
            pl.BlockSpec((nh, V_PAD, tm), lambda i: (0, 0, i)),
            side_spec,
        ],
        out_shape=[
            jax.ShapeDtypeStruct((s, C_WIDTH), BF16),
            jax.ShapeDtypeStruct((nh, QK_PAD, s), BF16),
            jax.ShapeDtypeStruct((nh, s, QK_PAD), BF16),
            jax.ShapeDtypeStruct((nh, V_PAD, s), BF16),
            jax.ShapeDtypeStruct(side.shape, BF16),
        ],
        scratch_shapes=[pltpu.VMEM((CONV_HALO, C_WIDTH), F32)],
        compiler_params=_params(("arbitrary",)),
        name="odd_prep",
    )(x, *consts, cos_t, sin_t, side)


def _causal_mask(s, lead=0):
    key = lax.broadcasted_iota(jnp.int32, s.shape, 0)
    qry = lead + lax.broadcasted_iota(jnp.int32, s.shape, 1)
    return jnp.where(key <= qry, s, MASK_VALUE)


def _attn_shifted_kernel(qt_ref, k_ref, vt_ref, o_ref, acc_ref, l_ref, *, tq, tk):
    qi = pl.program_id(1)
    acc_ref[...] = jnp.zeros_like(acc_ref)
    l_ref[...] = jnp.zeros_like(l_ref)
    r = tq // tk

    def block(j, diag_piece=None):
        q0 = 0 if diag_piece is None else diag_piece * tk
        start = pl.multiple_of(j * tk, tk)
        s = jnp.dot(k_ref[0, pl.ds(start, tk), :], qt_ref[0, :, q0:], preferred_element_type=F32)
        if diag_piece is not None:
            s = _causal_mask(s)
        p = jnp.exp2(s)
        l_ref[:, q0:] += jnp.sum(p.reshape(tk // SUBLANES, SUBLANES, p.shape[1]), axis=0)
        acc_ref[:, q0:] += jnp.dot(vt_ref[0, :V_HD, pl.ds(start, tk)], p.astype(BF16), preferred_element_type=F32)

    def body(t, carry):
        for b in range(r):
            block(r * t + b)
        return carry

    lax.fori_loop(0, qi, body, 0)
    for b in range(r):
        block(r * qi + b, diag_piece=b)
    out_t = acc_ref[...] / jnp.sum(l_ref[...], axis=0, keepdims=True)
    o_ref[...] = out_t.T.astype(o_ref.dtype)


def _attention_shifted(qt, k, vt, tq=2048, tk=512):
    nh, _, s = qt.shape
    return pl.pallas_call(
        functools.partial(_attn_shifted_kernel, tq=tq, tk=tk),
        grid=(nh, s // tq),
        in_specs=[
            pl.BlockSpec((1, QK_PAD, tq), lambda h, i: (h, 0, i)),
            pl.BlockSpec((1, s, QK_PAD), lambda h, i: (h, 0, 0)),
            pl.BlockSpec((1, V_PAD, s), lambda h, i: (h, 0, 0)),
        ],
        out_specs=pl.BlockSpec((tq, V_HD), lambda h, i: (i, h)),
        out_shape=jax.ShapeDtypeStruct((s, nh * V_HD), BF16),
        scratch_shapes=[pltpu.VMEM((V_HD, tq), F32), pltpu.VMEM((SUBLANES, tq), F32)],
        compiler_params=_params(("arbitrary", "arbitrary")),
        name="mla_attention_shifted",
    )(qt, k, vt)


def _attn_kernel(qt_ref, k_ref, vt_ref, o_ref, s0_ref, s1_ref, bm0_ref, bm1_ref, m_ref, acc_ref, *, tq, tk):
    qi = pl.program_id(1)
    m_ref[...] = jnp.full_like(m_ref, MASK_VALUE)
    acc_ref[...] = jnp.zeros_like(acc_ref)

    def scores(j, s_ref, bm_ref, diag_offset=None):
        q0 = 0 if diag_offset is None else diag_offset
        start = pl.multiple_of(j * tk, tk)
        s = jnp.dot(k_ref[0, pl.ds(start, tk), :], qt_ref[0, :, q0:], preferred_element_type=F32)
        if diag_offset is not None:
            s = _causal_mask(s)
        s_ref[:, q0:] = s
        bm_ref[:, q0:] = jnp.max(s, axis=0, keepdims=True)

    def consume(j, s_ref, bm_ref, q0=0):
        m_prev = m_ref[:, q0:]
        m_new = jnp.maximum(m_prev, bm_ref[:, q0:])
        alpha = jnp.exp2(m_prev - m_new)
        p = jnp.exp2((s_ref[:, q0:] - m_new).astype(BF16))
        start = pl.multiple_of(j * tk, tk)
        vt = vt_ref[0, :, pl.ds(start, tk)]
        acc_ref[:, q0:] = alpha * acc_ref[:, q0:] + jnp.dot(vt, p, preferred_element_type=F32)
        m_ref[:, q0:] = m_new

    r = tq // tk
    slots = ((s0_ref, bm0_ref), (s1_ref, bm1_ref))

    def trip(j0, then_diagonal):
        for b in range(r):
            first_diag = then_diagonal and b == r - 1
            scores(j0 + b + 1, *slots[(b + 1) % 2], diag_offset=0 if first_diag else None)
            consume(j0 + b, *slots[b % 2])

    def diagonal(jd):
        for b in range(r):
            if b + 1 < r:
                scores(jd + b + 1, *slots[(b + 1) % 2], diag_offset=(b + 1) * tk)
            consume(jd + b, *slots[b % 2], q0=b * tk)

    @pl.when(qi > 0)
    def _():
        scores(0, *slots[0])

    n_plain = jnp.maximum(qi - 1, 0)

    def body(t, carry):
        trip(r * (2 * t), False)
        trip(r * (2 * t + 1), False)
        return carry

    lax.fori_loop(0, n_plain // 2, body, 0)

    @pl.when(n_plain % 2 == 1)
    def _():
        trip(r * (n_plain - 1), False)

    @pl.when(qi > 0)
    def _():
        trip(r * (qi - 1), True)
        diagonal(r * qi)

    @pl.when(qi == 0)
    def _():
        scores(0, *slots[0], diag_offset=0)
        diagonal(0)

    acc = acc_ref[...]
    out_t = acc[:V_HD, :] / acc[V_HD:V_HD + 1, :]
    o_ref[...] = out_t.T.astype(o_ref.dtype)


def _attention(qt, k, vt, tq=1024, tk=512):
    nh, _, s = qt.shape
    assert tq % (2 * tk) == 0
    return pl.pallas_call(
        functools.partial(_attn_kernel, tq=tq, tk=tk),
        grid=(nh, s // tq),
        in_specs=[
            pl.BlockSpec((1, QK_PAD, tq), lambda h, i: (h, 0, i)),
            pl.BlockSpec((1, s, QK_PAD), lambda h, i: (h, 0, 0)),
            pl.BlockSpec((1, V_PAD, s), lambda h, i: (h, 0, 0)),
        ],
        out_specs=pl.BlockSpec((tq, V_HD), lambda h, i: (i, h)),
        out_shape=jax.ShapeDtypeStruct((s, nh * V_HD), BF16),
        scratch_shapes=[pltpu.VMEM((tk, tq), F32), pltpu.VMEM((tk, tq), F32), pltpu.VMEM((1, tq), F32),
                        pltpu.VMEM((1, tq), F32), pltpu.VMEM((1, tq), F32), pltpu.VMEM((V_PAD, tq), F32)],
        compiler_params=_params(("arbitrary", "arbitrary")),
        name="mla_attention",
    )(qt, k, vt)


def _pack_bf16_pairs(x):
    n = x.shape[1] // 2
    lo = pltpu.bitcast(x[:, :n].astype(BF16).astype(F32), jnp.uint32)
    hi = pltpu.bitcast(x[:, n:].astype(BF16).astype(F32), jnp.uint32)
    return (lo >> 16) | hi


def _unpack_bf16_pairs(p):
    lo = pltpu.bitcast(p << 16, F32)
    hi = pltpu.bitcast(p & jnp.uint32(0xFFFF0000), F32)
    return jnp.concatenate([lo, hi], axis=1)


def _odd_out_kernel(x_ref, yc_ref, yd_ref, mod_ref, nw_ref, wout_ref, wr_ref, before_ref, x_out_ref, h_ref, rw_ref,
                    ridx_ref, cnt_ref):
    d = D_MODEL
    x = x_ref[...]
    gate_m = mod_ref[:, 2 * d:3 * d]
    sh, sc = mod_ref[:, 3 * d:4 * d], mod_ref[:, 4 * d:5 * d]
    mix = (jnp.dot(yc_ref[...], wout_ref[:C_WIDTH, :], preferred_element_type=F32)
           + jnp.dot(yd_ref[...], wout_ref[C_WIDTH:, :], preferred_element_type=F32))
    x1 = x + gate_m * mix
    x_out_ref[...] = x1
    h = _rms_mod(x1, nw_ref[...], sc, sh)
    h_ref[...] = _pack_bf16_pairs(h)

    h_hi = h.astype(BF16)
    h_lo = (h - h_hi.astype(F32)).astype(BF16)
    hw = jnp.dot(h_hi, wr_ref[...], preferred_element_type=F32)
    logits = hw[:, :LANES] + (hw[:, LANES:] + jnp.dot(h_lo, wr_ref[:, :LANES], preferred_element_type=F32))
    lt = logits.T[:N_EXPERTS, :]
    ex = lax.broadcasted_iota(jnp.int32, lt.shape, 0)
    m1 = jnp.max(lt, axis=0, keepdims=True)
    i1 = jnp.min(jnp.where(lt == m1, ex, N_EXPERTS), axis=0, keepdims=True)
    rest = jnp.where(ex == i1, -jnp.inf, lt)
    m2 = jnp.max(rest, axis=0, keepdims=True)
    i2 = jnp.min(jnp.where(rest == m2, ex, N_EXPERTS), axis=0, keepdims=True)
    e2 = jnp.exp(m2 - m1)
    w1 = 1.0 / (1.0 + e2)
    w2 = e2 / (1.0 + e2)
    row = lax.broadcasted_iota(jnp.int32, (LANES, lt.shape[1]), 0)
    rw_ref[...] = jnp.where(row == 0, w1, jnp.where(row == 1, w2, 0.0)).T

    i = pl.program_id(0)

    @pl.when(i == 0)
    def _():
        cnt_ref[...] = jnp.zeros_like(cnt_ref)

    onehot = jnp.where((ex == i1) | (ex == i2), 1.0, 0.0)
    prefix = jnp.dot(onehot.astype(BF16), before_ref[...], preferred_element_type=F32) + cnt_ref[...]
    r1 = jnp.sum(jnp.where(ex == i1, prefix, 0.0), axis=0, keepdims=True)
    r2 = jnp.sum(jnp.where(ex == i2, prefix, 0.0), axis=0, keepdims=True)
    cnt_ref[...] += jnp.sum(onehot, axis=1, keepdims=True)
    ridx_ref[...] = jnp.where(ex == 0, i1, jnp.where(ex == 1, i2, jnp.where(
        ex == 2, r1.astype(jnp.int32), jnp.where(ex == 3, r2.astype(jnp.int32), 0))))


def _odd_out(x, yc, yd, mod, nw, w_out, w_router, tm=1024):
    s, d = x.shape
    return pl.pallas_call(
        _odd_out_kernel,
        grid=(s // tm,),
        in_specs=[
            pl.BlockSpec((tm, d), lambda i: (i, 0)),
            pl.BlockSpec((tm, C_WIDTH), lambda i: (i, 0)),
            pl.BlockSpec((tm, MLA_HEADS * V_HD), lambda i: (i, 0)),
            _const_spec(mod.shape), _const_spec(nw.shape), _const_spec(w_out.shape), _const_spec(w_router.shape),
            _const_spec((tm, tm)),
        ],
        out_specs=[
            pl.BlockSpec((tm, d), lambda i: (i, 0)),
            pl.BlockSpec((tm, d // 2), lambda i: (i, 0)),
            pl.BlockSpec((tm, LANES), lambda i: (i, 0)),
            pl.BlockSpec((SUBLANES, tm), lambda i: (0, i)),
            pl.BlockSpec((N_EXPERTS, 1), lambda i: (0, 0)),
        ],
        out_shape=[
            jax.ShapeDtypeStruct((s, d), F32),
            jax.ShapeDtypeStruct((s, d // 2), jnp.uint32),
            jax.ShapeDtypeStruct((s, LANES), F32),
            jax.ShapeDtypeStruct((SUBLANES, s), jnp.int32),
            jax.ShapeDtypeStruct((N_EXPERTS, 1), F32),
        ],
        compiler_params=_params(("arbitrary",)),
        name="odd_out_router",
    )(x, yc, yd, mod, nw, w_out, w_router, jnp.triu(jnp.ones((tm, tm), BF16), 1))


def _sc_workers():
    info = plsc.get_sparse_core_info()
    return info.num_cores, info.num_cores * info.num_subcores


def _sc_scatter_rows(x, idx0, idx1, out_rows):
    n, w = x.shape
    nc, nw = _sc_workers()
    per_w = n // nw
    nch = per_w // SC_ROWS
    assert nch % 2 == 0 and nch >= 2
    mesh = plsc.VectorSubcoreMesh(core_axis_name="c", subcore_axis_name="s")

    @functools.partial(
        pl.kernel, mesh=mesh, out_type=jax.ShapeDtypeStruct((out_rows, w), x.dtype),
        scratch_types=[pltpu.VMEM((nch, SC_ROWS), jnp.int32), pltpu.VMEM((nch, SC_ROWS), jnp.int32),
                       pltpu.VMEM((SC_ROWS, w), x.dtype), pltpu.VMEM((SC_ROWS, w), x.dtype),
                       pltpu.SemaphoreType.DMA, pltpu.SemaphoreType.DMA, pltpu.SemaphoreType.DMA,
                       pltpu.SemaphoreType.DMA],
        name="moe_dispatch")
    def scatter(x_hbm, i0_hbm, i1_hbm, out_hbm, i0_v, i1_v, buf0, buf1, lsem0, lsem1, ssem0, ssem1):
        wid = lax.axis_index("s") * nc + lax.axis_index("c")
        pltpu.sync_copy(i0_hbm.at[wid], i0_v)
        pltpu.sync_copy(i1_hbm.at[wid], i1_v)
        base = wid * per_w

        def load(c, buf, sem):
            return pltpu.make_async_copy(x_hbm.at[pl.ds(base + c * SC_ROWS, SC_ROWS)], buf, sem)

        def put(c, buf):
            first = pltpu.make_async_copy(buf, out_hbm.at[i0_v.at[c]], ssem0)
            second = pltpu.make_async_copy(buf, out_hbm.at[i1_v.at[c]], ssem1)
            first.start()
            second.start()
            first.wait()
            second.wait()

        def pair(c, prefetch_next):
            load(c + 1, buf1, lsem1).start()
            load(c, buf0, lsem0).wait()
            put(c, buf0)
            if prefetch_next:
                load(c + 2, buf0, lsem0).start()
            load(c + 1, buf1, lsem1).wait()
            put(c + 1, buf1)

        load(0, buf0, lsem0).start()

        def body(t, carry):
            pair(2 * t, True)
            return carry

        lax.fori_loop(0, nch // 2 - 1, body, 0)
        pair(nch - 2, False)

    return scatter(x, idx0.reshape(nw, nch, SC_ROWS), idx1.reshape(nw, nch, SC_ROWS))


def _sc_gather_rows(table, idx):
    _, w = table.shape
    b = idx.shape[0]
    nc, nw = _sc_workers()
    per_w = b // nw
    nch = per_w // SC_ROWS
    mesh = plsc.VectorSubcoreMesh(core_axis_name="c", subcore_axis_name="s")

    assert nch % 2 == 0 and nch >= 2

    @functools.partial(
        pl.kernel, mesh=mesh, out_type=jax.ShapeDtypeStruct((b, w), table.dtype),
        scratch_types=[pltpu.VMEM((nch, SC_ROWS), jnp.int32), pltpu.VMEM((SC_ROWS, w), table.dtype),
                       pltpu.VMEM((SC_ROWS, w), table.dtype), pltpu.SemaphoreType.DMA, pltpu.SemaphoreType.DMA],
        name="moe_combine_gather")
    def gather(table_hbm, idx_hbm, out_hbm, idx_v, buf0, buf1, sem0, sem1):
        wid = lax.axis_index("s") * nc + lax.axis_index("c")
        pltpu.sync_copy(idx_hbm.at[wid], idx_v)
        base = wid * per_w

        def fetch(c, buf, sem):
            return pltpu.make_async_copy(table_hbm.at[idx_v.at[c]], buf, sem)

        def put(c, buf):
            pltpu.sync_copy(buf, out_hbm.at[pl.ds(base + c * SC_ROWS, SC_ROWS)])

        def pair(c, prefetch_next):
            fetch(c + 1, buf1, sem1).start()
            fetch(c, buf0, sem0).wait()
            put(c, buf0)
            if prefetch_next:
                fetch(c + 2, buf0, sem0).start()
            fetch(c + 1, buf1, sem1).wait()
            put(c + 1, buf1)

        fetch(0, buf0, sem0).start()

        def body(t, carry):
            pair(2 * t, True)
            return carry

        lax.fori_loop(0, nch // 2 - 1, body, 0)
        pair(nch - 2, False)

    return gather(table, idx.reshape(nw, nch, SC_ROWS))


def _moe_kernel(te_ref, nv_ref, rows_ref, xs_ref, wg_ref, wu_ref, wd_ref, ys_ref, *, n_chunks):
    j = pl.program_id(0)
    tm = xs_ref.shape[0]
    tf = wd_ref.shape[1] // n_chunks

    def expert_ffn(rows):
        h = _unpack_bf16_pairs(xs_ref[:rows, :]).astype(BF16)
        y = None
        for c in range(n_chunks):
            cols = slice(c * tf, (c + 1) * tf)
            g = jnp.dot(h, wg_ref[0, :, cols], preferred_element_type=F32)
            u = jnp.dot(h, wu_ref[0, :, cols], preferred_element_type=F32)
            act = (g * jax.nn.sigmoid(g) * u).astype(BF16)
            yc = jnp.dot(act, wd_ref[0, cols, :], preferred_element_type=F32)
            y = yc if y is None else y + yc
        ys_ref[:rows, :] = _pack_bf16_pairs(y)

    @pl.when(j < nv_ref[0])
    def _():
        for rows in range(MOE_ROW_STEP, tm + 1, MOE_ROW_STEP):
            pl.when(rows_ref[j] == rows)(functools.partial(expert_ffn, rows))


def _moe_grouped(xs, tile_expert, n_valid, tile_rows, w_gu, w_down, tm, n_chunks=2):
    p_rows, dh = xs.shape
    d = 2 * dh
    ne, _, ff2 = w_gu.shape
    ff = ff2 // 2
    assert ff % (n_chunks * LANES) == 0

    def tile(j, nv):
        return jnp.minimum(j, nv[0] - 1)

    grid_spec = pltpu.PrefetchScalarGridSpec(
        num_scalar_prefetch=3,
        grid=(p_rows // tm,),
        in_specs=[
            pl.BlockSpec((tm, dh), lambda j, te, nv, tr: (tile(j, nv), 0)),
            pl.BlockSpec((1, d, ff), lambda j, te, nv, tr: (te[tile(j, nv)], 0, 0)),
            pl.BlockSpec((1, d, ff), lambda j, te, nv, tr: (te[tile(j, nv)], 0, 1)),
            pl.BlockSpec((1, ff, d), lambda j, te, nv, tr: (te[tile(j, nv)], 0, 0)),
        ],
        out_specs=pl.BlockSpec((tm, dh), lambda j, te, nv, tr: (tile(j, nv), 0)),
    )
    return pl.pallas_call(
        functools.partial(_moe_kernel, n_chunks=n_chunks),
        grid_spec=grid_spec,
        out_shape=jax.ShapeDtypeStruct((p_rows, dh), jnp.uint32),
        compiler_params=_params(("arbitrary",), vmem=MOE_VMEM_LIMIT),
        name="moe_experts",
    )(tile_expert, n_valid, tile_rows, xs, w_gu, w_gu, w_down)


def _moe_combine_kernel(x_ref, y0_ref, y1_ref, rw_ref, mod_ref, *rest):
    o_ref = rest[-1]
    d = D_MODEL
    w1 = rw_ref[:, 0:1]
    w2 = rw_ref[:, 1:2]
    y = w1 * _unpack_bf16_pairs(y0_ref[...]) + w2 * _unpack_bf16_pairs(y1_ref[...])
    o_ref[...] = x_ref[...] + mod_ref[:, 5 * d:6 * d] * y


def _moe_combine(x, yg, rw, mod, prev, part, n_parts, tm=1024):
    s, d = x.shape
    nb = s // tm // n_parts
    first = part * nb
    in_specs = [
        pl.BlockSpec((tm, d), lambda i: (first + i, 0)),
        pl.BlockSpec((tm, d // 2), lambda i: (i, 0)),
        pl.BlockSpec((tm, d // 2), lambda i: (nb + i, 0)),
        pl.BlockSpec((tm, LANES), lambda i: (first + i, 0)),
        _const_spec(mod.shape),
    ]
    args = [x, yg, yg, rw, mod]
    aliases = {}
    if prev is not None:
        in_specs.append(pl.BlockSpec(memory_space=pl.ANY))
        args.append(prev)
        aliases = {len(args) - 1: 0}
    return pl.pallas_call(
        _moe_combine_kernel,
        grid=(nb,),
        in_specs=in_specs,
        out_specs=pl.BlockSpec((tm, d), lambda i: (first + i, 0)),
        out_shape=jax.ShapeDtypeStruct((s, d), F32),
        input_output_aliases=aliases,
        compiler_params=_params(("arbitrary",)),
        name="moe_combine",
    )(*args)


def _moe_sparse(x, h_packed, rw, ridx, counts, mod, w_gu, w_down, tm=MOE_TILE):
    s = x.shape[0]
    ne = w_gu.shape[0]
    n_tiles = (2 * s) // tm + ne
    cnt = counts[:ne, 0].astype(jnp.int32)
    padded = ((cnt + tm - 1) // tm) * tm
    ends = jnp.cumsum(padded)
    offs = ends - padded
    experts = jnp.arange(ne, dtype=jnp.int32)
    off_of = lambda e: jnp.sum(jnp.where(e[:, None] == experts[None, :], offs[None, :], 0), axis=1)
    pos0 = off_of(ridx[0]) + ridx[2]
    pos1 = off_of(ridx[1]) + ridx[3]
    tile_start = jnp.arange(n_tiles, dtype=jnp.int32) * tm
    tile_expert = jnp.minimum(jnp.sum(tile_start[:, None] >= ends[None, :], axis=1), ne - 1).astype(jnp.int32)
    n_valid = (ends[-1] // tm).reshape(1).astype(jnp.int32)
    filled = jnp.clip((offs + cnt)[tile_expert] - tile_start, 0, tm)
    tile_rows = (((filled + MOE_ROW_STEP - 1) // MOE_ROW_STEP) * MOE_ROW_STEP).astype(jnp.int32)
    xs = _sc_scatter_rows(h_packed, pos0, pos1, n_tiles * tm)
    ys = _moe_grouped(xs, tile_expert, n_valid, tile_rows, w_gu, w_down, tm)
    out = None
    sp = s // COMBINE_PARTS
    for part in range(COMBINE_PARTS):
        rows = slice(part * sp, (part + 1) * sp)
        yg = _sc_gather_rows(ys, jnp.concatenate([pos0[rows], pos1[rows]]))
        out = _moe_combine(x, yg, rw, mod, out, part, COMBINE_PARTS)
    return out


def kernel(x, c, positions, norm_mix_w, norm_ffn_w, ada_w, ada_b, e_w_in, a_ln_w, a_ln_b, a_w_s, a_b_s, b_w_grp,
           b_scale, e_w_out, ffn_w_gu, ffn_w_down, o_w_in, c_conv_w, q_a_norm, w_uq, kv_norm, w_ukv, q_norm_w,
           k_norm_w, o_w_out, router_w, moe_w_gu, moe_w_down):
    bsz, s, d = x.shape
    assert bsz == 1 and d == D_MODEL
    depth = ada_w.shape[0]
    nh = MLA_HEADS
    xs = x.reshape(s, d)
    mod_even = _ada_mod_even(c, ada_w, ada_b)
    cos_t, sin_t = _rope_tables(positions)
    row = lambda a: a.reshape(1, -1)

    for layer in range(depth):
        i = layer // 2
        if layer % 2 == 0:
            m = mod_even[i]
            has_next = layer + 1 < depth
            dummy = jnp.zeros((s // 512, 16, LANES), F32)
            xs, (ffn_w_gu_bf16, ffn_w_down_bf16) = _even_mix(
                xs, m, row(norm_mix_w[layer]), e_w_in[i].astype(BF16), row(a_ln_w[i]), row(a_ln_b[i]), a_w_s[i],
                a_b_s[i].reshape(A_GROUPS, CHUNK, 1), b_w_grp[i].astype(BF16), row(b_scale[i]),
                e_w_out[i].astype(BF16), [ffn_w_gu[i], ffn_w_down[i]])
            xs, moe_w_gu_bf16, m = _ffn(xs, m, row(norm_ffn_w[layer]), ffn_w_gu_bf16, ffn_w_down_bf16,
                                        moe_w_gu[i] if has_next else dummy, c, ada_w, ada_b,
                                        min(layer + 1, depth - 1))
        else:
            o2 = 3 * C_WIDTH + Q_LORA + KV_LORA
            w_in = o_w_in[i][:, :o2].astype(BF16)
            w_kpe = jnp.pad(o_w_in[i][:, o2:], ((0, 0), (0, LANES - QK_ROPE))).astype(BF16)
            bound = (1.02 * QK_HD * QK_HD ** -0.5 * LOG2E) * jnp.max(jnp.abs(q_norm_w[i])) * jnp.max(jnp.abs(k_norm_w[i]))
            wkv = w_ukv[i].reshape(KV_LORA, nh, QK_NOPE + V_HD)
            w_k = wkv[:, :, :QK_NOPE].reshape(KV_LORA, nh * QK_NOPE).astype(BF16)
            w_v_t = wkv[:, :, QK_NOPE:].reshape(KV_LORA, nh * V_HD).T.astype(BF16)
            yc, qt, k, vt, moe_w_down_bf16 = _odd_prep(
                xs, m, row(norm_mix_w[layer]), w_in, w_kpe, c_conv_w[i], row(q_a_norm[i]), w_uq[i].T.astype(BF16),
                row(kv_norm[i]), w_k, w_v_t, q_norm_w[i].reshape(QK_HD, 1), row(k_norm_w[i]),
                k_norm_w[i].reshape(QK_HD, 1), jnp.full((1, LANES), -bound, F32), cos_t, sin_t, moe_w_down[i])
            yd = lax.cond(bound <= ATTN_MAX_BOUND, _attention_shifted, _attention, qt, k, vt)
            wr = jnp.pad(router_w[i], ((0, 0), (0, LANES - N_EXPERTS)))
            wr_hi = lax.reduce_precision(wr, exponent_bits=8, mantissa_bits=7)
            w_router = jnp.concatenate([wr_hi, wr - wr_hi], axis=1).astype(BF16)
            xs, hp, rw, ridx, counts = _odd_out(xs, yc, yd, m, row(norm_ffn_w[layer]), o_w_out[i].astype(BF16),
                                                w_router)
            xs = _moe_sparse(xs, hp, rw, ridx, counts, m, moe_w_gu_bf16, moe_w_down_bf16)
    return xs.reshape(bsz, s, d)
```

```python
import functools

import jax
import jax.numpy as jnp
from jax import lax
from jax.experimental import pallas as pl
from jax.experimental.pallas import tpu as pltpu
from jax.experimental.pallas import tpu_sc as plsc

D_MODEL = 1024
SEQ = 16384
EPS = 1e-6
CHUNK = 128
A_WIDTH = 512
A_GROUPS = 4
B_WIDTH = 512
POOL_WINDOWS = (2, 4, 8, 16)
B_HD = 128
C_WIDTH = 512
MLA_HEADS = 4
Q_LORA = 256
KV_LORA = 256
QK_NOPE = 128
QK_ROPE = 64
QK_HD = QK_NOPE + QK_ROPE
V_HD = 128
ROPE_THETA = 10000.0
D_FF = 2816
N_EXPERTS = 8
D_FF_EXPERT = 3584

LANES = 128
SUBLANES = 8
POOL_HALO = 16
CONV_HALO = 8
QK_PAD = 256
V_PAD = 144
LOG2E = 1.4426950408889634
MASK_VALUE = -1e30
MOE_TILE = 512
CONV_COLS = 256
ATTN_MAX_BOUND = 48.0
MOE_ROW_STEP = 128
COMBINE_PARTS = 4
SC_ROWS = 64
VMEM_LIMIT = 56 * 1024 * 1024
MOE_VMEM_LIMIT = 62 * 1024 * 1024

F32 = jnp.float32
BF16 = jnp.bfloat16


def _params(sem, vmem=VMEM_LIMIT, flags=None):
    return pltpu.CompilerParams(dimension_semantics=sem, vmem_limit_bytes=vmem, flags=flags)


def _const_spec(shape, single=False):
    nd = len(shape)
    return pl.BlockSpec(shape, lambda *_: (0,) * nd, pipeline_mode=pl.Buffered(1) if single else None)


def _rms_mod(x, nw, sc, sh):
    ms = jnp.mean(x * x, axis=-1, keepdims=True)
    return (x * lax.rsqrt(ms + EPS)) * (nw * (1.0 + sc)) + sh


def _ada_kernel(c_ref, w_ref, b_ref, pos_ref, invf_ref, o_ref, cos_ref, sin_ref):
    c = c_ref[...]
    ca = c * jax.nn.sigmoid(c)
    o_ref[0] = jnp.sum(w_ref[0] * ca, axis=0, keepdims=True) + b_ref[0]
    ang = pos_ref[...].astype(F32) * invf_ref[...]
    cos_ref[...] = jnp.cos(ang)
    sin_ref[...] = jnp.sin(ang)


def _ada_mod_and_rope(c, ada_w, ada_b, positions):
    depth, d, n = ada_w.shape
    s = positions.shape[-1]
    half = QK_ROPE // 2
    tn = 3072
    nj = n // tn
    steps = depth * nj
    ts = s // steps
    assert ts * steps == s and ts % LANES == 0
    inv_freq = ROPE_THETA ** (-jnp.arange(0, QK_ROPE, 2, dtype=F32) / QK_ROPE)
    table_spec = pl.BlockSpec((half, ts), lambda l, j: (0, l * nj + j))
    return pl.pallas_call(
        _ada_kernel,
        grid=(depth, nj),
        in_specs=[
            pl.BlockSpec((d, 1), lambda l, j: (0, 0)),
            pl.BlockSpec((1, d, tn), lambda l, j: (l, 0, j)),
            pl.BlockSpec((1, 1, tn), lambda l, j: (l, 0, j)),
            pl.BlockSpec((1, ts), lambda l, j: (0, l * nj + j)),
            pl.BlockSpec((half, 1), lambda l, j: (0, 0)),
        ],
        out_specs=[pl.BlockSpec((1, 1, tn), lambda l, j: (l, 0, j)), table_spec, table_spec],
        out_shape=[jax.ShapeDtypeStruct((depth, 1, n), F32), jax.ShapeDtypeStruct((half, s), F32),
                   jax.ShapeDtypeStruct((half, s), F32)],
        compiler_params=_params(("arbitrary", "arbitrary")),
        name="ada_mod",
    )(c.reshape(d, 1), ada_w, ada_b.reshape(depth, 1, n), positions.reshape(1, s), inv_freq.reshape(half, 1))


def _even_mix_kernel(x_ref, mod_ref, nw_ref, win_ref, lnw_ref, lnb_ref, ws_ref, bs_ref, wg_ref, bsc_ref,
                     wout_ref, *rest):
    n_sides = (len(rest) - 4) // 2
    o_ref = rest[n_sides]
    halo_ref, sv_ref, yb_ref = rest[-3:]
    for side_ref, side_out_ref in zip(rest[:n_sides], rest[n_sides + 1:2 * n_sides + 1]):
        side_out_ref[...] = side_ref[...].astype(BF16)
    tm = x_ref.shape[0]
    i = pl.program_id(0)

    @pl.when(i == 0)
    def _():
        halo_ref[...] = jnp.zeros_like(halo_ref)

    d = D_MODEL
    x = x_ref[...]
    sh, sc, gate = mod_ref[:, 0:d], mod_ref[:, d:2 * d], mod_ref[:, 2 * d:3 * d]
    h = _rms_mod(x, nw_ref[...], sc, sh).astype(BF16)
    p = jnp.dot(h, win_ref[...], preferred_element_type=F32)

    gl = jax.nn.gelu(p[:, :2 * A_WIDTH])
    u = gl[:, :A_WIDTH]
    v = gl[:, A_WIDTH:]
    mu = jnp.mean(v, axis=-1, keepdims=True)
    vc = v - mu
    var = jnp.mean(vc * vc, axis=-1, keepdims=True)
    vn = (vc * lax.rsqrt(var + EPS) * lnw_ref[...] + lnb_ref[...]).astype(BF16)
    row = lax.broadcasted_iota(jnp.int32, (CHUNK, CHUNK), 0)
    col = lax.broadcasted_iota(jnp.int32, (CHUNK, CHUNK), 1)
    for g in range(A_GROUPS):
        w = jnp.where(col <= row, ws_ref[g], 0.0).astype(BF16)
        b = bs_ref[g]
        for c in range(tm // CHUNK):
            blk = vn[c * CHUNK:(c + 1) * CHUNK, g * LANES:(g + 1) * LANES]
            sv_ref[c * CHUNK:(c + 1) * CHUNK, g * LANES:(g + 1) * LANES] = (
                jnp.dot(w, blk, preferred_element_type=F32) + b)
    ya = (u * sv_ref[...]).astype(BF16)

    pb = p[:, 2 * A_WIDTH:]
    ext = jnp.concatenate([halo_ref[...], pb], axis=0)
    halo_ref[...] = pb[tm - POOL_HALO:, :]
    t_glob = i * tm + lax.broadcasted_iota(jnp.int32, (tm, 1), 0)
    s = ext
    width = 1
    for g, win in enumerate(POOL_WINDOWS):
        while width < win:
            s = s + pltpu.roll(s, width, 0)
            width *= 2
        cnt = jnp.minimum(t_glob + 1, win).astype(F32)
        sl = slice(g * B_HD, (g + 1) * B_HD)
        pooled = s[POOL_HALO:, sl] / cnt
        dg = (pooled - pb[:, sl]).astype(BF16)
        yb_ref[:, sl] = jnp.dot(dg, wg_ref[g], preferred_element_type=F32)
    yb = (yb_ref[...] * bsc_ref[...]).astype(BF16)

    mix = (jnp.dot(ya, wout_ref[:A_WIDTH, :], preferred_element_type=F32)
           + jnp.dot(yb, wout_ref[A_WIDTH:, :], preferred_element_type=F32))
    o_ref[...] = x + gate * mix


def _side_cast_spec(side, steps, axis):
    ne = side.shape[0]
    parts = steps // ne
    assert parts * ne == steps and side.shape[axis] % parts == 0
    block = list(side.shape)
    block[0] = 1
    block[axis] //= parts
    assert block[1] % 16 == 0 and block[2] % LANES == 0
    if axis == 1:
        return pl.BlockSpec(tuple(block), lambda i: (i // parts, i % parts, 0))
    return pl.BlockSpec(tuple(block), lambda i: (i // parts, 0, i % parts))


def _row_cast_spec(w, steps):
    rows = -(-w.shape[0] // steps)
    rows = -(-rows // 16) * 16
    last = -(-w.shape[0] // rows) - 1
    return pl.BlockSpec((rows, w.shape[1]), lambda i: (jnp.minimum(i, last), 0))


def _even_mix(x, mod, nw, w_in, ln_w, ln_b, w_s, b_s, w_grp, b_scale, w_out, row_sides, tm=512):
    s, d = x.shape
    steps = s // tm
    sides = list(row_sides)
    side_specs = [_row_cast_spec(w, steps) for w in row_sides]
    outs = pl.pallas_call(
        _even_mix_kernel,
        grid=(steps,),
        in_specs=[
            pl.BlockSpec((tm, d), lambda i: (i, 0)),
            _const_spec(mod.shape), _const_spec(nw.shape), _const_spec(w_in.shape),
            _const_spec(ln_w.shape), _const_spec(ln_b.shape), _const_spec(w_s.shape), _const_spec(b_s.shape),
            _const_spec(w_grp.shape), _const_spec(b_scale.shape), _const_spec(w_out.shape), *side_specs,
        ],
        out_specs=[pl.BlockSpec((tm, d), lambda i: (i, 0)), *side_specs],
        out_shape=[jax.ShapeDtypeStruct((s, d), F32)] + [jax.ShapeDtypeStruct(w.shape, BF16) for w in sides],
        scratch_shapes=[pltpu.VMEM((POOL_HALO, B_WIDTH), F32), pltpu.VMEM((tm, A_WIDTH), F32),
                        pltpu.VMEM((tm, B_WIDTH), F32)],
        compiler_params=_params(("arbitrary",)),
        name="even_mix",
    )(x, mod, nw, w_in, ln_w, ln_b, w_s, b_s, w_grp, b_scale, w_out, *sides)
    return outs[0], outs[1:]


def _ffn_kernel(x_ref, mod_ref, nw_ref, wgu_ref, wd_ref, side_ref, o_ref, side_out_ref, *, n_chunks):
    side_out_ref[...] = side_ref[...].astype(BF16)
    d = D_MODEL
    x = x_ref[...]
    sh, sc, gate = mod_ref[:, 3 * d:4 * d], mod_ref[:, 4 * d:5 * d], mod_ref[:, 5 * d:6 * d]
    h = _rms_mod(x, nw_ref[...], sc, sh).astype(BF16)
    ff = wd_ref.shape[0]
    tf = ff // n_chunks
    acc = jnp.zeros(x.shape, F32)
    for f in range(n_chunks):
        g = jnp.dot(h, wgu_ref[:, f * tf:(f + 1) * tf], preferred_element_type=F32)
        u = jnp.dot(h, wgu_ref[:, ff + f * tf:ff + (f + 1) * tf], preferred_element_type=F32)
        act = (g * jax.nn.sigmoid(g) * u).astype(BF16)
        acc = acc + jnp.dot(act, wd_ref[f * tf:(f + 1) * tf, :], preferred_element_type=F32)
    o_ref[...] = x + gate * acc


def _ffn(x, mod, nw, w_gu, w_down, side, tm=512, n_chunks=11):
    s, d = x.shape
    steps = s // tm
    side_spec = _side_cast_spec(side, steps, 2)
    return pl.pallas_call(
        functools.partial(_ffn_kernel, n_chunks=n_chunks),
        grid=(steps,),
        in_specs=[
            pl.BlockSpec((tm, d), lambda i: (i, 0)),
            _const_spec(mod.shape), _const_spec(nw.shape), _const_spec(w_gu.shape, single=True),
            _const_spec(w_down.shape, single=True), side_spec,
        ],
        out_specs=[pl.BlockSpec((tm, d), lambda i: (i, 0)), side_spec],
        out_shape=[jax.ShapeDtypeStruct((s, d), F32), jax.ShapeDtypeStruct(side.shape, BF16)],
        compiler_params=_params(("arbitrary",)),
        name="ffn",
    )(x, mod, nw, w_gu, w_down, side)


def _odd_prep_kernel(x_ref, mod_ref, nw_ref, win_ref, wkpe_ref, cw_ref, qan_ref, wuqt_ref, kvn_ref, wk_ref, wvt_ref,
                     qnw_ref, knw_ref, knwc_ref, shift_ref, cost_ref, sint_ref, side_ref, yc_ref, qt_ref, k_ref, vt_ref,
                     side_out_ref, halo_ref):
    side_out_ref[...] = side_ref[...].astype(BF16)
    tm = x_ref.shape[0]
    i = pl.program_id(0)

    @pl.when(i == 0)
    def _():
        halo_ref[...] = jnp.zeros_like(halo_ref)

    d = D_MODEL
    x = x_ref[...]
    sh, sc = mod_ref[:, 0:d], mod_ref[:, d:2 * d]
    h = _rms_mod(x, nw_ref[...], sc, sh).astype(BF16)
    cw = C_WIDTH
    o0 = 3 * cw
    proj = lambda lo, hi: jnp.dot(h, win_ref[:, lo:hi], preferred_element_type=F32)
    kpe_pad = jnp.dot(h, wkpe_ref[...], preferred_element_type=F32)
    kpe = kpe_pad[:, :QK_ROPE]

    for c0 in range(0, cw, CONV_COLS):
        cols = slice(c0, c0 + CONV_COLS)
        z = proj(cw + c0, cw + c0 + CONV_COLS) * proj(2 * cw + c0, 2 * cw + c0 + CONV_COLS)
        ext = jnp.concatenate([halo_ref[:, cols], z], axis=0)
        halo_ref[:, cols] = z[tm - CONV_HALO:, :]
        z1 = pltpu.roll(ext, 1, 0)[CONV_HALO:, :]
        z2 = pltpu.roll(ext, 2, 0)[CONV_HALO:, :]
        conv = cw_ref[0:1, cols] * z2 + cw_ref[1:2, cols] * z1 + cw_ref[2:3, cols] * z
        yc_ref[:, cols] = (proj(c0, c0 + CONV_COLS) * conv).astype(BF16)

    cq = proj(o0, o0 + Q_LORA)
    ckv = proj(o0 + Q_LORA, o0 + Q_LORA + KV_LORA)
    cqn = cq * lax.rsqrt(jnp.mean(cq * cq, axis=-1, keepdims=True) + EPS) * qan_ref[...]
    ckvn = ckv * lax.rsqrt(jnp.mean(ckv * ckv, axis=-1, keepdims=True) + EPS) * kvn_ref[...]
    cqn_t = cqn.T.astype(BF16)
    ckvn_t = ckvn.T.astype(BF16)
    q_t = jnp.dot(wuqt_ref[...], cqn_t, preferred_element_type=F32)
    v_t = jnp.dot(wvt_ref[...], ckvn_t, preferred_element_type=F32)
    kn_all = jnp.dot(ckvn.astype(BF16), wk_ref[...], preferred_element_type=F32)

    nh = MLA_HEADS
    half = QK_ROPE // 2
    sm_scale = QK_HD ** -0.5
    cos_t, sin_t = cost_ref[...], sint_ref[...]
    kn_w = knw_ref[:, :QK_NOPE]
    kw_t = kpe_pad.T[:QK_ROPE, :] * knwc_ref[QK_NOPE:, :]
    k1, k2 = kw_t[:half, :], kw_t[half:, :]
    kr_t = jnp.concatenate([k1 * cos_t - k2 * sin_t, k2 * cos_t + k1 * sin_t,
                            jnp.zeros((LANES - QK_ROPE, tm), F32)], axis=0)
    kr = kr_t.T[:, :QK_ROPE]
    kpe_ss = jnp.sum(kpe * kpe, axis=-1, keepdims=True)
    npad = QK_PAD - QK_HD
    zrows = jnp.where(lax.broadcasted_iota(jnp.int32, (npad, tm), 0) == 0, shift_ref[:, 0:1], 0.0)
    zpad = jnp.where(lax.broadcasted_iota(jnp.int32, (tm, npad), 1) == 0, 1.0, 0.0)
    ones_rows = (lax.broadcasted_iota(jnp.int32, (V_PAD - V_HD, tm), 0) == 0).astype(BF16)
    for hd in range(nh):
        qh = q_t[hd * QK_HD:(hd + 1) * QK_HD, :]
        q_inv = lax.rsqrt(jnp.sum(qh * qh, axis=0, keepdims=True) / QK_HD + EPS) * (sm_scale * LOG2E)
        qw = qh * qnw_ref[...]
        x1, x2 = qw[QK_NOPE:QK_NOPE + half, :], qw[QK_NOPE + half:, :]
        qt_ref[hd] = jnp.concatenate(
            [qw[:QK_NOPE, :] * q_inv, (x1 * cos_t - x2 * sin_t) * q_inv, (x2 * cos_t + x1 * sin_t) * q_inv, zrows],
            axis=0).astype(BF16)
        kn = kn_all[:, hd * QK_NOPE:(hd + 1) * QK_NOPE]
        k_ss = jnp.sum(kn * kn, axis=-1, keepdims=True) + kpe_ss
        k_inv = lax.rsqrt(k_ss / QK_HD + EPS)
        k_ref[hd] = jnp.concatenate([kn * k_inv * kn_w, kr * k_inv, zpad], axis=1).astype(BF16)
        vt_ref[hd, :V_HD, :] = v_t[hd * V_HD:(hd + 1) * V_HD, :].astype(BF16)
        vt_ref[hd, V_HD:, :] = ones_rows


def _odd_prep(x, mod, nw, w_in, w_kpe, conv_w, q_a_norm, w_uq_t, kv_norm, w_k, w_v_t, q_norm_col, k_norm_w,
              k_norm_col, shift, cos_t, sin_t, side, tm=1024):
    s, d = x.shape
    nh = MLA_HEADS
    half = QK_ROPE // 2
    steps = s // tm
    side_spec = _side_cast_spec(side, steps, 1)
    consts = [mod, nw, w_in, w_kpe, conv_w, q_a_norm, w_uq_t, kv_norm, w_k, w_v_t, q_norm_col, k_norm_w, k_norm_col,
              shift]
    return pl.pallas_call(
        _odd_prep_kernel,
        grid=(steps,),
        in_specs=[pl.BlockSpec((tm, d), lambda i: (i, 0))] + [_const_spec(a.shape) for a in consts] + [
            pl.BlockSpec((half, tm), lambda i: (0, i)), pl.BlockSpec((half, tm), lambda i: (0, i)), side_spec],
        out_specs=[
            pl.BlockSpec((tm, C_WIDTH), lambda i: (i, 0)),
            pl.BlockSpec((nh, QK_PAD, tm), lambda i: (0, 0, i)),
            pl.BlockSpec((nh, tm, QK_PAD), lambda i: (0, i, 0)),
            pl.BlockSpec((nh, V_PAD, tm), lambda i: (0, 0, i)),
            side_spec,
        ],
        out_shape=[
            jax.ShapeDtypeStruct((s, C_WIDTH), BF16),
            jax.ShapeDtypeStruct((nh, QK_PAD, s), BF16),
            jax.ShapeDtypeStruct((nh, s, QK_PAD), BF16),
            jax.ShapeDtypeStruct((nh, V_PAD, s), BF16),
            jax.ShapeDtypeStruct(side.shape, BF16),
        ],
        scratch_shapes=[pltpu.VMEM((CONV_HALO, C_WIDTH), F32)],
        compiler_params=_params(("arbitrary",)),
        name="odd_prep",
    )(x, *consts, cos_t, sin_t, side)


def _causal_mask(s, lead=0):
    key = lax.broadcasted_iota(jnp.int32, s.shape, 0)
    qry = lead + lax.broadcasted_iota(jnp.int32, s.shape, 1)
    return jnp.where(key <= qry, s, MASK_VALUE)


def _attn_shifted_kernel(qt_ref, k_ref, vt_ref, o_ref, acc_ref, l_ref, *, tq, tk):
    qi = pl.program_id(1)
    acc_ref[...] = jnp.zeros_like(acc_ref)
    l_ref[...] = jnp.zeros_like(l_ref)
    r = tq // tk

    def block(j, diag_piece=None):
        q0 = 0 if diag_piece is None else diag_piece * tk
        start = pl.multiple_of(j * tk, tk)
        s = jnp.dot(k_ref[0, pl.ds(start, tk), :], qt_ref[0, :, q0:], preferred_element_type=F32)
        if diag_piece is not None:
            s = _causal_mask(s)
        p = jnp.exp2(s)
        l_ref[:, q0:] += jnp.sum(p.reshape(tk // SUBLANES, SUBLANES, p.shape[1]), axis=0)
        acc_ref[:, q0:] += jnp.dot(vt_ref[0, :V_HD, pl.ds(start, tk)], p.astype(BF16), preferred_element_type=F32)

    def body(t, carry):
        for b in range(r):
            block(r * t + b)
        return carry

    lax.fori_loop(0, qi, body, 0)
    for b in range(r):
        block(r * qi + b, diag_piece=b)
    out_t = acc_ref[...] / jnp.sum(l_ref[...], axis=0, keepdims=True)
    o_ref[...] = out_t.T.astype(o_ref.dtype)


def _attention_shifted(qt, k, vt, tq=2048, tk=512):
    nh, _, s = qt.shape
    return pl.pallas_call(
        functools.partial(_attn_shifted_kernel, tq=tq, tk=tk),
        grid=(nh, s // tq),
        in_specs=[
            pl.BlockSpec((1, QK_PAD, tq), lambda h, i: (h, 0, i)),
            pl.BlockSpec((1, s, QK_PAD), lambda h, i: (h, 0, 0)),
            pl.BlockSpec((1, V_PAD, s), lambda h, i: (h, 0, 0)),
        ],
        out_specs=pl.BlockSpec((tq, V_HD), lambda h, i: (i, h)),
        out_shape=jax.ShapeDtypeStruct((s, nh * V_HD), BF16),
        scratch_shapes=[pltpu.VMEM((V_HD, tq), F32), pltpu.VMEM((SUBLANES, tq), F32)],
        compiler_params=_params(("arbitrary", "arbitrary")),
        name="mla_attention_shifted",
    )(qt, k, vt)


def _attn_kernel(qt_ref, k_ref, vt_ref, o_ref, s0_ref, s1_ref, bm0_ref, bm1_ref, m_ref, acc_ref, *, tq, tk):
    qi = pl.program_id(1)
    m_ref[...] = jnp.full_like(m_ref, MASK_VALUE)
    acc_ref[...] = jnp.zeros_like(acc_ref)

    def scores(j, s_ref, bm_ref, diag_offset=None):
        q0 = 0 if diag_offset is None else diag_offset
        start = pl.multiple_of(j * tk, tk)
        s = jnp.dot(k_ref[0, pl.ds(start, tk), :], qt_ref[0, :, q0:], preferred_element_type=F32)
        if diag_offset is not None:
            s = _causal_mask(s)
        s_ref[:, q0:] = s
        bm_ref[:, q0:] = jnp.max(s, axis=0, keepdims=True)

    def consume(j, s_ref, bm_ref, q0=0):
        m_prev = m_ref[:, q0:]
        m_new = jnp.maximum(m_prev, bm_ref[:, q0:])
        alpha = jnp.exp2(m_prev - m_new)
        p = jnp.exp2((s_ref[:, q0:] - m_new).astype(BF16))
        start = pl.multiple_of(j * tk, tk)
        vt = vt_ref[0, :, pl.ds(start, tk)]
        acc_ref[:, q0:] = alpha * acc_ref[:, q0:] + jnp.dot(vt, p, preferred_element_type=F32)
        m_ref[:, q0:] = m_new

    r = tq // tk
    slots = ((s0_ref, bm0_ref), (s1_ref, bm1_ref))

    def trip(j0, then_diagonal):
        for b in range(r):
            first_diag = then_diagonal and b == r - 1
            scores(j0 + b + 1, *slots[(b + 1) % 2], diag_offset=0 if first_diag else None)
            consume(j0 + b, *slots[b % 2])

    def diagonal(jd):
        for b in range(r):
            if b + 1 < r:
                scores(jd + b + 1, *slots[(b + 1) % 2], diag_offset=(b + 1) * tk)
            consume(jd + b, *slots[b % 2], q0=b * tk)

    @pl.when(qi > 0)
    def _():
        scores(0, *slots[0])

    n_plain = jnp.maximum(qi - 1, 0)

    def body(t, carry):
        trip(r * (2 * t), False)
        trip(r * (2 * t + 1), False)
        return carry

    lax.fori_loop(0, n_plain // 2, body, 0)

    @pl.when(n_plain % 2 == 1)
    def _():
        trip(r * (n_plain - 1), False)

    @pl.when(qi > 0)
    def _():
        trip(r * (qi - 1), True)
        diagonal(r * qi)

    @pl.when(qi == 0)
    def _():
        scores(0, *slots[0], diag_offset=0)
        diagonal(0)

    acc = acc_ref[...]
    out_t = acc[:V_HD, :] / acc[V_HD:V_HD + 1, :]
    o_ref[...] = out_t.T.astype(o_ref.dtype)


def _attention(qt, k, vt, tq=1024, tk=512):
    nh, _, s = qt.shape
    assert tq % (2 * tk) == 0
    return pl.pallas_call(
        functools.partial(_attn_kernel, tq=tq, tk=tk),
        grid=(nh, s // tq),
        in_specs=[
            pl.BlockSpec((1, QK_PAD, tq), lambda h, i: (h, 0, i)),
            pl.BlockSpec((1, s, QK_PAD), lambda h, i: (h, 0, 0)),
            pl.BlockSpec((1, V_PAD, s), lambda h, i: (h, 0, 0)),
        ],
        out_specs=pl.BlockSpec((tq, V_HD), lambda h, i: (i, h)),
        out_shape=jax.ShapeDtypeStruct((s, nh * V_HD), BF16),
        scratch_shapes=[pltpu.VMEM((tk, tq), F32), pltpu.VMEM((tk, tq), F32), pltpu.VMEM((1, tq), F32),
                        pltpu.VMEM((1, tq), F32), pltpu.VMEM((1, tq), F32), pltpu.VMEM((V_PAD, tq), F32)],
        compiler_params=_params(("arbitrary", "arbitrary")),
        name="mla_attention",
    )(qt, k, vt)


def _pack_bf16_pairs(x):
    n = x.shape[1] // 2
    lo = pltpu.bitcast(x[:, :n].astype(BF16).astype(F32), jnp.uint32)
    hi = pltpu.bitcast(x[:, n:].astype(BF16).astype(F32), jnp.uint32)
    return (lo >> 16) | hi


def _unpack_bf16_pairs(p):
    lo = pltpu.bitcast(p << 16, F32)
    hi = pltpu.bitcast(p & jnp.uint32(0xFFFF0000), F32)
    return jnp.concatenate([lo, hi], axis=1)


def _odd_out_kernel(x_ref, yc_ref, yd_ref, mod_ref, nw_ref, wout_ref, wr_ref, before_ref, x_out_ref, h_ref, rw_ref,
                    ridx_ref, cnt_ref):
    d = D_MODEL
    x = x_ref[...]
    gate_m = mod_ref[:, 2 * d:3 * d]
    sh, sc = mod_ref[:, 3 * d:4 * d], mod_ref[:, 4 * d:5 * d]
    mix = (jnp.dot(yc_ref[...], wout_ref[:C_WIDTH, :], preferred_element_type=F32)
           + jnp.dot(yd_ref[...], wout_ref[C_WIDTH:, :], preferred_element_type=F32))
    x1 = x + gate_m * mix
    x_out_ref[...] = x1
    h = _rms_mod(x1, nw_ref[...], sc, sh)
    h_ref[...] = _pack_bf16_pairs(h)

    h_hi = h.astype(BF16)
    h_lo = (h - h_hi.astype(F32)).astype(BF16)
    hw = jnp.dot(h_hi, wr_ref[...], preferred_element_type=F32)
    logits = hw[:, :LANES] + (hw[:, LANES:] + jnp.dot(h_lo, wr_ref[:, :LANES], preferred_element_type=F32))
    lt = logits.T[:N_EXPERTS, :]
    ex = lax.broadcasted_iota(jnp.int32, lt.shape, 0)
    m1 = jnp.max(lt, axis=0, keepdims=True)
    i1 = jnp.min(jnp.where(lt == m1, ex, N_EXPERTS), axis=0, keepdims=True)
    rest = jnp.where(ex == i1, -jnp.inf, lt)
    m2 = jnp.max(rest, axis=0, keepdims=True)
    i2 = jnp.min(jnp.where(rest == m2, ex, N_EXPERTS), axis=0, keepdims=True)
    e2 = jnp.exp(m2 - m1)
    w1 = 1.0 / (1.0 + e2)
    w2 = e2 / (1.0 + e2)
    row = lax.broadcasted_iota(jnp.int32, (LANES, lt.shape[1]), 0)
    rw_ref[...] = jnp.where(row == 0, w1, jnp.where(row == 1, w2, 0.0)).T

    i = pl.program_id(0)

    @pl.when(i == 0)
    def _():
        cnt_ref[...] = jnp.zeros_like(cnt_ref)

    onehot = jnp.where((ex == i1) | (ex == i2), 1.0, 0.0)
    prefix = jnp.dot(onehot.astype(BF16), before_ref[...], preferred_element_type=F32) + cnt_ref[...]
    r1 = jnp.sum(jnp.where(ex == i1, prefix, 0.0), axis=0, keepdims=True)
    r2 = jnp.sum(jnp.where(ex == i2, prefix, 0.0), axis=0, keepdims=True)
    cnt_ref[...] += jnp.sum(onehot, axis=1, keepdims=True)
    ridx_ref[...] = jnp.where(ex == 0, i1, jnp.where(ex == 1, i2, jnp.where(
        ex == 2, r1.astype(jnp.int32), jnp.where(ex == 3, r2.astype(jnp.int32), 0))))


def _odd_out(x, yc, yd, mod, nw, w_out, w_router, tm=1024):
    s, d = x.shape
    return pl.pallas_call(
        _odd_out_kernel,
        grid=(s // tm,),
        in_specs=[
            pl.BlockSpec((tm, d), lambda i: (i, 0)),
            pl.BlockSpec((tm, C_WIDTH), lambda i: (i, 0)),
            pl.BlockSpec((tm, MLA_HEADS * V_HD), lambda i: (i, 0)),
            _const_spec(mod.shape), _const_spec(nw.shape), _const_spec(w_out.shape), _const_spec(w_router.shape),
            _const_spec((tm, tm)),
        ],
        out_specs=[
            pl.BlockSpec((tm, d), lambda i: (i, 0)),
            pl.BlockSpec((tm, d // 2), lambda i: (i, 0)),
            pl.BlockSpec((tm, LANES), lambda i: (i, 0)),
            pl.BlockSpec((SUBLANES, tm), lambda i: (0, i)),
            pl.BlockSpec((N_EXPERTS, 1), lambda i: (0, 0)),
        ],
        out_shape=[
            jax.ShapeDtypeStruct((s, d), F32),
            jax.ShapeDtypeStruct((s, d // 2), jnp.uint32),
            jax.ShapeDtypeStruct((s, LANES), F32),
            jax.ShapeDtypeStruct((SUBLANES, s), jnp.int32),
            jax.ShapeDtypeStruct((N_EXPERTS, 1), F32),
        ],
        compiler_params=_params(("arbitrary",)),
        name="odd_out_router",
    )(x, yc, yd, mod, nw, w_out, w_router, jnp.triu(jnp.ones((tm, tm), BF16), 1))


def _sc_workers():
    info = plsc.get_sparse_core_info()
    return info.num_cores, info.num_cores * info.num_subcores


def _sc_scatter_rows(x, idx0, idx1, out_rows):
    n, w = x.shape
    nc, nw = _sc_workers()
    per_w = n // nw
    nch = per_w // SC_ROWS
    assert nch % 2 == 0 and nch >= 2
    mesh = plsc.VectorSubcoreMesh(core_axis_name="c", subcore_axis_name="s")

    @functools.partial(
        pl.kernel, mesh=mesh, out_type=jax.ShapeDtypeStruct((out_rows, w), x.dtype),
        scratch_types=[pltpu.VMEM((nch, SC_ROWS), jnp.int32), pltpu.VMEM((nch, SC_ROWS), jnp.int32),
                       pltpu.VMEM((SC_ROWS, w), x.dtype), pltpu.VMEM((SC_ROWS, w), x.dtype),
                       pltpu.SemaphoreType.DMA, pltpu.SemaphoreType.DMA, pltpu.SemaphoreType.DMA,
                       pltpu.SemaphoreType.DMA],
        name="moe_dispatch")
    def scatter(x_hbm, i0_hbm, i1_hbm, out_hbm, i0_v, i1_v, buf0, buf1, lsem0, lsem1, ssem0, ssem1):
        wid = lax.axis_index("s") * nc + lax.axis_index("c")
        pltpu.sync_copy(i0_hbm.at[wid], i0_v)
        pltpu.sync_copy(i1_hbm.at[wid], i1_v)
        base = wid * per_w

        def load(c, buf, sem):
            return pltpu.make_async_copy(x_hbm.at[pl.ds(base + c * SC_ROWS, SC_ROWS)], buf, sem)

        def put(c, buf):
            first = pltpu.make_async_copy(buf, out_hbm.at[i0_v.at[c]], ssem0)
            second = pltpu.make_async_copy(buf, out_hbm.at[i1_v.at[c]], ssem1)
            first.start()
            second.start()
            first.wait()
            second.wait()

        def pair(c, prefetch_next):
            load(c + 1, buf1, lsem1).start()
            load(c, buf0, lsem0).wait()
            put(c, buf0)
            if prefetch_next:
                load(c + 2, buf0, lsem0).start()
            load(c + 1, buf1, lsem1).wait()
            put(c + 1, buf1)

        load(0, buf0, lsem0).start()

        def body(t, carry):
            pair(2 * t, True)
            return carry

        lax.fori_loop(0, nch // 2 - 1, body, 0)
        pair(nch - 2, False)

    return scatter(x, idx0.reshape(nw, nch, SC_ROWS), idx1.reshape(nw, nch, SC_ROWS))


def _sc_gather_rows(table, idx):
    _, w = table.shape
    b = idx.shape[0]
    nc, nw = _sc_workers()
    per_w = b // nw
    nch = per_w // SC_ROWS
    mesh = plsc.VectorSubcoreMesh(core_axis_name="c", subcore_axis_name="s")

    assert nch % 2 == 0 and nch >= 2

    @functools.partial(
        pl.kernel, mesh=mesh, out_type=jax.ShapeDtypeStruct((b, w), table.dtype),
        scratch_types=[pltpu.VMEM((nch, SC_ROWS), jnp.int32), pltpu.VMEM((SC_ROWS, w), table.dtype),
                       pltpu.VMEM((SC_ROWS, w), table.dtype), pltpu.SemaphoreType.DMA, pltpu.SemaphoreType.DMA],
        name="moe_combine_gather")
    def gather(table_hbm, idx_hbm, out_hbm, idx_v, buf0, buf1, sem0, sem1):
        wid = lax.axis_index("s") * nc + lax.axis_index("c")
        pltpu.sync_copy(idx_hbm.at[wid], idx_v)
        base = wid * per_w

        def fetch(c, buf, sem):
            return pltpu.make_async_copy(table_hbm.at[idx_v.at[c]], buf, sem)

        def put(c, buf):
            pltpu.sync_copy(buf, out_hbm.at[pl.ds(base + c * SC_ROWS, SC_ROWS)])

        def pair(c, prefetch_next):
            fetch(c + 1, buf1, sem1).start()
            fetch(c, buf0, sem0).wait()
            put(c, buf0)
            if prefetch_next:
                fetch(c + 2, buf0, sem0).start()
            fetch(c + 1, buf1, sem1).wait()
            put(c + 1, buf1)

        fetch(0, buf0, sem0).start()

        def body(t, carry):
            pair(2 * t, True)
            return carry

        lax.fori_loop(0, nch // 2 - 1, body, 0)
        pair(nch - 2, False)

    return gather(table, idx.reshape(nw, nch, SC_ROWS))


def _moe_kernel(te_ref, nv_ref, rows_ref, xs_ref, wg_ref, wu_ref, wd_ref, ys_ref, *, n_chunks):
    j = pl.program_id(0)
    tm = xs_ref.shape[0]
    tf = wd_ref.shape[1] // n_chunks

    def expert_ffn(rows):
        h = _unpack_bf16_pairs(xs_ref[:rows, :]).astype(BF16)
        y = None
        for c in range(n_chunks):
            cols = slice(c * tf, (c + 1) * tf)
            g = jnp.dot(h, wg_ref[0, :, cols], preferred_element_type=F32)
            u = jnp.dot(h, wu_ref[0, :, cols], preferred_element_type=F32)
            act = (g * jax.nn.sigmoid(g) * u).astype(BF16)
            yc = jnp.dot(act, wd_ref[0, cols, :], preferred_element_type=F32)
            y = yc if y is None else y + yc
        ys_ref[:rows, :] = _pack_bf16_pairs(y)

    @pl.when(j < nv_ref[0])
    def _():
        for rows in range(MOE_ROW_STEP, tm + 1, MOE_ROW_STEP):
            pl.when(rows_ref[j] == rows)(functools.partial(expert_ffn, rows))


def _moe_grouped(xs, tile_expert, n_valid, tile_rows, w_gu, w_down, tm, n_chunks=2):
    p_rows, dh = xs.shape
    d = 2 * dh
    ne, _, ff2 = w_gu.shape
    ff = ff2 // 2
    assert ff % (n_chunks * LANES) == 0

    def tile(j, nv):
        return jnp.minimum(j, nv[0] - 1)

    grid_spec = pltpu.PrefetchScalarGridSpec(
        num_scalar_prefetch=3,
        grid=(p_rows // tm,),
        in_specs=[
            pl.BlockSpec((tm, dh), lambda j, te, nv, tr: (tile(j, nv), 0)),
            pl.BlockSpec((1, d, ff), lambda j, te, nv, tr: (te[tile(j, nv)], 0, 0)),
            pl.BlockSpec((1, d, ff), lambda j, te, nv, tr: (te[tile(j, nv)], 0, 1)),
            pl.BlockSpec((1, ff, d), lambda j, te, nv, tr: (te[tile(j, nv)], 0, 0)),
        ],
        out_specs=pl.BlockSpec((tm, dh), lambda j, te, nv, tr: (tile(j, nv), 0)),
    )
    return pl.pallas_call(
        functools.partial(_moe_kernel, n_chunks=n_chunks),
        grid_spec=grid_spec,
        out_shape=jax.ShapeDtypeStruct((p_rows, dh), jnp.uint32),
        compiler_params=_params(("arbitrary",), vmem=MOE_VMEM_LIMIT),
        name="moe_experts",
    )(tile_expert, n_valid, tile_rows, xs, w_gu, w_gu, w_down)


def _moe_combine_kernel(x_ref, y0_ref, y1_ref, rw_ref, mod_ref, *rest):
    o_ref = rest[-1]
    d = D_MODEL
    w1 = rw_ref[:, 0:1]
    w2 = rw_ref[:, 1:2]
    y = w1 * _unpack_bf16_pairs(y0_ref[...]) + w2 * _unpack_bf16_pairs(y1_ref[...])
    o_ref[...] = x_ref[...] + mod_ref[:, 5 * d:6 * d] * y


def _moe_combine(x, yg, rw, mod, prev, part, n_parts, tm=512):
    s, d = x.shape
    nb = s // tm // n_parts
    first = part * nb
    in_specs = [
        pl.BlockSpec((tm, d), lambda i: (first + i, 0)),
        pl.BlockSpec((tm, d // 2), lambda i: (i, 0)),
        pl.BlockSpec((tm, d // 2), lambda i: (nb + i, 0)),
        pl.BlockSpec((tm, LANES), lambda i: (first + i, 0)),
        _const_spec(mod.shape),
    ]
    args = [x, yg, yg, rw, mod]
    aliases = {}
    if prev is not None:
        in_specs.append(pl.BlockSpec(memory_space=pl.ANY))
        args.append(prev)
        aliases = {len(args) - 1: 0}
    return pl.pallas_call(
        _moe_combine_kernel,
        grid=(nb,),
        in_specs=in_specs,
        out_specs=pl.BlockSpec((tm, d), lambda i: (first + i, 0)),
        out_shape=jax.ShapeDtypeStruct((s, d), F32),
        input_output_aliases=aliases,
        compiler_params=_params(("arbitrary",)),
        name="moe_combine",
    )(*args)


def _moe_sparse(x, h_packed, rw, ridx, counts, mod, w_gu, w_down, tm=MOE_TILE):
    s = x.shape[0]
    ne = w_gu.shape[0]
    n_tiles = (2 * s) // tm + ne
    cnt = counts[:ne, 0].astype(jnp.int32)
    padded = ((cnt + tm - 1) // tm) * tm
    ends = jnp.cumsum(padded)
    offs = ends - padded
    experts = jnp.arange(ne, dtype=jnp.int32)
    off_of = lambda e: jnp.sum(jnp.where(e[:, None] == experts[None, :], offs[None, :], 0), axis=1)
    pos0 = off_of(ridx[0]) + ridx[2]
    pos1 = off_of(ridx[1]) + ridx[3]
    tile_start = jnp.arange(n_tiles, dtype=jnp.int32) * tm
    tile_expert = jnp.minimum(jnp.sum(tile_start[:, None] >= ends[None, :], axis=1), ne - 1).astype(jnp.int32)
    n_valid = (ends[-1] // tm).reshape(1).astype(jnp.int32)
    filled = jnp.clip((offs + cnt)[tile_expert] - tile_start, 0, tm)
    tile_rows = (((filled + MOE_ROW_STEP - 1) // MOE_ROW_STEP) * MOE_ROW_STEP).astype(jnp.int32)
    xs = _sc_scatter_rows(h_packed, pos0, pos1, n_tiles * tm)
    ys = _moe_grouped(xs, tile_expert, n_valid, tile_rows, w_gu, w_down, tm)
    out = None
    sp = s // COMBINE_PARTS
    for part in range(COMBINE_PARTS):
        rows = slice(part * sp, (part + 1) * sp)
        yg = _sc_gather_rows(ys, jnp.concatenate([pos0[rows], pos1[rows]]))
        out = _moe_combine(x, yg, rw, mod, out, part, COMBINE_PARTS)
    return out


def kernel(x, c, positions, norm_mix_w, norm_ffn_w, ada_w, ada_b, e_w_in, a_ln_w, a_ln_b, a_w_s, a_b_s, b_w_grp,
           b_scale, e_w_out, ffn_w_gu, ffn_w_down, o_w_in, c_conv_w, q_a_norm, w_uq, kv_norm, w_ukv, q_norm_w,
           k_norm_w, o_w_out, router_w, moe_w_gu, moe_w_down):
    bsz, s, d = x.shape
    assert bsz == 1 and d == D_MODEL
    depth = ada_w.shape[0]
    nh = MLA_HEADS
    xs = x.reshape(s, d)
    mod, cos_t, sin_t = _ada_mod_and_rope(c, ada_w, ada_b, positions)
    row = lambda a: a.reshape(1, -1)

    for layer in range(depth):
        i = layer // 2
        m = mod[layer]
        if layer % 2 == 0:
            has_next = layer + 1 < depth
            dummy = jnp.zeros((s // 512, 16, LANES), F32)
            xs, (ffn_w_gu_bf16, ffn_w_down_bf16) = _even_mix(
                xs, m, row(norm_mix_w[layer]), e_w_in[i].astype(BF16), row(a_ln_w[i]), row(a_ln_b[i]), a_w_s[i],
                a_b_s[i].reshape(A_GROUPS, CHUNK, 1), b_w_grp[i].astype(BF16), row(b_scale[i]),
                e_w_out[i].astype(BF16), [ffn_w_gu[i], ffn_w_down[i]])
            xs, moe_w_gu_bf16 = _ffn(xs, m, row(norm_ffn_w[layer]), ffn_w_gu_bf16, ffn_w_down_bf16,
                                     moe_w_gu[i] if has_next else dummy)
        else:
            o2 = 3 * C_WIDTH + Q_LORA + KV_LORA
            w_in = o_w_in[i][:, :o2].astype(BF16)
            w_kpe = jnp.pad(o_w_in[i][:, o2:], ((0, 0), (0, LANES - QK_ROPE))).astype(BF16)
            bound = (1.02 * QK_HD * QK_HD ** -0.5 * LOG2E) * jnp.max(jnp.abs(q_norm_w[i])) * jnp.max(jnp.abs(k_norm_w[i]))
            wkv = w_ukv[i].reshape(KV_LORA, nh, QK_NOPE + V_HD)
            w_k = wkv[:, :, :QK_NOPE].reshape(KV_LORA, nh * QK_NOPE).astype(BF16)
            w_v_t = wkv[:, :, QK_NOPE:].reshape(KV_LORA, nh * V_HD).T.astype(BF16)
            yc, qt, k, vt, moe_w_down_bf16 = _odd_prep(
                xs, m, row(norm_mix_w[layer]), w_in, w_kpe, c_conv_w[i], row(q_a_norm[i]), w_uq[i].T.astype(BF16),
                row(kv_norm[i]), w_k, w_v_t, q_norm_w[i].reshape(QK_HD, 1), row(k_norm_w[i]),
                k_norm_w[i].reshape(QK_HD, 1), jnp.full((1, LANES), -bound, F32), cos_t, sin_t, moe_w_down[i])
            yd = lax.cond(bound <= ATTN_MAX_BOUND, _attention_shifted, _attention, qt, k, vt)
            wr = jnp.pad(router_w[i], ((0, 0), (0, LANES - N_EXPERTS)))
            wr_hi = lax.reduce_precision(wr, exponent_bits=8, mantissa_bits=7)
            w_router = jnp.concatenate([wr_hi, wr - wr_hi], axis=1).astype(BF16)
            xs, hp, rw, ridx, counts = _odd_out(xs, yc, yd, m, row(norm_ffn_w[layer]), o_w_out[i].astype(BF16),
                                                w_router)
            xs = _moe_sparse(xs, hp, rw, ridx, counts, m, moe_w_gu_bf16, moe_w_down_bf16)
    return xs.reshape(bsz, s, d)
```

```python
import functools

import jax
import jax.numpy as jnp
from jax import lax
from jax.experimental import pallas as pl
from jax.experimental.pallas import tpu as pltpu
from jax.experimental.pallas import tpu_sc as plsc

D_MODEL = 1024
SEQ = 16384
EPS = 1e-6
CHUNK = 128
A_WIDTH = 512
A_GROUPS = 4
B_WIDTH = 512
POOL_WINDOWS = (2, 4, 8, 16)
B_HD = 128
C_WIDTH = 512
MLA_HEADS = 4
Q_LORA = 256
KV_LORA = 256
QK_NOPE = 128
QK_ROPE = 64
QK_HD = QK_NOPE + QK_ROPE
V_HD = 128
ROPE_THETA = 10000.0
D_FF = 2816
N_EXPERTS = 8
D_FF_EXPERT = 3584

LANES = 128
SUBLANES = 8
POOL_HALO = 16
CONV_HALO = 8
QK_PAD = 256
V_PAD = 144
LOG2E = 1.4426950408889634
MASK_VALUE = -1e30
MOE_TILE = 512
CONV_COLS = 256
ATTN_MAX_BOUND = 48.0
MOE_ROW_STEP = 128
COMBINE_PARTS = 8
SC_ROWS = 64
VMEM_LIMIT = 56 * 1024 * 1024
MOE_VMEM_LIMIT = 62 * 1024 * 1024

F32 = jnp.float32
BF16 = jnp.bfloat16


def _params(sem, vmem=VMEM_LIMIT, flags=None):
    return pltpu.CompilerParams(dimension_semantics=sem, vmem_limit_bytes=vmem, flags=flags)


def _const_spec(shape, single=False):
    nd = len(shape)
    return pl.BlockSpec(shape, lambda *_: (0,) * nd, pipeline_mode=pl.Buffered(1) if single else None)


def _rms_mod(x, nw, sc, sh):
    ms = jnp.mean(x * x, axis=-1, keepdims=True)
    return (x * lax.rsqrt(ms + EPS)) * (nw * (1.0 + sc)) + sh


def _ada_kernel(c_ref, w_ref, b_ref, o_ref):
    c = c_ref[...]
    ca = c * jax.nn.sigmoid(c)
    o_ref[0] = jnp.sum(w_ref[0] * ca, axis=0, keepdims=True) + b_ref[0]


def _ada_mod(c, ada_w, ada_b):
    depth, d, n = ada_w.shape
    tn = 3072
    return pl.pallas_call(
        _ada_kernel,
        grid=(depth, n // tn),
        in_specs=[
            pl.BlockSpec((d, 1), lambda l, j: (0, 0)),
            pl.BlockSpec((1, d, tn), lambda l, j: (l, 0, j)),
            pl.BlockSpec((1, 1, tn), lambda l, j: (l, 0, j)),
        ],
        out_specs=pl.BlockSpec((1, 1, tn), lambda l, j: (l, 0, j)),
        out_shape=jax.ShapeDtypeStruct((depth, 1, n), F32),
        compiler_params=_params(("arbitrary", "arbitrary")),
        name="ada_mod",
    )(c.reshape(d, 1), ada_w, ada_b.reshape(depth, 1, n))


def _even_mix_kernel(x_ref, mod_ref, nw_ref, win_ref, lnw_ref, lnb_ref, ws_ref, bs_ref, wg_ref, bsc_ref,
                     wout_ref, *rest):
    n_sides = (len(rest) - 4) // 2
    o_ref = rest[n_sides]
    halo_ref, sv_ref, yb_ref = rest[-3:]
    for side_ref, side_out_ref in zip(rest[:n_sides], rest[n_sides + 1:2 * n_sides + 1]):
        side_out_ref[...] = side_ref[...].astype(BF16)
    tm = x_ref.shape[0]
    i = pl.program_id(0)

    @pl.when(i == 0)
    def _():
        halo_ref[...] = jnp.zeros_like(halo_ref)

    d = D_MODEL
    x = x_ref[...]
    sh, sc, gate = mod_ref[:, 0:d], mod_ref[:, d:2 * d], mod_ref[:, 2 * d:3 * d]
    h = _rms_mod(x, nw_ref[...], sc, sh).astype(BF16)
    p = jnp.dot(h, win_ref[...], preferred_element_type=F32)

    gl = jax.nn.gelu(p[:, :2 * A_WIDTH])
    u = gl[:, :A_WIDTH]
    v = gl[:, A_WIDTH:]
    mu = jnp.mean(v, axis=-1, keepdims=True)
    vc = v - mu
    var = jnp.mean(vc * vc, axis=-1, keepdims=True)
    vn = (vc * lax.rsqrt(var + EPS) * lnw_ref[...] + lnb_ref[...]).astype(BF16)
    row = lax.broadcasted_iota(jnp.int32, (CHUNK, CHUNK), 0)
    col = lax.broadcasted_iota(jnp.int32, (CHUNK, CHUNK), 1)
    for g in range(A_GROUPS):
        w = jnp.where(col <= row, ws_ref[g], 0.0).astype(BF16)
        b = bs_ref[g]
        for c in range(tm // CHUNK):
            blk = vn[c * CHUNK:(c + 1) * CHUNK, g * LANES:(g + 1) * LANES]
            sv_ref[c * CHUNK:(c + 1) * CHUNK, g * LANES:(g + 1) * LANES] = (
                jnp.dot(w, blk, preferred_element_type=F32) + b)
    ya = (u * sv_ref[...]).astype(BF16)

    pb = p[:, 2 * A_WIDTH:]
    ext = jnp.concatenate([halo_ref[...], pb], axis=0)
    halo_ref[...] = pb[tm - POOL_HALO:, :]
    t_glob = i * tm + lax.broadcasted_iota(jnp.int32, (tm, 1), 0)
    s = ext
    width = 1
    for g, win in enumerate(POOL_WINDOWS):
        while width < win:
            s = s + pltpu.roll(s, width, 0)
            width *= 2
        cnt = jnp.minimum(t_glob + 1, win).astype(F32)
        sl = slice(g * B_HD, (g + 1) * B_HD)
        pooled = s[POOL_HALO:, sl] / cnt
        dg = (pooled - pb[:, sl]).astype(BF16)
        yb_ref[:, sl] = jnp.dot(dg, wg_ref[g], preferred_element_type=F32)
    yb = (yb_ref[...] * bsc_ref[...]).astype(BF16)

    mix = (jnp.dot(ya, wout_ref[:A_WIDTH, :], preferred_element_type=F32)
           + jnp.dot(yb, wout_ref[A_WIDTH:, :], preferred_element_type=F32))
    o_ref[...] = x + gate * mix


def _side_cast_spec(side, steps, axis):
    ne = side.shape[0]
    parts = steps // ne
    assert parts * ne == steps and side.shape[axis] % parts == 0
    block = list(side.shape)
    block[0] = 1
    block[axis] //= parts
    assert block[1] % 16 == 0 and block[2] % LANES == 0
    if axis == 1:
        return pl.BlockSpec(tuple(block), lambda i: (i // parts, i % parts, 0))
    return pl.BlockSpec(tuple(block), lambda i: (i // parts, 0, i % parts))


def _row_cast_spec(w, steps):
    rows = -(-w.shape[0] // steps)
    rows = -(-rows // 16) * 16
    last = -(-w.shape[0] // rows) - 1
    return pl.BlockSpec((rows, w.shape[1]), lambda i: (jnp.minimum(i, last), 0))


def _even_mix(x, mod, nw, w_in, ln_w, ln_b, w_s, b_s, w_grp, b_scale, w_out, row_sides, tm=512):
    s, d = x.shape
    steps = s // tm
    sides = list(row_sides)
    side_specs = [_row_cast_spec(w, steps) for w in row_sides]
    outs = pl.pallas_call(
        _even_mix_kernel,
        grid=(steps,),
        in_specs=[
            pl.BlockSpec((tm, d), lambda i: (i, 0)),
            _const_spec(mod.shape), _const_spec(nw.shape), _const_spec(w_in.shape),
            _const_spec(ln_w.shape), _const_spec(ln_b.shape), _const_spec(w_s.shape), _const_spec(b_s.shape),
            _const_spec(w_grp.shape), _const_spec(b_scale.shape), _const_spec(w_out.shape), *side_specs,
        ],
        out_specs=[pl.BlockSpec((tm, d), lambda i: (i, 0)), *side_specs],
        out_shape=[jax.ShapeDtypeStruct((s, d), F32)] + [jax.ShapeDtypeStruct(w.shape, BF16) for w in sides],
        scratch_shapes=[pltpu.VMEM((POOL_HALO, B_WIDTH), F32), pltpu.VMEM((tm, A_WIDTH), F32),
                        pltpu.VMEM((tm, B_WIDTH), F32)],
        compiler_params=_params(("arbitrary",)),
        name="even_mix",
    )(x, mod, nw, w_in, ln_w, ln_b, w_s, b_s, w_grp, b_scale, w_out, *sides)
    return outs[0], outs[1:]


def _ffn_kernel(x_ref, mod_ref, nw_ref, wgu_ref, wd_ref, side_ref, o_ref, side_out_ref, *, n_chunks):
    side_out_ref[...] = side_ref[...].astype(BF16)
    d = D_MODEL
    x = x_ref[...]
    sh, sc, gate = mod_ref[:, 3 * d:4 * d], mod_ref[:, 4 * d:5 * d], mod_ref[:, 5 * d:6 * d]
    h = _rms_mod(x, nw_ref[...], sc, sh).astype(BF16)
    ff = wd_ref.shape[0]
    tf = ff // n_chunks
    acc = jnp.zeros(x.shape, F32)
    for f in range(n_chunks):
        g = jnp.dot(h, wgu_ref[:, f * tf:(f + 1) * tf], preferred_element_type=F32)
        u = jnp.dot(h, wgu_ref[:, ff + f * tf:ff + (f + 1) * tf], preferred_element_type=F32)
        act = (g * jax.nn.sigmoid(g) * u).astype(BF16)
        acc = acc + jnp.dot(act, wd_ref[f * tf:(f + 1) * tf, :], preferred_element_type=F32)
    o_ref[...] = x + gate * acc


def _ffn(x, mod, nw, w_gu, w_down, side, tm=512, n_chunks=11):
    s, d = x.shape
    steps = s // tm
    side_spec = _side_cast_spec(side, steps, 2)
    return pl.pallas_call(
        functools.partial(_ffn_kernel, n_chunks=n_chunks),
        grid=(steps,),
        in_specs=[
            pl.BlockSpec((tm, d), lambda i: (i, 0)),
            _const_spec(mod.shape), _const_spec(nw.shape), _const_spec(w_gu.shape, single=True),
            _const_spec(w_down.shape, single=True), side_spec,
        ],
        out_specs=[pl.BlockSpec((tm, d), lambda i: (i, 0)), side_spec],
        out_shape=[jax.ShapeDtypeStruct((s, d), F32), jax.ShapeDtypeStruct(side.shape, BF16)],
        compiler_params=_params(("arbitrary",)),
        name="ffn",
    )(x, mod, nw, w_gu, w_down, side)


def _rope_table_kernel(pos_ref, invf_ref, cos_ref, sin_ref):
    ang = pos_ref[...].astype(F32) * invf_ref[...]
    cos_ref[...] = jnp.cos(ang)
    sin_ref[...] = jnp.sin(ang)


def _rope_tables(positions):
    s = positions.shape[-1]
    half = QK_ROPE // 2
    inv_freq = ROPE_THETA ** (-jnp.arange(0, QK_ROPE, 2, dtype=F32) / QK_ROPE)
    return pl.pallas_call(
        _rope_table_kernel,
        out_shape=(jax.ShapeDtypeStruct((half, s), F32), jax.ShapeDtypeStruct((half, s), F32)),
        name="rope_tables",
    )(positions.reshape(1, s), inv_freq.reshape(half, 1))


def _odd_prep_kernel(x_ref, mod_ref, nw_ref, win_ref, wkpe_ref, cw_ref, qan_ref, wuqt_ref, kvn_ref, wk_ref, wvt_ref,
                     qnw_ref, knw_ref, knwc_ref, shift_ref, cost_ref, sint_ref, side_ref, yc_ref, qt_ref, k_ref, vt_ref,
                     side_out_ref, halo_ref):
    side_out_ref[...] = side_ref[...].astype(BF16)
    tm = x_ref.shape[0]
    i = pl.program_id(0)

    @pl.when(i == 0)
    def _():
        halo_ref[...] = jnp.zeros_like(halo_ref)

    d = D_MODEL
    x = x_ref[...]
    sh, sc = mod_ref[:, 0:d], mod_ref[:, d:2 * d]
    h = _rms_mod(x, nw_ref[...], sc, sh).astype(BF16)
    cw = C_WIDTH
    o0 = 3 * cw
    proj = lambda lo, hi: jnp.dot(h, win_ref[:, lo:hi], preferred_element_type=F32)
    kpe_pad = jnp.dot(h, wkpe_ref[...], preferred_element_type=F32)
    kpe = kpe_pad[:, :QK_ROPE]

    for c0 in range(0, cw, CONV_COLS):
        cols = slice(c0, c0 + CONV_COLS)
        z = proj(cw + c0, cw + c0 + CONV_COLS) * proj(2 * cw + c0, 2 * cw + c0 + CONV_COLS)
        ext = jnp.concatenate([halo_ref[:, cols], z], axis=0)
        halo_ref[:, cols] = z[tm - CONV_HALO:, :]
        z1 = pltpu.roll(ext, 1, 0)[CONV_HALO:, :]
        z2 = pltpu.roll(ext, 2, 0)[CONV_HALO:, :]
        conv = cw_ref[0:1, cols] * z2 + cw_ref[1:2, cols] * z1 + cw_ref[2:3, cols] * z
        yc_ref[:, cols] = (proj(c0, c0 + CONV_COLS) * conv).astype(BF16)

    cq = proj(o0, o0 + Q_LORA)
    ckv = proj(o0 + Q_LORA, o0 + Q_LORA + KV_LORA)
    cqn = cq * lax.rsqrt(jnp.mean(cq * cq, axis=-1, keepdims=True) + EPS) * qan_ref[...]
    ckvn = ckv * lax.rsqrt(jnp.mean(ckv * ckv, axis=-1, keepdims=True) + EPS) * kvn_ref[...]
    cqn_t = cqn.T.astype(BF16)
    ckvn_t = ckvn.T.astype(BF16)
    q_t = jnp.dot(wuqt_ref[...], cqn_t, preferred_element_type=F32)
    v_t = jnp.dot(wvt_ref[...], ckvn_t, preferred_element_type=F32)
    kn_all = jnp.dot(ckvn.astype(BF16), wk_ref[...], preferred_element_type=F32)

    nh = MLA_HEADS
    half = QK_ROPE // 2
    sm_scale = QK_HD ** -0.5
    cos_t, sin_t = cost_ref[...], sint_ref[...]
    kn_w = knw_ref[:, :QK_NOPE]
    kw_t = kpe_pad.T[:QK_ROPE, :] * knwc_ref[QK_NOPE:, :]
    k1, k2 = kw_t[:half, :], kw_t[half:, :]
    kr_t = jnp.concatenate([k1 * cos_t - k2 * sin_t, k2 * cos_t + k1 * sin_t,
                            jnp.zeros((LANES - QK_ROPE, tm), F32)], axis=0)
    kr = kr_t.T[:, :QK_ROPE]
    kpe_ss = jnp.sum(kpe * kpe, axis=-1, keepdims=True)
    npad = QK_PAD - QK_HD
    zrows = jnp.where(lax.broadcasted_iota(jnp.int32, (npad, tm), 0) == 0, shift_ref[:, 0:1], 0.0)
    zpad = jnp.where(lax.broadcasted_iota(jnp.int32, (tm, npad), 1) == 0, 1.0, 0.0)
    ones_rows = (lax.broadcasted_iota(jnp.int32, (V_PAD - V_HD, tm), 0) == 0).astype(BF16)
    for hd in range(nh):
        qh = q_t[hd * QK_HD:(hd + 1) * QK_HD, :]
        q_inv = lax.rsqrt(jnp.sum(qh * qh, axis=0, keepdims=True) / QK_HD + EPS) * (sm_scale * LOG2E)
        qw = qh * qnw_ref[...]
        x1, x2 = qw[QK_NOPE:QK_NOPE + half, :], qw[QK_NOPE + half:, :]
        qt_ref[hd] = jnp.concatenate(
            [qw[:QK_NOPE, :] * q_inv, (x1 * cos_t - x2 * sin_t) * q_inv, (x2 * cos_t + x1 * sin_t) * q_inv, zrows],
            axis=0).astype(BF16)
        kn = kn_all[:, hd * QK_NOPE:(hd + 1) * QK_NOPE]
        k_ss = jnp.sum(kn * kn, axis=-1, keepdims=True) + kpe_ss
        k_inv = lax.rsqrt(k_ss / QK_HD + EPS)
        k_ref[hd] = jnp.concatenate([kn * k_inv * kn_w, kr * k_inv, zpad], axis=1).astype(BF16)
        vt_ref[hd, :V_HD, :] = v_t[hd * V_HD:(hd + 1) * V_HD, :].astype(BF16)
        vt_ref[hd, V_HD:, :] = ones_rows


def _odd_prep(x, mod, nw, w_in, w_kpe, conv_w, q_a_norm, w_uq_t, kv_norm, w_k, w_v_t, q_norm_col, k_norm_w,
              k_norm_col, shift, cos_t, sin_t, side, tm=1024):
    s, d = x.shape
    nh = MLA_HEADS
    half = QK_ROPE // 2
    steps = s // tm
    side_spec = _side_cast_spec(side, steps, 1)
    consts = [mod, nw, w_in, w_kpe, conv_w, q_a_norm, w_uq_t, kv_norm, w_k, w_v_t, q_norm_col, k_norm_w, k_norm_col,
              shift]
    return pl.pallas_call(
        _odd_prep_kernel,
        grid=(steps,),
        in_specs=[pl.BlockSpec((tm, d), lambda i: (i, 0))] + [_const_spec(a.shape) for a in consts] + [
            pl.BlockSpec((half, tm), lambda i: (0, i)), pl.BlockSpec((half, tm), lambda i: (0, i)), side_spec],
        out_specs=[
            pl.BlockSpec((tm, C_WIDTH), lambda i: (i, 0)),
            pl.BlockSpec((nh, QK_PAD, tm), lambda i: (0, 0, i)),
            pl.BlockSpec((nh, tm, QK_PAD), lambda i: (0, i, 0)),
            pl.BlockSpec((nh, V_PAD, tm), lambda i: (0, 0, i)),
            side_spec,
        ],
        out_shape=[
            jax.ShapeDtypeStruct((s, C_WIDTH), BF16),
            jax.ShapeDtypeStruct((nh, QK_PAD, s), BF16),
            jax.ShapeDtypeStruct((nh, s, QK_PAD), BF16),
            jax.ShapeDtypeStruct((nh, V_PAD, s), BF16),
            jax.ShapeDtypeStruct(side.shape, BF16),
        ],
        scratch_shapes=[pltpu.VMEM((CONV_HALO, C_WIDTH), F32)],
        compiler_params=_params(("arbitrary",)),
        name="odd_prep",
    )(x, *consts, cos_t, sin_t, side)


def _causal_mask(s, lead=0):
    key = lax.broadcasted_iota(jnp.int32, s.shape, 0)
    qry = lead + lax.broadcasted_iota(jnp.int32, s.shape, 1)
    return jnp.where(key <= qry, s, MASK_VALUE)


def _attn_shifted_kernel(qt_ref, k_ref, vt_ref, o_ref, acc_ref, l_ref, *, tq, tk):
    qi = pl.program_id(1)
    acc_ref[...] = jnp.zeros_like(acc_ref)
    l_ref[...] = jnp.zeros_like(l_ref)
    r = tq // tk

    def block(j, diag_piece=None):
        q0 = 0 if diag_piece is None else diag_piece * tk
        start = pl.multiple_of(j * tk, tk)
        s = jnp.dot(k_ref[0, pl.ds(start, tk), :], qt_ref[0, :, q0:], preferred_element_type=F32)
        if diag_piece is not None:
            s = _causal_mask(s)
        p = jnp.exp2(s)
        l_ref[:, q0:] += jnp.sum(p.reshape(tk // SUBLANES, SUBLANES, p.shape[1]), axis=0)
        acc_ref[:, q0:] += jnp.dot(vt_ref[0, :V_HD, pl.ds(start, tk)], p.astype(BF16), preferred_element_type=F32)

    def body(t, carry):
        for b in range(r):
            block(r * t + b)
        return carry

    lax.fori_loop(0, qi, body, 0)
    for b in range(r):
        block(r * qi + b, diag_piece=b)
    out_t = acc_ref[...] / jnp.sum(l_ref[...], axis=0, keepdims=True)
    o_ref[...] = out_t.T.astype(o_ref.dtype)


def _attention_shifted(qt, k, vt, tq=2048, tk=512):
    nh, _, s = qt.shape
    return pl.pallas_call(
        functools.partial(_attn_shifted_kernel, tq=tq, tk=tk),
        grid=(nh, s // tq),
        in_specs=[
            pl.BlockSpec((1, QK_PAD, tq), lambda h, i: (h, 0, i)),
            pl.BlockSpec((1, s, QK_PAD), lambda h, i: (h, 0, 0)),
            pl.BlockSpec((1, V_PAD, s), lambda h, i: (h, 0, 0)),
        ],
        out_specs=pl.BlockSpec((tq, V_HD), lambda h, i: (i, h)),
        out_shape=jax.ShapeDtypeStruct((s, nh * V_HD), BF16),
        scratch_shapes=[pltpu.VMEM((V_HD, tq), F32), pltpu.VMEM((SUBLANES, tq), F32)],
        compiler_params=_params(("arbitrary", "arbitrary")),
        name="mla_attention_shifted",
    )(qt, k, vt)


def _attn_kernel(qt_ref, k_ref, vt_ref, o_ref, s0_ref, s1_ref, bm0_ref, bm1_ref, m_ref, acc_ref, *, tq, tk):
    qi = pl.program_id(1)
    m_ref[...] = jnp.full_like(m_ref, MASK_VALUE)
    acc_ref[...] = jnp.zeros_like(acc_ref)

    def scores(j, s_ref, bm_ref, diag_offset=None):
        q0 = 0 if diag_offset is None else diag_offset
        start = pl.multiple_of(j * tk, tk)
        s = jnp.dot(k_ref[0, pl.ds(start, tk), :], qt_ref[0, :, q0:], preferred_element_type=F32)
        if diag_offset is not None:
            s = _causal_mask(s)
        s_ref[:, q0:] = s
        bm_ref[:, q0:] = jnp.max(s, axis=0, keepdims=True)

    def consume(j, s_ref, bm_ref, q0=0):
        m_prev = m_ref[:, q0:]
        m_new = jnp.maximum(m_prev, bm_ref[:, q0:])
        alpha = jnp.exp2(m_prev - m_new)
        p = jnp.exp2((s_ref[:, q0:] - m_new).astype(BF16))
        start = pl.multiple_of(j * tk, tk)
        vt = vt_ref[0, :, pl.ds(start, tk)]
        acc_ref[:, q0:] = alpha * acc_ref[:, q0:] + jnp.dot(vt, p, preferred_element_type=F32)
        m_ref[:, q0:] = m_new

    r = tq // tk
    slots = ((s0_ref, bm0_ref), (s1_ref, bm1_ref))

    def trip(j0, then_diagonal):
        for b in range(r):
            first_diag = then_diagonal and b == r - 1
            scores(j0 + b + 1, *slots[(b + 1) % 2], diag_offset=0 if first_diag else None)
            consume(j0 + b, *slots[b % 2])

    def diagonal(jd):
        for b in range(r):
            if b + 1 < r:
                scores(jd + b + 1, *slots[(b + 1) % 2], diag_offset=(b + 1) * tk)
            consume(jd + b, *slots[b % 2], q0=b * tk)

    @pl.when(qi > 0)
    def _():
        scores(0, *slots[0])

    n_plain = jnp.maximum(qi - 1, 0)

    def body(t, carry):
        trip(r * (2 * t), False)
        trip(r * (2 * t + 1), False)
        return carry

    lax.fori_loop(0, n_plain // 2, body, 0)

    @pl.when(n_plain % 2 == 1)
    def _():
        trip(r * (n_plain - 1), False)

    @pl.when(qi > 0)
    def _():
        trip(r * (qi - 1), True)
        diagonal(r * qi)

    @pl.when(qi == 0)
    def _():
        scores(0, *slots[0], diag_offset=0)
        diagonal(0)

    acc = acc_ref[...]
    out_t = acc[:V_HD, :] / acc[V_HD:V_HD + 1, :]
    o_ref[...] = out_t.T.astype(o_ref.dtype)


def _attention(qt, k, vt, tq=1024, tk=512):
    nh, _, s = qt.shape
    assert tq % (2 * tk) == 0
    return pl.pallas_call(
        functools.partial(_attn_kernel, tq=tq, tk=tk),
        grid=(nh, s // tq),
        in_specs=[
            pl.BlockSpec((1, QK_PAD, tq), lambda h, i: (h, 0, i)),
            pl.BlockSpec((1, s, QK_PAD), lambda h, i: (h, 0, 0)),
            pl.BlockSpec((1, V_PAD, s), lambda h, i: (h, 0, 0)),
        ],
        out_specs=pl.BlockSpec((tq, V_HD), lambda h, i: (i, h)),
        out_shape=jax.ShapeDtypeStruct((s, nh * V_HD), BF16),
        scratch_shapes=[pltpu.VMEM((tk, tq), F32), pltpu.VMEM((tk, tq), F32), pltpu.VMEM((1, tq), F32),
                        pltpu.VMEM((1, tq), F32), pltpu.VMEM((1, tq), F32), pltpu.VMEM((V_PAD, tq), F32)],
        compiler_params=_params(("arbitrary", "arbitrary")),
        name="mla_attention",
    )(qt, k, vt)


def _pack_bf16_pairs(x):
    n = x.shape[1] // 2
    lo = pltpu.bitcast(x[:, :n].astype(BF16).astype(F32), jnp.uint32)
    hi = pltpu.bitcast(x[:, n:].astype(BF16).astype(F32), jnp.uint32)
    return (lo >> 16) | hi


def _unpack_bf16_pairs(p):
    lo = pltpu.bitcast(p << 16, F32)
    hi = pltpu.bitcast(p & jnp.uint32(0xFFFF0000), F32)
    return jnp.concatenate([lo, hi], axis=1)


def _odd_out_kernel(x_ref, yc_ref, yd_ref, mod_ref, nw_ref, wout_ref, wr_ref, before_ref, x_out_ref, h_ref, rw_ref,
                    ridx_ref, cnt_ref):
    d = D_MODEL
    x = x_ref[...]
    gate_m = mod_ref[:, 2 * d:3 * d]
    sh, sc = mod_ref[:, 3 * d:4 * d], mod_ref[:, 4 * d:5 * d]
    mix = (jnp.dot(yc_ref[...], wout_ref[:C_WIDTH, :], preferred_element_type=F32)
           + jnp.dot(yd_ref[...], wout_ref[C_WIDTH:, :], preferred_element_type=F32))
    x1 = x + gate_m * mix
    x_out_ref[...] = x1
    h = _rms_mod(x1, nw_ref[...], sc, sh)
    h_ref[...] = _pack_bf16_pairs(h)

    h_hi = h.astype(BF16)
    h_lo = (h - h_hi.astype(F32)).astype(BF16)
    hw = jnp.dot(h_hi, wr_ref[...], preferred_element_type=F32)
    logits = hw[:, :LANES] + (hw[:, LANES:] + jnp.dot(h_lo, wr_ref[:, :LANES], preferred_element_type=F32))
    lt = logits.T[:N_EXPERTS, :]
    ex = lax.broadcasted_iota(jnp.int32, lt.shape, 0)
    m1 = jnp.max(lt, axis=0, keepdims=True)
    i1 = jnp.min(jnp.where(lt == m1, ex, N_EXPERTS), axis=0, keepdims=True)
    rest = jnp.where(ex == i1, -jnp.inf, lt)
    m2 = jnp.max(rest, axis=0, keepdims=True)
    i2 = jnp.min(jnp.where(rest == m2, ex, N_EXPERTS), axis=0, keepdims=True)
    e2 = jnp.exp(m2 - m1)
    w1 = 1.0 / (1.0 + e2)
    w2 = e2 / (1.0 + e2)
    row = lax.broadcasted_iota(jnp.int32, (LANES, lt.shape[1]), 0)
    rw_ref[...] = jnp.where(row == 0, w1, jnp.where(row == 1, w2, 0.0)).T

    i = pl.program_id(0)

    @pl.when(i == 0)
    def _():
        cnt_ref[...] = jnp.zeros_like(cnt_ref)

    onehot = jnp.where((ex == i1) | (ex == i2), 1.0, 0.0)
    prefix = jnp.dot(onehot.astype(BF16), before_ref[...], preferred_element_type=F32) + cnt_ref[...]
    r1 = jnp.sum(jnp.where(ex == i1, prefix, 0.0), axis=0, keepdims=True)
    r2 = jnp.sum(jnp.where(ex == i2, prefix, 0.0), axis=0, keepdims=True)
    cnt_ref[...] += jnp.sum(onehot, axis=1, keepdims=True)
    ridx_ref[...] = jnp.where(ex == 0, i1, jnp.where(ex == 1, i2, jnp.where(
        ex == 2, r1.astype(jnp.int32), jnp.where(ex == 3, r2.astype(jnp.int32), 0))))


def _odd_out(x, yc, yd, mod, nw, w_out, w_router, tm=1024):
    s, d = x.shape
    return pl.pallas_call(
        _odd_out_kernel,
        grid=(s // tm,),
        in_specs=[
            pl.BlockSpec((tm, d), lambda i: (i, 0)),
            pl.BlockSpec((tm, C_WIDTH), lambda i: (i, 0)),
            pl.BlockSpec((tm, MLA_HEADS * V_HD), lambda i: (i, 0)),
            _const_spec(mod.shape), _const_spec(nw.shape), _const_spec(w_out.shape), _const_spec(w_router.shape),
            _const_spec((tm, tm)),
        ],
        out_specs=[
            pl.BlockSpec((tm, d), lambda i: (i, 0)),
            pl.BlockSpec((tm, d // 2), lambda i: (i, 0)),
            pl.BlockSpec((tm, LANES), lambda i: (i, 0)),
            pl.BlockSpec((SUBLANES, tm), lambda i: (0, i)),
            pl.BlockSpec((N_EXPERTS, 1), lambda i: (0, 0)),
        ],
        out_shape=[
            jax.ShapeDtypeStruct((s, d), F32),
            jax.ShapeDtypeStruct((s, d // 2), jnp.uint32),
            jax.ShapeDtypeStruct((s, LANES), F32),
            jax.ShapeDtypeStruct((SUBLANES, s), jnp.int32),
            jax.ShapeDtypeStruct((N_EXPERTS, 1), F32),
        ],
        compiler_params=_params(("arbitrary",)),
        name="odd_out_router",
    )(x, yc, yd, mod, nw, w_out, w_router, jnp.triu(jnp.ones((tm, tm), BF16), 1))


def _sc_workers():
    info = plsc.get_sparse_core_info()
    return info.num_cores, info.num_cores * info.num_subcores


def _sc_scatter_rows(x, idx0, idx1, out_rows):
    n, w = x.shape
    nc, nw = _sc_workers()
    per_w = n // nw
    nch = per_w // SC_ROWS
    assert nch % 2 == 0 and nch >= 2
    mesh = plsc.VectorSubcoreMesh(core_axis_name="c", subcore_axis_name="s")

    @functools.partial(
        pl.kernel, mesh=mesh, out_type=jax.ShapeDtypeStruct((out_rows, w), x.dtype),
        scratch_types=[pltpu.VMEM((nch, SC_ROWS), jnp.int32), pltpu.VMEM((nch, SC_ROWS), jnp.int32),
                       pltpu.VMEM((SC_ROWS, w), x.dtype), pltpu.VMEM((SC_ROWS, w), x.dtype),
                       pltpu.SemaphoreType.DMA, pltpu.SemaphoreType.DMA, pltpu.SemaphoreType.DMA,
                       pltpu.SemaphoreType.DMA],
        name="moe_dispatch")
    def scatter(x_hbm, i0_hbm, i1_hbm, out_hbm, i0_v, i1_v, buf0, buf1, lsem0, lsem1, ssem0, ssem1):
        wid = lax.axis_index("s") * nc + lax.axis_index("c")
        pltpu.sync_copy(i0_hbm.at[wid], i0_v)
        pltpu.sync_copy(i1_hbm.at[wid], i1_v)
        base = wid * per_w

        def load(c, buf, sem):
            return pltpu.make_async_copy(x_hbm.at[pl.ds(base + c * SC_ROWS, SC_ROWS)], buf, sem)

        def put(c, buf):
            first = pltpu.make_async_copy(buf, out_hbm.at[i0_v.at[c]], ssem0)
            second = pltpu.make_async_copy(buf, out_hbm.at[i1_v.at[c]], ssem1)
            first.start()
            second.start()
            first.wait()
            second.wait()

        def pair(c, prefetch_next):
            load(c + 1, buf1, lsem1).start()
            load(c, buf0, lsem0).wait()
            put(c, buf0)
            if prefetch_next:
                load(c + 2, buf0, lsem0).start()
            load(c + 1, buf1, lsem1).wait()
            put(c + 1, buf1)

        load(0, buf0, lsem0).start()

        def body(t, carry):
            pair(2 * t, True)
            return carry

        lax.fori_loop(0, nch // 2 - 1, body, 0)
        pair(nch - 2, False)

    return scatter(x, idx0.reshape(nw, nch, SC_ROWS), idx1.reshape(nw, nch, SC_ROWS))


def _sc_gather_rows(table, idx):
    _, w = table.shape
    b = idx.shape[0]
    nc, nw = _sc_workers()
    per_w = b // nw
    nch = per_w // SC_ROWS
    mesh = plsc.VectorSubcoreMesh(core_axis_name="c", subcore_axis_name="s")

    assert nch % 2 == 0 and nch >= 2

    @functools.partial(
        pl.kernel, mesh=mesh, out_type=jax.ShapeDtypeStruct((b, w), table.dtype),
        scratch_types=[pltpu.VMEM((nch, SC_ROWS), jnp.int32), pltpu.VMEM((SC_ROWS, w), table.dtype),
                       pltpu.VMEM((SC_ROWS, w), table.dtype), pltpu.SemaphoreType.DMA, pltpu.SemaphoreType.DMA],
        name="moe_combine_gather")
    def gather(table_hbm, idx_hbm, out_hbm, idx_v, buf0, buf1, sem0, sem1):
        wid = lax.axis_index("s") * nc + lax.axis_index("c")
        pltpu.sync_copy(idx_hbm.at[wid], idx_v)
        base = wid * per_w

        def fetch(c, buf, sem):
            return pltpu.make_async_copy(table_hbm.at[idx_v.at[c]], buf, sem)

        def put(c, buf):
            pltpu.sync_copy(buf, out_hbm.at[pl.ds(base + c * SC_ROWS, SC_ROWS)])

        def pair(c, prefetch_next):
            fetch(c + 1, buf1, sem1).start()
            fetch(c, buf0, sem0).wait()
            put(c, buf0)
            if prefetch_next:
                fetch(c + 2, buf0, sem0).start()
            fetch(c + 1, buf1, sem1).wait()
            put(c + 1, buf1)

        fetch(0, buf0, sem0).start()

        def body(t, carry):
            pair(2 * t, True)
            return carry

        lax.fori_loop(0, nch // 2 - 1, body, 0)
        pair(nch - 2, False)

    return gather(table, idx.reshape(nw, nch, SC_ROWS))


def _moe_kernel(te_ref, nv_ref, rows_ref, xs_ref, wg_ref, wu_ref, wd_ref, ys_ref, *, n_chunks):
    j = pl.program_id(0)
    tm = xs_ref.shape[0]
    tf = wd_ref.shape[1] // n_chunks

    def expert_ffn(rows):
        h = _unpack_bf16_pairs(xs_ref[:rows, :]).astype(BF16)
        y = None
        for c in range(n_chunks):
            cols = slice(c * tf, (c + 1) * tf)
            g = jnp.dot(h, wg_ref[0, :, cols], preferred_element_type=F32)
            u = jnp.dot(h, wu_ref[0, :, cols], preferred_element_type=F32)
            act = (g * jax.nn.sigmoid(g) * u).astype(BF16)
            yc = jnp.dot(act, wd_ref[0, cols, :], preferred_element_type=F32)
            y = yc if y is None else y + yc
        ys_ref[:rows, :] = _pack_bf16_pairs(y)

    @pl.when(j < nv_ref[0])
    def _():
        for rows in range(MOE_ROW_STEP, tm + 1, MOE_ROW_STEP):
            pl.when(rows_ref[j] == rows)(functools.partial(expert_ffn, rows))


def _moe_grouped(xs, tile_expert, n_valid, tile_rows, w_gu, w_down, tm, n_chunks=2):
    p_rows, dh = xs.shape
    d = 2 * dh
    ne, _, ff2 = w_gu.shape
    ff = ff2 // 2
    assert ff % (n_chunks * LANES) == 0

    def tile(j, nv):
        return jnp.minimum(j, nv[0] - 1)

    grid_spec = pltpu.PrefetchScalarGridSpec(
        num_scalar_prefetch=3,
        grid=(p_rows // tm,),
        in_specs=[
            pl.BlockSpec((tm, dh), lambda j, te, nv, tr: (tile(j, nv), 0)),
            pl.BlockSpec((1, d, ff), lambda j, te, nv, tr: (te[tile(j, nv)], 0, 0)),
            pl.BlockSpec((1, d, ff), lambda j, te, nv, tr: (te[tile(j, nv)], 0, 1)),
            pl.BlockSpec((1, ff, d), lambda j, te, nv, tr: (te[tile(j, nv)], 0, 0)),
        ],
        out_specs=pl.BlockSpec((tm, dh), lambda j, te, nv, tr: (tile(j, nv), 0)),
    )
    return pl.pallas_call(
        functools.partial(_moe_kernel, n_chunks=n_chunks),
        grid_spec=grid_spec,
        out_shape=jax.ShapeDtypeStruct((p_rows, dh), jnp.uint32),
        compiler_params=_params(("arbitrary",), vmem=MOE_VMEM_LIMIT),
        name="moe_experts",
    )(tile_expert, n_valid, tile_rows, xs, w_gu, w_gu, w_down)


def _moe_combine_kernel(x_ref, y0_ref, y1_ref, rw_ref, mod_ref, *rest):
    o_ref = rest[-1]
    d = D_MODEL
    w1 = rw_ref[:, 0:1]
    w2 = rw_ref[:, 1:2]
    y = w1 * _unpack_bf16_pairs(y0_ref[...]) + w2 * _unpack_bf16_pairs(y1_ref[...])
    o_ref[...] = x_ref[...] + mod_ref[:, 5 * d:6 * d] * y


def _moe_combine(x, yg, rw, mod, prev, part, n_parts, tm=512):
    s, d = x.shape
    nb = s // tm // n_parts
    first = part * nb
    in_specs = [
        pl.BlockSpec((tm, d), lambda i: (first + i, 0)),
        pl.BlockSpec((tm, d // 2), lambda i: (i, 0)),
        pl.BlockSpec((tm, d // 2), lambda i: (nb + i, 0)),
        pl.BlockSpec((tm, LANES), lambda i: (first + i, 0)),
        _const_spec(mod.shape),
    ]
    args = [x, yg, yg, rw, mod]
    aliases = {}
    if prev is not None:
        in_specs.append(pl.BlockSpec(memory_space=pl.ANY))
        args.append(prev)
        aliases = {len(args) - 1: 0}
    return pl.pallas_call(
        _moe_combine_kernel,
        grid=(nb,),
        in_specs=in_specs,
        out_specs=pl.BlockSpec((tm, d), lambda i: (first + i, 0)),
        out_shape=jax.ShapeDtypeStruct((s, d), F32),
        input_output_aliases=aliases,
        compiler_params=_params(("arbitrary",)),
        name="moe_combine",
    )(*args)


def _moe_sparse(x, h_packed, rw, ridx, counts, mod, w_gu, w_down, tm=MOE_TILE):
    s = x.shape[0]
    ne = w_gu.shape[0]
    n_tiles = (2 * s) // tm + ne
    cnt = counts[:ne, 0].astype(jnp.int32)
    padded = ((cnt + tm - 1) // tm) * tm
    ends = jnp.cumsum(padded)
    offs = ends - padded
    experts = jnp.arange(ne, dtype=jnp.int32)
    off_of = lambda e: jnp.sum(jnp.where(e[:, None] == experts[None, :], offs[None, :], 0), axis=1)
    pos0 = off_of(ridx[0]) + ridx[2]
    pos1 = off_of(ridx[1]) + ridx[3]
    tile_start = jnp.arange(n_tiles, dtype=jnp.int32) * tm
    tile_expert = jnp.minimum(jnp.sum(tile_start[:, None] >= ends[None, :], axis=1), ne - 1).astype(jnp.int32)
    n_valid = (ends[-1] // tm).reshape(1).astype(jnp.int32)
    filled = jnp.clip((offs + cnt)[tile_expert] - tile_start, 0, tm)
    tile_rows = (((filled + MOE_ROW_STEP - 1) // MOE_ROW_STEP) * MOE_ROW_STEP).astype(jnp.int32)
    xs = _sc_scatter_rows(h_packed, pos0, pos1, n_tiles * tm)
    ys = _moe_grouped(xs, tile_expert, n_valid, tile_rows, w_gu, w_down, tm)
    out = None
    sp = s // COMBINE_PARTS
    for part in range(COMBINE_PARTS):
        rows = slice(part * sp, (part + 1) * sp)
        yg = _sc_gather_rows(ys, jnp.concatenate([pos0[rows], pos1[rows]]))
        out = _moe_combine(x, yg, rw, mod, out, part, COMBINE_PARTS)
    return out


def kernel(x, c, positions, norm_mix_w, norm_ffn_w, ada_w, ada_b, e_w_in, a_ln_w, a_ln_b, a_w_s, a_b_s, b_w_grp,
           b_scale, e_w_out, ffn_w_gu, ffn_w_down, o_w_in, c_conv_w, q_a_norm, w_uq, kv_norm, w_ukv, q_norm_w,
           k_norm_w, o_w_out, router_w, moe_w_gu, moe_w_down):
    bsz, s, d = x.shape
    assert bsz == 1 and d == D_MODEL
    depth = ada_w.shape[0]
    nh = MLA_HEADS
    xs = x.reshape(s, d)
    mod = _ada_mod(c, ada_w, ada_b)
    cos_t, sin_t = _rope_tables(positions)
    row = lambda a: a.reshape(1, -1)

    for layer in range(depth):
        i = layer // 2
        m = mod[layer]
        if layer % 2 == 0:
            has_next = layer + 1 < depth
            dummy = jnp.zeros((s // 512, 16, LANES), F32)
            xs, (ffn_w_gu_bf16, ffn_w_down_bf16) = _even_mix(
                xs, m, row(norm_mix_w[layer]), e_w_in[i].astype(BF16), row(a_ln_w[i]), row(a_ln_b[i]), a_w_s[i],
                a_b_s[i].reshape(A_GROUPS, CHUNK, 1), b_w_grp[i].astype(BF16), row(b_scale[i]),
                e_w_out[i].astype(BF16), [ffn_w_gu[i], ffn_w_down[i]])
            xs, moe_w_gu_bf16 = _ffn(xs, m, row(norm_ffn_w[layer]), ffn_w_gu_bf16, ffn_w_down_bf16,
                                     moe_w_gu[i] if has_next else dummy)
        else:
            o2 = 3 * C_WIDTH + Q_LORA + KV_LORA
            w_in = o_w_in[i][:, :o2].astype(BF16)
            w_kpe = jnp.pad(o_w_in[i][:, o2:], ((0, 0), (0, LANES - QK_ROPE))).astype(BF16)
            bound = (1.02 * QK_HD * QK_HD ** -0.5 * LOG2E) * jnp.max(jnp.abs(q_norm_w[i])) * jnp.max(jnp.abs(k_norm_w[i]))
            wkv = w_ukv[i].reshape(KV_LORA, nh, QK_NOPE + V_HD)
            w_k = wkv[:, :, :QK_NOPE].reshape(KV_LORA, nh * QK_NOPE).astype(BF16)
            w_v_t = wkv[:, :, QK_NOPE:].reshape(KV_LORA, nh * V_HD).T.astype(BF16)
            yc, qt, k, vt, moe_w_down_bf16 = _odd_prep(
                xs, m, row(norm_mix_w[layer]), w_in, w_kpe, c_conv_w[i], row(q_a_norm[i]), w_uq[i].T.astype(BF16),
                row(kv_norm[i]), w_k, w_v_t, q_norm_w[i].reshape(QK_HD, 1), row(k_norm_w[i]),
                k_norm_w[i].reshape(QK_HD, 1), jnp.full((1, LANES), -bound, F32), cos_t, sin_t, moe_w_down[i])
            yd = lax.cond(bound <= ATTN_MAX_BOUND, _attention_shifted, _attention, qt, k, vt)
            wr = jnp.pad(router_w[i], ((0, 0), (0, LANES - N_EXPERTS)))
            wr_hi = lax.reduce_precision(wr, exponent_bits=8, mantissa_bits=7)
            w_router = jnp.concatenate([wr_hi, wr - wr_hi], axis=1).astype(BF16)
            xs, hp, rw, ridx, counts = _odd_out(xs, yc, yd, m, row(norm_ffn_w[layer]), o_w_out[i].astype(BF16),
                                                w_router)
            xs = _moe_sparse(xs, hp, rw, ridx, counts, m, moe_w_gu_bf16, moe_w_down_bf16)
    return xs.reshape(bsz, s, d)
```

```python
import functools

import jax
import jax.numpy as jnp
from jax import lax
from jax.experimental import pallas as pl
from jax.experimental.pallas import tpu as pltpu
from jax.experimental.pallas import tpu_sc as plsc

D_MODEL = 1024
SEQ = 16384
EPS = 1e-6
CHUNK = 128
A_WIDTH = 512
A_GROUPS = 4
B_WIDTH = 512
POOL_WINDOWS = (2, 4, 8, 16)
B_HD = 128
C_WIDTH = 512
MLA_HEADS = 4
Q_LORA = 256
KV_LORA = 256
QK_NOPE = 128
QK_ROPE = 64
QK_HD = QK_NOPE + QK_ROPE
V_HD = 128
ROPE_THETA = 10000.0
D_FF = 2816
N_EXPERTS = 8
D_FF_EXPERT = 3584

LANES = 128
SUBLANES = 8
POOL_HALO = 16
CONV_HALO = 8
QK_PAD = 256
V_PAD = 144
LOG2E = 1.4426950408889634
MASK_VALUE = -1e30
MOE_TILE = 512
CONV_COLS = 256
ATTN_HALVES = 2
ATTN_MAX_BOUND = 48.0
MOE_ROW_STEP = 128
COMBINE_PARTS = 8
SC_ROWS = 64
VMEM_LIMIT = 56 * 1024 * 1024
MOE_VMEM_LIMIT = 62 * 1024 * 1024

F32 = jnp.float32
BF16 = jnp.bfloat16


def _params(sem, vmem=VMEM_LIMIT, flags=None):
    return pltpu.CompilerParams(dimension_semantics=sem, vmem_limit_bytes=vmem, flags=flags)


def _const_spec(shape, single=False):
    nd = len(shape)
    return pl.BlockSpec(shape, lambda *_: (0,) * nd, pipeline_mode=pl.Buffered(1) if single else None)


def _rms_mod(x, nw, sc, sh):
    ms = jnp.mean(x * x, axis=-1, keepdims=True)
    return (x * lax.rsqrt(ms + EPS)) * (nw * (1.0 + sc)) + sh


def _ada_kernel(c_ref, w_ref, b_ref, o_ref):
    c = c_ref[...]
    ca = c * jax.nn.sigmoid(c)
    o_ref[0] = jnp.sum(w_ref[0] * ca, axis=0, keepdims=True) + b_ref[0]


def _ada_mod(c, ada_w, ada_b):
    depth, d, n = ada_w.shape
    tn = 3072
    return pl.pallas_call(
        _ada_kernel,
        grid=(depth, n // tn),
        in_specs=[
            pl.BlockSpec((d, 1), lambda l, j: (0, 0)),
            pl.BlockSpec((1, d, tn), lambda l, j: (l, 0, j)),
            pl.BlockSpec((1, 1, tn), lambda l, j: (l, 0, j)),
        ],
        out_specs=pl.BlockSpec((1, 1, tn), lambda l, j: (l, 0, j)),
        out_shape=jax.ShapeDtypeStruct((depth, 1, n), F32),
        compiler_params=_params(("arbitrary", "arbitrary")),
        name="ada_mod",
    )(c.reshape(d, 1), ada_w, ada_b.reshape(depth, 1, n))


def _even_mix_kernel(x_ref, mod_ref, nw_ref, win_ref, lnw_ref, lnb_ref, ws_ref, bs_ref, wg_ref, bsc_ref,
                     wout_ref, *rest):
    n_sides = (len(rest) - 4) // 2
    o_ref = rest[n_sides]
    halo_ref, sv_ref, yb_ref = rest[-3:]
    for side_ref, side_out_ref in zip(rest[:n_sides], rest[n_sides + 1:2 * n_sides + 1]):
        side_out_ref[...] = side_ref[...].astype(BF16)
    tm = x_ref.shape[0]
    i = pl.program_id(0)

    @pl.when(i == 0)
    def _():
        halo_ref[...] = jnp.zeros_like(halo_ref)

    d = D_MODEL
    x = x_ref[...]
    sh, sc, gate = mod_ref[:, 0:d], mod_ref[:, d:2 * d], mod_ref[:, 2 * d:3 * d]
    h = _rms_mod(x, nw_ref[...], sc, sh).astype(BF16)
    p = jnp.dot(h, win_ref[...], preferred_element_type=F32)

    gl = jax.nn.gelu(p[:, :2 * A_WIDTH])
    u = gl[:, :A_WIDTH]
    v = gl[:, A_WIDTH:]
    mu = jnp.mean(v, axis=-1, keepdims=True)
    vc = v - mu
    var = jnp.mean(vc * vc, axis=-1, keepdims=True)
    vn = (vc * lax.rsqrt(var + EPS) * lnw_ref[...] + lnb_ref[...]).astype(BF16)
    row = lax.broadcasted_iota(jnp.int32, (CHUNK, CHUNK), 0)
    col = lax.broadcasted_iota(jnp.int32, (CHUNK, CHUNK), 1)
    for g in range(A_GROUPS):
        w = jnp.where(col <= row, ws_ref[g], 0.0).astype(BF16)
        b = bs_ref[g]
        for c in range(tm // CHUNK):
            blk = vn[c * CHUNK:(c + 1) * CHUNK, g * LANES:(g + 1) * LANES]
            sv_ref[c * CHUNK:(c + 1) * CHUNK, g * LANES:(g + 1) * LANES] = (
                jnp.dot(w, blk, preferred_element_type=F32) + b)
    ya = (u * sv_ref[...]).astype(BF16)

    pb = p[:, 2 * A_WIDTH:]
    ext = jnp.concatenate([halo_ref[...], pb], axis=0)
    halo_ref[...] = pb[tm - POOL_HALO:, :]
    t_glob = i * tm + lax.broadcasted_iota(jnp.int32, (tm, 1), 0)
    s = ext
    width = 1
    for g, win in enumerate(POOL_WINDOWS):
        while width < win:
            s = s + pltpu.roll(s, width, 0)
            width *= 2
        cnt = jnp.minimum(t_glob + 1, win).astype(F32)
        sl = slice(g * B_HD, (g + 1) * B_HD)
        pooled = s[POOL_HALO:, sl] / cnt
        dg = (pooled - pb[:, sl]).astype(BF16)
        yb_ref[:, sl] = jnp.dot(dg, wg_ref[g], preferred_element_type=F32)
    yb = (yb_ref[...] * bsc_ref[...]).astype(BF16)

    mix = (jnp.dot(ya, wout_ref[:A_WIDTH, :], preferred_element_type=F32)
           + jnp.dot(yb, wout_ref[A_WIDTH:, :], preferred_element_type=F32))
    o_ref[...] = x + gate * mix


def _side_cast_spec(side, steps, axis):
    ne = side.shape[0]
    parts = steps // ne
    assert parts * ne == steps and side.shape[axis] % parts == 0
    block = list(side.shape)
    block[0] = 1
    block[axis] //= parts
    assert block[1] % 16 == 0 and block[2] % LANES == 0
    if axis == 1:
        return pl.BlockSpec(tuple(block), lambda i: (i // parts, i % parts, 0))
    return pl.BlockSpec(tuple(block), lambda i: (i // parts, 0, i % parts))


def _row_cast_spec(w, steps):
    rows = -(-w.shape[0] // steps)
    rows = -(-rows // 16) * 16
    last = -(-w.shape[0] // rows) - 1
    return pl.BlockSpec((rows, w.shape[1]), lambda i: (jnp.minimum(i, last), 0))


def _even_mix(x, mod, nw, w_in, ln_w, ln_b, w_s, b_s, w_grp, b_scale, w_out, row_sides, tm=512):
    s, d = x.shape
    steps = s // tm
    sides = list(row_sides)
    side_specs = [_row_cast_spec(w, steps) for w in row_sides]
    outs = pl.pallas_call(
        _even_mix_kernel,
        grid=(steps,),
        in_specs=[
            pl.BlockSpec((tm, d), lambda i: (i, 0)),
            _const_spec(mod.shape), _const_spec(nw.shape), _const_spec(w_in.shape),
            _const_spec(ln_w.shape), _const_spec(ln_b.shape), _const_spec(w_s.shape), _const_spec(b_s.shape),
            _const_spec(w_grp.shape), _const_spec(b_scale.shape), _const_spec(w_out.shape), *side_specs,
        ],
        out_specs=[pl.BlockSpec((tm, d), lambda i: (i, 0)), *side_specs],
        out_shape=[jax.ShapeDtypeStruct((s, d), F32)] + [jax.ShapeDtypeStruct(w.shape, BF16) for w in sides],
        scratch_shapes=[pltpu.VMEM((POOL_HALO, B_WIDTH), F32), pltpu.VMEM((tm, A_WIDTH), F32),
                        pltpu.VMEM((tm, B_WIDTH), F32)],
        compiler_params=_params(("arbitrary",)),
        name="even_mix",
    )(x, mod, nw, w_in, ln_w, ln_b, w_s, b_s, w_grp, b_scale, w_out, *sides)
    return outs[0], outs[1:]


def _ffn_kernel(x_ref, mod_ref, nw_ref, wgu_ref, wd_ref, side_ref, o_ref, side_out_ref, *, n_chunks):
    side_out_ref[...] = side_ref[...].astype(BF16)
    d = D_MODEL
    x = x_ref[...]
    sh, sc, gate = mod_ref[:, 3 * d:4 * d], mod_ref[:, 4 * d:5 * d], mod_ref[:, 5 * d:6 * d]
    h = _rms_mod(x, nw_ref[...], sc, sh).astype(BF16)
    ff = wd_ref.shape[0]
    tf = ff // n_chunks
    acc = jnp.zeros(x.shape, F32)
    for f in range(n_chunks):
        g = jnp.dot(h, wgu_ref[:, f * tf:(f + 1) * tf], preferred_element_type=F32)
        u = jnp.dot(h, wgu_ref[:, ff + f * tf:ff + (f + 1) * tf], preferred_element_type=F32)
        act = (g * jax.nn.sigmoid(g) * u).astype(BF16)
        acc = acc + jnp.dot(act, wd_ref[f * tf:(f + 1) * tf, :], preferred_element_type=F32)
    o_ref[...] = x + gate * acc


def _ffn(x, mod, nw, w_gu, w_down, side, tm=512, n_chunks=11):
    s, d = x.shape
    steps = s // tm
    side_spec = _side_cast_spec(side, steps, 2)
    return pl.pallas_call(
        functools.partial(_ffn_kernel, n_chunks=n_chunks),
        grid=(steps,),
        in_specs=[
            pl.BlockSpec((tm, d), lambda i: (i, 0)),
            _const_spec(mod.shape), _const_spec(nw.shape), _const_spec(w_gu.shape, single=True),
            _const_spec(w_down.shape, single=True), side_spec,
        ],
        out_specs=[pl.BlockSpec((tm, d), lambda i: (i, 0)), side_spec],
        out_shape=[jax.ShapeDtypeStruct((s, d), F32), jax.ShapeDtypeStruct(side.shape, BF16)],
        compiler_params=_params(("arbitrary",)),
        name="ffn",
    )(x, mod, nw, w_gu, w_down, side)


def _rope_table_kernel(pos_ref, invf_ref, cos_ref, sin_ref):
    ang = pos_ref[...].astype(F32) * invf_ref[...]
    cos_ref[...] = jnp.cos(ang)
    sin_ref[...] = jnp.sin(ang)


def _rope_tables(positions):
    s = positions.shape[-1]
    half = QK_ROPE // 2
    inv_freq = ROPE_THETA ** (-jnp.arange(0, QK_ROPE, 2, dtype=F32) / QK_ROPE)
    return pl.pallas_call(
        _rope_table_kernel,
        out_shape=(jax.ShapeDtypeStruct((half, s), F32), jax.ShapeDtypeStruct((half, s), F32)),
        name="rope_tables",
    )(positions.reshape(1, s), inv_freq.reshape(half, 1))


def _odd_prep_kernel(x_ref, mod_ref, nw_ref, win_ref, wkpe_ref, cw_ref, qan_ref, wuqt_ref, kvn_ref, wk_ref, wvt_ref,
                     qnw_ref, knw_ref, knwc_ref, shift_ref, cost_ref, sint_ref, side_ref, yc_ref, qt_ref, k_ref, vt_ref,
                     side_out_ref, halo_ref):
    side_out_ref[...] = side_ref[...].astype(BF16)
    tm = x_ref.shape[0]
    i = pl.program_id(0)

    @pl.when(i == 0)
    def _():
        halo_ref[...] = jnp.zeros_like(halo_ref)

    d = D_MODEL
    x = x_ref[...]
    sh, sc = mod_ref[:, 0:d], mod_ref[:, d:2 * d]
    h = _rms_mod(x, nw_ref[...], sc, sh).astype(BF16)
    cw = C_WIDTH
    o0 = 3 * cw
    proj = lambda lo, hi: jnp.dot(h, win_ref[:, lo:hi], preferred_element_type=F32)
    kpe_pad = jnp.dot(h, wkpe_ref[...], preferred_element_type=F32)
    kpe = kpe_pad[:, :QK_ROPE]

    for c0 in range(0, cw, CONV_COLS):
        cols = slice(c0, c0 + CONV_COLS)
        z = proj(cw + c0, cw + c0 + CONV_COLS) * proj(2 * cw + c0, 2 * cw + c0 + CONV_COLS)
        ext = jnp.concatenate([halo_ref[:, cols], z], axis=0)
        halo_ref[:, cols] = z[tm - CONV_HALO:, :]
        z1 = pltpu.roll(ext, 1, 0)[CONV_HALO:, :]
        z2 = pltpu.roll(ext, 2, 0)[CONV_HALO:, :]
        conv = cw_ref[0:1, cols] * z2 + cw_ref[1:2, cols] * z1 + cw_ref[2:3, cols] * z
        yc_ref[:, cols] = (proj(c0, c0 + CONV_COLS) * conv).astype(BF16)

    cq = proj(o0, o0 + Q_LORA)
    ckv = proj(o0 + Q_LORA, o0 + Q_LORA + KV_LORA)
    cqn = cq * lax.rsqrt(jnp.mean(cq * cq, axis=-1, keepdims=True) + EPS) * qan_ref[...]
    ckvn = ckv * lax.rsqrt(jnp.mean(ckv * ckv, axis=-1, keepdims=True) + EPS) * kvn_ref[...]
    cqn_t = cqn.T.astype(BF16)
    ckvn_t = ckvn.T.astype(BF16)
    q_t = jnp.dot(wuqt_ref[...], cqn_t, preferred_element_type=F32)
    v_t = jnp.dot(wvt_ref[...], ckvn_t, preferred_element_type=F32)
    kn_all = jnp.dot(ckvn.astype(BF16), wk_ref[...], preferred_element_type=F32)

    nh = MLA_HEADS
    half = QK_ROPE // 2
    sm_scale = QK_HD ** -0.5
    cos_t, sin_t = cost_ref[...], sint_ref[...]
    kn_w = knw_ref[:, :QK_NOPE]
    kw_t = kpe_pad.T[:QK_ROPE, :] * knwc_ref[QK_NOPE:, :]
    k1, k2 = kw_t[:half, :], kw_t[half:, :]
    kr_t = jnp.concatenate([k1 * cos_t - k2 * sin_t, k2 * cos_t + k1 * sin_t,
                            jnp.zeros((LANES - QK_ROPE, tm), F32)], axis=0)
    kr = kr_t.T[:, :QK_ROPE]
    kpe_ss = jnp.sum(kpe * kpe, axis=-1, keepdims=True)
    npad = QK_PAD - QK_HD
    zrows = jnp.where(lax.broadcasted_iota(jnp.int32, (npad, tm), 0) == 0, shift_ref[:, 0:1], 0.0)
    zpad = jnp.where(lax.broadcasted_iota(jnp.int32, (tm, npad), 1) == 0, 1.0, 0.0)
    ones_rows = (lax.broadcasted_iota(jnp.int32, (V_PAD - V_HD, tm), 0) == 0).astype(BF16)
    for hd in range(nh):
        qh = q_t[hd * QK_HD:(hd + 1) * QK_HD, :]
        q_inv = lax.rsqrt(jnp.sum(qh * qh, axis=0, keepdims=True) / QK_HD + EPS) * (sm_scale * LOG2E)
        qw = qh * qnw_ref[...]
        x1, x2 = qw[QK_NOPE:QK_NOPE + half, :], qw[QK_NOPE + half:, :]
        qt_ref[hd] = jnp.concatenate(
            [qw[:QK_NOPE, :] * q_inv, (x1 * cos_t - x2 * sin_t) * q_inv, (x2 * cos_t + x1 * sin_t) * q_inv, zrows],
            axis=0).astype(BF16)
        kn = kn_all[:, hd * QK_NOPE:(hd + 1) * QK_NOPE]
        k_ss = jnp.sum(kn * kn, axis=-1, keepdims=True) + kpe_ss
        k_inv = lax.rsqrt(k_ss / QK_HD + EPS)
        k_ref[hd] = jnp.concatenate([kn * k_inv * kn_w, kr * k_inv, zpad], axis=1).astype(BF16)
        vt_ref[hd, :V_HD, :] = v_t[hd * V_HD:(hd + 1) * V_HD, :].astype(BF16)
        vt_ref[hd, V_HD:, :] = ones_rows


def _odd_prep(x, mod, nw, w_in, w_kpe, conv_w, q_a_norm, w_uq_t, kv_norm, w_k, w_v_t, q_norm_col, k_norm_w,
              k_norm_col, shift, cos_t, sin_t, side, tm=1024):
    s, d = x.shape
    nh = MLA_HEADS
    half = QK_ROPE // 2
    steps = s // tm
    side_spec = _side_cast_spec(side, steps, 1)
    consts = [mod, nw, w_in, w_kpe, conv_w, q_a_norm, w_uq_t, kv_norm, w_k, w_v_t, q_norm_col, k_norm_w, k_norm_col,
              shift]
    return pl.pallas_call(
        _odd_prep_kernel,
        grid=(steps,),
        in_specs=[pl.BlockSpec((tm, d), lambda i: (i, 0))] + [_const_spec(a.shape) for a in consts] + [
            pl.BlockSpec((half, tm), lambda i: (0, i)), pl.BlockSpec((half, tm), lambda i: (0, i)), side_spec],
        out_specs=[
            pl.BlockSpec((tm, C_WIDTH), lambda i: (i, 0)),
            pl.BlockSpec((nh, QK_PAD, tm), lambda i: (0, 0, i)),
            pl.BlockSpec((nh, tm, QK_PAD), lambda i: (0, i, 0)),
            pl.BlockSpec((nh, V_PAD, tm), lambda i: (0, 0, i)),
            side_spec,
        ],
        out_shape=[
            jax.ShapeDtypeStruct((s, C_WIDTH), BF16),
            jax.ShapeDtypeStruct((nh, QK_PAD, s), BF16),
            jax.ShapeDtypeStruct((nh, s, QK_PAD), BF16),
            jax.ShapeDtypeStruct((nh, V_PAD, s), BF16),
            jax.ShapeDtypeStruct(side.shape, BF16),
        ],
        scratch_shapes=[pltpu.VMEM((CONV_HALO, C_WIDTH), F32)],
        compiler_params=_params(("arbitrary",)),
        name="odd_prep",
    )(x, *consts, cos_t, sin_t, side)


def _causal_mask(s, lead=0):
    key = lax.broadcasted_iota(jnp.int32, s.shape, 0)
    qry = lead + lax.broadcasted_iota(jnp.int32, s.shape, 1)
    return jnp.where(key <= qry, s, MASK_VALUE)


def _attn_shifted_kernel(qt_ref, k_ref, vt_ref, o_ref, acc_ref, l_ref, *, tq, tk):
    qi = pl.program_id(1)
    acc_ref[...] = jnp.zeros_like(acc_ref)
    l_ref[...] = jnp.zeros_like(l_ref)
    r = tq // tk

    def block(j, diag_piece=None):
        q0 = 0 if diag_piece is None else diag_piece * tk
        start = pl.multiple_of(j * tk, tk)
        k = k_ref[0, pl.ds(start, tk), :]
        vt = vt_ref[0, :V_HD, pl.ds(start, tk)]
        width = (tq - q0 + ATTN_HALVES - 1) // ATTN_HALVES
        width = -(-width // tk) * tk
        for c0 in range(q0, tq, width):
            cols = slice(c0, min(c0 + width, tq))
            s = jnp.dot(k, qt_ref[0, :, cols], preferred_element_type=F32)
            if diag_piece is not None and c0 == q0:
                s = _causal_mask(s)
            p = jnp.exp2(s)
            l_ref[:, cols] += jnp.sum(p.reshape(tk // SUBLANES, SUBLANES, p.shape[1]), axis=0)
            acc_ref[:, cols] += jnp.dot(vt, p.astype(BF16), preferred_element_type=F32)

    def body(t, carry):
        for b in range(r):
            block(r * t + b)
        return carry

    lax.fori_loop(0, qi, body, 0)
    for b in range(r):
        block(r * qi + b, diag_piece=b)
    out_t = acc_ref[...] / jnp.sum(l_ref[...], axis=0, keepdims=True)
    o_ref[...] = out_t.T.astype(o_ref.dtype)


def _attention_shifted(qt, k, vt, tq=2048, tk=512):
    nh, _, s = qt.shape
    return pl.pallas_call(
        functools.partial(_attn_shifted_kernel, tq=tq, tk=tk),
        grid=(nh, s // tq),
        in_specs=[
            pl.BlockSpec((1, QK_PAD, tq), lambda h, i: (h, 0, i)),
            pl.BlockSpec((1, s, QK_PAD), lambda h, i: (h, 0, 0)),
            pl.BlockSpec((1, V_PAD, s), lambda h, i: (h, 0, 0)),
        ],
        out_specs=pl.BlockSpec((tq, V_HD), lambda h, i: (i, h)),
        out_shape=jax.ShapeDtypeStruct((s, nh * V_HD), BF16),
        scratch_shapes=[pltpu.VMEM((V_HD, tq), F32), pltpu.VMEM((SUBLANES, tq), F32)],
        compiler_params=_params(("arbitrary", "arbitrary")),
        name="mla_attention_shifted",
    )(qt, k, vt)


def _attn_kernel(qt_ref, k_ref, vt_ref, o_ref, s0_ref, s1_ref, bm0_ref, bm1_ref, m_ref, acc_ref, *, tq, tk):
    qi = pl.program_id(1)
    m_ref[...] = jnp.full_like(m_ref, MASK_VALUE)
    acc_ref[...] = jnp.zeros_like(acc_ref)

    def scores(j, s_ref, bm_ref, diag_offset=None):
        q0 = 0 if diag_offset is None else diag_offset
        start = pl.multiple_of(j * tk, tk)
        s = jnp.dot(k_ref[0, pl.ds(start, tk), :], qt_ref[0, :, q0:], preferred_element_type=F32)
        if diag_offset is not None:
            s = _causal_mask(s)
        s_ref[:, q0:] = s
        bm_ref[:, q0:] = jnp.max(s, axis=0, keepdims=True)

    def consume(j, s_ref, bm_ref, q0=0):
        m_prev = m_ref[:, q0:]
        m_new = jnp.maximum(m_prev, bm_ref[:, q0:])
        alpha = jnp.exp2(m_prev - m_new)
        p = jnp.exp2((s_ref[:, q0:] - m_new).astype(BF16))
        start = pl.multiple_of(j * tk, tk)
        vt = vt_ref[0, :, pl.ds(start, tk)]
        acc_ref[:, q0:] = alpha * acc_ref[:, q0:] + jnp.dot(vt, p, preferred_element_type=F32)
        m_ref[:, q0:] = m_new

    r = tq // tk
    slots = ((s0_ref, bm0_ref), (s1_ref, bm1_ref))

    def trip(j0, then_diagonal):
        for b in range(r):
            first_diag = then_diagonal and b == r - 1
            scores(j0 + b + 1, *slots[(b + 1) % 2], diag_offset=0 if first_diag else None)
            consume(j0 + b, *slots[b % 2])

    def diagonal(jd):
        for b in range(r):
            if b + 1 < r:
                scores(jd + b + 1, *slots[(b + 1) % 2], diag_offset=(b + 1) * tk)
            consume(jd + b, *slots[b % 2], q0=b * tk)

    @pl.when(qi > 0)
    def _():
        scores(0, *slots[0])

    n_plain = jnp.maximum(qi - 1, 0)

    def body(t, carry):
        trip(r * (2 * t), False)
        trip(r * (2 * t + 1), False)
        return carry

    lax.fori_loop(0, n_plain // 2, body, 0)

    @pl.when(n_plain % 2 == 1)
    def _():
        trip(r * (n_plain - 1), False)

    @pl.when(qi > 0)
    def _():
        trip(r * (qi - 1), True)
        diagonal(r * qi)

    @pl.when(qi == 0)
    def _():
        scores(0, *slots[0], diag_offset=0)
        diagonal(0)

    acc = acc_ref[...]
    out_t = acc[:V_HD, :] / acc[V_HD:V_HD + 1, :]
    o_ref[...] = out_t.T.astype(o_ref.dtype)


def _attention(qt, k, vt, tq=1024, tk=512):
    nh, _, s = qt.shape
    assert tq % (2 * tk) == 0
    return pl.pallas_call(
        functools.partial(_attn_kernel, tq=tq, tk=tk),
        grid=(nh, s // tq),
        in_specs=[
            pl.BlockSpec((1, QK_PAD, tq), lambda h, i: (h, 0, i)),
            pl.BlockSpec((1, s, QK_PAD), lambda h, i: (h, 0, 0)),
            pl.BlockSpec((1, V_PAD, s), lambda h, i: (h, 0, 0)),
        ],
        out_specs=pl.BlockSpec((tq, V_HD), lambda h, i: (i, h)),
        out_shape=jax.ShapeDtypeStruct((s, nh * V_HD), BF16),
        scratch_shapes=[pltpu.VMEM((tk, tq), F32), pltpu.VMEM((tk, tq), F32), pltpu.VMEM((1, tq), F32),
                        pltpu.VMEM((1, tq), F32), pltpu.VMEM((1, tq), F32), pltpu.VMEM((V_PAD, tq), F32)],
        compiler_params=_params(("arbitrary", "arbitrary")),
        name="mla_attention",
    )(qt, k, vt)


def _pack_bf16_pairs(x):
    n = x.shape[1] // 2
    lo = pltpu.bitcast(x[:, :n].astype(BF16).astype(F32), jnp.uint32)
    hi = pltpu.bitcast(x[:, n:].astype(BF16).astype(F32), jnp.uint32)
    return (lo >> 16) | hi


def _unpack_bf16_pairs(p):
    lo = pltpu.bitcast(p << 16, F32)
    hi = pltpu.bitcast(p & jnp.uint32(0xFFFF0000), F32)
    return jnp.concatenate([lo, hi], axis=1)


def _odd_out_kernel(x_ref, yc_ref, yd_ref, mod_ref, nw_ref, wout_ref, wr_ref, before_ref, x_out_ref, h_ref, rw_ref,
                    ridx_ref, cnt_ref):
    d = D_MODEL
    x = x_ref[...]
    gate_m = mod_ref[:, 2 * d:3 * d]
    sh, sc = mod_ref[:, 3 * d:4 * d], mod_ref[:, 4 * d:5 * d]
    mix = (jnp.dot(yc_ref[...], wout_ref[:C_WIDTH, :], preferred_element_type=F32)
           + jnp.dot(yd_ref[...], wout_ref[C_WIDTH:, :], preferred_element_type=F32))
    x1 = x + gate_m * mix
    x_out_ref[...] = x1
    h = _rms_mod(x1, nw_ref[...], sc, sh)
    h_ref[...] = _pack_bf16_pairs(h)

    h_hi = h.astype(BF16)
    h_lo = (h - h_hi.astype(F32)).astype(BF16)
    hw = jnp.dot(h_hi, wr_ref[...], preferred_element_type=F32)
    logits = hw[:, :LANES] + (hw[:, LANES:] + jnp.dot(h_lo, wr_ref[:, :LANES], preferred_element_type=F32))
    lt = logits.T[:N_EXPERTS, :]
    ex = lax.broadcasted_iota(jnp.int32, lt.shape, 0)
    m1 = jnp.max(lt, axis=0, keepdims=True)
    i1 = jnp.min(jnp.where(lt == m1, ex, N_EXPERTS), axis=0, keepdims=True)
    rest = jnp.where(ex == i1, -jnp.inf, lt)
    m2 = jnp.max(rest, axis=0, keepdims=True)
    i2 = jnp.min(jnp.where(rest == m2, ex, N_EXPERTS), axis=0, keepdims=True)
    e2 = jnp.exp(m2 - m1)
    w1 = 1.0 / (1.0 + e2)
    w2 = e2 / (1.0 + e2)
    row = lax.broadcasted_iota(jnp.int32, (LANES, lt.shape[1]), 0)
    rw_ref[...] = jnp.where(row == 0, w1, jnp.where(row == 1, w2, 0.0)).T

    i = pl.program_id(0)

    @pl.when(i == 0)
    def _():
        cnt_ref[...] = jnp.zeros_like(cnt_ref)

    onehot = jnp.where((ex == i1) | (ex == i2), 1.0, 0.0)
    prefix = jnp.dot(onehot.astype(BF16), before_ref[...], preferred_element_type=F32) + cnt_ref[...]
    r1 = jnp.sum(jnp.where(ex == i1, prefix, 0.0), axis=0, keepdims=True)
    r2 = jnp.sum(jnp.where(ex == i2, prefix, 0.0), axis=0, keepdims=True)
    cnt_ref[...] += jnp.sum(onehot, axis=1, keepdims=True)
    ridx_ref[...] = jnp.where(ex == 0, i1, jnp.where(ex == 1, i2, jnp.where(
        ex == 2, r1.astype(jnp.int32), jnp.where(ex == 3, r2.astype(jnp.int32), 0))))


def _odd_out(x, yc, yd, mod, nw, w_out, w_router, tm=1024):
    s, d = x.shape
    return pl.pallas_call(
        _odd_out_kernel,
        grid=(s // tm,),
        in_specs=[
            pl.BlockSpec((tm, d), lambda i: (i, 0)),
            pl.BlockSpec((tm, C_WIDTH), lambda i: (i, 0)),
            pl.BlockSpec((tm, MLA_HEADS * V_HD), lambda i: (i, 0)),
            _const_spec(mod.shape), _const_spec(nw.shape), _const_spec(w_out.shape), _const_spec(w_router.shape),
            _const_spec((tm, tm)),
        ],
        out_specs=[
            pl.BlockSpec((tm, d), lambda i: (i, 0)),
            pl.BlockSpec((tm, d // 2), lambda i: (i, 0)),
            pl.BlockSpec((tm, LANES), lambda i: (i, 0)),
            pl.BlockSpec((SUBLANES, tm), lambda i: (0, i)),
            pl.BlockSpec((N_EXPERTS, 1), lambda i: (0, 0)),
        ],
        out_shape=[
            jax.ShapeDtypeStruct((s, d), F32),
            jax.ShapeDtypeStruct((s, d // 2), jnp.uint32),
            jax.ShapeDtypeStruct((s, LANES), F32),
            jax.ShapeDtypeStruct((SUBLANES, s), jnp.int32),
            jax.ShapeDtypeStruct((N_EXPERTS, 1), F32),
        ],
        compiler_params=_params(("arbitrary",)),
        name="odd_out_router",
    )(x, yc, yd, mod, nw, w_out, w_router, jnp.triu(jnp.ones((tm, tm), BF16), 1))


def _sc_workers():
    info = plsc.get_sparse_core_info()
    return info.num_cores, info.num_cores * info.num_subcores


def _sc_scatter_rows(x, idx0, idx1, out_rows):
    n, w = x.shape
    nc, nw = _sc_workers()
    per_w = n // nw
    nch = per_w // SC_ROWS
    assert nch % 2 == 0 and nch >= 2
    mesh = plsc.VectorSubcoreMesh(core_axis_name="c", subcore_axis_name="s")

    @functools.partial(
        pl.kernel, mesh=mesh, out_type=jax.ShapeDtypeStruct((out_rows, w), x.dtype),
        scratch_types=[pltpu.VMEM((nch, SC_ROWS), jnp.int32), pltpu.VMEM((nch, SC_ROWS), jnp.int32),
                       pltpu.VMEM((SC_ROWS, w), x.dtype), pltpu.VMEM((SC_ROWS, w), x.dtype),
                       pltpu.SemaphoreType.DMA, pltpu.SemaphoreType.DMA, pltpu.SemaphoreType.DMA,
                       pltpu.SemaphoreType.DMA],
        name="moe_dispatch")
    def scatter(x_hbm, i0_hbm, i1_hbm, out_hbm, i0_v, i1_v, buf0, buf1, lsem0, lsem1, ssem0, ssem1):
        wid = lax.axis_index("s") * nc + lax.axis_index("c")
        pltpu.sync_copy(i0_hbm.at[wid], i0_v)
        pltpu.sync_copy(i1_hbm.at[wid], i1_v)
        base = wid * per_w

        def load(c, buf, sem):
            return pltpu.make_async_copy(x_hbm.at[pl.ds(base + c * SC_ROWS, SC_ROWS)], buf, sem)

        def put(c, buf):
            first = pltpu.make_async_copy(buf, out_hbm.at[i0_v.at[c]], ssem0)
            second = pltpu.make_async_copy(buf, out_hbm.at[i1_v.at[c]], ssem1)
            first.start()
            second.start()
            first.wait()
            second.wait()

        def pair(c, prefetch_next):
            load(c + 1, buf1, lsem1).start()
            load(c, buf0, lsem0).wait()
            put(c, buf0)
            if prefetch_next:
                load(c + 2, buf0, lsem0).start()
            load(c + 1, buf1, lsem1).wait()
            put(c + 1, buf1)

        load(0, buf0, lsem0).start()

        def body(t, carry):
            pair(2 * t, True)
            return carry

        lax.fori_loop(0, nch // 2 - 1, body, 0)
        pair(nch - 2, False)

    return scatter(x, idx0.reshape(nw, nch, SC_ROWS), idx1.reshape(nw, nch, SC_ROWS))


def _sc_gather_rows(table, idx):
    _, w = table.shape
    b = idx.shape[0]
    nc, nw = _sc_workers()
    per_w = b // nw
    nch = per_w // SC_ROWS
    mesh = plsc.VectorSubcoreMesh(core_axis_name="c", subcore_axis_name="s")

    assert nch % 2 == 0 and nch >= 2

    @functools.partial(
        pl.kernel, mesh=mesh, out_type=jax.ShapeDtypeStruct((b, w), table.dtype),
        scratch_types=[pltpu.VMEM((nch, SC_ROWS), jnp.int32), pltpu.VMEM((SC_ROWS, w), table.dtype),
                       pltpu.VMEM((SC_ROWS, w), table.dtype), pltpu.SemaphoreType.DMA, pltpu.SemaphoreType.DMA],
        name="moe_combine_gather")
    def gather(table_hbm, idx_hbm, out_hbm, idx_v, buf0, buf1, sem0, sem1):
        wid = lax.axis_index("s") * nc + lax.axis_index("c")
        pltpu.sync_copy(idx_hbm.at[wid], idx_v)
        base = wid * per_w

        def fetch(c, buf, sem):
            return pltpu.make_async_copy(table_hbm.at[idx_v.at[c]], buf, sem)

        def put(c, buf):
            pltpu.sync_copy(buf, out_hbm.at[pl.ds(base + c * SC_ROWS, SC_ROWS)])

        def pair(c, prefetch_next):
            fetch(c + 1, buf1, sem1).start()
            fetch(c, buf0, sem0).wait()
            put(c, buf0)
            if prefetch_next:
                fetch(c + 2, buf0, sem0).start()
            fetch(c + 1, buf1, sem1).wait()
            put(c + 1, buf1)

        fetch(0, buf0, sem0).start()

        def body(t, carry):
            pair(2 * t, True)
            return carry

        lax.fori_loop(0, nch // 2 - 1, body, 0)
        pair(nch - 2, False)

    return gather(table, idx.reshape(nw, nch, SC_ROWS))


def _moe_kernel(te_ref, nv_ref, rows_ref, xs_ref, wg_ref, wu_ref, wd_ref, ys_ref, *, n_chunks):
    j = pl.program_id(0)
    tm = xs_ref.shape[0]
    tf = wd_ref.shape[1] // n_chunks

    def expert_ffn(rows):
        h = _unpack_bf16_pairs(xs_ref[:rows, :]).astype(BF16)
        y = None
        for c in range(n_chunks):
            cols = slice(c * tf, (c + 1) * tf)
            g = jnp.dot(h, wg_ref[0, :, cols], preferred_element_type=F32)
            u = jnp.dot(h, wu_ref[0, :, cols], preferred_element_type=F32)
            act = (g * jax.nn.sigmoid(g) * u).astype(BF16)
            yc = jnp.dot(act, wd_ref[0, cols, :], preferred_element_type=F32)
            y = yc if y is None else y + yc
        ys_ref[:rows, :] = _pack_bf16_pairs(y)

    @pl.when(j < nv_ref[0])
    def _():
        for rows in range(MOE_ROW_STEP, tm + 1, MOE_ROW_STEP):
            pl.when(rows_ref[j] == rows)(functools.partial(expert_ffn, rows))


def _moe_grouped(xs, tile_expert, n_valid, tile_rows, w_gu, w_down, tm, n_chunks=2):
    p_rows, dh = xs.shape
    d = 2 * dh
    ne, _, ff2 = w_gu.shape
    ff = ff2 // 2
    assert ff % (n_chunks * LANES) == 0

    def tile(j, nv):
        return jnp.minimum(j, nv[0] - 1)

    grid_spec = pltpu.PrefetchScalarGridSpec(
        num_scalar_prefetch=3,
        grid=(p_rows // tm,),
        in_specs=[
            pl.BlockSpec((tm, dh), lambda j, te, nv, tr: (tile(j, nv), 0)),
            pl.BlockSpec((1, d, ff), lambda j, te, nv, tr: (te[tile(j, nv)], 0, 0)),
            pl.BlockSpec((1, d, ff), lambda j, te, nv, tr: (te[tile(j, nv)], 0, 1)),
            pl.BlockSpec((1, ff, d), lambda j, te, nv, tr: (te[tile(j, nv)], 0, 0)),
        ],
        out_specs=pl.BlockSpec((tm, dh), lambda j, te, nv, tr: (tile(j, nv), 0)),
    )
    return pl.pallas_call(
        functools.partial(_moe_kernel, n_chunks=n_chunks),
        grid_spec=grid_spec,
        out_shape=jax.ShapeDtypeStruct((p_rows, dh), jnp.uint32),
        compiler_params=_params(("arbitrary",), vmem=MOE_VMEM_LIMIT),
        name="moe_experts",
    )(tile_expert, n_valid, tile_rows, xs, w_gu, w_gu, w_down)


def _moe_combine_kernel(x_ref, y0_ref, y1_ref, rw_ref, mod_ref, *rest):
    o_ref = rest[-1]
    d = D_MODEL
    w1 = rw_ref[:, 0:1]
    w2 = rw_ref[:, 1:2]
    y = w1 * _unpack_bf16_pairs(y0_ref[...]) + w2 * _unpack_bf16_pairs(y1_ref[...])
    o_ref[...] = x_ref[...] + mod_ref[:, 5 * d:6 * d] * y


def _moe_combine(x, yg, rw, mod, prev, part, n_parts, tm=512):
    s, d = x.shape
    nb = s // tm // n_parts
    first = part * nb
    in_specs = [
        pl.BlockSpec((tm, d), lambda i: (first + i, 0)),
        pl.BlockSpec((tm, d // 2), lambda i: (i, 0)),
        pl.BlockSpec((tm, d // 2), lambda i: (nb + i, 0)),
        pl.BlockSpec((tm, LANES), lambda i: (first + i, 0)),
        _const_spec(mod.shape),
    ]
    args = [x, yg, yg, rw, mod]
    aliases = {}
    if prev is not None:
        in_specs.append(pl.BlockSpec(memory_space=pl.ANY))
        args.append(prev)
        aliases = {len(args) - 1: 0}
    return pl.pallas_call(
        _moe_combine_kernel,
        grid=(nb,),
        in_specs=in_specs,
        out_specs=pl.BlockSpec((tm, d), lambda i: (first + i, 0)),
        out_shape=jax.ShapeDtypeStruct((s, d), F32),
        input_output_aliases=aliases,
        compiler_params=_params(("arbitrary",)),
        name="moe_combine",
    )(*args)


def _moe_sparse(x, h_packed, rw, ridx, counts, mod, w_gu, w_down, tm=MOE_TILE):
    s = x.shape[0]
    ne = w_gu.shape[0]
    n_tiles = (2 * s) // tm + ne
    cnt = counts[:ne, 0].astype(jnp.int32)
    padded = ((cnt + tm - 1) // tm) * tm
    ends = jnp.cumsum(padded)
    offs = ends - padded
    experts = jnp.arange(ne, dtype=jnp.int32)
    off_of = lambda e: jnp.sum(jnp.where(e[:, None] == experts[None, :], offs[None, :], 0), axis=1)
    pos0 = off_of(ridx[0]) + ridx[2]
    pos1 = off_of(ridx[1]) + ridx[3]
    tile_start = jnp.arange(n_tiles, dtype=jnp.int32) * tm
    tile_expert = jnp.minimum(jnp.sum(tile_start[:, None] >= ends[None, :], axis=1), ne - 1).astype(jnp.int32)
    n_valid = (ends[-1] // tm).reshape(1).astype(jnp.int32)
    filled = jnp.clip((offs + cnt)[tile_expert] - tile_start, 0, tm)
    tile_rows = (((filled + MOE_ROW_STEP - 1) // MOE_ROW_STEP) * MOE_ROW_STEP).astype(jnp.int32)
    xs = _sc_scatter_rows(h_packed, pos0, pos1, n_tiles * tm)
    ys = _moe_grouped(xs, tile_expert, n_valid, tile_rows, w_gu, w_down, tm)
    out = None
    sp = s // COMBINE_PARTS
    for part in range(COMBINE_PARTS):
        rows = slice(part * sp, (part + 1) * sp)
        yg = _sc_gather_rows(ys, jnp.concatenate([pos0[rows], pos1[rows]]))
        out = _moe_combine(x, yg, rw, mod, out, part, COMBINE_PARTS)
    return out


def kernel(x, c, positions, norm_mix_w, norm_ffn_w, ada_w, ada_b, e_w_in, a_ln_w, a_ln_b, a_w_s, a_b_s, b_w_grp,
           b_scale, e_w_out, ffn_w_gu, ffn_w_down, o_w_in, c_conv_w, q_a_norm, w_uq, kv_norm, w_ukv, q_norm_w,
           k_norm_w, o_w_out, router_w, moe_w_gu, moe_w_down):
    bsz, s, d = x.shape
    assert bsz == 1 and d == D_MODEL
    depth = ada_w.shape[0]
    nh = MLA_HEADS
    xs = x.reshape(s, d)
    mod = _ada_mod(c, ada_w, ada_b)
    cos_t, sin_t = _rope_tables(positions)
    row = lambda a: a.reshape(1, -1)

    for layer in range(depth):
        i = layer // 2
        m = mod[layer]
        if layer % 2 == 0:
            has_next = layer + 1 < depth
            dummy = jnp.zeros((s // 512, 16, LANES), F32)
            xs, (ffn_w_gu_bf16, ffn_w_down_bf16) = _even_mix(
                xs, m, row(norm_mix_w[layer]), e_w_in[i].astype(BF16), row(a_ln_w[i]), row(a_ln_b[i]), a_w_s[i],
                a_b_s[i].reshape(A_GROUPS, CHUNK, 1), b_w_grp[i].astype(BF16), row(b_scale[i]),
                e_w_out[i].astype(BF16), [ffn_w_gu[i], ffn_w_down[i]])
            xs, moe_w_gu_bf16 = _ffn(xs, m, row(norm_ffn_w[layer]), ffn_w_gu_bf16, ffn_w_down_bf16,
                                     moe_w_gu[i] if has_next else dummy)
        else:
            o2 = 3 * C_WIDTH + Q_LORA + KV_LORA
            w_in = o_w_in[i][:, :o2].astype(BF16)
            w_kpe = jnp.pad(o_w_in[i][:, o2:], ((0, 0), (0, LANES - QK_ROPE))).astype(BF16)
            bound = (1.02 * QK_HD * QK_HD ** -0.5 * LOG2E) * jnp.max(jnp.abs(q_norm_w[i])) * jnp.max(jnp.abs(k_norm_w[i]))
            wkv = w_ukv[i].reshape(KV_LORA, nh, QK_NOPE + V_HD)
            w_k = wkv[:, :, :QK_NOPE].reshape(KV_LORA, nh * QK_NOPE).astype(BF16)
            w_v_t = wkv[:, :, QK_NOPE:].reshape(KV_LORA, nh * V_HD).T.astype(BF16)
            yc, qt, k, vt, moe_w_down_bf16 = _odd_prep(
                xs, m, row(norm_mix_w[layer]), w_in, w_kpe, c_conv_w[i], row(q_a_norm[i]), w_uq[i].T.astype(BF16),
                row(kv_norm[i]), w_k, w_v_t, q_norm_w[i].reshape(QK_HD, 1), row(k_norm_w[i]),
                k_norm_w[i].reshape(QK_HD, 1), jnp.full((1, LANES), -bound, F32), cos_t, sin_t, moe_w_down[i])
            yd = lax.cond(bound <= ATTN_MAX_BOUND, _attention_shifted, _attention, qt, k, vt)
            wr = jnp.pad(router_w[i], ((0, 0), (0, LANES - N_EXPERTS)))
            wr_hi = lax.reduce_precision(wr, exponent_bits=8, mantissa_bits=7)
            w_router = jnp.concatenate([wr_hi, wr - wr_hi], axis=1).astype(BF16)
            xs, hp, rw, ridx, counts = _odd_out(xs, yc, yd, m, row(norm_ffn_w[layer]), o_w_out[i].astype(BF16),
                                                w_router)
            xs = _moe_sparse(xs, hp, rw, ridx, counts, m, moe_w_gu_bf16, moe_w_down_bf16)
    return xs.reshape(bsz, s, d)
```
